```python
import math
import jax, jax.numpy as jnp
from jax import lax
import numpy as np

D_MODEL = 2048
BATCH = 8
SEQ = 2048
DEPTH = 1

D_MIX = D_MODEL
D_CONV = D_MIX // 2
D_ATTN = D_MIX - D_CONV
CONV_WIDTH = 31
ATTN_HEAD_DIM = 64
V_HEAD_DIM = 2 * ATTN_HEAD_DIM
N_ATTN_HEADS = D_ATTN // V_HEAD_DIM
Q_BLOCK = 128
D_IN_PROJ = 2 * D_CONV + 3 * D_ATTN
N_GROUPS = 8
EXPERTS_PER_GROUP = 8
N_EXPERTS = N_GROUPS * EXPERTS_PER_GROUP
TOP_K = 2
D_EXPERT = 512
MOE_BLOCK = 128
RMS_EPS = 1e-6
SUBLN_EPS = 1e-5
LN_EPS = 1e-5

kernel_name = "hybrid_conv_diffattn_hmoe"


def rmsnorm(x, w, eps=RMS_EPS):
    xf = x.astype(jnp.float32)
    y = xf * lax.rsqrt(jnp.mean(xf * xf, axis=-1, keepdims=True) + eps)
    return (y * w.astype(jnp.float32)).astype(x.dtype)


def conformer_conv(a, g, dw_w, dw_b, ln_w, ln_b):
    u = a * jax.nn.sigmoid(g)
    u = lax.conv_general_dilated(
        u, dw_w[:, None, :].astype(u.dtype), window_strides=(1,),
        padding=[(CONV_WIDTH - 1, 0)], dimension_numbers=("NWC", "WIO", "NWC"),
        feature_group_count=D_CONV) + dw_b
    uf = u.astype(jnp.float32)
    mu = jnp.mean(uf, axis=-1, keepdims=True)
    var = jnp.mean(jnp.square(uf - mu), axis=-1, keepdims=True)
    un = (uf - mu) * lax.rsqrt(var + LN_EPS) * ln_w.astype(jnp.float32) + ln_b.astype(jnp.float32)
    return jax.nn.silu(un).astype(a.dtype)


def diff_attention(q, k, v, lam_q1, lam_k1, lam_q2, lam_k2, subln_w, lambda_init):
    B, S = q.shape[0], q.shape[1]
    H, d = N_ATTN_HEADS, ATTN_HEAD_DIM
    q = q.reshape(B, S, H, 2, d)
    k = k.reshape(B, S, H, 2, d)
    v = v.reshape(B, S, H, V_HEAD_DIM)
    f32 = jnp.float32
    lam = (jnp.exp(jnp.sum(lam_q1.astype(f32) * lam_k1.astype(f32)))
           - jnp.exp(jnp.sum(lam_q2.astype(f32) * lam_k2.astype(f32))) + lambda_init)
    scale = d ** -0.5
    n_blk = S // Q_BLOCK
    qb = q.reshape(B, n_blk, Q_BLOCK, H, 2, d).transpose(1, 0, 2, 3, 4, 5)
    k_pos = jnp.arange(S)
    neg = jnp.finfo(f32).min

    def block(args):
        qi, i = args
        s = jnp.einsum("bqhcd,bkhcd->bhcqk", qi, k, preferred_element_type=f32) * scale
        q_pos = i * Q_BLOCK + jnp.arange(Q_BLOCK)
        causal = q_pos[:, None] >= k_pos[None, :]
        p = jax.nn.softmax(jnp.where(causal, s, neg), axis=-1)
        diff = p[:, :, 0] - lam * p[:, :, 1]
        return jnp.einsum("bhqk,bkhe->bqhe", diff.astype(v.dtype), v)

    o = lax.map(block, (qb, jnp.arange(n_blk)))
    o = o.transpose(1, 0, 2, 3, 4).reshape(B, S, H, V_HEAD_DIM)
    o = rmsnorm(o, subln_w, SUBLN_EPS) * (1.0 - lambda_init)
    return o.reshape(B, S, D_ATTN)


def expert_dispatch(u, expert_id, gate, w1, w3, w2):
    T, D = u.shape
    A = expert_id.shape[0]
    n_blocks = (A + N_EXPERTS * (MOE_BLOCK - 1) + MOE_BLOCK - 1) // MOE_BLOCK
    P = n_blocks * MOE_BLOCK
    token_id = jnp.arange(A, dtype=jnp.int32) // TOP_K
    order = jnp.argsort(expert_id)
    e_sorted = expert_id[order]
    counts = jnp.zeros((N_EXPERTS,), jnp.int32).at[expert_id].add(1)
    padded = (counts + MOE_BLOCK - 1) // MOE_BLOCK * MOE_BLOCK
    starts = jnp.cumsum(counts) - counts
    pad_ends = jnp.cumsum(padded)
    pad_starts = pad_ends - padded
    dest = pad_starts[e_sorted] + jnp.arange(A, dtype=jnp.int32) - starts[e_sorted]
    row_tok = jnp.zeros((P,), jnp.int32).at[dest].set(token_id[order])
    row_gate = jnp.zeros((P,), gate.dtype).at[dest].set(gate[order])
    block_expert = jnp.minimum(
        jnp.searchsorted(pad_ends, jnp.arange(n_blocks, dtype=jnp.int32) * MOE_BLOCK, side="right"),
        N_EXPERTS - 1)
    xb = u[row_tok].reshape(n_blocks, MOE_BLOCK, D)

    def run(args):
        xi, e = args
        hdn = jax.nn.silu(xi @ w1[e]) * (xi @ w3[e])
        return hdn @ w2[e]

    yb = lax.map(run, (xb, block_expert)).reshape(P, D)
    return jnp.zeros((T, D), u.dtype).at[row_tok].add(yb * row_gate[:, None].astype(u.dtype))


def hier_moe(u, w_group, b_group, w_expert_gate, b_expert_gate, w1, w3, w2):
    B, S, D = u.shape
    T = B * S
    ut = u.reshape(T, D)
    f32 = jnp.float32
    g_logits = (ut @ w_group).astype(f32) + b_group.astype(f32)
    g_prob = jax.nn.softmax(g_logits, axis=-1)
    _, g_sel = lax.top_k(g_logits, 1)
    g_w = jnp.take_along_axis(g_prob, g_sel, axis=-1)
    e_logits = ((ut @ w_expert_gate).astype(f32).reshape(T, N_GROUPS, EXPERTS_PER_GROUP)
                + b_expert_gate.astype(f32))
    e_in_group = e_logits[jnp.arange(T), g_sel[:, 0]]
    top_v, top_i = lax.top_k(e_in_group, TOP_K)
    gate = g_w * jax.nn.softmax(top_v, axis=-1)
    expert_id = (g_sel * EXPERTS_PER_GROUP + top_i).astype(jnp.int32)
    y = expert_dispatch(ut, expert_id.reshape(-1), gate.reshape(-1), w1, w3, w2)
    return y.reshape(B, S, D)


def setup_inputs(seed: int = 0) -> dict:
    key = jax.random.key(seed)
    ks = jax.random.split(key, 24)
    f32 = jnp.float32
    L = DEPTH

    def nrm(k, shape, scale):
        return jax.random.normal(k, shape, f32) * scale

    return {
        "x": nrm(ks[0], (BATCH, SEQ, D_MODEL), 1.0),
        "mix_norm_w": 1.0 + nrm(ks[1], (L, D_MODEL), 0.01),
        "w_in": nrm(ks[2], (L, D_MODEL, D_IN_PROJ), D_MODEL ** -0.5),
        "conv_dw_w": nrm(ks[3], (L, CONV_WIDTH, D_CONV), CONV_WIDTH ** -0.5),
        "conv_dw_b": nrm(ks[4], (L, D_CONV), 0.01),
        "conv_ln_w": 1.0 + nrm(ks[5], (L, D_CONV), 0.01),
        "conv_ln_b": nrm(ks[6], (L, D_CONV), 0.01),
        "lam_q1": nrm(ks[7], (L, ATTN_HEAD_DIM), 0.1),
        "lam_k1": nrm(ks[8], (L, ATTN_HEAD_DIM), 0.1),
        "lam_q2": nrm(ks[9], (L, ATTN_HEAD_DIM), 0.1),
        "lam_k2": nrm(ks[10], (L, ATTN_HEAD_DIM), 0.1),
        "attn_subln_w": 1.0 + nrm(ks[11], (L, V_HEAD_DIM), 0.01),
        "w_out": nrm(ks[12], (L, D_MIX, D_MODEL), D_MIX ** -0.5),
        "ffn_norm_w": 1.0 + nrm(ks[13], (L, D_MODEL), 0.01),
        "w_group": nrm(ks[14], (L, D_MODEL, N_GROUPS), D_MODEL ** -0.5),
        "b_group": nrm(ks[15], (L, N_GROUPS), 0.01),
        "w_expert_gate": nrm(ks[16], (L, D_MODEL, N_EXPERTS), D_MODEL ** -0.5),
        "b_expert_gate": nrm(ks[17], (L, N_GROUPS, EXPERTS_PER_GROUP), 0.01),
        "w1": nrm(ks[18], (L, N_EXPERTS, D_MODEL, D_EXPERT), D_MODEL ** -0.5),
        "w3": nrm(ks[19], (L, N_EXPERTS, D_MODEL, D_EXPERT), D_MODEL ** -0.5),
        "w2": nrm(ks[20], (L, N_EXPERTS, D_EXPERT, D_MODEL), D_EXPERT ** -0.5),
        "final_norm_w": 1.0 + nrm(ks[21], (D_MODEL,), 0.01),
    }


def reference(x, mix_norm_w, w_in, conv_dw_w, conv_dw_b, conv_ln_w, conv_ln_b,
              lam_q1, lam_k1, lam_q2, lam_k2, attn_subln_w, w_out, ffn_norm_w,
              w_group, b_group, w_expert_gate, b_expert_gate, w1, w3, w2, final_norm_w):
    h = x
    splits = [D_CONV, 2 * D_CONV, 2 * D_CONV + D_ATTN, 2 * D_CONV + 2 * D_ATTN]
    for l in range(DEPTH):
        lambda_init = 0.8 - 0.6 * math.exp(-0.3 * l)
        u = rmsnorm(h, mix_norm_w[l])
        proj = jnp.einsum("bsd,de->bse", u, w_in[l])
        conv_a, conv_g, q, k, v = jnp.split(proj, splits, axis=-1)
        y_conv = conformer_conv(conv_a, conv_g, conv_dw_w[l], conv_dw_b[l],
                                conv_ln_w[l], conv_ln_b[l])
        y_attn = diff_attention(q, k, v, lam_q1[l], lam_k1[l], lam_q2[l], lam_k2[l],
                                attn_subln_w[l], lambda_init)
        y_mix = jnp.concatenate([y_conv, y_attn], axis=-1)
        h = h + jnp.einsum("bse,ed->bsd", y_mix, w_out[l])
        h = h + hier_moe(rmsnorm(h, ffn_norm_w[l]), w_group[l], b_group[l],
                         w_expert_gate[l], b_expert_gate[l], w1[l], w3[l], w2[l])
    return rmsnorm(h, final_norm_w)
```

```python
import functools
import math

import jax
import jax.numpy as jnp
from jax import lax
from jax.experimental import pallas as pl
from jax.experimental.pallas import tpu as pltpu

F32 = jnp.float32
BF16 = jnp.bfloat16

RMS_EPS = 1e-6
SUBLN_EPS = 1e-5
LN_EPS = 1e-5
TOP_K = 2
LANES = 128
NEG_BIG = -1e30
MIB = 1024 * 1024


def _cparams(sem, vmem_mib):
    return pltpu.CompilerParams(dimension_semantics=sem, vmem_limit_bytes=vmem_mib * MIB)


def _norm_inproj_kernel(x_ref, nw_ref, w_ref, o_ref, u_ref, *, eps, rc):
    @pl.when(pl.program_id(1) == 0)
    def _():
        def body(r, carry):
            rows = pl.ds(pl.multiple_of(r * rc, rc), rc)
            x = x_ref[rows, :]
            ms = jnp.mean(x * x, axis=-1, keepdims=True)
            u_ref[rows, :] = (x * lax.rsqrt(ms + eps) * nw_ref[...]).astype(u_ref.dtype)
            return carry
        lax.fori_loop(0, x_ref.shape[0] // rc, body, 0)

    o_ref[...] = jnp.dot(u_ref[...], w_ref[...], preferred_element_type=F32).astype(o_ref.dtype)


def _norm_inproj(x2, nw, w_bf):
    T, D = x2.shape
    N = w_bf.shape[1]
    tm = min(1024, T)
    tn = 1024
    return pl.pallas_call(
        functools.partial(_norm_inproj_kernel, eps=RMS_EPS, rc=128),
        grid=(T // tm, N // tn),
        in_specs=[pl.BlockSpec((tm, D), lambda i, j: (i, 0)),
                  pl.BlockSpec((1, D), lambda i, j: (0, 0)),
                  pl.BlockSpec((D, tn), lambda i, j: (0, j))],
        out_specs=pl.BlockSpec((tm, tn), lambda i, j: (i, j)),
        out_shape=jax.ShapeDtypeStruct((T, N), BF16),
        scratch_shapes=[pltpu.VMEM((tm, D), BF16)],
        compiler_params=_cparams(("arbitrary", "arbitrary"), 48),
        name="norm_inproj",
    )(x2, nw, w_bf)


def _conv_kernel(a_ref, g_ref, w_ref, b_ref, lw_ref, lb_ref, o_ref, ubuf, cbuf, *, ts, kw, halo, eps):
    s = pl.program_id(1)
    C = a_ref.shape[1]

    @pl.when(s == 0)
    def _():
        ubuf[0:halo, :] = jnp.zeros((halo, C), F32)

    @pl.when(s > 0)
    def _():
        ubuf[0:halo, :] = ubuf[ts:ts + halo, :]

    rg = 64
    for r0 in range(0, ts, rg):
        a = a_ref[r0:r0 + rg, :].astype(F32)
        g = g_ref[r0:r0 + rg, :].astype(F32)
        ubuf[halo + r0:halo + r0 + rg, :] = a * jax.nn.sigmoid(g)

    off = halo - (kw - 1)
    rc = 32
    for c in range(C // LANES):
        cs = slice(c * LANES, (c + 1) * LANES)
        for r0 in range(0, ts, rc):
            acc = jnp.broadcast_to(b_ref[0:1, cs], (rc, LANES))
            for k in range(kw):
                acc = acc + w_ref[k:k + 1, cs] * ubuf[r0 + off + k:r0 + off + k + rc, cs]
            cbuf[r0:r0 + rc, cs] = acc

    for r0 in range(0, ts, rg):
        cv = cbuf[r0:r0 + rg, :]
        mu = jnp.mean(cv, axis=-1, keepdims=True)
        d = cv - mu
        var = jnp.mean(d * d, axis=-1, keepdims=True)
        un = d * lax.rsqrt(var + eps) * lw_ref[...] + lb_ref[...]
        o_ref[r0:r0 + rg, :] = (un * jax.nn.sigmoid(un)).astype(o_ref.dtype)


def _conformer(proj, dw_w, dw_b, ln_w, ln_b, B, S):
    T = proj.shape[0]
    kw, C = dw_w.shape
    ts = min(256, S)
    halo = 32
    assert kw - 1 <= halo and S % ts == 0 and ts >= halo
    ns = S // ts
    wp = jnp.zeros((halo, C), F32).at[:kw].set(dw_w)
    return pl.pallas_call(
        functools.partial(_conv_kernel, ts=ts, kw=kw, halo=halo, eps=LN_EPS),
        grid=(B, ns),
        in_specs=[pl.BlockSpec((ts, C), lambda b, s: (b * ns + s, 0)),
                  pl.BlockSpec((ts, C), lambda b, s: (b * ns + s, 1)),
                  pl.BlockSpec((halo, C), lambda b, s: (0, 0)),
                  pl.BlockSpec((1, C), lambda b, s: (0, 0)),
                  pl.BlockSpec((1, C), lambda b, s: (0, 0)),
                  pl.BlockSpec((1, C), lambda b, s: (0, 0))],
        out_specs=pl.BlockSpec((ts, C), lambda b, s: (b * ns + s, 0)),
        out_shape=jax.ShapeDtypeStruct((T, C), BF16),
        scratch_shapes=[pltpu.VMEM((ts + halo, C), F32), pltpu.VMEM((ts, C), F32)],
        compiler_params=_cparams(("arbitrary", "arbitrary"), 32),
        name="conformer",
    )(proj, proj, wp, dw_b.reshape(1, C), ln_w.reshape(1, C), ln_b.reshape(1, C))


def _attn_kernel(lq1_ref, lk1_ref, lq2_ref, lk2_ref, q_ref, k_ref, v_ref, sw_ref, o_ref,
                 vx_ref, acc_ref, m_ref, *, tq, hd, lam_init, eps):
    i = pl.program_id(2)
    vd = v_ref.shape[1]

    @pl.when(i == 0)
    def _():
        vx_ref[:, 0:vd] = v_ref[...]
        vx_ref[:, vd:2 * vd] = jnp.ones((v_ref.shape[0], vd), vx_ref.dtype)

    lam = (jnp.exp(jnp.sum(lq1_ref[...] * lk1_ref[...], axis=-1, keepdims=True))
           - jnp.exp(jnp.sum(lq2_ref[...] * lk2_ref[...], axis=-1, keepdims=True)) + lam_init)

    q = q_ref[...]
    qs = q * jnp.asarray(hd ** -0.5, q.dtype)
    lane = lax.broadcasted_iota(jnp.int32, q.shape, 1)
    zero = jnp.zeros_like(qs)
    qq = jnp.concatenate([jnp.where(lane < hd, qs, zero), jnp.where(lane >= hd, qs, zero)], axis=0)

    m_ref[...] = jnp.full(m_ref.shape, NEG_BIG, F32)
    acc_ref[...] = jnp.zeros(acc_ref.shape, F32)

    def step(j, masked):
        rows = pl.ds(pl.multiple_of(j * tq, tq), tq)
        kb = k_ref[rows, :]
        sc = lax.dot_general(qq, kb, (((1,), (1,)), ((), ())), preferred_element_type=F32)
        if masked:
            r = lax.broadcasted_iota(jnp.int32, sc.shape, 0)
            c = lax.broadcasted_iota(jnp.int32, sc.shape, 1)
            rq = jnp.where(r >= tq, r - tq, r)
            sc = jnp.where(c <= rq, sc, NEG_BIG)
        m_old = m_ref[...]
        m_new = jnp.maximum(m_old, jnp.max(sc, axis=-1, keepdims=True))
        alpha = jnp.exp(m_old - m_new)
        p = jnp.exp(sc - m_new).astype(vx_ref.dtype)
        acc_ref[...] = alpha * acc_ref[...] + jnp.dot(p, vx_ref[rows, :], preferred_element_type=F32)
        m_ref[...] = m_new

    def body(j, carry):
        step(j, False)
        return carry

    lax.fori_loop(0, i, body, 0)
    step(i, True)

    acc = acc_ref[...]
    o12 = acc[:, 0:vd] / acc[:, vd:2 * vd]
    o = o12[0:tq] - lam * o12[tq:2 * tq]
    ms = jnp.mean(o * o, axis=-1, keepdims=True)
    y = o * lax.rsqrt(ms + eps) * sw_ref[...] * (1.0 - lam_init)
    o_ref[...] = y.astype(o_ref.dtype)


def _diff_attn(proj, lq1, lk1, lq2, lk2, subln_w, B, S, d_conv, d_attn, lam_init):
    T = proj.shape[0]
    vd = subln_w.shape[0]
    hd = lq1.shape[0]
    assert vd == LANES and 2 * hd == vd
    H = d_attn // vd
    tq = min(256, S)
    nq = S // tq
    qc = 2 * d_conv // vd
    kc = qc + H
    vc = kc + H
    lspec = pl.BlockSpec((1, hd), lambda b, h, i: (0, 0))
    return pl.pallas_call(
        functools.partial(_attn_kernel, tq=tq, hd=hd, lam_init=lam_init, eps=SUBLN_EPS),
        grid=(B, H, nq),
        in_specs=[lspec, lspec, lspec, lspec,
                  pl.BlockSpec((tq, vd), lambda b, h, i: (b * nq + i, qc + h)),
                  pl.BlockSpec((S, vd), lambda b, h, i: (b, kc + h)),
                  pl.BlockSpec((S, vd), lambda b, h, i: (b, vc + h)),
                  pl.BlockSpec((1, vd), lambda b, h, i: (0, 0))],
        out_specs=pl.BlockSpec((tq, vd), lambda b, h, i: (b * nq + i, h)),
        out_shape=jax.ShapeDtypeStruct((T, d_attn), BF16),
        scratch_shapes=[pltpu.VMEM((S, 2 * vd), BF16),
                        pltpu.VMEM((2 * tq, 2 * vd), F32),
                        pltpu.VMEM((2 * tq, 1), F32)],
        compiler_params=_cparams(("arbitrary", "arbitrary", "arbitrary"), 32),
        name="diff_attn",
    )(lq1.reshape(1, hd), lk1.reshape(1, hd), lq2.reshape(1, hd), lk2.reshape(1, hd),
      proj, proj, proj, subln_w.reshape(1, vd))


def _outproj_kernel(x_ref, yc_ref, ya_ref, wo_ref, fw_ref, wrh_ref, wrl_ref, br_ref, h_ref, lg_ref, *, eps):
    dc = yc_ref.shape[1]
    mix = (jnp.dot(yc_ref[...], wo_ref[0:dc, :], preferred_element_type=F32)
           + jnp.dot(ya_ref[...], wo_ref[dc:, :], preferred_element_type=F32))
    h = x_ref[...] + mix
    h_ref[...] = h
    ms = jnp.mean(h * h, axis=-1, keepdims=True)
    un = h * lax.rsqrt(ms + eps) * fw_ref[...]
    hi = un.astype(BF16)
    lo = (un - hi.astype(F32)).astype(BF16)
    lg = (jnp.dot(hi, wrh_ref[...], preferred_element_type=F32)
          + jnp.dot(lo, wrh_ref[...], preferred_element_type=F32)
          + jnp.dot(hi, wrl_ref[...], preferred_element_type=F32))
    lg_ref[...] = lg + br_ref[...]


def _outproj(x2, y_conv, y_attn, wo_bf, ffn_w, wr_hi, wr_lo, br):
    T, D = x2.shape
    dc = y_conv.shape[1]
    da = y_attn.shape[1]
    tm = min(256, T)
    return pl.pallas_call(
        functools.partial(_outproj_kernel, eps=RMS_EPS),
        grid=(T // tm,),
        in_specs=[pl.BlockSpec((tm, D), lambda i: (i, 0)),
                  pl.BlockSpec((tm, dc), lambda i: (i, 0)),
                  pl.BlockSpec((tm, da), lambda i: (i, 0)),
                  pl.BlockSpec((dc + da, D), lambda i: (0, 0)),
                  pl.BlockSpec((1, D), lambda i: (0, 0)),
                  pl.BlockSpec((D, LANES), lambda i: (0, 0)),
                  pl.BlockSpec((D, LANES), lambda i: (0, 0)),
                  pl.BlockSpec((1, LANES), lambda i: (0, 0))],
        out_specs=[pl.BlockSpec((tm, D), lambda i: (i, 0)),
                   pl.BlockSpec((tm, LANES), lambda i: (i, 0))],
        out_shape=[jax.ShapeDtypeStruct((T, D), F32), jax.ShapeDtypeStruct((T, LANES), F32)],
        compiler_params=_cparams(("arbitrary",), 48),
        name="outproj",
    )(x2, y_conv, y_attn, wo_bf, ffn_w, wr_hi, wr_lo, br)


def _route_kernel(lg_ref, info_ref, cnt_ref, *, ng, epg):
    i = pl.program_id(0)
    lg = lg_ref[...]
    tm = lg.shape[0]
    lane = lax.broadcasted_iota(jnp.int32, lg.shape, 1)
    lanef = lane.astype(F32)
    ne = ng * epg

    def first_argmax(vals):
        mx = jnp.max(vals, axis=-1, keepdims=True)
        idx = jnp.min(jnp.where(vals == mx, lanef, float(LANES)), axis=-1, keepdims=True)
        return mx, idx

    gmask = lane < ng
    gl = jnp.where(gmask, lg, NEG_BIG)
    gmax, gsel = first_argmax(gl)
    gsum = jnp.sum(jnp.where(gmask, jnp.exp(gl - gmax), 0.0), axis=-1, keepdims=True)
    g_w = 1.0 / gsum
    lo = gsel * epg + ng
    emask = (lanef >= lo) & (lanef < lo + epg)
    el = jnp.where(emask, lg, NEG_BIG)
    v1, i1 = first_argmax(el)
    el2 = jnp.where(lanef == i1, NEG_BIG, el)
    v2, i2 = first_argmax(el2)
    e2 = jnp.exp(v2 - v1)
    p1 = 1.0 / (1.0 + e2)
    gate1 = g_w * p1
    gate2 = g_w * (e2 * p1)

    oh1 = lanef == i1
    oh2 = lanef == i2
    cmat = jnp.where(oh1 | oh2, 1.0, 0.0).astype(BF16)
    r = lax.broadcasted_iota(jnp.int32, (tm, tm), 0)
    c = lax.broadcasted_iota(jnp.int32, (tm, tm), 1)
    tri = jnp.where(c < r, 1.0, 0.0).astype(BF16)

    @pl.when(i == 0)
    def _():
        cnt_ref[...] = jnp.zeros(cnt_ref.shape, F32)

    carry = cnt_ref[0:1, :]
    prefix = jnp.dot(tri, cmat, preferred_element_type=F32) + carry
    rank1 = jnp.sum(jnp.where(oh1, prefix, 0.0), axis=-1, keepdims=True)
    rank2 = jnp.sum(jnp.where(oh2, prefix, 0.0), axis=-1, keepdims=True)
    cnt_ref[...] = jnp.broadcast_to(carry + jnp.sum(cmat.astype(F32), axis=0, keepdims=True), cnt_ref.shape)

    info = jnp.where(lane == 0, i1 - ng,
           jnp.where(lane == 1, i2 - ng,
           jnp.where(lane == 2, gate1,
           jnp.where(lane == 3, gate2,
           jnp.where(lane == 4, rank1,
           jnp.where(lane == 5, rank2, 0.0))))))
    info_ref[...] = info


def _route(logits, ng, epg):
    T = logits.shape[0]
    tm = min(512, T)
    return pl.pallas_call(
        functools.partial(_route_kernel, ng=ng, epg=epg),
        grid=(T // tm,),
        in_specs=[pl.BlockSpec((tm, LANES), lambda i: (i, 0))],
        out_specs=[pl.BlockSpec((tm, LANES), lambda i: (i, 0)),
                   pl.BlockSpec((8, LANES), lambda i: (0, 0))],
        out_shape=[jax.ShapeDtypeStruct((T, LANES), F32), jax.ShapeDtypeStruct((8, LANES), F32)],
        compiler_params=_cparams(("arbitrary",), 32),
        name="route",
    )(logits)


def _row_gather(src_hbm, idx_ref, base, dst, sem, n):
    def body(r, carry):
        pltpu.make_async_copy(src_hbm.at[pl.ds(idx_ref[base + r], 1)], dst.at[pl.ds(r, 1)], sem).start()
        return carry
    lax.fori_loop(0, n, body, 0, unroll=8)


def _row_gather_wait(src_hbm, dst, sem, n):
    pltpu.make_async_copy(src_hbm.at[pl.ds(0, n)], dst, sem).wait()


def _experts_kernel(bexp_ref, nused_ref, rtok_ref, h_hbm, fw_ref, w1_ref, w3_ref, w2_ref, y_ref,
                    xbuf, sems, w1b, w3b, w2b, *, blk, eps):
    i = pl.program_id(0)
    nused = nused_ref[0]
    slot = i % 2

    @pl.when(i == 0)
    def _():
        _row_gather(h_hbm, rtok_ref, 0, xbuf.at[0], sems.at[0], blk)

    @pl.when(i + 1 < nused)
    def _():
        _row_gather(h_hbm, rtok_ref, (i + 1) * blk, xbuf.at[1 - slot], sems.at[1 - slot], blk)

    @pl.when(i < nused)
    def _():
        changed = jnp.logical_or(i == 0, bexp_ref[i] != bexp_ref[jnp.maximum(i - 1, 0)])

        @pl.when(changed)
        def _():
            w1b[...] = w1_ref[0].astype(BF16)
            w3b[...] = w3_ref[0].astype(BF16)
            w2b[...] = w2_ref[0].astype(BF16)

        _row_gather_wait(h_hbm, xbuf.at[slot], sems.at[slot], blk)
        hrow = xbuf[slot]
        ms = jnp.mean(hrow * hrow, axis=-1, keepdims=True)
        xn = (hrow * lax.rsqrt(ms + eps) * fw_ref[...]).astype(BF16)
        a = jnp.dot(xn, w1b[...], preferred_element_type=F32)
        b = jnp.dot(xn, w3b[...], preferred_element_type=F32)
        hdn = (a * jax.nn.sigmoid(a) * b).astype(BF16)
        y_ref[...] = jnp.dot(hdn, w2b[...], preferred_element_type=F32)

    @pl.when(i >= nused)
    def _():
        y_ref[...] = jnp.zeros(y_ref.shape, F32)


def _experts(bexp, nused, row_tok, h, ffn_w, w1, w3, w2, blk):
    T, D = h.shape
    E, _, Fh = w1.shape
    nblk = bexp.shape[0]
    grid_spec = pltpu.PrefetchScalarGridSpec(
        num_scalar_prefetch=3,
        grid=(nblk,),
        in_specs=[pl.BlockSpec(memory_space=pl.ANY),
                  pl.BlockSpec((1, D), lambda i, be, nu, rt: (0, 0)),
                  pl.BlockSpec((1, D, Fh), lambda i, be, nu, rt: (be[i], 0, 0)),
                  pl.BlockSpec((1, D, Fh), lambda i, be, nu, rt: (be[i], 0, 0)),
                  pl.BlockSpec((1, Fh, D), lambda i, be, nu, rt: (be[i], 0, 0))],
        out_specs=pl.BlockSpec((blk, D), lambda i, be, nu, rt: (i, 0)),
        scratch_shapes=[pltpu.VMEM((2, blk, D), F32),
                        pltpu.SemaphoreType.DMA((2,)),
                        pltpu.VMEM((D, Fh), BF16),
                        pltpu.VMEM((D, Fh), BF16),
                        pltpu.VMEM((Fh, D), BF16)],
    )
    return pl.pallas_call(
        functools.partial(_experts_kernel, blk=blk, eps=RMS_EPS),
        grid_spec=grid_spec,
        out_shape=jax.ShapeDtypeStruct((nblk * blk, D), F32),
        compiler_params=_cparams(("arbitrary",), 56),
        name="experts",
    )(bexp, nused, row_tok, h, ffn_w, w1, w3, w2)


def _combine_kernel(dest_ref, h_ref, info_ref, y_hbm, fw_ref, o_ref, ybuf, sems, *, tm, eps):
    i = pl.program_id(0)
    n = pl.num_programs(0)
    slot = i % 2
    nrow = TOP_K * tm

    @pl.when(i == 0)
    def _():
        _row_gather(y_hbm, dest_ref, 0, ybuf.at[0], sems.at[0], nrow)

    @pl.when(i + 1 < n)
    def _():
        _row_gather(y_hbm, dest_ref, (i + 1) * nrow, ybuf.at[1 - slot], sems.at[1 - slot], nrow)

    _row_gather_wait(y_hbm, ybuf.at[slot], sems.at[slot], nrow)
    info = info_ref[...]
    g1 = info[:, 2:3]
    g2 = info[:, 3:4]
    yb = ybuf[slot]
    hn = h_ref[...] + (g1 * yb[0:tm] + g2 * yb[tm:nrow])
    ms = jnp.mean(hn * hn, axis=-1, keepdims=True)
    o_ref[...] = hn * lax.rsqrt(ms + eps) * fw_ref[...]


def _combine(dest_km, h, info, yb, final_w, tm):
    T, D = h.shape
    grid_spec = pltpu.PrefetchScalarGridSpec(
        num_scalar_prefetch=1,
        grid=(T // tm,),
        in_specs=[pl.BlockSpec((tm, D), lambda i, d: (i, 0)),
                  pl.BlockSpec((tm, LANES), lambda i, d: (i, 0)),
                  pl.BlockSpec(memory_space=pl.ANY),
                  pl.BlockSpec((1, D), lambda i, d: (0, 0))],
        out_specs=pl.BlockSpec((tm, D), lambda i, d: (i, 0)),
        scratch_shapes=[pltpu.VMEM((2, TOP_K * tm, D), F32),
                        pltpu.SemaphoreType.DMA((2,))],
    )
    return pl.pallas_call(
        functools.partial(_combine_kernel, tm=tm, eps=RMS_EPS),
        grid_spec=grid_spec,
        out_shape=jax.ShapeDtypeStruct((T, D), F32),
        compiler_params=_cparams(("arbitrary",), 48),
        name="combine",
    )(dest_km, h, info, yb, final_w)


def _layer(h_in, l, B, S, mix_norm_w, w_in, conv_dw_w, conv_dw_b, conv_ln_w, conv_ln_b,
           lam_q1, lam_k1, lam_q2, lam_k2, attn_subln_w, w_out, ffn_norm_w,
           w_group, b_group, w_expert_gate, b_expert_gate, w1, w3, w2):
    T, D = h_in.shape
    d_conv = conv_dw_w.shape[1]
    d_attn = (w_in.shape[1] - 2 * d_conv) // 3
    ng = w_group.shape[1]
    ne = w_expert_gate.shape[1]
    epg = ne // ng
    assert ng + ne <= LANES
    lam_init = 0.8 - 0.6 * math.exp(-0.3 * l)

    proj = _norm_inproj(h_in, mix_norm_w.reshape(1, D), w_in.astype(BF16))
    y_conv = _conformer(proj, conv_dw_w, conv_dw_b, conv_ln_w, conv_ln_b, B, S)
    y_attn = _diff_attn(proj, lam_q1, lam_k1, lam_q2, lam_k2, attn_subln_w, B, S, d_conv, d_attn, lam_init)

    wr = jnp.zeros((D, LANES), F32).at[:, :ng].set(w_group).at[:, ng:ng + ne].set(w_expert_gate)
    wr_hi = wr.astype(BF16)
    wr_lo = (wr - wr_hi.astype(F32)).astype(BF16)
    br = jnp.zeros((1, LANES), F32).at[0, :ng].set(b_group).at[0, ng:ng + ne].set(b_expert_gate.reshape(-1))
    h, logits = _outproj(h_in, y_conv, y_attn, w_out.astype(BF16), ffn_norm_w.reshape(1, D), wr_hi, wr_lo, br)

    info, cnt = _route(logits, ng, epg)

    blk = 256
    A = T * TOP_K
    nblk = (A + ne * (blk - 1) + blk - 1) // blk
    counts = cnt[0, ng:ng + ne].astype(jnp.int32)
    padded = (counts + blk - 1) // blk * blk
    pad_ends = jnp.cumsum(padded)
    pad_starts = pad_ends - padded
    eid = info[:, 0:TOP_K].astype(jnp.int32)
    rank = info[:, 4:4 + TOP_K].astype(jnp.int32)
    dest = pad_starts[eid] + rank
    tok = jnp.broadcast_to(jnp.arange(T, dtype=jnp.int32)[:, None], (T, TOP_K))
    row_tok = jnp.zeros((nblk * blk,), jnp.int32).at[dest.reshape(-1)].set(tok.reshape(-1))
    nused = (pad_ends[-1] // blk).astype(jnp.int32)
    bidx = jnp.minimum(jnp.arange(nblk, dtype=jnp.int32), nused - 1)
    bexp = jnp.minimum(jnp.searchsorted(pad_ends, bidx * blk, side="right"), ne - 1).astype(jnp.int32)

    yb = _experts(bexp, nused.reshape(1), row_tok, h, ffn_norm_w.reshape(1, D), w1, w3, w2, blk)
    return h, info, dest, yb


def kernel(x, mix_norm_w, w_in, conv_dw_w, conv_dw_b, conv_ln_w, conv_ln_b, lam_q1, lam_k1, lam_q2, lam_k2,
           attn_subln_w, w_out, ffn_norm_w, w_group, b_group, w_expert_gate, b_expert_gate, w1, w3, w2,
           final_norm_w):
    B, S, D = x.shape
    depth = w_in.shape[0]
    assert depth == 1
    T = B * S
    tmc = min(256, T)
    h = x.reshape(T, D)
    for l in range(depth):
        h, info, dest, yb = _layer(
            h, l, B, S, mix_norm_w[l], w_in[l], conv_dw_w[l], conv_dw_b[l], conv_ln_w[l], conv_ln_b[l],
            lam_q1[l], lam_k1[l], lam_q2[l], lam_k2[l], attn_subln_w[l], w_out[l], ffn_norm_w[l],
            w_group[l], b_group[l], w_expert_gate[l], b_expert_gate[l], w1[l], w3[l], w2[l])
        dest_km = dest.reshape(T // tmc, tmc, TOP_K).transpose(0, 2, 1).reshape(-1)
        h = _combine(dest_km, h, info, yb, final_norm_w.reshape(1, D), tmc)
    return h.reshape(B, S, D)
```

```python
import functools
import math

import jax
import jax.numpy as jnp
from jax import lax
from jax.experimental import pallas as pl
from jax.experimental.pallas import tpu as pltpu

F32 = jnp.float32
BF16 = jnp.bfloat16

RMS_EPS = 1e-6
SUBLN_EPS = 1e-5
LN_EPS = 1e-5
TOP_K = 2
LANES = 128
NEG_BIG = -1e30
MIB = 1024 * 1024


def _cparams(sem, vmem_mib):
    return pltpu.CompilerParams(dimension_semantics=sem, vmem_limit_bytes=vmem_mib * MIB)


def _norm_inproj_kernel(x_ref, nw_ref, w_ref, o_ref, u_ref, *, eps, rc):
    @pl.when(pl.program_id(1) == 0)
    def _():
        def body(r, carry):
            rows = pl.ds(pl.multiple_of(r * rc, rc), rc)
            x = x_ref[rows, :]
            ms = jnp.mean(x * x, axis=-1, keepdims=True)
            u_ref[rows, :] = (x * lax.rsqrt(ms + eps) * nw_ref[...]).astype(u_ref.dtype)
            return carry
        lax.fori_loop(0, x_ref.shape[0] // rc, body, 0)

    o_ref[...] = jnp.dot(u_ref[...], w_ref[...], preferred_element_type=F32).astype(o_ref.dtype)


def _norm_inproj(x2, nw, w_bf):
    T, D = x2.shape
    N = w_bf.shape[1]
    tm = min(1024, T)
    tn = 1024
    return pl.pallas_call(
        functools.partial(_norm_inproj_kernel, eps=RMS_EPS, rc=128),
        grid=(T // tm, N // tn),
        in_specs=[pl.BlockSpec((tm, D), lambda i, j: (i, 0)),
                  pl.BlockSpec((1, D), lambda i, j: (0, 0)),
                  pl.BlockSpec((D, tn), lambda i, j: (0, j))],
        out_specs=pl.BlockSpec((tm, tn), lambda i, j: (i, j)),
        out_shape=jax.ShapeDtypeStruct((T, N), BF16),
        scratch_shapes=[pltpu.VMEM((tm, D), BF16)],
        compiler_params=_cparams(("arbitrary", "arbitrary"), 48),
        name="norm_inproj",
    )(x2, nw, w_bf)


def _conv_kernel(a_ref, g_ref, w_ref, b_ref, lw_ref, lb_ref, o_ref, ubuf, cbuf, *, ts, kw, halo, eps):
    s = pl.program_id(1)
    C = a_ref.shape[1]

    @pl.when(s == 0)
    def _():
        ubuf[0:halo, :] = jnp.zeros((halo, C), F32)

    @pl.when(s > 0)
    def _():
        ubuf[0:halo, :] = ubuf[ts:ts + halo, :]

    rg = 64
    for r0 in range(0, ts, rg):
        a = a_ref[r0:r0 + rg, :].astype(F32)
        g = g_ref[r0:r0 + rg, :].astype(F32)
        ubuf[halo + r0:halo + r0 + rg, :] = a * jax.nn.sigmoid(g)

    off = halo - (kw - 1)
    rc = 32
    for c in range(C // LANES):
        cs = slice(c * LANES, (c + 1) * LANES)
        for r0 in range(0, ts, rc):
            acc = jnp.broadcast_to(b_ref[0:1, cs], (rc, LANES))
            for k in range(kw):
                acc = acc + w_ref[k:k + 1, cs] * ubuf[r0 + off + k:r0 + off + k + rc, cs]
            cbuf[r0:r0 + rc, cs] = acc

    for r0 in range(0, ts, rg):
        cv = cbuf[r0:r0 + rg, :]
        mu = jnp.mean(cv, axis=-1, keepdims=True)
        d = cv - mu
        var = jnp.mean(d * d, axis=-1, keepdims=True)
        un = d * lax.rsqrt(var + eps) * lw_ref[...] + lb_ref[...]
        o_ref[r0:r0 + rg, :] = (un * jax.nn.sigmoid(un)).astype(o_ref.dtype)


def _conformer(proj, dw_w, dw_b, ln_w, ln_b, B, S):
    T = proj.shape[0]
    kw, C = dw_w.shape
    ts = min(256, S)
    halo = 32
    assert kw - 1 <= halo and S % ts == 0 and ts >= halo
    ns = S // ts
    wp = jnp.zeros((halo, C), F32).at[:kw].set(dw_w)
    return pl.pallas_call(
        functools.partial(_conv_kernel, ts=ts, kw=kw, halo=halo, eps=LN_EPS),
        grid=(B, ns),
        in_specs=[pl.BlockSpec((ts, C), lambda b, s: (b * ns + s, 0)),
                  pl.BlockSpec((ts, C), lambda b, s: (b * ns + s, 1)),
                  pl.BlockSpec((halo, C), lambda b, s: (0, 0)),
                  pl.BlockSpec((1, C), lambda b, s: (0, 0)),
                  pl.BlockSpec((1, C), lambda b, s: (0, 0)),
                  pl.BlockSpec((1, C), lambda b, s: (0, 0))],
        out_specs=pl.BlockSpec((ts, C), lambda b, s: (b * ns + s, 0)),
        out_shape=jax.ShapeDtypeStruct((T, C), BF16),
        scratch_shapes=[pltpu.VMEM((ts + halo, C), F32), pltpu.VMEM((ts, C), F32)],
        compiler_params=_cparams(("arbitrary", "arbitrary"), 32),
        name="conformer",
    )(proj, proj, wp, dw_b.reshape(1, C), ln_w.reshape(1, C), ln_b.reshape(1, C))


def _attn_kernel(lq1_ref, lk1_ref, lq2_ref, lk2_ref, q_ref, k_ref, v_ref, sw_ref, o_ref,
                 vxt_ref, qq_ref, *scr, tq, hd, hp, ones_rows, lam_init, eps):
    i = pl.program_id(2)
    vd = 2 * hd
    nkb = v_ref.shape[0] // tq
    acc_refs, m_refs, sa_refs, sb_refs = (scr[n * hp:(n + 1) * hp] for n in range(4))

    @pl.when(i == 0)
    def _():
        for u in range(hp):
            for jb in range(nkb):
                vblk = v_ref[jb * tq:(jb + 1) * tq, u * vd:(u + 1) * vd].astype(F32)
                vxt_ref[u, jb, 0:vd, :] = vblk.T.astype(vxt_ref.dtype)
                vxt_ref[u, jb, vd:vd + ones_rows, :] = jnp.ones((ones_rows, tq), vxt_ref.dtype)

    lam = (jnp.exp(jnp.sum(lq1_ref[...] * lk1_ref[...], axis=-1, keepdims=True))
           - jnp.exp(jnp.sum(lq2_ref[...] * lk2_ref[...], axis=-1, keepdims=True)) + lam_init)

    for u in range(hp):
        q = q_ref[:, u * vd:(u + 1) * vd]
        qs = q * jnp.asarray(hd ** -0.5, q.dtype)
        lane = lax.broadcasted_iota(jnp.int32, q.shape, 1)
        zero = jnp.zeros_like(qs)
        qq_ref[u, 0:tq, :] = jnp.where(lane < hd, qs, zero)
        qq_ref[u, tq:2 * tq, :] = jnp.where(lane >= hd, qs, zero)
        acc_refs[u][...] = jnp.zeros(acc_refs[u].shape, F32)
        m_refs[u][...] = jnp.full(m_refs[u].shape, NEG_BIG, F32)

    def scores(j, u, dst):
        rows = pl.ds(pl.multiple_of(j * tq, tq), tq)
        kb = k_ref[rows, u * vd:(u + 1) * vd]
        dst[...] = lax.dot_general(kb, qq_ref[u], (((1,), (1,)), ((), ())), preferred_element_type=F32)

    def softmax_pv(j, u, src, masked):
        st = src[...]
        if masked:
            r = lax.broadcasted_iota(jnp.int32, st.shape, 0)
            c = lax.broadcasted_iota(jnp.int32, st.shape, 1)
            st = jnp.where(r <= jnp.where(c >= tq, c - tq, c), st, NEG_BIG)
        m_old = m_refs[u][...]
        m_new = jnp.maximum(m_old, jnp.max(st, axis=0, keepdims=True))
        alpha = jnp.exp(m_old - m_new)
        pt = jnp.exp(st - m_new).astype(vxt_ref.dtype)
        pv = jnp.dot(vxt_ref[u, j], pt, preferred_element_type=F32)
        acc_refs[u][...] = alpha * acc_refs[u][...] + pv
        m_refs[u][...] = m_new

    def half(j, cur, nxt):
        for u in range(hp):
            scores(j + 1, u, nxt[u])
            softmax_pv(j, u, cur[u], False)

    for u in range(hp):
        scores(0, u, sa_refs[u])

    def pair(t, carry):
        half(2 * t, sa_refs, sb_refs)
        half(2 * t + 1, sb_refs, sa_refs)
        return carry

    lax.fori_loop(0, i // 2, pair, 0)

    @pl.when(i % 2 == 1)
    def _():
        half(i - 1, sa_refs, sb_refs)
        for u in range(hp):
            softmax_pv(i, u, sb_refs[u], True)

    @pl.when(i % 2 == 0)
    def _():
        for u in range(hp):
            softmax_pv(i, u, sa_refs[u], True)

    for u in range(hp):
        acc = acc_refs[u][...]
        o12 = acc[0:vd] * (1.0 / acc[vd:vd + 1])
        ot = o12[:, 0:tq] - lam * o12[:, tq:2 * tq]
        msq = jnp.mean(ot * ot, axis=0, keepdims=True)
        o = (ot * lax.rsqrt(msq + eps)).T
        o_ref[:, u * vd:(u + 1) * vd] = (o * sw_ref[...] * (1.0 - lam_init)).astype(o_ref.dtype)


def _diff_attn(proj, lq1, lk1, lq2, lk2, subln_w, B, S, d_conv, d_attn, lam_init):
    T = proj.shape[0]
    vd = subln_w.shape[0]
    hd = lq1.shape[0]
    assert vd == LANES and 2 * hd == vd
    H = d_attn // vd
    hp = 4
    ones_rows = 16
    tq = min(256, S)
    nq = S // tq
    assert H % hp == 0
    qc = 2 * d_conv // (hp * vd)
    kc = qc + H // hp
    vc = kc + H // hp
    lspec = pl.BlockSpec((1, hd), lambda b, h, i: (0, 0))
    return pl.pallas_call(
        functools.partial(_attn_kernel, tq=tq, hd=hd, hp=hp, ones_rows=ones_rows, lam_init=lam_init, eps=SUBLN_EPS),
        grid=(B, H // hp, nq),
        in_specs=[lspec, lspec, lspec, lspec,
                  pl.BlockSpec((tq, hp * vd), lambda b, h, i: (b * nq + i, qc + h)),
                  pl.BlockSpec((S, hp * vd), lambda b, h, i: (b, kc + h)),
                  pl.BlockSpec((S, hp * vd), lambda b, h, i: (b, vc + h)),
                  pl.BlockSpec((1, vd), lambda b, h, i: (0, 0))],
        out_specs=pl.BlockSpec((tq, hp * vd), lambda b, h, i: (b * nq + i, h)),
        out_shape=jax.ShapeDtypeStruct((T, d_attn), BF16),
        scratch_shapes=[pltpu.VMEM((hp, S // tq, vd + ones_rows, tq), BF16),
                        pltpu.VMEM((hp, 2 * tq, vd), BF16)]
                       + [pltpu.VMEM((vd + ones_rows, 2 * tq), F32) for _ in range(hp)]
                       + [pltpu.VMEM((1, 2 * tq), F32) for _ in range(hp)]
                       + [pltpu.VMEM((tq, 2 * tq), F32) for _ in range(2 * hp)],
        compiler_params=_cparams(("arbitrary", "arbitrary", "arbitrary"), 40),
        name="diff_attn",
    )(lq1.reshape(1, hd), lk1.reshape(1, hd), lq2.reshape(1, hd), lk2.reshape(1, hd),
      proj, proj, proj, subln_w.reshape(1, vd))


def _outproj_kernel(x_ref, yc_ref, ya_ref, wo_ref, fw_ref, wrh_ref, wrl_ref, br_ref, h_ref, lg_ref, *, eps):
    dc = yc_ref.shape[1]
    mix = (jnp.dot(yc_ref[...], wo_ref[0:dc, :], preferred_element_type=F32)
           + jnp.dot(ya_ref[...], wo_ref[dc:, :], preferred_element_type=F32))
    h = x_ref[...] + mix
    h_ref[...] = h
    ms = jnp.mean(h * h, axis=-1, keepdims=True)
    un = h * lax.rsqrt(ms + eps) * fw_ref[...]
    hi = un.astype(BF16)
    lo = (un - hi.astype(F32)).astype(BF16)
    lg = (jnp.dot(hi, wrh_ref[...], preferred_element_type=F32)
          + jnp.dot(lo, wrh_ref[...], preferred_element_type=F32)
          + jnp.dot(hi, wrl_ref[...], preferred_element_type=F32))
    lg_ref[...] = lg + br_ref[...]


def _outproj(x2, y_conv, y_attn, wo_bf, ffn_w, wr_hi, wr_lo, br):
    T, D = x2.shape
    dc = y_conv.shape[1]
    da = y_attn.shape[1]
    tm = min(256, T)
    return pl.pallas_call(
        functools.partial(_outproj_kernel, eps=RMS_EPS),
        grid=(T // tm,),
        in_specs=[pl.BlockSpec((tm, D), lambda i: (i, 0)),
                  pl.BlockSpec((tm, dc), lambda i: (i, 0)),
                  pl.BlockSpec((tm, da), lambda i: (i, 0)),
                  pl.BlockSpec((dc + da, D), lambda i: (0, 0)),
                  pl.BlockSpec((1, D), lambda i: (0, 0)),
                  pl.BlockSpec((D, LANES), lambda i: (0, 0)),
                  pl.BlockSpec((D, LANES), lambda i: (0, 0)),
                  pl.BlockSpec((1, LANES), lambda i: (0, 0))],
        out_specs=[pl.BlockSpec((tm, D), lambda i: (i, 0)),
                   pl.BlockSpec((tm, LANES), lambda i: (i, 0))],
        out_shape=[jax.ShapeDtypeStruct((T, D), F32), jax.ShapeDtypeStruct((T, LANES), F32)],
        compiler_params=_cparams(("arbitrary",), 48),
        name="outproj",
    )(x2, y_conv, y_attn, wo_bf, ffn_w, wr_hi, wr_lo, br)


def _route_kernel(lg_ref, info_ref, cnt_ref, *, ng, epg):
    i = pl.program_id(0)
    lg = lg_ref[...]
    tm = lg.shape[0]
    lane = lax.broadcasted_iota(jnp.int32, lg.shape, 1)
    lanef = lane.astype(F32)
    ne = ng * epg

    def first_argmax(vals):
        mx = jnp.max(vals, axis=-1, keepdims=True)
        idx = jnp.min(jnp.where(vals == mx, lanef, float(LANES)), axis=-1, keepdims=True)
        return mx, idx

    gmask = lane < ng
    gl = jnp.where(gmask, lg, NEG_BIG)
    gmax, gsel = first_argmax(gl)
    gsum = jnp.sum(jnp.where(gmask, jnp.exp(gl - gmax), 0.0), axis=-1, keepdims=True)
    g_w = 1.0 / gsum
    lo = gsel * epg + ng
    emask = (lanef >= lo) & (lanef < lo + epg)
    el = jnp.where(emask, lg, NEG_BIG)
    v1, i1 = first_argmax(el)
    el2 = jnp.where(lanef == i1, NEG_BIG, el)
    v2, i2 = first_argmax(el2)
    e2 = jnp.exp(v2 - v1)
    p1 = 1.0 / (1.0 + e2)
    gate1 = g_w * p1
    gate2 = g_w * (e2 * p1)

    oh1 = lanef == i1
    oh2 = lanef == i2
    cmat = jnp.where(oh1 | oh2, 1.0, 0.0).astype(BF16)
    r = lax.broadcasted_iota(jnp.int32, (tm, tm), 0)
    c = lax.broadcasted_iota(jnp.int32, (tm, tm), 1)
    tri = jnp.where(c < r, 1.0, 0.0).astype(BF16)

    @pl.when(i == 0)
    def _():
        cnt_ref[...] = jnp.zeros(cnt_ref.shape, F32)

    carry = cnt_ref[0:1, :]
    prefix = jnp.dot(tri, cmat, preferred_element_type=F32) + carry
    rank1 = jnp.sum(jnp.where(oh1, prefix, 0.0), axis=-1, keepdims=True)
    rank2 = jnp.sum(jnp.where(oh2, prefix, 0.0), axis=-1, keepdims=True)
    cnt_ref[...] = jnp.broadcast_to(carry + jnp.sum(cmat.astype(F32), axis=0, keepdims=True), cnt_ref.shape)

    info = jnp.where(lane == 0, i1 - ng,
           jnp.where(lane == 1, i2 - ng,
           jnp.where(lane == 2, gate1,
           jnp.where(lane == 3, gate2,
           jnp.where(lane == 4, rank1,
           jnp.where(lane == 5, rank2, 0.0))))))
    info_ref[...] = info


def _route(logits, ng, epg):
    T = logits.shape[0]
    tm = min(512, T)
    return pl.pallas_call(
        functools.partial(_route_kernel, ng=ng, epg=epg),
        grid=(T // tm,),
        in_specs=[pl.BlockSpec((tm, LANES), lambda i: (i, 0))],
        out_specs=[pl.BlockSpec((tm, LANES), lambda i: (i, 0)),
                   pl.BlockSpec((8, LANES), lambda i: (0, 0))],
        out_shape=[jax.ShapeDtypeStruct((T, LANES), F32), jax.ShapeDtypeStruct((8, LANES), F32)],
        compiler_params=_cparams(("arbitrary",), 32),
        name="route",
    )(logits)


def _row_gather(src_hbm, idx_ref, base, dst, sem, n):
    def body(r, carry):
        pltpu.make_async_copy(src_hbm.at[pl.ds(idx_ref[base + r], 1)], dst.at[pl.ds(r, 1)], sem).start()
        return carry
    lax.fori_loop(0, n, body, 0, unroll=8)


def _row_gather_wait(src_hbm, dst, sem, n):
    pltpu.make_async_copy(src_hbm.at[pl.ds(0, n)], dst, sem).wait()


def _experts_kernel(bexp_ref, nused_ref, rtok_ref, h_hbm, fw_ref, w1_ref, w3_ref, w2_ref, y_ref,
                    xbuf, sems, w1b, w3b, w2b, *, blk, eps):
    i = pl.program_id(0)
    nused = nused_ref[0]
    slot = i % 2

    @pl.when(i == 0)
    def _():
        _row_gather(h_hbm, rtok_ref, 0, xbuf.at[0], sems.at[0], blk)

    @pl.when(i + 1 < nused)
    def _():
        _row_gather(h_hbm, rtok_ref, (i + 1) * blk, xbuf.at[1 - slot], sems.at[1 - slot], blk)

    @pl.when(i < nused)
    def _():
        changed = jnp.logical_or(i == 0, bexp_ref[i] != bexp_ref[jnp.maximum(i - 1, 0)])

        @pl.when(changed)
        def _():
            w1b[...] = w1_ref[0].astype(BF16)
            w3b[...] = w3_ref[0].astype(BF16)
            w2b[...] = w2_ref[0].astype(BF16)

        _row_gather_wait(h_hbm, xbuf.at[slot], sems.at[slot], blk)
        hrow = xbuf[slot]
        ms = jnp.mean(hrow * hrow, axis=-1, keepdims=True)
        xn = (hrow * lax.rsqrt(ms + eps) * fw_ref[...]).astype(BF16)
        a = jnp.dot(xn, w1b[...], preferred_element_type=F32)
        b = jnp.dot(xn, w3b[...], preferred_element_type=F32)
        hdn = (a * jax.nn.sigmoid(a) * b).astype(BF16)
        y_ref[...] = jnp.dot(hdn, w2b[...], preferred_element_type=F32)

    @pl.when(i >= nused)
    def _():
        y_ref[...] = jnp.zeros(y_ref.shape, F32)


def _experts(bexp, nused, row_tok, h, ffn_w, w1, w3, w2, blk):
    T, D = h.shape
    E, _, Fh = w1.shape
    nblk = bexp.shape[0]
    grid_spec = pltpu.PrefetchScalarGridSpec(
        num_scalar_prefetch=3,
        grid=(nblk,),
        in_specs=[pl.BlockSpec(memory_space=pl.ANY),
                  pl.BlockSpec((1, D), lambda i, be, nu, rt: (0, 0)),
                  pl.BlockSpec((1, D, Fh), lambda i, be, nu, rt: (be[i], 0, 0)),
                  pl.BlockSpec((1, D, Fh), lambda i, be, nu, rt: (be[i], 0, 0)),
                  pl.BlockSpec((1, Fh, D), lambda i, be, nu, rt: (be[i], 0, 0))],
        out_specs=pl.BlockSpec((blk, D), lambda i, be, nu, rt: (i, 0)),
        scratch_shapes=[pltpu.VMEM((2, blk, D), F32),
                        pltpu.SemaphoreType.DMA((2,)),
                        pltpu.VMEM((D, Fh), BF16),
                        pltpu.VMEM((D, Fh), BF16),
                        pltpu.VMEM((Fh, D), BF16)],
    )
    return pl.pallas_call(
        functools.partial(_experts_kernel, blk=blk, eps=RMS_EPS),
        grid_spec=grid_spec,
        out_shape=jax.ShapeDtypeStruct((nblk * blk, D), F32),
        compiler_params=_cparams(("arbitrary",), 56),
        name="experts",
    )(bexp, nused, row_tok, h, ffn_w, w1, w3, w2)


def _combine_kernel(dest_ref, h_ref, info_ref, y_hbm, fw_ref, o_ref, ybuf, sems, *, tm, eps):
    i = pl.program_id(0)
    n = pl.num_programs(0)
    slot = i % 2
    nrow = TOP_K * tm

    @pl.when(i == 0)
    def _():
        _row_gather(y_hbm, dest_ref, 0, ybuf.at[0], sems.at[0], nrow)

    @pl.when(i + 1 < n)
    def _():
        _row_gather(y_hbm, dest_ref, (i + 1) * nrow, ybuf.at[1 - slot], sems.at[1 - slot], nrow)

    _row_gather_wait(y_hbm, ybuf.at[slot], sems.at[slot], nrow)
    info = info_ref[...]
    g1 = info[:, 2:3]
    g2 = info[:, 3:4]
    yb = ybuf[slot]
    hn = h_ref[...] + (g1 * yb[0:tm] + g2 * yb[tm:nrow])
    ms = jnp.mean(hn * hn, axis=-1, keepdims=True)
    o_ref[...] = hn * lax.rsqrt(ms + eps) * fw_ref[...]


def _combine(dest_km, h, info, yb, final_w, tm):
    T, D = h.shape
    grid_spec = pltpu.PrefetchScalarGridSpec(
        num_scalar_prefetch=1,
        grid=(T // tm,),
        in_specs=[pl.BlockSpec((tm, D), lambda i, d: (i, 0)),
                  pl.BlockSpec((tm, LANES), lambda i, d: (i, 0)),
                  pl.BlockSpec(memory_space=pl.ANY),
                  pl.BlockSpec((1, D), lambda i, d: (0, 0))],
        out_specs=pl.BlockSpec((tm, D), lambda i, d: (i, 0)),
        scratch_shapes=[pltpu.VMEM((2, TOP_K * tm, D), F32),
                        pltpu.SemaphoreType.DMA((2,))],
    )
    return pl.pallas_call(
        functools.partial(_combine_kernel, tm=tm, eps=RMS_EPS),
        grid_spec=grid_spec,
        out_shape=jax.ShapeDtypeStruct((T, D), F32),
        compiler_params=_cparams(("arbitrary",), 48),
        name="combine",
    )(dest_km, h, info, yb, final_w)


def _layer(h_in, l, B, S, mix_norm_w, w_in, conv_dw_w, conv_dw_b, conv_ln_w, conv_ln_b,
           lam_q1, lam_k1, lam_q2, lam_k2, attn_subln_w, w_out, ffn_norm_w,
           w_group, b_group, w_expert_gate, b_expert_gate, w1, w3, w2):
    T, D = h_in.shape
    d_conv = conv_dw_w.shape[1]
    d_attn = (w_in.shape[1] - 2 * d_conv) // 3
    ng = w_group.shape[1]
    ne = w_expert_gate.shape[1]
    epg = ne // ng
    assert ng + ne <= LANES
    lam_init = 0.8 - 0.6 * math.exp(-0.3 * l)

    proj = _norm_inproj(h_in, mix_norm_w.reshape(1, D), w_in.astype(BF16))
    y_conv = _conformer(proj, conv_dw_w, conv_dw_b, conv_ln_w, conv_ln_b, B, S)
    y_attn = _diff_attn(proj, lam_q1, lam_k1, lam_q2, lam_k2, attn_subln_w, B, S, d_conv, d_attn, lam_init)

    wr = jnp.zeros((D, LANES), F32).at[:, :ng].set(w_group).at[:, ng:ng + ne].set(w_expert_gate)
    wr_hi = wr.astype(BF16)
    wr_lo = (wr - wr_hi.astype(F32)).astype(BF16)
    br = jnp.zeros((1, LANES), F32).at[0, :ng].set(b_group).at[0, ng:ng + ne].set(b_expert_gate.reshape(-1))
    h, logits = _outproj(h_in, y_conv, y_attn, w_out.astype(BF16), ffn_norm_w.reshape(1, D), wr_hi, wr_lo, br)

    info, cnt = _route(logits, ng, epg)

    blk = 256
    A = T * TOP_K
    nblk = (A + ne * (blk - 1) + blk - 1) // blk
    counts = cnt[0, ng:ng + ne].astype(jnp.int32)
    padded = (counts + blk - 1) // blk * blk
    pad_ends = jnp.cumsum(padded)
    pad_starts = pad_ends - padded
    eid = info[:, 0:TOP_K].astype(jnp.int32)
    rank = info[:, 4:4 + TOP_K].astype(jnp.int32)
    dest = pad_starts[eid] + rank
    tok = jnp.broadcast_to(jnp.arange(T, dtype=jnp.int32)[:, None], (T, TOP_K))
    row_tok = jnp.zeros((nblk * blk,), jnp.int32).at[dest.reshape(-1)].set(tok.reshape(-1))
    nused = (pad_ends[-1] // blk).astype(jnp.int32)
    bidx = jnp.minimum(jnp.arange(nblk, dtype=jnp.int32), nused - 1)
    bexp = jnp.minimum(jnp.searchsorted(pad_ends, bidx * blk, side="right"), ne - 1).astype(jnp.int32)

    yb = _experts(bexp, nused.reshape(1), row_tok, h, ffn_norm_w.reshape(1, D), w1, w3, w2, blk)
    return h, info, dest, yb


def kernel(x, mix_norm_w, w_in, conv_dw_w, conv_dw_b, conv_ln_w, conv_ln_b, lam_q1, lam_k1, lam_q2, lam_k2,
           attn_subln_w, w_out, ffn_norm_w, w_group, b_group, w_expert_gate, b_expert_gate, w1, w3, w2,
           final_norm_w):
    B, S, D = x.shape
    depth = w_in.shape[0]
    assert depth == 1
    T = B * S
    tmc = min(256, T)
    h = x.reshape(T, D)
    for l in range(depth):
        h, info, dest, yb = _layer(
            h, l, B, S, mix_norm_w[l], w_in[l], conv_dw_w[l], conv_dw_b[l], conv_ln_w[l], conv_ln_b[l],
            lam_q1[l], lam_k1[l], lam_q2[l], lam_k2[l], attn_subln_w[l], w_out[l], ffn_norm_w[l],
            w_group[l], b_group[l], w_expert_gate[l], b_expert_gate[l], w1[l], w3[l], w2[l])
        dest_km = dest.reshape(T // tmc, tmc, TOP_K).transpose(0, 2, 1).reshape(-1)
        h = _combine(dest_km, h, info, yb, final_norm_w.reshape(1, D), tmc)
    return h.reshape(B, S, D)
```

```python
import functools
import math

import jax
import jax.numpy as jnp
from jax import lax
from jax.experimental import pallas as pl
from jax.experimental.pallas import tpu as pltpu

F32 = jnp.float32
BF16 = jnp.bfloat16

RMS_EPS = 1e-6
SUBLN_EPS = 1e-5
LN_EPS = 1e-5
TOP_K = 2
LANES = 128
NEG_BIG = -1e30
MIB = 1024 * 1024


def _cparams(sem, vmem_mib):
    return pltpu.CompilerParams(dimension_semantics=sem, vmem_limit_bytes=vmem_mib * MIB)


def _norm_inproj_kernel(x_ref, nw_ref, w_ref, o_ref, u_ref, *, eps, rc):
    @pl.when(pl.program_id(1) == 0)
    def _():
        def body(r, carry):
            rows = pl.ds(pl.multiple_of(r * rc, rc), rc)
            x = x_ref[rows, :]
            ms = jnp.mean(x * x, axis=-1, keepdims=True)
            u_ref[rows, :] = (x * lax.rsqrt(ms + eps) * nw_ref[...]).astype(u_ref.dtype)
            return carry
        lax.fori_loop(0, x_ref.shape[0] // rc, body, 0)

    o_ref[...] = jnp.dot(u_ref[...], w_ref[...], preferred_element_type=F32).astype(o_ref.dtype)


def _norm_inproj(x2, nw, w_bf):
    T, D = x2.shape
    N = w_bf.shape[1]
    tm = min(1024, T)
    tn = 1024
    return pl.pallas_call(
        functools.partial(_norm_inproj_kernel, eps=RMS_EPS, rc=128),
        grid=(T // tm, N // tn),
        in_specs=[pl.BlockSpec((tm, D), lambda i, j: (i, 0)),
                  pl.BlockSpec((1, D), lambda i, j: (0, 0)),
                  pl.BlockSpec((D, tn), lambda i, j: (0, j))],
        out_specs=pl.BlockSpec((tm, tn), lambda i, j: (i, j)),
        out_shape=jax.ShapeDtypeStruct((T, N), BF16),
        scratch_shapes=[pltpu.VMEM((tm, D), BF16)],
        compiler_params=_cparams(("arbitrary", "arbitrary"), 48),
        name="norm_inproj",
    )(x2, nw, w_bf)


def _conv_kernel(a_ref, g_ref, w_ref, b_ref, lw_ref, lb_ref, o_ref, ubuf, cbuf, *, ts, kw, halo, eps):
    s = pl.program_id(1)
    C = a_ref.shape[1]

    @pl.when(s == 0)
    def _():
        ubuf[0:halo, :] = jnp.zeros((halo, C), F32)

    @pl.when(s > 0)
    def _():
        ubuf[0:halo, :] = ubuf[ts:ts + halo, :]

    rg = 64
    for r0 in range(0, ts, rg):
        a = a_ref[r0:r0 + rg, :].astype(F32)
        g = g_ref[r0:r0 + rg, :].astype(F32)
        ubuf[halo + r0:halo + r0 + rg, :] = a * jax.nn.sigmoid(g)

    off = halo - (kw - 1)
    rc = 32
    for c in range(C // LANES):
        cs = slice(c * LANES, (c + 1) * LANES)
        for r0 in range(0, ts, rc):
            acc = jnp.broadcast_to(b_ref[0:1, cs], (rc, LANES))
            for k in range(kw):
                acc = acc + w_ref[k:k + 1, cs] * ubuf[r0 + off + k:r0 + off + k + rc, cs]
            cbuf[r0:r0 + rc, cs] = acc

    for r0 in range(0, ts, rg):
        cv = cbuf[r0:r0 + rg, :]
        mu = jnp.mean(cv, axis=-1, keepdims=True)
        d = cv - mu
        var = jnp.mean(d * d, axis=-1, keepdims=True)
        un = d * lax.rsqrt(var + eps) * lw_ref[...] + lb_ref[...]
        o_ref[r0:r0 + rg, :] = (un * jax.nn.sigmoid(un)).astype(o_ref.dtype)


def _conformer(proj, dw_w, dw_b, ln_w, ln_b, B, S):
    T = proj.shape[0]
    kw, C = dw_w.shape
    ts = min(256, S)
    halo = 32
    assert kw - 1 <= halo and S % ts == 0 and ts >= halo
    ns = S // ts
    wp = jnp.zeros((halo, C), F32).at[:kw].set(dw_w)
    return pl.pallas_call(
        functools.partial(_conv_kernel, ts=ts, kw=kw, halo=halo, eps=LN_EPS),
        grid=(B, ns),
        in_specs=[pl.BlockSpec((ts, C), lambda b, s: (b * ns + s, 0)),
                  pl.BlockSpec((ts, C), lambda b, s: (b * ns + s, 1)),
                  pl.BlockSpec((halo, C), lambda b, s: (0, 0)),
                  pl.BlockSpec((1, C), lambda b, s: (0, 0)),
                  pl.BlockSpec((1, C), lambda b, s: (0, 0)),
                  pl.BlockSpec((1, C), lambda b, s: (0, 0))],
        out_specs=pl.BlockSpec((ts, C), lambda b, s: (b * ns + s, 0)),
        out_shape=jax.ShapeDtypeStruct((T, C), BF16),
        scratch_shapes=[pltpu.VMEM((ts + halo, C), F32), pltpu.VMEM((ts, C), F32)],
        compiler_params=_cparams(("arbitrary", "arbitrary"), 32),
        name="conformer",
    )(proj, proj, wp, dw_b.reshape(1, C), ln_w.reshape(1, C), ln_b.reshape(1, C))


def _attn_kernel(lq1_ref, lk1_ref, lq2_ref, lk2_ref, q_ref, k_ref, v_ref, sw_ref, o_ref,
                 vxt_ref, qq_ref, *scr, tq, hd, hp, ones_rows, lam_init, eps):
    i = pl.program_id(2)
    vd = 2 * hd
    nkb = v_ref.shape[0] // tq
    acc_refs, m_refs, sa_refs, sb_refs = (scr[n * hp:(n + 1) * hp] for n in range(4))

    @pl.when(i == 0)
    def _():
        for u in range(hp):
            for jb in range(nkb):
                vblk = v_ref[jb * tq:(jb + 1) * tq, u * vd:(u + 1) * vd].astype(F32)
                vxt_ref[u, jb, 0:vd, :] = vblk.T.astype(vxt_ref.dtype)
                vxt_ref[u, jb, vd:vd + ones_rows, :] = jnp.ones((ones_rows, tq), vxt_ref.dtype)

    lam = (jnp.exp(jnp.sum(lq1_ref[...] * lk1_ref[...], axis=-1, keepdims=True))
           - jnp.exp(jnp.sum(lq2_ref[...] * lk2_ref[...], axis=-1, keepdims=True)) + lam_init)

    for u in range(hp):
        q = q_ref[:, u * vd:(u + 1) * vd]
        qs = q * jnp.asarray(hd ** -0.5, q.dtype)
        lane = lax.broadcasted_iota(jnp.int32, q.shape, 1)
        zero = jnp.zeros_like(qs)
        qq_ref[u, 0:tq, :] = jnp.where(lane < hd, qs, zero)
        qq_ref[u, tq:2 * tq, :] = jnp.where(lane >= hd, qs, zero)
        acc_refs[u][...] = jnp.zeros(acc_refs[u].shape, F32)
        m_refs[u][...] = jnp.full(m_refs[u].shape, NEG_BIG, F32)

    def scores(j, u, dst):
        rows = pl.ds(pl.multiple_of(j * tq, tq), tq)
        kb = k_ref[rows, u * vd:(u + 1) * vd]
        dst[...] = lax.dot_general(kb, qq_ref[u], (((1,), (1,)), ((), ())), preferred_element_type=F32)

    def softmax_pv(j, u, src, masked):
        st = src[...]
        if masked:
            r = lax.broadcasted_iota(jnp.int32, st.shape, 0)
            c = lax.broadcasted_iota(jnp.int32, st.shape, 1)
            st = jnp.where(r <= jnp.where(c >= tq, c - tq, c), st, NEG_BIG)
        m_old = m_refs[u][...]
        m_new = jnp.maximum(m_old, jnp.max(st, axis=0, keepdims=True))
        alpha = jnp.exp(m_old - m_new)
        pt = jnp.exp(st - m_new).astype(vxt_ref.dtype)
        pv = jnp.dot(vxt_ref[u, j], pt, preferred_element_type=F32)
        acc_refs[u][...] = alpha * acc_refs[u][...] + pv
        m_refs[u][...] = m_new

    def half(j, cur, nxt):
        for u in range(hp):
            scores(j + 1, u, nxt[u])
            softmax_pv(j, u, cur[u], False)

    for u in range(hp):
        scores(0, u, sa_refs[u])

    def pair(t, carry):
        half(2 * t, sa_refs, sb_refs)
        half(2 * t + 1, sb_refs, sa_refs)
        return carry

    lax.fori_loop(0, i // 2, pair, 0)

    @pl.when(i % 2 == 1)
    def _():
        half(i - 1, sa_refs, sb_refs)
        for u in range(hp):
            softmax_pv(i, u, sb_refs[u], True)

    @pl.when(i % 2 == 0)
    def _():
        for u in range(hp):
            softmax_pv(i, u, sa_refs[u], True)

    for u in range(hp):
        acc = acc_refs[u][...]
        o12 = acc[0:vd] * (1.0 / acc[vd:vd + 1])
        ot = o12[:, 0:tq] - lam * o12[:, tq:2 * tq]
        msq = jnp.mean(ot * ot, axis=0, keepdims=True)
        o = (ot * lax.rsqrt(msq + eps)).T
        o_ref[:, u * vd:(u + 1) * vd] = (o * sw_ref[...] * (1.0 - lam_init)).astype(o_ref.dtype)


def _diff_attn(proj, lq1, lk1, lq2, lk2, subln_w, B, S, d_conv, d_attn, lam_init):
    T = proj.shape[0]
    vd = subln_w.shape[0]
    hd = lq1.shape[0]
    assert vd == LANES and 2 * hd == vd
    H = d_attn // vd
    hp = 4
    ones_rows = 16
    tq = min(256, S)
    nq = S // tq
    assert H % hp == 0
    qc = 2 * d_conv // (hp * vd)
    kc = qc + H // hp
    vc = kc + H // hp
    lspec = pl.BlockSpec((1, hd), lambda b, h, i: (0, 0))
    return pl.pallas_call(
        functools.partial(_attn_kernel, tq=tq, hd=hd, hp=hp, ones_rows=ones_rows, lam_init=lam_init, eps=SUBLN_EPS),
        grid=(B, H // hp, nq),
        in_specs=[lspec, lspec, lspec, lspec,
                  pl.BlockSpec((tq, hp * vd), lambda b, h, i: (b * nq + i, qc + h)),
                  pl.BlockSpec((S, hp * vd), lambda b, h, i: (b, kc + h)),
                  pl.BlockSpec((S, hp * vd), lambda b, h, i: (b, vc + h)),
                  pl.BlockSpec((1, vd), lambda b, h, i: (0, 0))],
        out_specs=pl.BlockSpec((tq, hp * vd), lambda b, h, i: (b * nq + i, h)),
        out_shape=jax.ShapeDtypeStruct((T, d_attn), BF16),
        scratch_shapes=[pltpu.VMEM((hp, S // tq, vd + ones_rows, tq), BF16),
                        pltpu.VMEM((hp, 2 * tq, vd), BF16)]
                       + [pltpu.VMEM((vd + ones_rows, 2 * tq), F32) for _ in range(hp)]
                       + [pltpu.VMEM((1, 2 * tq), F32) for _ in range(hp)]
                       + [pltpu.VMEM((tq, 2 * tq), F32) for _ in range(2 * hp)],
        compiler_params=_cparams(("arbitrary", "arbitrary", "arbitrary"), 40),
        name="diff_attn",
    )(lq1.reshape(1, hd), lk1.reshape(1, hd), lq2.reshape(1, hd), lk2.reshape(1, hd),
      proj, proj, proj, subln_w.reshape(1, vd))


def _pack_bf16_pairs(v):
    n = v.shape[1] // 2
    bits = lax.bitcast_convert_type(v.astype(BF16).astype(F32), jnp.uint32)
    return (bits[:, 0:n] >> 16) | (bits[:, n:2 * n] & jnp.uint32(0xFFFF0000))


def _unpack_bf16_pairs(w):
    lo = lax.bitcast_convert_type(w << 16, F32)
    hi = lax.bitcast_convert_type(w & jnp.uint32(0xFFFF0000), F32)
    return lo, hi


def _outproj_kernel(x_ref, yc_ref, ya_ref, wo_ref, fw_ref, wrh_ref, wrl_ref, br_ref, h_ref, lg_ref, ug_ref, *, eps):
    dc = yc_ref.shape[1]
    mix = (jnp.dot(yc_ref[...], wo_ref[0:dc, :], preferred_element_type=F32)
           + jnp.dot(ya_ref[...], wo_ref[dc:, :], preferred_element_type=F32))
    h = x_ref[...] + mix
    h_ref[...] = h
    ms = jnp.mean(h * h, axis=-1, keepdims=True)
    un = h * lax.rsqrt(ms + eps) * fw_ref[...]
    hi = un.astype(BF16)
    lo = (un - hi.astype(F32)).astype(BF16)
    lg = (jnp.dot(hi, wrh_ref[...], preferred_element_type=F32)
          + jnp.dot(lo, wrh_ref[...], preferred_element_type=F32)
          + jnp.dot(hi, wrl_ref[...], preferred_element_type=F32))
    lg_ref[...] = lg + br_ref[...]
    ug_ref[...] = _pack_bf16_pairs(un)


def _outproj(x2, y_conv, y_attn, wo_bf, ffn_w, wr_hi, wr_lo, br):
    T, D = x2.shape
    dc = y_conv.shape[1]
    da = y_attn.shape[1]
    tm = min(256, T)
    return pl.pallas_call(
        functools.partial(_outproj_kernel, eps=RMS_EPS),
        grid=(T // tm,),
        in_specs=[pl.BlockSpec((tm, D), lambda i: (i, 0)),
                  pl.BlockSpec((tm, dc), lambda i: (i, 0)),
                  pl.BlockSpec((tm, da), lambda i: (i, 0)),
                  pl.BlockSpec((dc + da, D), lambda i: (0, 0)),
                  pl.BlockSpec((1, D), lambda i: (0, 0)),
                  pl.BlockSpec((D, LANES), lambda i: (0, 0)),
                  pl.BlockSpec((D, LANES), lambda i: (0, 0)),
                  pl.BlockSpec((1, LANES), lambda i: (0, 0))],
        out_specs=[pl.BlockSpec((tm, D), lambda i: (i, 0)),
                   pl.BlockSpec((tm, LANES), lambda i: (i, 0)),
                   pl.BlockSpec((tm, D // 2), lambda i: (i, 0))],
        out_shape=[jax.ShapeDtypeStruct((T, D), F32), jax.ShapeDtypeStruct((T, LANES), F32),
                   jax.ShapeDtypeStruct((T, D // 2), jnp.uint32)],
        compiler_params=_cparams(("arbitrary",), 48),
        name="outproj",
    )(x2, y_conv, y_attn, wo_bf, ffn_w, wr_hi, wr_lo, br)


def _route_kernel(lg_ref, info_ref, cnt_ref, *, ng, epg):
    i = pl.program_id(0)
    lg = lg_ref[...]
    tm = lg.shape[0]
    lane = lax.broadcasted_iota(jnp.int32, lg.shape, 1)
    lanef = lane.astype(F32)
    ne = ng * epg

    def first_argmax(vals):
        mx = jnp.max(vals, axis=-1, keepdims=True)
        idx = jnp.min(jnp.where(vals == mx, lanef, float(LANES)), axis=-1, keepdims=True)
        return mx, idx

    gmask = lane < ng
    gl = jnp.where(gmask, lg, NEG_BIG)
    gmax, gsel = first_argmax(gl)
    gsum = jnp.sum(jnp.where(gmask, jnp.exp(gl - gmax), 0.0), axis=-1, keepdims=True)
    g_w = 1.0 / gsum
    lo = gsel * epg + ng
    emask = (lanef >= lo) & (lanef < lo + epg)
    el = jnp.where(emask, lg, NEG_BIG)
    v1, i1 = first_argmax(el)
    el2 = jnp.where(lanef == i1, NEG_BIG, el)
    v2, i2 = first_argmax(el2)
    e2 = jnp.exp(v2 - v1)
    p1 = 1.0 / (1.0 + e2)
    gate1 = g_w * p1
    gate2 = g_w * (e2 * p1)

    oh1 = lanef == i1
    oh2 = lanef == i2
    cmat = jnp.where(oh1 | oh2, 1.0, 0.0).astype(BF16)
    r = lax.broadcasted_iota(jnp.int32, (tm, tm), 0)
    c = lax.broadcasted_iota(jnp.int32, (tm, tm), 1)
    tri = jnp.where(c < r, 1.0, 0.0).astype(BF16)

    @pl.when(i == 0)
    def _():
        cnt_ref[...] = jnp.zeros(cnt_ref.shape, F32)

    carry = cnt_ref[0:1, :]
    prefix = jnp.dot(tri, cmat, preferred_element_type=F32) + carry
    rank1 = jnp.sum(jnp.where(oh1, prefix, 0.0), axis=-1, keepdims=True)
    rank2 = jnp.sum(jnp.where(oh2, prefix, 0.0), axis=-1, keepdims=True)
    cnt_ref[...] = jnp.broadcast_to(carry + jnp.sum(cmat.astype(F32), axis=0, keepdims=True), cnt_ref.shape)

    info = jnp.where(lane == 0, i1 - ng,
           jnp.where(lane == 1, i2 - ng,
           jnp.where(lane == 2, gate1,
           jnp.where(lane == 3, gate2,
           jnp.where(lane == 4, rank1,
           jnp.where(lane == 5, rank2, 0.0))))))
    info_ref[...] = info


def _route(logits, ng, epg):
    T = logits.shape[0]
    tm = min(512, T)
    return pl.pallas_call(
        functools.partial(_route_kernel, ng=ng, epg=epg),
        grid=(T // tm,),
        in_specs=[pl.BlockSpec((tm, LANES), lambda i: (i, 0))],
        out_specs=[pl.BlockSpec((tm, LANES), lambda i: (i, 0)),
                   pl.BlockSpec((8, LANES), lambda i: (0, 0))],
        out_shape=[jax.ShapeDtypeStruct((T, LANES), F32), jax.ShapeDtypeStruct((8, LANES), F32)],
        compiler_params=_cparams(("arbitrary",), 32),
        name="route",
    )(logits)


ROW_UNROLL = 8


def _dispatch_kernel(dest_ref, pend_ref, ug_ref, xs_hbm, ring, zbuf, sems, zsem, *, tm, blk, ne):
    i = pl.program_id(0)
    n = pl.num_programs(0)
    slot = i % 2

    def seg_tail(e):
        end = pend_ref[e]
        start = jnp.where(e == 0, 0, pend_ref[jnp.maximum(e - 1, 0)])
        tail = pl.multiple_of(jnp.maximum(end - blk, 0), blk)
        return pltpu.make_async_copy(zbuf, xs_hbm.at[pl.ds(tail, blk)], zsem), end > start

    @pl.when(i == 0)
    def _():
        zbuf[...] = jnp.zeros(zbuf.shape, zbuf.dtype)

        def zstart(e, carry):
            cp, nonempty = seg_tail(e)

            @pl.when(nonempty)
            def _():
                cp.start()
            return carry

        def zwait(e, carry):
            cp, nonempty = seg_tail(e)

            @pl.when(nonempty)
            def _():
                cp.wait()
            return carry

        def spare(b):
            row = pl.multiple_of(b * blk, blk)
            return pltpu.make_async_copy(zbuf, xs_hbm.at[pl.ds(row, blk)], zsem), row >= pend_ref[ne - 1]

        def sstart(b, carry):
            cp, unused = spare(b)

            @pl.when(unused)
            def _():
                cp.start()
            return carry

        def swait(b, carry):
            cp, unused = spare(b)

            @pl.when(unused)
            def _():
                cp.wait()
            return carry

        nblk = xs_hbm.shape[0] // blk
        lax.fori_loop(0, ne, zstart, 0)
        lax.fori_loop(0, nblk, sstart, 0)
        lax.fori_loop(0, ne, zwait, 0)
        lax.fori_loop(0, nblk, swait, 0)

    def drain(s):
        for _ in range(TOP_K):
            pltpu.make_async_copy(ring.at[s], xs_hbm.at[pl.ds(0, tm)], sems.at[s]).wait()

    @pl.when(i >= 2)
    def _():
        drain(slot)

    ring[slot] = ug_ref[...]

    def issue(c, carry):
        r0 = pl.multiple_of(c * ROW_UNROLL, ROW_UNROLL)
        for s in range(ROW_UNROLL):
            for k in range(TOP_K):
                d = dest_ref[(i * tm + r0 + s) * TOP_K + k]
                pltpu.make_async_copy(ring.at[slot, pl.ds(r0 + s, 1)], xs_hbm.at[pl.ds(d, 1)], sems.at[slot]).start()
        return carry

    lax.fori_loop(0, tm // ROW_UNROLL, issue, 0)

    @pl.when(i == n - 1)
    def _():
        drain(slot)

        @pl.when(n >= 2)
        def _():
            drain(1 - slot)


def _dispatch(dest_flat, pad_ends, ug, nrows, tm, blk):
    T, W = ug.shape
    ne = pad_ends.shape[0]
    grid_spec = pltpu.PrefetchScalarGridSpec(
        num_scalar_prefetch=2,
        grid=(T // tm,),
        in_specs=[pl.BlockSpec((tm, W), lambda i, d, p: (i, 0))],
        out_specs=pl.BlockSpec(memory_space=pl.ANY),
        scratch_shapes=[pltpu.VMEM((2, tm, W), ug.dtype),
                        pltpu.VMEM((blk, W), ug.dtype),
                        pltpu.SemaphoreType.DMA((2,)),
                        pltpu.SemaphoreType.DMA(())],
    )
    return pl.pallas_call(
        functools.partial(_dispatch_kernel, tm=tm, blk=blk, ne=ne),
        grid_spec=grid_spec,
        out_shape=jax.ShapeDtypeStruct((nrows, W), ug.dtype),
        compiler_params=_cparams(("arbitrary",), 32),
        name="dispatch",
    )(dest_flat, pad_ends, ug)


def _experts_kernel(bexp_ref, first_ref, slot_ref, next_ref, nused_ref, x_ref, w1_hbm, w3_hbm, w2_hbm, y_ref,
                    w1b, w3b, w2b, wsems):
    i = pl.program_id(0)
    nused = nused_ref[0]

    def weight_copies(e, s):
        return (pltpu.make_async_copy(w1_hbm.at[e], w1b.at[s], wsems.at[s, 0]),
                pltpu.make_async_copy(w3_hbm.at[e], w3b.at[s], wsems.at[s, 1]),
                pltpu.make_async_copy(w2_hbm.at[e], w2b.at[s], wsems.at[s, 2]))

    @pl.when(jnp.logical_and(i == 0, nused > 0))
    def _():
        for cp in weight_copies(bexp_ref[0], 0):
            cp.start()

    @pl.when(i < nused)
    def _():
        s = slot_ref[i]

        @pl.when(first_ref[i] == 1)
        def _():
            @pl.when(next_ref[i] >= 0)
            def _():
                for cp in weight_copies(next_ref[i], 1 - s):
                    cp.start()
            for cp in weight_copies(bexp_ref[i], s):
                cp.wait()

        x_lo, x_hi = _unpack_bf16_pairs(x_ref[...])
        half = x_lo.shape[1]
        a = (jnp.dot(x_lo, w1b[s, 0:half, :], preferred_element_type=F32)
             + jnp.dot(x_hi, w1b[s, half:2 * half, :], preferred_element_type=F32))
        b = (jnp.dot(x_lo, w3b[s, 0:half, :], preferred_element_type=F32)
             + jnp.dot(x_hi, w3b[s, half:2 * half, :], preferred_element_type=F32))
        hdn = a * jax.nn.sigmoid(a) * b
        y = jnp.dot(hdn, w2b[s], preferred_element_type=F32)
        y_ref[...] = _pack_bf16_pairs(y)

    @pl.when(i >= nused)
    def _():
        y_ref[...] = jnp.zeros(y_ref.shape, y_ref.dtype)


def _experts(bexp, first, slot, nxt, nused, xs, w1, w3, w2, blk):
    P, W = xs.shape
    E, D, Fh = w1.shape
    nblk = P // blk
    smap = lambda i, *_: (i, 0)
    grid_spec = pltpu.PrefetchScalarGridSpec(
        num_scalar_prefetch=5,
        grid=(nblk,),
        in_specs=[pl.BlockSpec((blk, W), smap),
                  pl.BlockSpec(memory_space=pl.ANY),
                  pl.BlockSpec(memory_space=pl.ANY),
                  pl.BlockSpec(memory_space=pl.ANY)],
        out_specs=pl.BlockSpec((blk, W), smap),
        scratch_shapes=[pltpu.VMEM((2, D, Fh), w1.dtype),
                        pltpu.VMEM((2, D, Fh), w3.dtype),
                        pltpu.VMEM((2, Fh, D), w2.dtype),
                        pltpu.SemaphoreType.DMA((2, 3))],
    )
    return pl.pallas_call(
        _experts_kernel,
        grid_spec=grid_spec,
        out_shape=jax.ShapeDtypeStruct((P, W), xs.dtype),
        compiler_params=_cparams(("arbitrary",), 48),
        name="experts",
    )(bexp, first, slot, nxt, nused, xs, w1, w3, w2)


def _row_gather(src_hbm, idx_ref, base, dst, sem, n):
    def body(c, carry):
        r0 = pl.multiple_of(c * ROW_UNROLL, ROW_UNROLL)
        for s in range(ROW_UNROLL):
            pltpu.make_async_copy(src_hbm.at[pl.ds(idx_ref[base + r0 + s], 1)], dst.at[pl.ds(r0 + s, 1)], sem).start()
        return carry
    lax.fori_loop(0, n // ROW_UNROLL, body, 0)


def _row_gather_wait(src_hbm, dst, sem, n):
    pltpu.make_async_copy(src_hbm.at[pl.ds(0, n)], dst, sem).wait()


def _combine_kernel(dest_ref, h_ref, info_ref, y_hbm, fw_ref, o_ref, ybuf, sems, *, tm, eps):
    i = pl.program_id(0)
    n = pl.num_programs(0)
    slot = i % 2
    nrow = TOP_K * tm

    @pl.when(i == 0)
    def _():
        _row_gather(y_hbm, dest_ref, 0, ybuf.at[0], sems.at[0], nrow)

    @pl.when(i + 1 < n)
    def _():
        _row_gather(y_hbm, dest_ref, (i + 1) * nrow, ybuf.at[1 - slot], sems.at[1 - slot], nrow)

    _row_gather_wait(y_hbm, ybuf.at[slot], sems.at[slot], nrow)
    info = info_ref[...]
    g1 = info[:, 2:3]
    g2 = info[:, 3:4]
    y1_lo, y1_hi = _unpack_bf16_pairs(ybuf[slot, 0:tm, :])
    y2_lo, y2_hi = _unpack_bf16_pairs(ybuf[slot, tm:nrow, :])
    half = y1_lo.shape[1]
    h_lo = h_ref[:, 0:half] + (g1 * y1_lo + g2 * y2_lo)
    h_hi = h_ref[:, half:2 * half] + (g1 * y1_hi + g2 * y2_hi)
    ms = (jnp.sum(h_lo * h_lo, axis=-1, keepdims=True) + jnp.sum(h_hi * h_hi, axis=-1, keepdims=True)) / (2 * half)
    r = lax.rsqrt(ms + eps)
    o_ref[:, 0:half] = h_lo * r * fw_ref[:, 0:half]
    o_ref[:, half:2 * half] = h_hi * r * fw_ref[:, half:2 * half]


def _combine(dest_km, h, info, yb, final_w, tm):
    T, D = h.shape
    W = yb.shape[1]
    grid_spec = pltpu.PrefetchScalarGridSpec(
        num_scalar_prefetch=1,
        grid=(T // tm,),
        in_specs=[pl.BlockSpec((tm, D), lambda i, d: (i, 0)),
                  pl.BlockSpec((tm, LANES), lambda i, d: (i, 0)),
                  pl.BlockSpec(memory_space=pl.ANY),
                  pl.BlockSpec((1, D), lambda i, d: (0, 0))],
        out_specs=pl.BlockSpec((tm, D), lambda i, d: (i, 0)),
        scratch_shapes=[pltpu.VMEM((2, TOP_K * tm, W), yb.dtype),
                        pltpu.SemaphoreType.DMA((2,))],
    )
    return pl.pallas_call(
        functools.partial(_combine_kernel, tm=tm, eps=RMS_EPS),
        grid_spec=grid_spec,
        out_shape=jax.ShapeDtypeStruct((T, D), F32),
        compiler_params=_cparams(("arbitrary",), 48),
        name="combine",
    )(dest_km, h, info, yb, final_w)


def _layer(h_in, l, B, S, mix_norm_w, w_in, conv_dw_w, conv_dw_b, conv_ln_w, conv_ln_b,
           lam_q1, lam_k1, lam_q2, lam_k2, attn_subln_w, w_out, ffn_norm_w,
           w_group, b_group, w_expert_gate, b_expert_gate, w1, w3, w2):
    T, D = h_in.shape
    d_conv = conv_dw_w.shape[1]
    d_attn = (w_in.shape[1] - 2 * d_conv) // 3
    ng = w_group.shape[1]
    ne = w_expert_gate.shape[1]
    epg = ne // ng
    assert ng + ne <= LANES
    lam_init = 0.8 - 0.6 * math.exp(-0.3 * l)

    proj = _norm_inproj(h_in, mix_norm_w.reshape(1, D), w_in.astype(BF16))
    y_conv = _conformer(proj, conv_dw_w, conv_dw_b, conv_ln_w, conv_ln_b, B, S)
    y_attn = _diff_attn(proj, lam_q1, lam_k1, lam_q2, lam_k2, attn_subln_w, B, S, d_conv, d_attn, lam_init)

    wr = jnp.concatenate([w_group, w_expert_gate, jnp.zeros((D, LANES - ng - ne), F32)], axis=1)
    wr_hi = wr.astype(BF16)
    wr_lo = (wr - wr_hi.astype(F32)).astype(BF16)
    br = jnp.concatenate([b_group, b_expert_gate.reshape(-1), jnp.zeros((LANES - ng - ne,), F32)]).reshape(1, LANES)
    h, logits, ug = _outproj(h_in, y_conv, y_attn, w_out.astype(BF16), ffn_norm_w.reshape(1, D), wr_hi, wr_lo, br)

    info, cnt = _route(logits, ng, epg)

    blk = 256
    tmd = min(256, T)
    A = T * TOP_K
    nblk = (A + ne * (blk - 1) + blk - 1) // blk
    i32 = jnp.int32
    counts = cnt[0, ng:ng + ne].astype(i32)
    padded = (counts + blk - 1) // blk * blk
    pad_ends = jnp.cumsum(padded).astype(i32)
    pad_starts = pad_ends - padded
    eid = info[:, 0:TOP_K].astype(i32)
    rank = info[:, 4:4 + TOP_K].astype(i32)
    onehot = eid[:, :, None] == jnp.arange(ne, dtype=i32)[None, None, :]
    dest = jnp.sum(jnp.where(onehot, pad_starts[None, None, :], 0), axis=-1).astype(i32) + rank
    nused = (pad_ends[-1] // blk).astype(i32)
    bpos = jnp.arange(nblk, dtype=i32)
    bexp = jnp.minimum(jnp.searchsorted(pad_ends, jnp.minimum(bpos, nused - 1) * blk, side="right"), ne - 1).astype(i32)
    first = ((bpos < nused) & ((bpos == 0) | (bexp != jnp.roll(bexp, 1)))).astype(i32)
    slot = ((jnp.cumsum(first) - 1) % 2).astype(i32)
    used_idx = jnp.where(padded > 0, jnp.arange(ne, dtype=i32), ne)
    suffix_min = lax.cummin(used_idx, reverse=True)
    next_used = jnp.concatenate([suffix_min[1:], jnp.full((1,), ne, i32)])
    next_used = jnp.where(next_used >= ne, -1, next_used)
    nxt = next_used[bexp].astype(i32)

    xs = _dispatch(dest.reshape(-1), pad_ends, ug, nblk * blk, tmd, blk)
    yb = _experts(bexp, first, slot, nxt, nused.reshape(1), xs, w1, w3, w2, blk)
    return h, info, dest, yb


def kernel(x, mix_norm_w, w_in, conv_dw_w, conv_dw_b, conv_ln_w, conv_ln_b, lam_q1, lam_k1, lam_q2, lam_k2,
           attn_subln_w, w_out, ffn_norm_w, w_group, b_group, w_expert_gate, b_expert_gate, w1, w3, w2,
           final_norm_w):
    B, S, D = x.shape
    depth = w_in.shape[0]
    assert depth == 1
    T = B * S
    tmc = min(256, T)
    h = x.reshape(T, D)
    for l in range(depth):
        h, info, dest, yb = _layer(
            h, l, B, S, mix_norm_w[l], w_in[l], conv_dw_w[l], conv_dw_b[l], conv_ln_w[l], conv_ln_b[l],
            lam_q1[l], lam_k1[l], lam_q2[l], lam_k2[l], attn_subln_w[l], w_out[l], ffn_norm_w[l],
            w_group[l], b_group[l], w_expert_gate[l], b_expert_gate[l], w1[l], w3[l], w2[l])
        dest_km = dest.reshape(T // tmc, tmc, TOP_K).transpose(0, 2, 1).reshape(-1)
        h = _combine(dest_km, h, info, yb, final_norm_w.reshape(1, D), tmc)
    return h.reshape(B, S, D)
```

```python
import functools
import math

import jax
import jax.numpy as jnp
from jax import lax
from jax.experimental import pallas as pl
from jax.experimental.pallas import tpu as pltpu

F32 = jnp.float32
BF16 = jnp.bfloat16

RMS_EPS = 1e-6
SUBLN_EPS = 1e-5
LN_EPS = 1e-5
TOP_K = 2
LANES = 128
SUBLANES = 8
NEG_BIG = -1e30
MIB = 1024 * 1024


def _cparams(sem, vmem_mib):
    return pltpu.CompilerParams(dimension_semantics=sem, vmem_limit_bytes=vmem_mib * MIB)


def _norm_inproj_kernel(x_ref, nw_ref, w_ref, o_ref, u_ref, *, eps, rc):
    @pl.when(pl.program_id(1) == 0)
    def _():
        def body(r, carry):
            rows = pl.ds(pl.multiple_of(r * rc, rc), rc)
            x = x_ref[rows, :]
            ms = jnp.mean(x * x, axis=-1, keepdims=True)
            u_ref[rows, :] = (x * lax.rsqrt(ms + eps) * nw_ref[...]).astype(u_ref.dtype)
            return carry
        lax.fori_loop(0, x_ref.shape[0] // rc, body, 0)

    o_ref[...] = jnp.dot(u_ref[...], w_ref[...], preferred_element_type=F32).astype(o_ref.dtype)


def _norm_inproj(x2, nw, w_bf):
    T, D = x2.shape
    N = w_bf.shape[1]
    tm = min(1024, T)
    tn = 1024
    return pl.pallas_call(
        functools.partial(_norm_inproj_kernel, eps=RMS_EPS, rc=128),
        grid=(T // tm, N // tn),
        in_specs=[pl.BlockSpec((tm, D), lambda i, j: (i, 0)),
                  pl.BlockSpec((1, D), lambda i, j: (0, 0)),
                  pl.BlockSpec((D, tn), lambda i, j: (0, j))],
        out_specs=pl.BlockSpec((tm, tn), lambda i, j: (i, j)),
        out_shape=jax.ShapeDtypeStruct((T, N), BF16),
        scratch_shapes=[pltpu.VMEM((tm, D), BF16)],
        compiler_params=_cparams(("arbitrary", "arbitrary"), 48),
        name="norm_inproj",
    )(x2, nw, w_bf)


def _conv_kernel(a_ref, g_ref, w_ref, b_ref, lw_ref, lb_ref, o_ref, ubuf, cbuf, sh, *, ts, kw, halo, eps):
    s = pl.program_id(1)
    C = a_ref.shape[1]

    @pl.when(s == 0)
    def _():
        ubuf[0:halo, :] = jnp.zeros((halo, C), F32)

    @pl.when(s > 0)
    def _():
        ubuf[0:halo, :] = ubuf[ts:ts + halo, :]

    rg = 64
    for r0 in range(0, ts, rg):
        a = a_ref[r0:r0 + rg, :].astype(F32)
        g = g_ref[r0:r0 + rg, :].astype(F32)
        ubuf[halo + r0:halo + r0 + rg, :] = a * jax.nn.sigmoid(g)

    nsh = sh.shape[1]
    for r in range(1, SUBLANES):
        for i0 in range(0, nsh, rg):
            n = min(rg, nsh - i0)
            sh[r - 1, i0:i0 + n, :] = ubuf[i0 + r:i0 + r + n, :]

    off = halo - (kw - 1)
    rc = 64

    def conv_ln_rows(ri, carry):
        r0 = pl.multiple_of(ri * rc, rc)
        for c in range(C // LANES):
            cs = slice(c * LANES, (c + 1) * LANES)
            acc = jnp.broadcast_to(b_ref[0:1, cs], (rc // SUBLANES, SUBLANES, LANES))
            for k in range(kw):
                q, r = divmod(off + k, SUBLANES)
                rows = pl.ds(r0 + q * SUBLANES, rc)
                tap = ubuf[rows, cs] if r == 0 else sh[r - 1, rows, cs]
                acc = acc + w_ref[k, :, cs] * tap.reshape(rc // SUBLANES, SUBLANES, LANES)
            cbuf[pl.ds(r0, rc), cs] = acc.reshape(rc, LANES)
        cv = cbuf[pl.ds(r0, rc), :]
        mu = jnp.mean(cv, axis=-1, keepdims=True)
        d = cv - mu
        var = jnp.mean(d * d, axis=-1, keepdims=True)
        un = d * lax.rsqrt(var + eps) * lw_ref[...] + lb_ref[...]
        o_ref[pl.ds(r0, rc), :] = (un * jax.nn.sigmoid(un)).astype(o_ref.dtype)
        return carry

    lax.fori_loop(0, ts // rc, conv_ln_rows, 0)


def _conformer(proj, dw_w, dw_b, ln_w, ln_b, B, S):
    T = proj.shape[0]
    kw, C = dw_w.shape
    ts = min(256, S)
    halo = 32
    assert kw - 1 <= halo and S % ts == 0 and ts >= halo
    ns = S // ts
    wp = jnp.broadcast_to(dw_w[:, None, :], (kw, SUBLANES, C))
    return pl.pallas_call(
        functools.partial(_conv_kernel, ts=ts, kw=kw, halo=halo, eps=LN_EPS),
        grid=(B, ns),
        in_specs=[pl.BlockSpec((ts, C), lambda b, s: (b * ns + s, 0)),
                  pl.BlockSpec((ts, C), lambda b, s: (b * ns + s, 1)),
                  pl.BlockSpec((kw, SUBLANES, C), lambda b, s: (0, 0, 0)),
                  pl.BlockSpec((1, C), lambda b, s: (0, 0)),
                  pl.BlockSpec((1, C), lambda b, s: (0, 0)),
                  pl.BlockSpec((1, C), lambda b, s: (0, 0))],
        out_specs=pl.BlockSpec((ts, C), lambda b, s: (b * ns + s, 0)),
        out_shape=jax.ShapeDtypeStruct((T, C), BF16),
        scratch_shapes=[pltpu.VMEM((ts + halo, C), F32), pltpu.VMEM((ts, C), F32),
                        pltpu.VMEM((SUBLANES - 1, ts + halo - SUBLANES, C), F32)],
        compiler_params=_cparams(("arbitrary", "arbitrary"), 32),
        name="conformer",
    )(proj, proj, wp, dw_b.reshape(1, C), ln_w.reshape(1, C), ln_b.reshape(1, C))


def _attn_kernel(lq1_ref, lk1_ref, lq2_ref, lk2_ref, q_ref, k_ref, v_ref, sw_ref, o_ref,
                 vxt_ref, qq_ref, *scr, tq, hd, hp, ones_rows, lam_init, eps):
    i = pl.program_id(2)
    vd = 2 * hd
    nkb = v_ref.shape[0] // tq
    acc_refs, m_refs, sa_refs, sb_refs = (scr[n * hp:(n + 1) * hp] for n in range(4))

    @pl.when(i == 0)
    def _():
        for u in range(hp):
            for jb in range(nkb):
                vblk = v_ref[jb * tq:(jb + 1) * tq, u * vd:(u + 1) * vd].astype(F32)
                vxt_ref[u, jb, 0:vd, :] = vblk.T.astype(vxt_ref.dtype)
                vxt_ref[u, jb, vd:vd + ones_rows, :] = jnp.ones((ones_rows, tq), vxt_ref.dtype)

    lam = (jnp.exp(jnp.sum(lq1_ref[...] * lk1_ref[...], axis=-1, keepdims=True))
           - jnp.exp(jnp.sum(lq2_ref[...] * lk2_ref[...], axis=-1, keepdims=True)) + lam_init)

    for u in range(hp):
        q = q_ref[:, u * vd:(u + 1) * vd]
        qs = q * jnp.asarray(hd ** -0.5, q.dtype)
        lane = lax.broadcasted_iota(jnp.int32, q.shape, 1)
        zero = jnp.zeros_like(qs)
        qq_ref[u, 0:tq, :] = jnp.where(lane < hd, qs, zero)
        qq_ref[u, tq:2 * tq, :] = jnp.where(lane >= hd, qs, zero)
        acc_refs[u][...] = jnp.zeros(acc_refs[u].shape, F32)
        m_refs[u][...] = jnp.full(m_refs[u].shape, NEG_BIG, F32)

    def scores(j, u, dst):
        rows = pl.ds(pl.multiple_of(j * tq, tq), tq)
        kb = k_ref[rows, u * vd:(u + 1) * vd]
        dst[...] = lax.dot_general(kb, qq_ref[u], (((1,), (1,)), ((), ())), preferred_element_type=F32)

    def softmax_pv(j, u, src, masked):
        st = src[...]
        if masked:
            r = lax.broadcasted_iota(jnp.int32, st.shape, 0)
            c = lax.broadcasted_iota(jnp.int32, st.shape, 1)
            st = jnp.where(r <= jnp.where(c >= tq, c - tq, c), st, NEG_BIG)
        m_old = m_refs[u][...]
        m_new = jnp.maximum(m_old, jnp.max(st, axis=0, keepdims=True))
        alpha = jnp.exp(m_old - m_new)
        pt = jnp.exp(st - m_new).astype(vxt_ref.dtype)
        pv = jnp.dot(vxt_ref[u, j], pt, preferred_element_type=F32)
        acc_refs[u][...] = alpha * acc_refs[u][...] + pv
        m_refs[u][...] = m_new

    def half(j, cur, nxt):
        for u in range(hp):
            scores(j + 1, u, nxt[u])
            softmax_pv(j, u, cur[u], False)

    for u in range(hp):
        scores(0, u, sa_refs[u])

    def pair(t, carry):
        half(2 * t, sa_refs, sb_refs)
        half(2 * t + 1, sb_refs, sa_refs)
        return carry

    lax.fori_loop(0, i // 2, pair, 0)

    @pl.when(i % 2 == 1)
    def _():
        half(i - 1, sa_refs, sb_refs)
        for u in range(hp):
            softmax_pv(i, u, sb_refs[u], True)

    @pl.when(i % 2 == 0)
    def _():
        for u in range(hp):
            softmax_pv(i, u, sa_refs[u], True)

    for u in range(hp):
        acc = acc_refs[u][...]
        o12 = acc[0:vd] * (1.0 / acc[vd:vd + 1])
        ot = o12[:, 0:tq] - lam * o12[:, tq:2 * tq]
        msq = jnp.mean(ot * ot, axis=0, keepdims=True)
        o = (ot * lax.rsqrt(msq + eps)).T
        o_ref[:, u * vd:(u + 1) * vd] = (o * sw_ref[...] * (1.0 - lam_init)).astype(o_ref.dtype)


def _diff_attn(proj, lq1, lk1, lq2, lk2, subln_w, B, S, d_conv, d_attn, lam_init):
    T = proj.shape[0]
    vd = subln_w.shape[0]
    hd = lq1.shape[0]
    assert vd == LANES and 2 * hd == vd
    H = d_attn // vd
    hp = 4
    ones_rows = 16
    tq = min(256, S)
    nq = S // tq
    assert H % hp == 0
    qc = 2 * d_conv // (hp * vd)
    kc = qc + H // hp
    vc = kc + H // hp
    lspec = pl.BlockSpec((1, hd), lambda b, h, i: (0, 0))
    return pl.pallas_call(
        functools.partial(_attn_kernel, tq=tq, hd=hd, hp=hp, ones_rows=ones_rows, lam_init=lam_init, eps=SUBLN_EPS),
        grid=(B, H // hp, nq),
        in_specs=[lspec, lspec, lspec, lspec,
                  pl.BlockSpec((tq, hp * vd), lambda b, h, i: (b * nq + i, qc + h)),
                  pl.BlockSpec((S, hp * vd), lambda b, h, i: (b, kc + h)),
                  pl.BlockSpec((S, hp * vd), lambda b, h, i: (b, vc + h)),
                  pl.BlockSpec((1, vd), lambda b, h, i: (0, 0))],
        out_specs=pl.BlockSpec((tq, hp * vd), lambda b, h, i: (b * nq + i, h)),
        out_shape=jax.ShapeDtypeStruct((T, d_attn), BF16),
        scratch_shapes=[pltpu.VMEM((hp, S // tq, vd + ones_rows, tq), BF16),
                        pltpu.VMEM((hp, 2 * tq, vd), BF16)]
                       + [pltpu.VMEM((vd + ones_rows, 2 * tq), F32) for _ in range(hp)]
                       + [pltpu.VMEM((1, 2 * tq), F32) for _ in range(hp)]
                       + [pltpu.VMEM((tq, 2 * tq), F32) for _ in range(2 * hp)],
        compiler_params=_cparams(("arbitrary", "arbitrary", "arbitrary"), 40),
        name="diff_attn",
    )(lq1.reshape(1, hd), lk1.reshape(1, hd), lq2.reshape(1, hd), lk2.reshape(1, hd),
      proj, proj, proj, subln_w.reshape(1, vd))


def _pack_bf16_pairs(v):
    n = v.shape[1] // 2
    bits = lax.bitcast_convert_type(v.astype(BF16).astype(F32), jnp.uint32)
    return (bits[:, 0:n] >> 16) | (bits[:, n:2 * n] & jnp.uint32(0xFFFF0000))


def _unpack_bf16_pairs(w):
    lo = lax.bitcast_convert_type(w << 16, F32)
    hi = lax.bitcast_convert_type(w & jnp.uint32(0xFFFF0000), F32)
    return lo, hi


def _outproj_kernel(x_ref, yc_ref, ya_ref, wo_ref, fw_ref, wr_ref, br_ref, h_ref, lg_ref, ug_ref, *, eps, nsub):
    dc = yc_ref.shape[1]
    sub = x_ref.shape[0] // nsub
    mixes = []
    for t in range(nsub):
        rows = slice(t * sub, (t + 1) * sub)
        mixes.append(jnp.dot(yc_ref[rows, :], wo_ref[0:dc, :], preferred_element_type=F32)
                     + jnp.dot(ya_ref[rows, :], wo_ref[dc:, :], preferred_element_type=F32))
    for t in range(nsub):
        rows = slice(t * sub, (t + 1) * sub)
        h = x_ref[rows, :] + mixes[t]
        h_ref[rows, :] = h
        ms = jnp.mean(h * h, axis=-1, keepdims=True)
        un = h * lax.rsqrt(ms + eps) * fw_ref[...]
        hi = un.astype(BF16)
        lo = (un - hi.astype(F32)).astype(BF16)
        hh = jnp.dot(hi, wr_ref[...], preferred_element_type=F32)
        lh = jnp.dot(lo, wr_ref[:, 0:LANES], preferred_element_type=F32)
        lg_ref[rows, :] = hh[:, 0:LANES] + hh[:, LANES:2 * LANES] + lh + br_ref[...]
        ug_ref[rows, :] = _pack_bf16_pairs(un)


def _outproj(x2, y_conv, y_attn, wo_bf, ffn_w, wr_hl, br):
    T, D = x2.shape
    dc = y_conv.shape[1]
    da = y_attn.shape[1]
    tm = min(512, T)
    nsub = 2 if tm % 512 == 0 else 1
    return pl.pallas_call(
        functools.partial(_outproj_kernel, eps=RMS_EPS, nsub=nsub),
        grid=(T // tm,),
        in_specs=[pl.BlockSpec((tm, D), lambda i: (i, 0)),
                  pl.BlockSpec((tm, dc), lambda i: (i, 0)),
                  pl.BlockSpec((tm, da), lambda i: (i, 0)),
                  pl.BlockSpec((dc + da, D), lambda i: (0, 0), pipeline_mode=pl.Buffered(1)),
                  pl.BlockSpec((1, D), lambda i: (0, 0)),
                  pl.BlockSpec((D, 2 * LANES), lambda i: (0, 0)),
                  pl.BlockSpec((1, LANES), lambda i: (0, 0))],
        out_specs=[pl.BlockSpec((tm, D), lambda i: (i, 0)),
                   pl.BlockSpec((tm, LANES), lambda i: (i, 0)),
                   pl.BlockSpec((tm, D // 2), lambda i: (i, 0))],
        out_shape=[jax.ShapeDtypeStruct((T, D), F32), jax.ShapeDtypeStruct((T, LANES), F32),
                   jax.ShapeDtypeStruct((T, D // 2), jnp.uint32)],
        compiler_params=_cparams(("arbitrary",), 56),
        name="outproj",
    )(x2, y_conv, y_attn, wo_bf, ffn_w, wr_hl, br)


def _route_kernel(lg_ref, info_ref, cnt_ref, *, ng, epg):
    i = pl.program_id(0)
    lg = lg_ref[...]
    tm = lg.shape[0]
    lane = lax.broadcasted_iota(jnp.int32, lg.shape, 1)
    lanef = lane.astype(F32)
    ne = ng * epg

    def first_argmax(vals):
        mx = jnp.max(vals, axis=-1, keepdims=True)
        idx = jnp.min(jnp.where(vals == mx, lanef, float(LANES)), axis=-1, keepdims=True)
        return mx, idx

    gmask = lane < ng
    gl = jnp.where(gmask, lg, NEG_BIG)
    gmax, gsel = first_argmax(gl)
    gsum = jnp.sum(jnp.where(gmask, jnp.exp(gl - gmax), 0.0), axis=-1, keepdims=True)
    g_w = 1.0 / gsum
    lo = gsel * epg + ng
    emask = (lanef >= lo) & (lanef < lo + epg)
    el = jnp.where(emask, lg, NEG_BIG)
    v1, i1 = first_argmax(el)
    el2 = jnp.where(lanef == i1, NEG_BIG, el)
    v2, i2 = first_argmax(el2)
    e2 = jnp.exp(v2 - v1)
    p1 = 1.0 / (1.0 + e2)
    gate1 = g_w * p1
    gate2 = g_w * (e2 * p1)

    oh1 = lanef == i1
    oh2 = lanef == i2
    cmat = jnp.where(oh1 | oh2, 1.0, 0.0).astype(BF16)
    r = lax.broadcasted_iota(jnp.int32, (tm, tm), 0)
    c = lax.broadcasted_iota(jnp.int32, (tm, tm), 1)
    tri = jnp.where(c < r, 1.0, 0.0).astype(BF16)

    @pl.when(i == 0)
    def _():
        cnt_ref[...] = jnp.zeros(cnt_ref.shape, F32)

    carry = cnt_ref[0:1, :]
    prefix = jnp.dot(tri, cmat, preferred_element_type=F32) + carry
    rank1 = jnp.sum(jnp.where(oh1, prefix, 0.0), axis=-1, keepdims=True)
    rank2 = jnp.sum(jnp.where(oh2, prefix, 0.0), axis=-1, keepdims=True)
    cnt_ref[...] = jnp.broadcast_to(carry + jnp.sum(cmat.astype(F32), axis=0, keepdims=True), cnt_ref.shape)

    info = jnp.where(lane == 0, i1 - ng,
           jnp.where(lane == 1, i2 - ng,
           jnp.where(lane == 2, gate1,
           jnp.where(lane == 3, gate2,
           jnp.where(lane == 4, rank1,
           jnp.where(lane == 5, rank2, 0.0))))))
    info_ref[...] = info


def _route(logits, ng, epg):
    T = logits.shape[0]
    tm = min(512, T)
    return pl.pallas_call(
        functools.partial(_route_kernel, ng=ng, epg=epg),
        grid=(T // tm,),
        in_specs=[pl.BlockSpec((tm, LANES), lambda i: (i, 0))],
        out_specs=[pl.BlockSpec((tm, LANES), lambda i: (i, 0)),
                   pl.BlockSpec((8, LANES), lambda i: (0, 0))],
        out_shape=[jax.ShapeDtypeStruct((T, LANES), F32), jax.ShapeDtypeStruct((8, LANES), F32)],
        compiler_params=_cparams(("arbitrary",), 32),
        name="route",
    )(logits)


ROW_UNROLL = 8


def _dispatch_kernel(dest_ref, pend_ref, ug_ref, xs_hbm, ring, zbuf, sems, zsem, *, tm, blk, ne):
    i = pl.program_id(0)
    n = pl.num_programs(0)
    slot = i % 2

    def seg_tail(e):
        end = pend_ref[e]
        start = jnp.where(e == 0, 0, pend_ref[jnp.maximum(e - 1, 0)])
        tail = pl.multiple_of(jnp.maximum(end - blk, 0), blk)
        return pltpu.make_async_copy(zbuf, xs_hbm.at[pl.ds(tail, blk)], zsem), end > start

    @pl.when(i == 0)
    def _():
        zbuf[...] = jnp.zeros(zbuf.shape, zbuf.dtype)

        def zstart(e, carry):
            cp, nonempty = seg_tail(e)

            @pl.when(nonempty)
            def _():
                cp.start()
            return carry

        def zwait(e, carry):
            cp, nonempty = seg_tail(e)

            @pl.when(nonempty)
            def _():
                cp.wait()
            return carry

        def spare(b):
            row = pl.multiple_of(b * blk, blk)
            return pltpu.make_async_copy(zbuf, xs_hbm.at[pl.ds(row, blk)], zsem), row >= pend_ref[ne - 1]

        def sstart(b, carry):
            cp, unused = spare(b)

            @pl.when(unused)
            def _():
                cp.start()
            return carry

        def swait(b, carry):
            cp, unused = spare(b)

            @pl.when(unused)
            def _():
                cp.wait()
            return carry

        nblk = xs_hbm.shape[0] // blk
        lax.fori_loop(0, ne, zstart, 0)
        lax.fori_loop(0, nblk, sstart, 0)
        lax.fori_loop(0, ne, zwait, 0)
        lax.fori_loop(0, nblk, swait, 0)

    def drain(s):
        for _ in range(TOP_K):
            pltpu.make_async_copy(ring.at[s], xs_hbm.at[pl.ds(0, tm)], sems.at[s]).wait()

    @pl.when(i >= 2)
    def _():
        drain(slot)

    ring[slot] = ug_ref[...]

    def issue(c, carry):
        r0 = pl.multiple_of(c * ROW_UNROLL, ROW_UNROLL)
        for s in range(ROW_UNROLL):
            for k in range(TOP_K):
                d = dest_ref[(i * tm + r0 + s) * TOP_K + k]
                pltpu.make_async_copy(ring.at[slot, pl.ds(r0 + s, 1)], xs_hbm.at[pl.ds(d, 1)], sems.at[slot]).start()
        return carry

    lax.fori_loop(0, tm // ROW_UNROLL, issue, 0)

    @pl.when(i == n - 1)
    def _():
        drain(slot)

        @pl.when(n >= 2)
        def _():
            drain(1 - slot)


def _dispatch(dest_flat, pad_ends, ug, nrows, tm, blk):
    T, W = ug.shape
    ne = pad_ends.shape[0]
    grid_spec = pltpu.PrefetchScalarGridSpec(
        num_scalar_prefetch=2,
        grid=(T // tm,),
        in_specs=[pl.BlockSpec((tm, W), lambda i, d, p: (i, 0))],
        out_specs=pl.BlockSpec(memory_space=pl.ANY),
        scratch_shapes=[pltpu.VMEM((2, tm, W), ug.dtype),
                        pltpu.VMEM((blk, W), ug.dtype),
                        pltpu.SemaphoreType.DMA((2,)),
                        pltpu.SemaphoreType.DMA(())],
    )
    return pl.pallas_call(
        functools.partial(_dispatch_kernel, tm=tm, blk=blk, ne=ne),
        grid_spec=grid_spec,
        out_shape=jax.ShapeDtypeStruct((nrows, W), ug.dtype),
        compiler_params=_cparams(("arbitrary",), 32),
        name="dispatch",
    )(dest_flat, pad_ends, ug)


def _experts_kernel(bexp_ref, first_ref, slot_ref, next_ref, nused_ref, x_ref, w1_hbm, w3_hbm, w2_hbm, y_ref,
                    w1b, w3b, w2b, wsems):
    i = pl.program_id(0)
    nused = nused_ref[0]

    def weight_copies(e, s):
        return (pltpu.make_async_copy(w1_hbm.at[e], w1b.at[s], wsems.at[s, 0]),
                pltpu.make_async_copy(w3_hbm.at[e], w3b.at[s], wsems.at[s, 1]),
                pltpu.make_async_copy(w2_hbm.at[e], w2b.at[s], wsems.at[s, 2]))

    @pl.when(jnp.logical_and(i == 0, nused > 0))
    def _():
        for cp in weight_copies(bexp_ref[0], 0):
            cp.start()

    @pl.when(i < nused)
    def _():
        s = slot_ref[i]

        @pl.when(first_ref[i] == 1)
        def _():
            @pl.when(next_ref[i] >= 0)
            def _():
                for cp in weight_copies(next_ref[i], 1 - s):
                    cp.start()
            for cp in weight_copies(bexp_ref[i], s):
                cp.wait()

        x_lo, x_hi = _unpack_bf16_pairs(x_ref[...])
        half = x_lo.shape[1]
        a = (jnp.dot(x_lo, w1b[s, 0:half, :], preferred_element_type=F32)
             + jnp.dot(x_hi, w1b[s, half:2 * half, :], preferred_element_type=F32))
        b = (jnp.dot(x_lo, w3b[s, 0:half, :], preferred_element_type=F32)
             + jnp.dot(x_hi, w3b[s, half:2 * half, :], preferred_element_type=F32))
        hdn = a * jax.nn.sigmoid(a) * b
        y = jnp.dot(hdn, w2b[s], preferred_element_type=F32)
        y_ref[...] = _pack_bf16_pairs(y)

    @pl.when(i >= nused)
    def _():
        y_ref[...] = jnp.zeros(y_ref.shape, y_ref.dtype)


def _experts(bexp, first, slot, nxt, nused, xs, w1, w3, w2, blk):
    P, W = xs.shape
    E, D, Fh = w1.shape
    nblk = P // blk
    smap = lambda i, *_: (i, 0)
    grid_spec = pltpu.PrefetchScalarGridSpec(
        num_scalar_prefetch=5,
        grid=(nblk,),
        in_specs=[pl.BlockSpec((blk, W), smap),
                  pl.BlockSpec(memory_space=pl.ANY),
                  pl.BlockSpec(memory_space=pl.ANY),
                  pl.BlockSpec(memory_space=pl.ANY)],
        out_specs=pl.BlockSpec((blk, W), smap),
        scratch_shapes=[pltpu.VMEM((2, D, Fh), w1.dtype),
                        pltpu.VMEM((2, D, Fh), w3.dtype),
                        pltpu.VMEM((2, Fh, D), w2.dtype),
                        pltpu.SemaphoreType.DMA((2, 3))],
    )
    return pl.pallas_call(
        _experts_kernel,
        grid_spec=grid_spec,
        out_shape=jax.ShapeDtypeStruct((P, W), xs.dtype),
        compiler_params=_cparams(("arbitrary",), 48),
        name="experts",
    )(bexp, first, slot, nxt, nused, xs, w1, w3, w2)


def _row_gather(src_hbm, idx_ref, base, dst, sem, n):
    def body(c, carry):
        r0 = pl.multiple_of(c * ROW_UNROLL, ROW_UNROLL)
        for s in range(ROW_UNROLL):
            pltpu.make_async_copy(src_hbm.at[pl.ds(idx_ref[base + r0 + s], 1)], dst.at[pl.ds(r0 + s, 1)], sem).start()
        return carry
    lax.fori_loop(0, n // ROW_UNROLL, body, 0)


def _row_gather_wait(src_hbm, dst, sem, n):
    pltpu.make_async_copy(src_hbm.at[pl.ds(0, n)], dst, sem).wait()


def _combine_kernel(dest_ref, h_ref, info_ref, y_hbm, fw_ref, o_ref, ybuf, sems, *, tm, eps):
    i = pl.program_id(0)
    n = pl.num_programs(0)
    slot = i % 2
    nrow = TOP_K * tm

    @pl.when(i == 0)
    def _():
        _row_gather(y_hbm, dest_ref, 0, ybuf.at[0], sems.at[0], nrow)

    @pl.when(i + 1 < n)
    def _():
        _row_gather(y_hbm, dest_ref, (i + 1) * nrow, ybuf.at[1 - slot], sems.at[1 - slot], nrow)

    _row_gather_wait(y_hbm, ybuf.at[slot], sems.at[slot], nrow)
    info = info_ref[...]
    g1 = info[:, 2:3]
    g2 = info[:, 3:4]
    y1_lo, y1_hi = _unpack_bf16_pairs(ybuf[slot, 0:tm, :])
    y2_lo, y2_hi = _unpack_bf16_pairs(ybuf[slot, tm:nrow, :])
    half = y1_lo.shape[1]
    h_lo = h_ref[:, 0:half] + (g1 * y1_lo + g2 * y2_lo)
    h_hi = h_ref[:, half:2 * half] + (g1 * y1_hi + g2 * y2_hi)
    ms = (jnp.sum(h_lo * h_lo, axis=-1, keepdims=True) + jnp.sum(h_hi * h_hi, axis=-1, keepdims=True)) / (2 * half)
    r = lax.rsqrt(ms + eps)
    o_ref[:, 0:half] = h_lo * r * fw_ref[:, 0:half]
    o_ref[:, half:2 * half] = h_hi * r * fw_ref[:, half:2 * half]


def _combine(dest_km, h, info, yb, final_w, tm):
    T, D = h.shape
    W = yb.shape[1]
    grid_spec = pltpu.PrefetchScalarGridSpec(
        num_scalar_prefetch=1,
        grid=(T // tm,),
        in_specs=[pl.BlockSpec((tm, D), lambda i, d: (i, 0)),
                  pl.BlockSpec((tm, LANES), lambda i, d: (i, 0)),
                  pl.BlockSpec(memory_space=pl.ANY),
                  pl.BlockSpec((1, D), lambda i, d: (0, 0))],
        out_specs=pl.BlockSpec((tm, D), lambda i, d: (i, 0)),
        scratch_shapes=[pltpu.VMEM((2, TOP_K * tm, W), yb.dtype),
                        pltpu.SemaphoreType.DMA((2,))],
    )
    return pl.pallas_call(
        functools.partial(_combine_kernel, tm=tm, eps=RMS_EPS),
        grid_spec=grid_spec,
        out_shape=jax.ShapeDtypeStruct((T, D), F32),
        compiler_params=_cparams(("arbitrary",), 48),
        name="combine",
    )(dest_km, h, info, yb, final_w)


def _layer(h_in, l, B, S, mix_norm_w, w_in, conv_dw_w, conv_dw_b, conv_ln_w, conv_ln_b,
           lam_q1, lam_k1, lam_q2, lam_k2, attn_subln_w, w_out, ffn_norm_w,
           w_group, b_group, w_expert_gate, b_expert_gate, w1, w3, w2):
    T, D = h_in.shape
    d_conv = conv_dw_w.shape[1]
    d_attn = (w_in.shape[1] - 2 * d_conv) // 3
    ng = w_group.shape[1]
    ne = w_expert_gate.shape[1]
    epg = ne // ng
    assert ng + ne <= LANES
    lam_init = 0.8 - 0.6 * math.exp(-0.3 * l)

    proj = _norm_inproj(h_in, mix_norm_w.reshape(1, D), w_in.astype(BF16))
    y_conv = _conformer(proj, conv_dw_w, conv_dw_b, conv_ln_w, conv_ln_b, B, S)
    y_attn = _diff_attn(proj, lam_q1, lam_k1, lam_q2, lam_k2, attn_subln_w, B, S, d_conv, d_attn, lam_init)

    wr = jnp.concatenate([w_group, w_expert_gate, jnp.zeros((D, LANES - ng - ne), F32)], axis=1)
    wr_hi = wr.astype(BF16)
    wr_lo = (wr - wr_hi.astype(F32)).astype(BF16)
    br = jnp.concatenate([b_group, b_expert_gate.reshape(-1), jnp.zeros((LANES - ng - ne,), F32)]).reshape(1, LANES)
    wr_hl = jnp.concatenate([wr_hi, wr_lo], axis=1)
    h, logits, ug = _outproj(h_in, y_conv, y_attn, w_out.astype(BF16), ffn_norm_w.reshape(1, D), wr_hl, br)

    info, cnt = _route(logits, ng, epg)

    blk = 256
    tmd = min(256, T)
    A = T * TOP_K
    nblk = (A + ne * (blk - 1) + blk - 1) // blk
    i32 = jnp.int32
    counts = cnt[0, ng:ng + ne].astype(i32)
    padded = (counts + blk - 1) // blk * blk
    pad_ends = jnp.cumsum(padded).astype(i32)
    pad_starts = pad_ends - padded
    eid = info[:, 0:TOP_K].astype(i32)
    rank = info[:, 4:4 + TOP_K].astype(i32)
    onehot = eid[:, :, None] == jnp.arange(ne, dtype=i32)[None, None, :]
    dest = jnp.sum(jnp.where(onehot, pad_starts[None, None, :], 0), axis=-1).astype(i32) + rank
    nused = (pad_ends[-1] // blk).astype(i32)
    bpos = jnp.arange(nblk, dtype=i32)
    bexp = jnp.minimum(jnp.searchsorted(pad_ends, jnp.minimum(bpos, nused - 1) * blk, side="right"), ne - 1).astype(i32)
    first = ((bpos < nused) & ((bpos == 0) | (bexp != jnp.roll(bexp, 1)))).astype(i32)
    slot = ((jnp.cumsum(first) - 1) % 2).astype(i32)
    used_idx = jnp.where(padded > 0, jnp.arange(ne, dtype=i32), ne)
    suffix_min = lax.cummin(used_idx, reverse=True)
    next_used = jnp.concatenate([suffix_min[1:], jnp.full((1,), ne, i32)])
    next_used = jnp.where(next_used >= ne, -1, next_used)
    nxt = next_used[bexp].astype(i32)

    xs = _dispatch(dest.reshape(-1), pad_ends, ug, nblk * blk, tmd, blk)
    yb = _experts(bexp, first, slot, nxt, nused.reshape(1), xs, w1, w3, w2, blk)
    return h, info, dest, yb


def kernel(x, mix_norm_w, w_in, conv_dw_w, conv_dw_b, conv_ln_w, conv_ln_b, lam_q1, lam_k1, lam_q2, lam_k2,
           attn_subln_w, w_out, ffn_norm_w, w_group, b_group, w_expert_gate, b_expert_gate, w1, w3, w2,
           final_norm_w):
    B, S, D = x.shape
    depth = w_in.shape[0]
    assert depth == 1
    T = B * S
    tmc = min(256, T)
    h = x.reshape(T, D)
    for l in range(depth):
        h, info, dest, yb = _layer(
            h, l, B, S, mix_norm_w[l], w_in[l], conv_dw_w[l], conv_dw_b[l], conv_ln_w[l], conv_ln_b[l],
            lam_q1[l], lam_k1[l], lam_q2[l], lam_k2[l], attn_subln_w[l], w_out[l], ffn_norm_w[l],
            w_group[l], b_group[l], w_expert_gate[l], b_expert_gate[l], w1[l], w3[l], w2[l])
        dest_km = dest.reshape(T // tmc, tmc, TOP_K).transpose(0, 2, 1).reshape(-1)
        h = _combine(dest_km, h, info, yb, final_norm_w.reshape(1, D), tmc)
    return h.reshape(B, S, D)
```

```python
import functools
import math

import jax
import jax.numpy as jnp
from jax import lax
from jax.experimental import pallas as pl
from jax.experimental.pallas import tpu as pltpu

F32 = jnp.float32
BF16 = jnp.bfloat16

RMS_EPS = 1e-6
SUBLN_EPS = 1e-5
LN_EPS = 1e-5
TOP_K = 2
LANES = 128
SUBLANES = 8
NEG_BIG = -1e30
MIB = 1024 * 1024


def _cparams(sem, vmem_mib):
    return pltpu.CompilerParams(dimension_semantics=sem, vmem_limit_bytes=vmem_mib * MIB)


def _norm_inproj_kernel(x_ref, nw_ref, w_ref, o_ref, u_ref, *, eps, rc):
    @pl.when(pl.program_id(1) == 0)
    def _():
        def body(r, carry):
            rows = pl.ds(pl.multiple_of(r * rc, rc), rc)
            x = x_ref[rows, :]
            ms = jnp.mean(x * x, axis=-1, keepdims=True)
            u_ref[rows, :] = (x * lax.rsqrt(ms + eps) * nw_ref[...]).astype(u_ref.dtype)
            return carry
        lax.fori_loop(0, x_ref.shape[0] // rc, body, 0)

    o_ref[...] = jnp.dot(u_ref[...], w_ref[...], preferred_element_type=F32).astype(o_ref.dtype)


def _norm_inproj(x2, nw, w_bf):
    T, D = x2.shape
    N = w_bf.shape[1]
    tm = min(1024, T)
    tn = 1024
    return pl.pallas_call(
        functools.partial(_norm_inproj_kernel, eps=RMS_EPS, rc=128),
        grid=(T // tm, N // tn),
        in_specs=[pl.BlockSpec((tm, D), lambda i, j: (i, 0)),
                  pl.BlockSpec((1, D), lambda i, j: (0, 0)),
                  pl.BlockSpec((D, tn), lambda i, j: (0, j))],
        out_specs=pl.BlockSpec((tm, tn), lambda i, j: (i, j)),
        out_shape=jax.ShapeDtypeStruct((T, N), BF16),
        scratch_shapes=[pltpu.VMEM((tm, D), BF16)],
        compiler_params=_cparams(("arbitrary", "arbitrary"), 48),
        name="norm_inproj",
    )(x2, nw, w_bf)


def _conv_kernel(a_ref, g_ref, w_ref, b_ref, lw_ref, lb_ref, o_ref, ubuf, cbuf, sh, *, ts, kw, halo, eps):
    s = pl.program_id(1)
    C = a_ref.shape[1]

    @pl.when(s == 0)
    def _():
        ubuf[0:halo, :] = jnp.zeros((halo, C), F32)

    @pl.when(s > 0)
    def _():
        ubuf[0:halo, :] = ubuf[ts:ts + halo, :]

    rg = 64
    for r0 in range(0, ts, rg):
        a = a_ref[r0:r0 + rg, :].astype(F32)
        g = g_ref[r0:r0 + rg, :].astype(F32)
        ubuf[halo + r0:halo + r0 + rg, :] = a * jax.nn.sigmoid(g)

    nsh = sh.shape[1]
    for r in range(1, SUBLANES):
        for i0 in range(0, nsh, rg):
            n = min(rg, nsh - i0)
            sh[r - 1, i0:i0 + n, :] = ubuf[i0 + r:i0 + r + n, :]

    off = halo - (kw - 1)
    rc = 64

    def conv_ln_rows(ri, carry):
        r0 = pl.multiple_of(ri * rc, rc)
        for c in range(C // LANES):
            cs = slice(c * LANES, (c + 1) * LANES)
            acc = jnp.broadcast_to(b_ref[0:1, cs], (rc // SUBLANES, SUBLANES, LANES))
            for k in range(kw):
                q, r = divmod(off + k, SUBLANES)
                rows = pl.ds(r0 + q * SUBLANES, rc)
                tap = ubuf[rows, cs] if r == 0 else sh[r - 1, rows, cs]
                acc = acc + w_ref[k, :, cs] * tap.reshape(rc // SUBLANES, SUBLANES, LANES)
            cbuf[pl.ds(r0, rc), cs] = acc.reshape(rc, LANES)
        cv = cbuf[pl.ds(r0, rc), :]
        mu = jnp.mean(cv, axis=-1, keepdims=True)
        d = cv - mu
        var = jnp.mean(d * d, axis=-1, keepdims=True)
        un = d * lax.rsqrt(var + eps) * lw_ref[...] + lb_ref[...]
        o_ref[pl.ds(r0, rc), :] = (un * jax.nn.sigmoid(un)).astype(o_ref.dtype)
        return carry

    lax.fori_loop(0, ts // rc, conv_ln_rows, 0)


def _conformer(proj, dw_w, dw_b, ln_w, ln_b, B, S):
    T = proj.shape[0]
    kw, C = dw_w.shape
    ts = min(256, S)
    halo = 32
    assert kw - 1 <= halo and S % ts == 0 and ts >= halo
    ns = S // ts
    wp = jnp.broadcast_to(dw_w[:, None, :], (kw, SUBLANES, C))
    return pl.pallas_call(
        functools.partial(_conv_kernel, ts=ts, kw=kw, halo=halo, eps=LN_EPS),
        grid=(B, ns),
        in_specs=[pl.BlockSpec((ts, C), lambda b, s: (b * ns + s, 0)),
                  pl.BlockSpec((ts, C), lambda b, s: (b * ns + s, 1)),
                  pl.BlockSpec((kw, SUBLANES, C), lambda b, s: (0, 0, 0)),
                  pl.BlockSpec((1, C), lambda b, s: (0, 0)),
                  pl.BlockSpec((1, C), lambda b, s: (0, 0)),
                  pl.BlockSpec((1, C), lambda b, s: (0, 0))],
        out_specs=pl.BlockSpec((ts, C), lambda b, s: (b * ns + s, 0)),
        out_shape=jax.ShapeDtypeStruct((T, C), BF16),
        scratch_shapes=[pltpu.VMEM((ts + halo, C), F32), pltpu.VMEM((ts, C), F32),
                        pltpu.VMEM((SUBLANES - 1, ts + halo - SUBLANES, C), F32)],
        compiler_params=_cparams(("arbitrary", "arbitrary"), 32),
        name="conformer",
    )(proj, proj, wp, dw_b.reshape(1, C), ln_w.reshape(1, C), ln_b.reshape(1, C))


def _attn_kernel(lq1_ref, lk1_ref, lq2_ref, lk2_ref, q_ref, k_ref, v_ref, sw_ref, o_ref,
                 vxt_ref, qq_ref, *scr, tq, hd, hp, ones_rows, lam_init, eps):
    i = pl.program_id(2)
    vd = 2 * hd
    nkb = v_ref.shape[0] // tq
    acc_refs, m_refs, sa_refs, sb_refs = (scr[n * hp:(n + 1) * hp] for n in range(4))

    @pl.when(i == 0)
    def _():
        for u in range(hp):
            for jb in range(nkb):
                vblk = v_ref[jb * tq:(jb + 1) * tq, u * vd:(u + 1) * vd].astype(F32)
                vxt_ref[u, jb, 0:vd, :] = vblk.T.astype(vxt_ref.dtype)
                vxt_ref[u, jb, vd:vd + ones_rows, :] = jnp.ones((ones_rows, tq), vxt_ref.dtype)

    lam = (jnp.exp(jnp.sum(lq1_ref[...] * lk1_ref[...], axis=-1, keepdims=True))
           - jnp.exp(jnp.sum(lq2_ref[...] * lk2_ref[...], axis=-1, keepdims=True)) + lam_init)

    for u in range(hp):
        q = q_ref[:, u * vd:(u + 1) * vd]
        qs = q * jnp.asarray(hd ** -0.5, q.dtype)
        lane = lax.broadcasted_iota(jnp.int32, q.shape, 1)
        zero = jnp.zeros_like(qs)
        qq_ref[u, 0:tq, :] = jnp.where(lane < hd, qs, zero)
        qq_ref[u, tq:2 * tq, :] = jnp.where(lane >= hd, qs, zero)
        acc_refs[u][...] = jnp.zeros(acc_refs[u].shape, F32)
        m_refs[u][...] = jnp.full(m_refs[u].shape, NEG_BIG, F32)

    def scores(j, u, dst):
        rows = pl.ds(pl.multiple_of(j * tq, tq), tq)
        kb = k_ref[rows, u * vd:(u + 1) * vd]
        dst[...] = lax.dot_general(kb, qq_ref[u], (((1,), (1,)), ((), ())), preferred_element_type=F32)

    def softmax_pv(j, u, src, masked):
        st = src[...]
        if masked:
            r = lax.broadcasted_iota(jnp.int32, st.shape, 0)
            c = lax.broadcasted_iota(jnp.int32, st.shape, 1)
            st = jnp.where(r <= jnp.where(c >= tq, c - tq, c), st, NEG_BIG)
        m_old = m_refs[u][...]
        m_new = jnp.maximum(m_old, jnp.max(st, axis=0, keepdims=True))
        alpha = jnp.exp(m_old - m_new)
        pt = jnp.exp(st - m_new).astype(vxt_ref.dtype)
        pv = jnp.dot(vxt_ref[u, j], pt, preferred_element_type=F32)
        acc_refs[u][...] = alpha * acc_refs[u][...] + pv
        m_refs[u][...] = m_new

    def half(j, cur, nxt):
        for u in range(hp):
            scores(j + 1, u, nxt[u])
            softmax_pv(j, u, cur[u], False)

    for u in range(hp):
        scores(0, u, sa_refs[u])

    def pair(t, carry):
        half(2 * t, sa_refs, sb_refs)
        half(2 * t + 1, sb_refs, sa_refs)
        return carry

    lax.fori_loop(0, i // 2, pair, 0)

    @pl.when(i % 2 == 1)
    def _():
        half(i - 1, sa_refs, sb_refs)
        for u in range(hp):
            softmax_pv(i, u, sb_refs[u], True)

    @pl.when(i % 2 == 0)
    def _():
        for u in range(hp):
            softmax_pv(i, u, sa_refs[u], True)

    for u in range(hp):
        acc = acc_refs[u][...]
        o12 = acc[0:vd] * (1.0 / acc[vd:vd + 1])
        ot = o12[:, 0:tq] - lam * o12[:, tq:2 * tq]
        msq = jnp.mean(ot * ot, axis=0, keepdims=True)
        o = (ot * lax.rsqrt(msq + eps)).T
        o_ref[:, u * vd:(u + 1) * vd] = (o * sw_ref[...] * (1.0 - lam_init)).astype(o_ref.dtype)


def _diff_attn(proj, lq1, lk1, lq2, lk2, subln_w, B, S, d_conv, d_attn, lam_init):
    T = proj.shape[0]
    vd = subln_w.shape[0]
    hd = lq1.shape[0]
    assert vd == LANES and 2 * hd == vd
    H = d_attn // vd
    hp = 4
    ones_rows = 16
    tq = min(256, S)
    nq = S // tq
    assert H % hp == 0
    qc = 2 * d_conv // (hp * vd)
    kc = qc + H // hp
    vc = kc + H // hp
    lspec = pl.BlockSpec((1, hd), lambda b, h, i: (0, 0))
    return pl.pallas_call(
        functools.partial(_attn_kernel, tq=tq, hd=hd, hp=hp, ones_rows=ones_rows, lam_init=lam_init, eps=SUBLN_EPS),
        grid=(B, H // hp, nq),
        in_specs=[lspec, lspec, lspec, lspec,
                  pl.BlockSpec((tq, hp * vd), lambda b, h, i: (b * nq + i, qc + h)),
                  pl.BlockSpec((S, hp * vd), lambda b, h, i: (b, kc + h)),
                  pl.BlockSpec((S, hp * vd), lambda b, h, i: (b, vc + h)),
                  pl.BlockSpec((1, vd), lambda b, h, i: (0, 0))],
        out_specs=pl.BlockSpec((tq, hp * vd), lambda b, h, i: (b * nq + i, h)),
        out_shape=jax.ShapeDtypeStruct((T, d_attn), BF16),
        scratch_shapes=[pltpu.VMEM((hp, S // tq, vd + ones_rows, tq), BF16),
                        pltpu.VMEM((hp, 2 * tq, vd), BF16)]
                       + [pltpu.VMEM((vd + ones_rows, 2 * tq), F32) for _ in range(hp)]
                       + [pltpu.VMEM((1, 2 * tq), F32) for _ in range(hp)]
                       + [pltpu.VMEM((tq, 2 * tq), F32) for _ in range(2 * hp)],
        compiler_params=_cparams(("arbitrary", "arbitrary", "arbitrary"), 40),
        name="diff_attn",
    )(lq1.reshape(1, hd), lk1.reshape(1, hd), lq2.reshape(1, hd), lk2.reshape(1, hd),
      proj, proj, proj, subln_w.reshape(1, vd))


def _pack_bf16_pairs(v):
    n = v.shape[1] // 2
    bits = lax.bitcast_convert_type(v.astype(BF16).astype(F32), jnp.uint32)
    return (bits[:, 0:n] >> 16) | (bits[:, n:2 * n] & jnp.uint32(0xFFFF0000))


def _unpack_bf16_pairs(w):
    lo = lax.bitcast_convert_type(w << 16, F32)
    hi = lax.bitcast_convert_type(w & jnp.uint32(0xFFFF0000), F32)
    return lo, hi


def _outproj_kernel(x_ref, yc_ref, ya_ref, wo_ref, fw_ref, wr_ref, br_ref, h_ref, lg_ref, ug_ref, *, eps, nsub):
    dc = yc_ref.shape[1]
    sub = x_ref.shape[0] // nsub
    mixes = []
    for t in range(nsub):
        rows = slice(t * sub, (t + 1) * sub)
        mixes.append(jnp.dot(yc_ref[rows, :], wo_ref[0:dc, :], preferred_element_type=F32)
                     + jnp.dot(ya_ref[rows, :], wo_ref[dc:, :], preferred_element_type=F32))
    for t in range(nsub):
        rows = slice(t * sub, (t + 1) * sub)
        h = x_ref[rows, :] + mixes[t]
        h_ref[rows, :] = h
        ms = jnp.mean(h * h, axis=-1, keepdims=True)
        un = h * lax.rsqrt(ms + eps) * fw_ref[...]
        hi = un.astype(BF16)
        lo = (un - hi.astype(F32)).astype(BF16)
        hh = jnp.dot(hi, wr_ref[...], preferred_element_type=F32)
        lh = jnp.dot(lo, wr_ref[:, 0:LANES], preferred_element_type=F32)
        lg_ref[rows, :] = hh[:, 0:LANES] + hh[:, LANES:2 * LANES] + lh + br_ref[...]
        ug_ref[rows, :] = _pack_bf16_pairs(un)


def _outproj(x2, y_conv, y_attn, wo_bf, ffn_w, wr_hl, br):
    T, D = x2.shape
    dc = y_conv.shape[1]
    da = y_attn.shape[1]
    tm = min(512, T)
    nsub = 2 if tm % 512 == 0 else 1
    return pl.pallas_call(
        functools.partial(_outproj_kernel, eps=RMS_EPS, nsub=nsub),
        grid=(T // tm,),
        in_specs=[pl.BlockSpec((tm, D), lambda i: (i, 0)),
                  pl.BlockSpec((tm, dc), lambda i: (i, 0)),
                  pl.BlockSpec((tm, da), lambda i: (i, 0)),
                  pl.BlockSpec((dc + da, D), lambda i: (0, 0), pipeline_mode=pl.Buffered(1)),
                  pl.BlockSpec((1, D), lambda i: (0, 0)),
                  pl.BlockSpec((D, 2 * LANES), lambda i: (0, 0)),
                  pl.BlockSpec((1, LANES), lambda i: (0, 0))],
        out_specs=[pl.BlockSpec((tm, D), lambda i: (i, 0)),
                   pl.BlockSpec((tm, LANES), lambda i: (i, 0)),
                   pl.BlockSpec((tm, D // 2), lambda i: (i, 0))],
        out_shape=[jax.ShapeDtypeStruct((T, D), F32), jax.ShapeDtypeStruct((T, LANES), F32),
                   jax.ShapeDtypeStruct((T, D // 2), jnp.uint32)],
        compiler_params=_cparams(("arbitrary",), 56),
        name="outproj",
    )(x2, y_conv, y_attn, wo_bf, ffn_w, wr_hl, br)


def _route_kernel(lg_ref, info_ref, cnt_ref, *, ng, epg):
    i = pl.program_id(0)
    lg = lg_ref[...]
    tm = lg.shape[0]
    lane = lax.broadcasted_iota(jnp.int32, lg.shape, 1)
    lanef = lane.astype(F32)
    ne = ng * epg

    def first_argmax(vals):
        mx = jnp.max(vals, axis=-1, keepdims=True)
        idx = jnp.min(jnp.where(vals == mx, lanef, float(LANES)), axis=-1, keepdims=True)
        return mx, idx

    gmask = lane < ng
    gl = jnp.where(gmask, lg, NEG_BIG)
    gmax, gsel = first_argmax(gl)
    gsum = jnp.sum(jnp.where(gmask, jnp.exp(gl - gmax), 0.0), axis=-1, keepdims=True)
    g_w = 1.0 / gsum
    lo = gsel * epg + ng
    emask = (lanef >= lo) & (lanef < lo + epg)
    el = jnp.where(emask, lg, NEG_BIG)
    v1, i1 = first_argmax(el)
    el2 = jnp.where(lanef == i1, NEG_BIG, el)
    v2, i2 = first_argmax(el2)
    e2 = jnp.exp(v2 - v1)
    p1 = 1.0 / (1.0 + e2)
    gate1 = g_w * p1
    gate2 = g_w * (e2 * p1)

    oh1 = lanef == i1
    oh2 = lanef == i2
    cmat = jnp.where(oh1 | oh2, 1.0, 0.0).astype(BF16)
    r = lax.broadcasted_iota(jnp.int32, (tm, tm), 0)
    c = lax.broadcasted_iota(jnp.int32, (tm, tm), 1)
    tri = jnp.where(c < r, 1.0, 0.0).astype(BF16)

    @pl.when(i == 0)
    def _():
        cnt_ref[...] = jnp.zeros(cnt_ref.shape, F32)

    carry = cnt_ref[0:1, :]
    prefix = jnp.dot(tri, cmat, preferred_element_type=F32) + carry
    rank1 = jnp.sum(jnp.where(oh1, prefix, 0.0), axis=-1, keepdims=True)
    rank2 = jnp.sum(jnp.where(oh2, prefix, 0.0), axis=-1, keepdims=True)
    cnt_ref[...] = jnp.broadcast_to(carry + jnp.sum(cmat.astype(F32), axis=0, keepdims=True), cnt_ref.shape)

    info = jnp.where(lane == 0, i1 - ng,
           jnp.where(lane == 1, i2 - ng,
           jnp.where(lane == 2, gate1,
           jnp.where(lane == 3, gate2,
           jnp.where(lane == 4, rank1,
           jnp.where(lane == 5, rank2, 0.0))))))
    info_ref[...] = info


def _route(logits, ng, epg):
    T = logits.shape[0]
    tm = min(512, T)
    return pl.pallas_call(
        functools.partial(_route_kernel, ng=ng, epg=epg),
        grid=(T // tm,),
        in_specs=[pl.BlockSpec((tm, LANES), lambda i: (i, 0))],
        out_specs=[pl.BlockSpec((tm, LANES), lambda i: (i, 0)),
                   pl.BlockSpec((8, LANES), lambda i: (0, 0))],
        out_shape=[jax.ShapeDtypeStruct((T, LANES), F32), jax.ShapeDtypeStruct((8, LANES), F32)],
        compiler_params=_cparams(("arbitrary",), 32),
        name="route",
    )(logits)


ROW_UNROLL = 8


def _dispatch_kernel(dest_ref, pend_ref, ug_ref, xs_hbm, ring, zbuf, sems, zsem, *, tm, blk, ne):
    i = pl.program_id(0)
    n = pl.num_programs(0)
    slot = i % 2

    def seg_tail(e):
        end = pend_ref[e]
        start = jnp.where(e == 0, 0, pend_ref[jnp.maximum(e - 1, 0)])
        tail = pl.multiple_of(jnp.maximum(end - blk, 0), blk)
        return pltpu.make_async_copy(zbuf, xs_hbm.at[pl.ds(tail, blk)], zsem), end > start

    @pl.when(i == 0)
    def _():
        zbuf[...] = jnp.zeros(zbuf.shape, zbuf.dtype)

        def zstart(e, carry):
            cp, nonempty = seg_tail(e)

            @pl.when(nonempty)
            def _():
                cp.start()
            return carry

        def zwait(e, carry):
            cp, nonempty = seg_tail(e)

            @pl.when(nonempty)
            def _():
                cp.wait()
            return carry

        def spare(b):
            row = pl.multiple_of(b * blk, blk)
            return pltpu.make_async_copy(zbuf, xs_hbm.at[pl.ds(row, blk)], zsem), row >= pend_ref[ne - 1]

        def sstart(b, carry):
            cp, unused = spare(b)

            @pl.when(unused)
            def _():
                cp.start()
            return carry

        def swait(b, carry):
            cp, unused = spare(b)

            @pl.when(unused)
            def _():
                cp.wait()
            return carry

        nblk = xs_hbm.shape[0] // blk
        lax.fori_loop(0, ne, zstart, 0)
        lax.fori_loop(0, nblk, sstart, 0)
        lax.fori_loop(0, ne, zwait, 0)
        lax.fori_loop(0, nblk, swait, 0)

    def drain(s):
        for _ in range(TOP_K):
            pltpu.make_async_copy(ring.at[s], xs_hbm.at[pl.ds(0, tm)], sems.at[s]).wait()

    @pl.when(i >= 2)
    def _():
        drain(slot)

    ring[slot] = ug_ref[...]

    def issue(c, carry):
        r0 = pl.multiple_of(c * ROW_UNROLL, ROW_UNROLL)
        for s in range(ROW_UNROLL):
            for k in range(TOP_K):
                d = dest_ref[(i * tm + r0 + s) * TOP_K + k]
                pltpu.make_async_copy(ring.at[slot, pl.ds(r0 + s, 1)], xs_hbm.at[pl.ds(d, 1)],
                                      sems.at[slot]).start(priority=k % 2)
        return carry

    lax.fori_loop(0, tm // ROW_UNROLL, issue, 0)

    @pl.when(i == n - 1)
    def _():
        drain(slot)

        @pl.when(n >= 2)
        def _():
            drain(1 - slot)


def _dispatch(dest_flat, pad_ends, ug, nrows, tm, blk):
    T, W = ug.shape
    ne = pad_ends.shape[0]
    grid_spec = pltpu.PrefetchScalarGridSpec(
        num_scalar_prefetch=2,
        grid=(T // tm,),
        in_specs=[pl.BlockSpec((tm, W), lambda i, d, p: (i, 0))],
        out_specs=pl.BlockSpec(memory_space=pl.ANY),
        scratch_shapes=[pltpu.VMEM((2, tm, W), ug.dtype),
                        pltpu.VMEM((blk, W), ug.dtype),
                        pltpu.SemaphoreType.DMA((2,)),
                        pltpu.SemaphoreType.DMA(())],
    )
    return pl.pallas_call(
        functools.partial(_dispatch_kernel, tm=tm, blk=blk, ne=ne),
        grid_spec=grid_spec,
        out_shape=jax.ShapeDtypeStruct((nrows, W), ug.dtype),
        compiler_params=_cparams(("arbitrary",), 32),
        name="dispatch",
    )(dest_flat, pad_ends, ug)


WEIGHT_DMA_PRIORITY = (0, 1, 1)


def _experts_kernel(bexp_ref, first_ref, slot_ref, next_ref, nused_ref, x_ref, w1_hbm, w3_hbm, w2_hbm, y_ref,
                    w1b, w3b, w2b, wsems):
    i = pl.program_id(0)
    nused = nused_ref[0]

    def weight_copies(e, s):
        return (pltpu.make_async_copy(w1_hbm.at[e], w1b.at[s], wsems.at[s, 0]),
                pltpu.make_async_copy(w3_hbm.at[e], w3b.at[s], wsems.at[s, 1]),
                pltpu.make_async_copy(w2_hbm.at[e], w2b.at[s], wsems.at[s, 2]))

    @pl.when(jnp.logical_and(i == 0, nused > 0))
    def _():
        for n, cp in enumerate(weight_copies(bexp_ref[0], 0)):
            cp.start(priority=WEIGHT_DMA_PRIORITY[n])

    @pl.when(i < nused)
    def _():
        s = slot_ref[i]

        @pl.when(first_ref[i] == 1)
        def _():
            @pl.when(next_ref[i] >= 0)
            def _():
                for n, cp in enumerate(weight_copies(next_ref[i], 1 - s)):
                    cp.start(priority=WEIGHT_DMA_PRIORITY[n])
            for cp in weight_copies(bexp_ref[i], s):
                cp.wait()

        x_lo, x_hi = _unpack_bf16_pairs(x_ref[...])
        half = x_lo.shape[1]
        a = (jnp.dot(x_lo, w1b[s, 0:half, :], preferred_element_type=F32)
             + jnp.dot(x_hi, w1b[s, half:2 * half, :], preferred_element_type=F32))
        b = (jnp.dot(x_lo, w3b[s, 0:half, :], preferred_element_type=F32)
             + jnp.dot(x_hi, w3b[s, half:2 * half, :], preferred_element_type=F32))
        hdn = a * jax.nn.sigmoid(a) * b
        y = jnp.dot(hdn, w2b[s], preferred_element_type=F32)
        y_ref[...] = _pack_bf16_pairs(y)

    @pl.when(i >= nused)
    def _():
        y_ref[...] = jnp.zeros(y_ref.shape, y_ref.dtype)


def _experts(bexp, first, slot, nxt, nused, xs, w1, w3, w2, blk):
    P, W = xs.shape
    E, D, Fh = w1.shape
    nblk = P // blk
    smap = lambda i, *_: (i, 0)
    grid_spec = pltpu.PrefetchScalarGridSpec(
        num_scalar_prefetch=5,
        grid=(nblk,),
        in_specs=[pl.BlockSpec((blk, W), smap),
                  pl.BlockSpec(memory_space=pl.ANY),
                  pl.BlockSpec(memory_space=pl.ANY),
                  pl.BlockSpec(memory_space=pl.ANY)],
        out_specs=pl.BlockSpec((blk, W), smap),
        scratch_shapes=[pltpu.VMEM((2, D, Fh), w1.dtype),
                        pltpu.VMEM((2, D, Fh), w3.dtype),
                        pltpu.VMEM((2, Fh, D), w2.dtype),
                        pltpu.SemaphoreType.DMA((2, 3))],
    )
    return pl.pallas_call(
        _experts_kernel,
        grid_spec=grid_spec,
        out_shape=jax.ShapeDtypeStruct((P, W), xs.dtype),
        compiler_params=_cparams(("arbitrary",), 48),
        name="experts",
    )(bexp, first, slot, nxt, nused, xs, w1, w3, w2)


def _row_gather(src_hbm, idx_ref, base, dst, sem, n):
    def body(c, carry):
        r0 = pl.multiple_of(c * ROW_UNROLL, ROW_UNROLL)
        for s in range(ROW_UNROLL):
            pltpu.make_async_copy(src_hbm.at[pl.ds(idx_ref[base + r0 + s], 1)], dst.at[pl.ds(r0 + s, 1)],
                                  sem).start(priority=s % 2)
        return carry
    lax.fori_loop(0, n // ROW_UNROLL, body, 0)


def _row_gather_wait(src_hbm, dst, sem, n):
    pltpu.make_async_copy(src_hbm.at[pl.ds(0, n)], dst, sem).wait()


def _combine_kernel(dest_ref, h_ref, info_ref, y_hbm, fw_ref, o_ref, ybuf, sems, *, tm, eps):
    i = pl.program_id(0)
    n = pl.num_programs(0)
    slot = i % 2
    nrow = TOP_K * tm

    @pl.when(i == 0)
    def _():
        _row_gather(y_hbm, dest_ref, 0, ybuf.at[0], sems.at[0], nrow)

    @pl.when(i + 1 < n)
    def _():
        _row_gather(y_hbm, dest_ref, (i + 1) * nrow, ybuf.at[1 - slot], sems.at[1 - slot], nrow)

    _row_gather_wait(y_hbm, ybuf.at[slot], sems.at[slot], nrow)
    info = info_ref[...]
    g1 = info[:, 2:3]
    g2 = info[:, 3:4]
    y1_lo, y1_hi = _unpack_bf16_pairs(ybuf[slot, 0:tm, :])
    y2_lo, y2_hi = _unpack_bf16_pairs(ybuf[slot, tm:nrow, :])
    half = y1_lo.shape[1]
    h_lo = h_ref[:, 0:half] + (g1 * y1_lo + g2 * y2_lo)
    h_hi = h_ref[:, half:2 * half] + (g1 * y1_hi + g2 * y2_hi)
    ms = (jnp.sum(h_lo * h_lo, axis=-1, keepdims=True) + jnp.sum(h_hi * h_hi, axis=-1, keepdims=True)) / (2 * half)
    r = lax.rsqrt(ms + eps)
    o_ref[:, 0:half] = h_lo * r * fw_ref[:, 0:half]
    o_ref[:, half:2 * half] = h_hi * r * fw_ref[:, half:2 * half]


def _combine(dest_km, h, info, yb, final_w, tm):
    T, D = h.shape
    W = yb.shape[1]
    grid_spec = pltpu.PrefetchScalarGridSpec(
        num_scalar_prefetch=1,
        grid=(T // tm,),
        in_specs=[pl.BlockSpec((tm, D), lambda i, d: (i, 0)),
                  pl.BlockSpec((tm, LANES), lambda i, d: (i, 0)),
                  pl.BlockSpec(memory_space=pl.ANY),
                  pl.BlockSpec((1, D), lambda i, d: (0, 0))],
        out_specs=pl.BlockSpec((tm, D), lambda i, d: (i, 0)),
        scratch_shapes=[pltpu.VMEM((2, TOP_K * tm, W), yb.dtype),
                        pltpu.SemaphoreType.DMA((2,))],
    )
    return pl.pallas_call(
        functools.partial(_combine_kernel, tm=tm, eps=RMS_EPS),
        grid_spec=grid_spec,
        out_shape=jax.ShapeDtypeStruct((T, D), F32),
        compiler_params=_cparams(("arbitrary",), 48),
        name="combine",
    )(dest_km, h, info, yb, final_w)


def _layer(h_in, l, B, S, mix_norm_w, w_in, conv_dw_w, conv_dw_b, conv_ln_w, conv_ln_b,
           lam_q1, lam_k1, lam_q2, lam_k2, attn_subln_w, w_out, ffn_norm_w,
           w_group, b_group, w_expert_gate, b_expert_gate, w1, w3, w2):
    T, D = h_in.shape
    d_conv = conv_dw_w.shape[1]
    d_attn = (w_in.shape[1] - 2 * d_conv) // 3
    ng = w_group.shape[1]
    ne = w_expert_gate.shape[1]
    epg = ne // ng
    assert ng + ne <= LANES
    lam_init = 0.8 - 0.6 * math.exp(-0.3 * l)

    proj = _norm_inproj(h_in, mix_norm_w.reshape(1, D), w_in.astype(BF16))
    y_conv = _conformer(proj, conv_dw_w, conv_dw_b, conv_ln_w, conv_ln_b, B, S)
    y_attn = _diff_attn(proj, lam_q1, lam_k1, lam_q2, lam_k2, attn_subln_w, B, S, d_conv, d_attn, lam_init)

    wr = jnp.concatenate([w_group, w_expert_gate, jnp.zeros((D, LANES - ng - ne), F32)], axis=1)
    wr_hi = wr.astype(BF16)
    wr_lo = (wr - wr_hi.astype(F32)).astype(BF16)
    br = jnp.concatenate([b_group, b_expert_gate.reshape(-1), jnp.zeros((LANES - ng - ne,), F32)]).reshape(1, LANES)
    wr_hl = jnp.concatenate([wr_hi, wr_lo], axis=1)
    h, logits, ug = _outproj(h_in, y_conv, y_attn, w_out.astype(BF16), ffn_norm_w.reshape(1, D), wr_hl, br)

    info, cnt = _route(logits, ng, epg)

    blk = 256
    tmd = min(256, T)
    A = T * TOP_K
    nblk = (A + ne * (blk - 1) + blk - 1) // blk
    i32 = jnp.int32
    counts = cnt[0, ng:ng + ne].astype(i32)
    padded = (counts + blk - 1) // blk * blk
    pad_ends = jnp.cumsum(padded).astype(i32)
    pad_starts = pad_ends - padded
    eid = info[:, 0:TOP_K].astype(i32)
    rank = info[:, 4:4 + TOP_K].astype(i32)
    onehot = eid[:, :, None] == jnp.arange(ne, dtype=i32)[None, None, :]
    dest = jnp.sum(jnp.where(onehot, pad_starts[None, None, :], 0), axis=-1).astype(i32) + rank
    nused = (pad_ends[-1] // blk).astype(i32)
    bpos = jnp.arange(nblk, dtype=i32)
    bexp = jnp.minimum(jnp.searchsorted(pad_ends, jnp.minimum(bpos, nused - 1) * blk, side="right"), ne - 1).astype(i32)
    first = ((bpos < nused) & ((bpos == 0) | (bexp != jnp.roll(bexp, 1)))).astype(i32)
    slot = ((jnp.cumsum(first) - 1) % 2).astype(i32)
    used_idx = jnp.where(padded > 0, jnp.arange(ne, dtype=i32), ne)
    suffix_min = lax.cummin(used_idx, reverse=True)
    next_used = jnp.concatenate([suffix_min[1:], jnp.full((1,), ne, i32)])
    next_used = jnp.where(next_used >= ne, -1, next_used)
    nxt = next_used[bexp].astype(i32)

    xs = _dispatch(dest.reshape(-1), pad_ends, ug, nblk * blk, tmd, blk)
    yb = _experts(bexp, first, slot, nxt, nused.reshape(1), xs, w1, w3, w2, blk)
    return h, info, dest, yb


def kernel(x, mix_norm_w, w_in, conv_dw_w, conv_dw_b, conv_ln_w, conv_ln_b, lam_q1, lam_k1, lam_q2, lam_k2,
           attn_subln_w, w_out, ffn_norm_w, w_group, b_group, w_expert_gate, b_expert_gate, w1, w3, w2,
           final_norm_w):
    B, S, D = x.shape
    depth = w_in.shape[0]
    assert depth == 1
    T = B * S
    tmc = min(256, T)
    h = x.reshape(T, D)
    for l in range(depth):
        h, info, dest, yb = _layer(
            h, l, B, S, mix_norm_w[l], w_in[l], conv_dw_w[l], conv_dw_b[l], conv_ln_w[l], conv_ln_b[l],
            lam_q1[l], lam_k1[l], lam_q2[l], lam_k2[l], attn_subln_w[l], w_out[l], ffn_norm_w[l],
            w_group[l], b_group[l], w_expert_gate[l], b_expert_gate[l], w1[l], w3[l], w2[l])
        dest_km = dest.reshape(T // tmc, tmc, TOP_K).transpose(0, 2, 1).reshape(-1)
        h = _combine(dest_km, h, info, yb, final_norm_w.reshape(1, D), tmc)
    return h.reshape(B, S, D)
```

```python
import functools
import math

import jax
import jax.numpy as jnp
from jax import lax
from jax.experimental import pallas as pl
from jax.experimental.pallas import tpu as pltpu

F32 = jnp.float32
BF16 = jnp.bfloat16

RMS_EPS = 1e-6
SUBLN_EPS = 1e-5
LN_EPS = 1e-5
TOP_K = 2
LANES = 128
SUBLANES = 8
NEG_BIG = -1e30
MIB = 1024 * 1024


def _cparams(sem, vmem_mib):
    return pltpu.CompilerParams(dimension_semantics=sem, vmem_limit_bytes=vmem_mib * MIB)


def _norm_inproj_kernel(x_ref, nw_ref, w_ref, o_ref, u_ref, *, eps, rc):
    @pl.when(pl.program_id(1) == 0)
    def _():
        def body(r, carry):
            rows = pl.ds(pl.multiple_of(r * rc, rc), rc)
            x = x_ref[rows, :]
            ms = jnp.mean(x * x, axis=-1, keepdims=True)
            u_ref[rows, :] = (x * lax.rsqrt(ms + eps) * nw_ref[...]).astype(u_ref.dtype)
            return carry
        lax.fori_loop(0, x_ref.shape[0] // rc, body, 0)

    o_ref[...] = jnp.dot(u_ref[...], w_ref[...], preferred_element_type=F32).astype(o_ref.dtype)


def _norm_inproj(x2, nw, w_bf):
    T, D = x2.shape
    N = w_bf.shape[1]
    tm = min(1024, T)
    tn = 1024
    return pl.pallas_call(
        functools.partial(_norm_inproj_kernel, eps=RMS_EPS, rc=128),
        grid=(T // tm, N // tn),
        in_specs=[pl.BlockSpec((tm, D), lambda i, j: (i, 0)),
                  pl.BlockSpec((1, D), lambda i, j: (0, 0)),
                  pl.BlockSpec((D, tn), lambda i, j: (0, j))],
        out_specs=pl.BlockSpec((tm, tn), lambda i, j: (i, j)),
        out_shape=jax.ShapeDtypeStruct((T, N), BF16),
        scratch_shapes=[pltpu.VMEM((tm, D), BF16)],
        compiler_params=_cparams(("arbitrary", "arbitrary"), 48),
        name="norm_inproj",
    )(x2, nw, w_bf)


def _conv_kernel(a_ref, g_ref, w_ref, b_ref, lw_ref, lb_ref, o_ref, ubuf, cbuf, sh, *, ts, kw, halo, eps):
    s = pl.program_id(1)
    C = a_ref.shape[1]

    @pl.when(s == 0)
    def _():
        ubuf[0:halo, :] = jnp.zeros((halo, C), F32)

    @pl.when(s > 0)
    def _():
        ubuf[0:halo, :] = ubuf[ts:ts + halo, :]

    rg = 64
    for r0 in range(0, ts, rg):
        a = a_ref[r0:r0 + rg, :].astype(F32)
        g = g_ref[r0:r0 + rg, :].astype(F32)
        ubuf[halo + r0:halo + r0 + rg, :] = a * jax.nn.sigmoid(g)

    nsh = sh.shape[1]
    for r in range(1, SUBLANES):
        for i0 in range(0, nsh, rg):
            n = min(rg, nsh - i0)
            sh[r - 1, i0:i0 + n, :] = ubuf[i0 + r:i0 + r + n, :]

    off = halo - (kw - 1)
    rc = 64

    def conv_ln_rows(ri, carry):
        r0 = pl.multiple_of(ri * rc, rc)
        for c in range(C // LANES):
            cs = slice(c * LANES, (c + 1) * LANES)
            acc = jnp.broadcast_to(b_ref[0:1, cs], (rc // SUBLANES, SUBLANES, LANES))
            for k in range(kw):
                q, r = divmod(off + k, SUBLANES)
                rows = pl.ds(r0 + q * SUBLANES, rc)
                tap = ubuf[rows, cs] if r == 0 else sh[r - 1, rows, cs]
                acc = acc + w_ref[k, :, cs] * tap.reshape(rc // SUBLANES, SUBLANES, LANES)
            cbuf[pl.ds(r0, rc), cs] = acc.reshape(rc, LANES)
        cv = cbuf[pl.ds(r0, rc), :]
        mu = jnp.mean(cv, axis=-1, keepdims=True)
        d = cv - mu
        var = jnp.mean(d * d, axis=-1, keepdims=True)
        un = d * lax.rsqrt(var + eps) * lw_ref[...] + lb_ref[...]
        o_ref[pl.ds(r0, rc), :] = (un * jax.nn.sigmoid(un)).astype(o_ref.dtype)
        return carry

    lax.fori_loop(0, ts // rc, conv_ln_rows, 0)


def _conformer(proj, dw_w, dw_b, ln_w, ln_b, B, S):
    T = proj.shape[0]
    kw, C = dw_w.shape
    ts = min(256, S)
    halo = 32
    assert kw - 1 <= halo and S % ts == 0 and ts >= halo
    ns = S // ts
    wp = jnp.broadcast_to(dw_w[:, None, :], (kw, SUBLANES, C))
    return pl.pallas_call(
        functools.partial(_conv_kernel, ts=ts, kw=kw, halo=halo, eps=LN_EPS),
        grid=(B, ns),
        in_specs=[pl.BlockSpec((ts, C), lambda b, s: (b * ns + s, 0)),
                  pl.BlockSpec((ts, C), lambda b, s: (b * ns + s, 1)),
                  pl.BlockSpec((kw, SUBLANES, C), lambda b, s: (0, 0, 0)),
                  pl.BlockSpec((1, C), lambda b, s: (0, 0)),
                  pl.BlockSpec((1, C), lambda b, s: (0, 0)),
                  pl.BlockSpec((1, C), lambda b, s: (0, 0))],
        out_specs=pl.BlockSpec((ts, C), lambda b, s: (b * ns + s, 0)),
        out_shape=jax.ShapeDtypeStruct((T, C), BF16),
        scratch_shapes=[pltpu.VMEM((ts + halo, C), F32), pltpu.VMEM((ts, C), F32),
                        pltpu.VMEM((SUBLANES - 1, ts + halo - SUBLANES, C), F32)],
        compiler_params=_cparams(("arbitrary", "arbitrary"), 32),
        name="conformer",
    )(proj, proj, wp, dw_b.reshape(1, C), ln_w.reshape(1, C), ln_b.reshape(1, C))


def _attn_kernel(lq1_ref, lk1_ref, lq2_ref, lk2_ref, q_ref, k_ref, v_ref, sw_ref, o_ref,
                 vxt_ref, qq_ref, *scr, tq, hd, hp, ones_rows, lam_init, eps):
    i = pl.program_id(2)
    vd = 2 * hd
    nkb = v_ref.shape[0] // tq
    acc_refs, m_refs, sa_refs, sb_refs = (scr[n * hp:(n + 1) * hp] for n in range(4))

    @pl.when(i == 0)
    def _():
        for u in range(hp):
            for jb in range(nkb):
                vblk = v_ref[jb * tq:(jb + 1) * tq, u * vd:(u + 1) * vd].astype(F32)
                vxt_ref[u, jb, 0:vd, :] = vblk.T.astype(vxt_ref.dtype)
                vxt_ref[u, jb, vd:vd + ones_rows, :] = jnp.ones((ones_rows, tq), vxt_ref.dtype)

    lam = (jnp.exp(jnp.sum(lq1_ref[...] * lk1_ref[...], axis=-1, keepdims=True))
           - jnp.exp(jnp.sum(lq2_ref[...] * lk2_ref[...], axis=-1, keepdims=True)) + lam_init)

    for u in range(hp):
        q = q_ref[:, u * vd:(u + 1) * vd]
        qs = q * jnp.asarray(hd ** -0.5, q.dtype)
        lane = lax.broadcasted_iota(jnp.int32, q.shape, 1)
        zero = jnp.zeros_like(qs)
        qq_ref[u, 0:tq, :] = jnp.where(lane < hd, qs, zero)
        qq_ref[u, tq:2 * tq, :] = jnp.where(lane >= hd, qs, zero)
        acc_refs[u][...] = jnp.zeros(acc_refs[u].shape, F32)
        m_refs[u][...] = jnp.full(m_refs[u].shape, NEG_BIG, F32)

    def scores(j, u, dst):
        rows = pl.ds(pl.multiple_of(j * tq, tq), tq)
        kb = k_ref[rows, u * vd:(u + 1) * vd]
        dst[...] = lax.dot_general(kb, qq_ref[u], (((1,), (1,)), ((), ())), preferred_element_type=F32)

    def softmax_pv(j, u, src, masked):
        st = src[...]
        if masked:
            r = lax.broadcasted_iota(jnp.int32, st.shape, 0)
            c = lax.broadcasted_iota(jnp.int32, st.shape, 1)
            st = jnp.where(r <= jnp.where(c >= tq, c - tq, c), st, NEG_BIG)
        m_old = m_refs[u][...]
        m_new = jnp.maximum(m_old, jnp.max(st, axis=0, keepdims=True))
        alpha = jnp.exp(m_old - m_new)
        pt = jnp.exp(st - m_new).astype(vxt_ref.dtype)
        pv = jnp.dot(vxt_ref[u, j], pt, preferred_element_type=F32)
        acc_refs[u][...] = alpha * acc_refs[u][...] + pv
        m_refs[u][...] = m_new

    def half(j, cur, nxt):
        for u in range(hp):
            scores(j + 1, u, nxt[u])
            softmax_pv(j, u, cur[u], False)

    for u in range(hp):
        scores(0, u, sa_refs[u])

    def pair(t, carry):
        half(2 * t, sa_refs, sb_refs)
        half(2 * t + 1, sb_refs, sa_refs)
        return carry

    lax.fori_loop(0, i // 2, pair, 0)

    @pl.when(i % 2 == 1)
    def _():
        half(i - 1, sa_refs, sb_refs)
        for u in range(hp):
            softmax_pv(i, u, sb_refs[u], True)

    @pl.when(i % 2 == 0)
    def _():
        for u in range(hp):
            softmax_pv(i, u, sa_refs[u], True)

    for u in range(hp):
        acc = acc_refs[u][...]
        o12 = acc[0:vd] * (1.0 / acc[vd:vd + 1])
        ot = o12[:, 0:tq] - lam * o12[:, tq:2 * tq]
        msq = jnp.mean(ot * ot, axis=0, keepdims=True)
        o = (ot * lax.rsqrt(msq + eps)).T
        o_ref[:, u * vd:(u + 1) * vd] = (o * sw_ref[...] * (1.0 - lam_init)).astype(o_ref.dtype)


def _diff_attn(proj, lq1, lk1, lq2, lk2, subln_w, B, S, d_conv, d_attn, lam_init):
    T = proj.shape[0]
    vd = subln_w.shape[0]
    hd = lq1.shape[0]
    assert vd == LANES and 2 * hd == vd
    H = d_attn // vd
    hp = 4
    ones_rows = 16
    tq = min(256, S)
    nq = S // tq
    assert H % hp == 0
    qc = 2 * d_conv // (hp * vd)
    kc = qc + H // hp
    vc = kc + H // hp
    lspec = pl.BlockSpec((1, hd), lambda b, h, i: (0, 0))
    return pl.pallas_call(
        functools.partial(_attn_kernel, tq=tq, hd=hd, hp=hp, ones_rows=ones_rows, lam_init=lam_init, eps=SUBLN_EPS),
        grid=(B, H // hp, nq),
        in_specs=[lspec, lspec, lspec, lspec,
                  pl.BlockSpec((tq, hp * vd), lambda b, h, i: (b * nq + i, qc + h)),
                  pl.BlockSpec((S, hp * vd), lambda b, h, i: (b, kc + h)),
                  pl.BlockSpec((S, hp * vd), lambda b, h, i: (b, vc + h)),
                  pl.BlockSpec((1, vd), lambda b, h, i: (0, 0))],
        out_specs=pl.BlockSpec((tq, hp * vd), lambda b, h, i: (b * nq + i, h)),
        out_shape=jax.ShapeDtypeStruct((T, d_attn), BF16),
        scratch_shapes=[pltpu.VMEM((hp, S // tq, vd + ones_rows, tq), BF16),
                        pltpu.VMEM((hp, 2 * tq, vd), BF16)]
                       + [pltpu.VMEM((vd + ones_rows, 2 * tq), F32) for _ in range(hp)]
                       + [pltpu.VMEM((1, 2 * tq), F32) for _ in range(hp)]
                       + [pltpu.VMEM((tq, 2 * tq), F32) for _ in range(2 * hp)],
        compiler_params=_cparams(("arbitrary", "arbitrary", "arbitrary"), 40),
        name="diff_attn",
    )(lq1.reshape(1, hd), lk1.reshape(1, hd), lq2.reshape(1, hd), lk2.reshape(1, hd),
      proj, proj, proj, subln_w.reshape(1, vd))


def _pack_bf16_pairs(v):
    n = v.shape[1] // 2
    bits = lax.bitcast_convert_type(v.astype(BF16).astype(F32), jnp.uint32)
    return (bits[:, 0:n] >> 16) | (bits[:, n:2 * n] & jnp.uint32(0xFFFF0000))


def _unpack_bf16_pairs(w):
    lo = lax.bitcast_convert_type(w << 16, F32)
    hi = lax.bitcast_convert_type(w & jnp.uint32(0xFFFF0000), F32)
    return lo, hi


def _store_row_tiles(ref, row0, words):
    m, w = words.shape
    assert w == SUBLANES * LANES
    for s in range(SUBLANES):
        ref[pl.ds(row0 * SUBLANES + s, m, stride=SUBLANES), :] = words[:, s * LANES:(s + 1) * LANES]


def _load_row_tiles(ref, row0, m):
    return jnp.concatenate([ref[pl.ds(row0 * SUBLANES + s, m, stride=SUBLANES), :] for s in range(SUBLANES)], axis=1)


def _outproj_kernel(x_ref, yc_ref, ya_ref, wo_ref, fw_ref, wr_ref, br_ref, h_ref, lg_ref, ug_ref, *, eps, nsub):
    dc = yc_ref.shape[1]
    sub = x_ref.shape[0] // nsub
    mixes = []
    for t in range(nsub):
        rows = slice(t * sub, (t + 1) * sub)
        mixes.append(jnp.dot(yc_ref[rows, :], wo_ref[0:dc, :], preferred_element_type=F32)
                     + jnp.dot(ya_ref[rows, :], wo_ref[dc:, :], preferred_element_type=F32))
    for t in range(nsub):
        rows = slice(t * sub, (t + 1) * sub)
        h = x_ref[rows, :] + mixes[t]
        h_ref[rows, :] = h
        ms = jnp.mean(h * h, axis=-1, keepdims=True)
        un = h * lax.rsqrt(ms + eps) * fw_ref[...]
        hi = un.astype(BF16)
        lo = (un - hi.astype(F32)).astype(BF16)
        hh = jnp.dot(hi, wr_ref[...], preferred_element_type=F32)
        lh = jnp.dot(lo, wr_ref[:, 0:LANES], preferred_element_type=F32)
        lg_ref[rows, :] = hh[:, 0:LANES] + hh[:, LANES:2 * LANES] + lh + br_ref[...]
        _store_row_tiles(ug_ref, t * sub, _pack_bf16_pairs(un))


def _outproj(x2, y_conv, y_attn, wo_bf, ffn_w, wr_hl, br):
    T, D = x2.shape
    dc = y_conv.shape[1]
    da = y_attn.shape[1]
    tm = min(512, T)
    nsub = 2 if tm % 512 == 0 else 1
    return pl.pallas_call(
        functools.partial(_outproj_kernel, eps=RMS_EPS, nsub=nsub),
        grid=(T // tm,),
        in_specs=[pl.BlockSpec((tm, D), lambda i: (i, 0)),
                  pl.BlockSpec((tm, dc), lambda i: (i, 0)),
                  pl.BlockSpec((tm, da), lambda i: (i, 0)),
                  pl.BlockSpec((dc + da, D), lambda i: (0, 0), pipeline_mode=pl.Buffered(1)),
                  pl.BlockSpec((1, D), lambda i: (0, 0)),
                  pl.BlockSpec((D, 2 * LANES), lambda i: (0, 0)),
                  pl.BlockSpec((1, LANES), lambda i: (0, 0))],
        out_specs=[pl.BlockSpec((tm, D), lambda i: (i, 0)),
                   pl.BlockSpec((tm, LANES), lambda i: (i, 0)),
                   pl.BlockSpec((tm * SUBLANES, LANES), lambda i: (i, 0))],
        out_shape=[jax.ShapeDtypeStruct((T, D), F32), jax.ShapeDtypeStruct((T, LANES), F32),
                   jax.ShapeDtypeStruct((T * SUBLANES, LANES), jnp.uint32)],
        compiler_params=_cparams(("arbitrary",), 56),
        name="outproj",
    )(x2, y_conv, y_attn, wo_bf, ffn_w, wr_hl, br)


def _route_kernel(lg_ref, info_ref, cnt_ref, *, ng, epg):
    i = pl.program_id(0)
    lg = lg_ref[...]
    tm = lg.shape[0]
    lane = lax.broadcasted_iota(jnp.int32, lg.shape, 1)
    lanef = lane.astype(F32)
    ne = ng * epg

    def first_argmax(vals):
        mx = jnp.max(vals, axis=-1, keepdims=True)
        idx = jnp.min(jnp.where(vals == mx, lanef, float(LANES)), axis=-1, keepdims=True)
        return mx, idx

    gmask = lane < ng
    gl = jnp.where(gmask, lg, NEG_BIG)
    gmax, gsel = first_argmax(gl)
    gsum = jnp.sum(jnp.where(gmask, jnp.exp(gl - gmax), 0.0), axis=-1, keepdims=True)
    g_w = 1.0 / gsum
    lo = gsel * epg + ng
    emask = (lanef >= lo) & (lanef < lo + epg)
    el = jnp.where(emask, lg, NEG_BIG)
    v1, i1 = first_argmax(el)
    el2 = jnp.where(lanef == i1, NEG_BIG, el)
    v2, i2 = first_argmax(el2)
    e2 = jnp.exp(v2 - v1)
    p1 = 1.0 / (1.0 + e2)
    gate1 = g_w * p1
    gate2 = g_w * (e2 * p1)

    oh1 = lanef == i1
    oh2 = lanef == i2
    cmat = jnp.where(oh1 | oh2, 1.0, 0.0).astype(BF16)
    r = lax.broadcasted_iota(jnp.int32, (tm, tm), 0)
    c = lax.broadcasted_iota(jnp.int32, (tm, tm), 1)
    tri = jnp.where(c < r, 1.0, 0.0).astype(BF16)

    @pl.when(i == 0)
    def _():
        cnt_ref[...] = jnp.zeros(cnt_ref.shape, F32)

    carry = cnt_ref[0:1, :]
    prefix = jnp.dot(tri, cmat, preferred_element_type=F32) + carry
    rank1 = jnp.sum(jnp.where(oh1, prefix, 0.0), axis=-1, keepdims=True)
    rank2 = jnp.sum(jnp.where(oh2, prefix, 0.0), axis=-1, keepdims=True)
    cnt_ref[...] = jnp.broadcast_to(carry + jnp.sum(cmat.astype(F32), axis=0, keepdims=True), cnt_ref.shape)

    info = jnp.where(lane == 0, i1 - ng,
           jnp.where(lane == 1, i2 - ng,
           jnp.where(lane == 2, gate1,
           jnp.where(lane == 3, gate2,
           jnp.where(lane == 4, rank1,
           jnp.where(lane == 5, rank2, 0.0))))))
    info_ref[...] = info


def _route(logits, ng, epg):
    T = logits.shape[0]
    tm = min(512, T)
    return pl.pallas_call(
        functools.partial(_route_kernel, ng=ng, epg=epg),
        grid=(T // tm,),
        in_specs=[pl.BlockSpec((tm, LANES), lambda i: (i, 0))],
        out_specs=[pl.BlockSpec((tm, LANES), lambda i: (i, 0)),
                   pl.BlockSpec((8, LANES), lambda i: (0, 0))],
        out_shape=[jax.ShapeDtypeStruct((T, LANES), F32), jax.ShapeDtypeStruct((8, LANES), F32)],
        compiler_params=_cparams(("arbitrary",), 32),
        name="route",
    )(logits)


ROW_UNROLL = 8


def _dispatch_kernel(dest_ref, pend_ref, ug_ref, xs_hbm, ring, zbuf, sems, zsem, *, tm, blk, ne):
    i = pl.program_id(0)
    n = pl.num_programs(0)
    slot = i % 2

    def seg_tail(e):
        end = pend_ref[e]
        start = jnp.where(e == 0, 0, pend_ref[jnp.maximum(e - 1, 0)])
        tail = pl.multiple_of(jnp.maximum(end - blk, 0) * SUBLANES, blk * SUBLANES)
        return pltpu.make_async_copy(zbuf, xs_hbm.at[pl.ds(tail, blk * SUBLANES)], zsem), end > start

    @pl.when(i == 0)
    def _():
        zbuf[...] = jnp.zeros(zbuf.shape, zbuf.dtype)

        def zstart(e, carry):
            cp, nonempty = seg_tail(e)

            @pl.when(nonempty)
            def _():
                cp.start()
            return carry

        def zwait(e, carry):
            cp, nonempty = seg_tail(e)

            @pl.when(nonempty)
            def _():
                cp.wait()
            return carry

        def spare(b):
            row = pl.multiple_of(b * blk, blk)
            tile = pl.multiple_of(row * SUBLANES, blk * SUBLANES)
            return pltpu.make_async_copy(zbuf, xs_hbm.at[pl.ds(tile, blk * SUBLANES)], zsem), row >= pend_ref[ne - 1]

        def sstart(b, carry):
            cp, unused = spare(b)

            @pl.when(unused)
            def _():
                cp.start()
            return carry

        def swait(b, carry):
            cp, unused = spare(b)

            @pl.when(unused)
            def _():
                cp.wait()
            return carry

        nblk = xs_hbm.shape[0] // (blk * SUBLANES)
        lax.fori_loop(0, ne, zstart, 0)
        lax.fori_loop(0, nblk, sstart, 0)
        lax.fori_loop(0, ne, zwait, 0)
        lax.fori_loop(0, nblk, swait, 0)

    def drain(s):
        for _ in range(TOP_K):
            pltpu.make_async_copy(ring.at[s], xs_hbm.at[pl.ds(0, tm * SUBLANES)], sems.at[s]).wait()

    @pl.when(i >= 2)
    def _():
        drain(slot)

    ring[slot] = ug_ref[...]

    def issue(c, carry):
        r0 = pl.multiple_of(c * ROW_UNROLL, ROW_UNROLL)
        for s in range(ROW_UNROLL):
            for k in range(TOP_K):
                d = dest_ref[(i * tm + r0 + s) * TOP_K + k]
                src = ring.at[slot, pl.ds(pl.multiple_of((r0 + s) * SUBLANES, SUBLANES), SUBLANES)]
                dst = xs_hbm.at[pl.ds(pl.multiple_of(d * SUBLANES, SUBLANES), SUBLANES)]
                pltpu.make_async_copy(src, dst, sems.at[slot]).start(priority=k % 2)
        return carry

    lax.fori_loop(0, tm // ROW_UNROLL, issue, 0)

    @pl.when(i == n - 1)
    def _():
        drain(slot)

        @pl.when(n >= 2)
        def _():
            drain(1 - slot)


def _dispatch(dest_flat, pad_ends, ug, nrows, tm, blk):
    T = ug.shape[0] // SUBLANES
    ne = pad_ends.shape[0]
    grid_spec = pltpu.PrefetchScalarGridSpec(
        num_scalar_prefetch=2,
        grid=(T // tm,),
        in_specs=[pl.BlockSpec((tm * SUBLANES, LANES), lambda i, d, p: (i, 0))],
        out_specs=pl.BlockSpec(memory_space=pl.ANY),
        scratch_shapes=[pltpu.VMEM((2, tm * SUBLANES, LANES), ug.dtype),
                        pltpu.VMEM((blk * SUBLANES, LANES), ug.dtype),
                        pltpu.SemaphoreType.DMA((2,)),
                        pltpu.SemaphoreType.DMA(())],
    )
    return pl.pallas_call(
        functools.partial(_dispatch_kernel, tm=tm, blk=blk, ne=ne),
        grid_spec=grid_spec,
        out_shape=jax.ShapeDtypeStruct((nrows * SUBLANES, LANES), ug.dtype),
        compiler_params=_cparams(("arbitrary",), 32),
        name="dispatch",
    )(dest_flat, pad_ends, ug)


WEIGHT_DMA_PRIORITY = (0, 1, 1)


def _experts_kernel(bexp_ref, first_ref, slot_ref, next_ref, nused_ref, x_ref, w1_hbm, w3_hbm, w2_hbm, y_ref,
                    w1b, w3b, w2b, wsems):
    i = pl.program_id(0)
    nused = nused_ref[0]

    def weight_copies(e, s):
        return (pltpu.make_async_copy(w1_hbm.at[e], w1b.at[s], wsems.at[s, 0]),
                pltpu.make_async_copy(w3_hbm.at[e], w3b.at[s], wsems.at[s, 1]),
                pltpu.make_async_copy(w2_hbm.at[e], w2b.at[s], wsems.at[s, 2]))

    @pl.when(jnp.logical_and(i == 0, nused > 0))
    def _():
        for n, cp in enumerate(weight_copies(bexp_ref[0], 0)):
            cp.start(priority=WEIGHT_DMA_PRIORITY[n])

    @pl.when(i < nused)
    def _():
        s = slot_ref[i]

        @pl.when(first_ref[i] == 1)
        def _():
            @pl.when(next_ref[i] >= 0)
            def _():
                for n, cp in enumerate(weight_copies(next_ref[i], 1 - s)):
                    cp.start(priority=WEIGHT_DMA_PRIORITY[n])
            for cp in weight_copies(bexp_ref[i], s):
                cp.wait()

        blk = x_ref.shape[0] // SUBLANES
        x_lo, x_hi = _unpack_bf16_pairs(_load_row_tiles(x_ref, 0, blk))
        half = x_lo.shape[1]
        a = (jnp.dot(x_lo, w1b[s, 0:half, :], preferred_element_type=F32)
             + jnp.dot(x_hi, w1b[s, half:2 * half, :], preferred_element_type=F32))
        b = (jnp.dot(x_lo, w3b[s, 0:half, :], preferred_element_type=F32)
             + jnp.dot(x_hi, w3b[s, half:2 * half, :], preferred_element_type=F32))
        hdn = a * jax.nn.sigmoid(a) * b
        y = jnp.dot(hdn, w2b[s], preferred_element_type=F32)
        _store_row_tiles(y_ref, 0, _pack_bf16_pairs(y))

    @pl.when(i >= nused)
    def _():
        y_ref[...] = jnp.zeros(y_ref.shape, y_ref.dtype)


def _experts(bexp, first, slot, nxt, nused, xs, w1, w3, w2, blk):
    P = xs.shape[0] // SUBLANES
    E, D, Fh = w1.shape
    nblk = P // blk
    smap = lambda i, *_: (i, 0)
    grid_spec = pltpu.PrefetchScalarGridSpec(
        num_scalar_prefetch=5,
        grid=(nblk,),
        in_specs=[pl.BlockSpec((blk * SUBLANES, LANES), smap),
                  pl.BlockSpec(memory_space=pl.ANY),
                  pl.BlockSpec(memory_space=pl.ANY),
                  pl.BlockSpec(memory_space=pl.ANY)],
        out_specs=pl.BlockSpec((blk * SUBLANES, LANES), smap),
        scratch_shapes=[pltpu.VMEM((2, D, Fh), w1.dtype),
                        pltpu.VMEM((2, D, Fh), w3.dtype),
                        pltpu.VMEM((2, Fh, D), w2.dtype),
                        pltpu.SemaphoreType.DMA((2, 3))],
    )
    return pl.pallas_call(
        _experts_kernel,
        grid_spec=grid_spec,
        out_shape=jax.ShapeDtypeStruct(xs.shape, xs.dtype),
        compiler_params=_cparams(("arbitrary",), 48),
        name="experts",
    )(bexp, first, slot, nxt, nused, xs, w1, w3, w2)


def _row_gather(src_hbm, idx_ref, base, dst, sem, n):
    def body(c, carry):
        r0 = pl.multiple_of(c * ROW_UNROLL, ROW_UNROLL)
        for s in range(ROW_UNROLL):
            d = idx_ref[base + r0 + s]
            pltpu.make_async_copy(src_hbm.at[pl.ds(pl.multiple_of(d * SUBLANES, SUBLANES), SUBLANES)],
                                  dst.at[pl.ds(pl.multiple_of((r0 + s) * SUBLANES, SUBLANES), SUBLANES)],
                                  sem).start(priority=s % 2)
        return carry
    lax.fori_loop(0, n // ROW_UNROLL, body, 0)


def _row_gather_wait(src_hbm, dst, sem, n):
    pltpu.make_async_copy(src_hbm.at[pl.ds(0, n * SUBLANES)], dst, sem).wait()


def _combine_kernel(dest_ref, h_ref, info_ref, y_hbm, fw_ref, o_ref, ybuf, sems, *, tm, eps):
    i = pl.program_id(0)
    n = pl.num_programs(0)
    slot = i % 2
    nrow = TOP_K * tm

    @pl.when(i == 0)
    def _():
        _row_gather(y_hbm, dest_ref, 0, ybuf.at[0], sems.at[0], nrow)

    @pl.when(i + 1 < n)
    def _():
        _row_gather(y_hbm, dest_ref, (i + 1) * nrow, ybuf.at[1 - slot], sems.at[1 - slot], nrow)

    _row_gather_wait(y_hbm, ybuf.at[slot], sems.at[slot], nrow)
    info = info_ref[...]
    g1 = info[:, 2:3]
    g2 = info[:, 3:4]
    y1_lo, y1_hi = _unpack_bf16_pairs(_load_row_tiles(ybuf.at[slot], 0, tm))
    y2_lo, y2_hi = _unpack_bf16_pairs(_load_row_tiles(ybuf.at[slot], tm, tm))
    half = y1_lo.shape[1]
    h_lo = h_ref[:, 0:half] + (g1 * y1_lo + g2 * y2_lo)
    h_hi = h_ref[:, half:2 * half] + (g1 * y1_hi + g2 * y2_hi)
    ms = (jnp.sum(h_lo * h_lo, axis=-1, keepdims=True) + jnp.sum(h_hi * h_hi, axis=-1, keepdims=True)) / (2 * half)
    r = lax.rsqrt(ms + eps)
    o_ref[:, 0:half] = h_lo * r * fw_ref[:, 0:half]
    o_ref[:, half:2 * half] = h_hi * r * fw_ref[:, half:2 * half]


def _combine(dest_km, h, info, yb, final_w, tm):
    T, D = h.shape
    grid_spec = pltpu.PrefetchScalarGridSpec(
        num_scalar_prefetch=1,
        grid=(T // tm,),
        in_specs=[pl.BlockSpec((tm, D), lambda i, d: (i, 0)),
                  pl.BlockSpec((tm, LANES), lambda i, d: (i, 0)),
                  pl.BlockSpec(memory_space=pl.ANY),
                  pl.BlockSpec((1, D), lambda i, d: (0, 0))],
        out_specs=pl.BlockSpec((tm, D), lambda i, d: (i, 0)),
        scratch_shapes=[pltpu.VMEM((2, TOP_K * tm * SUBLANES, LANES), yb.dtype),
                        pltpu.SemaphoreType.DMA((2,))],
    )
    return pl.pallas_call(
        functools.partial(_combine_kernel, tm=tm, eps=RMS_EPS),
        grid_spec=grid_spec,
        out_shape=jax.ShapeDtypeStruct((T, D), F32),
        compiler_params=_cparams(("arbitrary",), 48),
        name="combine",
    )(dest_km, h, info, yb, final_w)


def _layer(h_in, l, B, S, mix_norm_w, w_in, conv_dw_w, conv_dw_b, conv_ln_w, conv_ln_b,
           lam_q1, lam_k1, lam_q2, lam_k2, attn_subln_w, w_out, ffn_norm_w,
           w_group, b_group, w_expert_gate, b_expert_gate, w1, w3, w2):
    T, D = h_in.shape
    d_conv = conv_dw_w.shape[1]
    d_attn = (w_in.shape[1] - 2 * d_conv) // 3
    ng = w_group.shape[1]
    ne = w_expert_gate.shape[1]
    epg = ne // ng
    assert ng + ne <= LANES
    lam_init = 0.8 - 0.6 * math.exp(-0.3 * l)

    proj = _norm_inproj(h_in, mix_norm_w.reshape(1, D), w_in.astype(BF16))
    y_conv = _conformer(proj, conv_dw_w, conv_dw_b, conv_ln_w, conv_ln_b, B, S)
    y_attn = _diff_attn(proj, lam_q1, lam_k1, lam_q2, lam_k2, attn_subln_w, B, S, d_conv, d_attn, lam_init)

    wr = jnp.concatenate([w_group, w_expert_gate, jnp.zeros((D, LANES - ng - ne), F32)], axis=1)
    wr_hi = wr.astype(BF16)
    wr_lo = (wr - wr_hi.astype(F32)).astype(BF16)
    br = jnp.concatenate([b_group, b_expert_gate.reshape(-1), jnp.zeros((LANES - ng - ne,), F32)]).reshape(1, LANES)
    wr_hl = jnp.concatenate([wr_hi, wr_lo], axis=1)
    h, logits, ug = _outproj(h_in, y_conv, y_attn, w_out.astype(BF16), ffn_norm_w.reshape(1, D), wr_hl, br)

    info, cnt = _route(logits, ng, epg)

    blk = 256
    tmd = min(256, T)
    A = T * TOP_K
    nblk = (A + ne * (blk - 1) + blk - 1) // blk
    i32 = jnp.int32
    counts = cnt[0, ng:ng + ne].astype(i32)
    padded = (counts + blk - 1) // blk * blk
    pad_ends = jnp.cumsum(padded).astype(i32)
    pad_starts = pad_ends - padded
    eid = info[:, 0:TOP_K].astype(i32)
    rank = info[:, 4:4 + TOP_K].astype(i32)
    onehot = eid[:, :, None] == jnp.arange(ne, dtype=i32)[None, None, :]
    dest = jnp.sum(jnp.where(onehot, pad_starts[None, None, :], 0), axis=-1).astype(i32) + rank
    nused = (pad_ends[-1] // blk).astype(i32)
    bpos = jnp.arange(nblk, dtype=i32)
    bexp = jnp.minimum(jnp.searchsorted(pad_ends, jnp.minimum(bpos, nused - 1) * blk, side="right"), ne - 1).astype(i32)
    first = ((bpos < nused) & ((bpos == 0) | (bexp != jnp.roll(bexp, 1)))).astype(i32)
    slot = ((jnp.cumsum(first) - 1) % 2).astype(i32)
    used_idx = jnp.where(padded > 0, jnp.arange(ne, dtype=i32), ne)
    suffix_min = lax.cummin(used_idx, reverse=True)
    next_used = jnp.concatenate([suffix_min[1:], jnp.full((1,), ne, i32)])
    next_used = jnp.where(next_used >= ne, -1, next_used)
    nxt = next_used[bexp].astype(i32)

    xs = _dispatch(dest.reshape(-1), pad_ends, ug, nblk * blk, tmd, blk)
    yb = _experts(bexp, first, slot, nxt, nused.reshape(1), xs, w1, w3, w2, blk)
    return h, info, dest, yb


def kernel(x, mix_norm_w, w_in, conv_dw_w, conv_dw_b, conv_ln_w, conv_ln_b, lam_q1, lam_k1, lam_q2, lam_k2,
           attn_subln_w, w_out, ffn_norm_w, w_group, b_group, w_expert_gate, b_expert_gate, w1, w3, w2,
           final_norm_w):
    B, S, D = x.shape
    depth = w_in.shape[0]
    assert depth == 1
    T = B * S
    tmc = min(256, T)
    h = x.reshape(T, D)
    for l in range(depth):
        h, info, dest, yb = _layer(
            h, l, B, S, mix_norm_w[l], w_in[l], conv_dw_w[l], conv_dw_b[l], conv_ln_w[l], conv_ln_b[l],
            lam_q1[l], lam_k1[l], lam_q2[l], lam_k2[l], attn_subln_w[l], w_out[l], ffn_norm_w[l],
            w_group[l], b_group[l], w_expert_gate[l], b_expert_gate[l], w1[l], w3[l], w2[l])
        dest_km = dest.reshape(T // tmc, tmc, TOP_K).transpose(0, 2, 1).reshape(-1)
        h = _combine(dest_km, h, info, yb, final_norm_w.reshape(1, D), tmc)
    return h.reshape(B, S, D)
```

```python
import functools
import math

import jax
import jax.numpy as jnp
from jax import lax
from jax.experimental import pallas as pl
from jax.experimental.pallas import tpu as pltpu

F32 = jnp.float32
BF16 = jnp.bfloat16

RMS_EPS = 1e-6
SUBLN_EPS = 1e-5
LN_EPS = 1e-5
TOP_K = 2
LANES = 128
SUBLANES = 8
NEG_BIG = -1e30
MIB = 1024 * 1024


def _cparams(sem, vmem_mib):
    return pltpu.CompilerParams(dimension_semantics=sem, vmem_limit_bytes=vmem_mib * MIB)


def _norm_inproj_kernel(x_ref, nw_ref, w_ref, o_ref, u_ref, *, eps, rc):
    @pl.when(pl.program_id(1) == 0)
    def _():
        def body(r, carry):
            rows = pl.ds(pl.multiple_of(r * rc, rc), rc)
            x = x_ref[rows, :]
            ms = jnp.mean(x * x, axis=-1, keepdims=True)
            u_ref[rows, :] = (x * lax.rsqrt(ms + eps) * nw_ref[...]).astype(u_ref.dtype)
            return carry
        lax.fori_loop(0, x_ref.shape[0] // rc, body, 0)

    o_ref[...] = jnp.dot(u_ref[...], w_ref[...], preferred_element_type=F32).astype(o_ref.dtype)


def _norm_inproj(x2, nw, w_bf):
    T, D = x2.shape
    N = w_bf.shape[1]
    tm = min(1024, T)
    tn = 1024
    return pl.pallas_call(
        functools.partial(_norm_inproj_kernel, eps=RMS_EPS, rc=128),
        grid=(T // tm, N // tn),
        in_specs=[pl.BlockSpec((tm, D), lambda i, j: (i, 0)),
                  pl.BlockSpec((1, D), lambda i, j: (0, 0)),
                  pl.BlockSpec((D, tn), lambda i, j: (0, j))],
        out_specs=pl.BlockSpec((tm, tn), lambda i, j: (i, j)),
        out_shape=jax.ShapeDtypeStruct((T, N), BF16),
        scratch_shapes=[pltpu.VMEM((tm, D), BF16)],
        compiler_params=_cparams(("arbitrary", "arbitrary"), 48),
        name="norm_inproj",
    )(x2, nw, w_bf)


def _conv_kernel(a_ref, g_ref, w_ref, b_ref, lw_ref, lb_ref, o_ref, ubuf, cbuf, sh, *, ts, kw, halo, eps):
    s = pl.program_id(1)
    C = a_ref.shape[1]

    @pl.when(s == 0)
    def _():
        ubuf[0:halo, :] = jnp.zeros((halo, C), F32)

    @pl.when(s > 0)
    def _():
        ubuf[0:halo, :] = ubuf[ts:ts + halo, :]

    rg = 64
    for r0 in range(0, ts, rg):
        a = a_ref[r0:r0 + rg, :].astype(F32)
        g = g_ref[r0:r0 + rg, :].astype(F32)
        ubuf[halo + r0:halo + r0 + rg, :] = a * jax.nn.sigmoid(g)

    nsh = sh.shape[1]
    for r in range(1, SUBLANES):
        for i0 in range(0, nsh, rg):
            n = min(rg, nsh - i0)
            sh[r - 1, i0:i0 + n, :] = ubuf[i0 + r:i0 + r + n, :]

    off = halo - (kw - 1)
    rc = 64

    def conv_ln_rows(ri, carry):
        r0 = pl.multiple_of(ri * rc, rc)
        for c in range(C // LANES):
            cs = slice(c * LANES, (c + 1) * LANES)
            acc = jnp.broadcast_to(b_ref[0:1, cs], (rc // SUBLANES, SUBLANES, LANES))
            for k in range(kw):
                q, r = divmod(off + k, SUBLANES)
                rows = pl.ds(r0 + q * SUBLANES, rc)
                tap = ubuf[rows, cs] if r == 0 else sh[r - 1, rows, cs]
                acc = acc + w_ref[k, :, cs] * tap.reshape(rc // SUBLANES, SUBLANES, LANES)
            cbuf[pl.ds(r0, rc), cs] = acc.reshape(rc, LANES)
        cv = cbuf[pl.ds(r0, rc), :]
        mu = jnp.mean(cv, axis=-1, keepdims=True)
        d = cv - mu
        var = jnp.mean(d * d, axis=-1, keepdims=True)
        un = d * lax.rsqrt(var + eps) * lw_ref[...] + lb_ref[...]
        o_ref[pl.ds(r0, rc), :] = (un * jax.nn.sigmoid(un)).astype(o_ref.dtype)
        return carry

    lax.fori_loop(0, ts // rc, conv_ln_rows, 0)


def _conformer(proj, dw_w, dw_b, ln_w, ln_b, B, S):
    T = proj.shape[0]
    kw, C = dw_w.shape
    ts = min(256, S)
    halo = 32
    assert kw - 1 <= halo and S % ts == 0 and ts >= halo
    ns = S // ts
    wp = jnp.broadcast_to(dw_w[:, None, :], (kw, SUBLANES, C))
    return pl.pallas_call(
        functools.partial(_conv_kernel, ts=ts, kw=kw, halo=halo, eps=LN_EPS),
        grid=(B, ns),
        in_specs=[pl.BlockSpec((ts, C), lambda b, s: (b * ns + s, 0)),
                  pl.BlockSpec((ts, C), lambda b, s: (b * ns + s, 1)),
                  pl.BlockSpec((kw, SUBLANES, C), lambda b, s: (0, 0, 0)),
                  pl.BlockSpec((1, C), lambda b, s: (0, 0)),
                  pl.BlockSpec((1, C), lambda b, s: (0, 0)),
                  pl.BlockSpec((1, C), lambda b, s: (0, 0))],
        out_specs=pl.BlockSpec((ts, C), lambda b, s: (b * ns + s, 0)),
        out_shape=jax.ShapeDtypeStruct((T, C), BF16),
        scratch_shapes=[pltpu.VMEM((ts + halo, C), F32), pltpu.VMEM((ts, C), F32),
                        pltpu.VMEM((SUBLANES - 1, ts + halo - SUBLANES, C), F32)],
        compiler_params=_cparams(("arbitrary", "arbitrary"), 32),
        name="conformer",
    )(proj, proj, wp, dw_b.reshape(1, C), ln_w.reshape(1, C), ln_b.reshape(1, C))


def _attn_kernel(lq1_ref, lk1_ref, lq2_ref, lk2_ref, q_ref, k_ref, v_ref, sw_ref, o_ref,
                 vxt_ref, qq_ref, *scr, tq, hd, hp, ones_rows, lam_init, eps):
    i = pl.program_id(2)
    vd = 2 * hd
    nkb = v_ref.shape[0] // tq
    acc_refs, m_refs, sa_refs, sb_refs = (scr[n * hp:(n + 1) * hp] for n in range(4))

    @pl.when(i == 0)
    def _():
        for u in range(hp):
            for jb in range(nkb):
                vblk = v_ref[jb * tq:(jb + 1) * tq, u * vd:(u + 1) * vd].astype(F32)
                vxt_ref[u, jb, 0:vd, :] = vblk.T.astype(vxt_ref.dtype)
                vxt_ref[u, jb, vd:vd + ones_rows, :] = jnp.ones((ones_rows, tq), vxt_ref.dtype)

    lam = (jnp.exp(jnp.sum(lq1_ref[...] * lk1_ref[...], axis=-1, keepdims=True))
           - jnp.exp(jnp.sum(lq2_ref[...] * lk2_ref[...], axis=-1, keepdims=True)) + lam_init)

    for u in range(hp):
        q = q_ref[:, u * vd:(u + 1) * vd]
        qs = q * jnp.asarray(hd ** -0.5, q.dtype)
        lane = lax.broadcasted_iota(jnp.int32, q.shape, 1)
        zero = jnp.zeros_like(qs)
        qq_ref[u, 0:tq, :] = jnp.where(lane < hd, qs, zero)
        qq_ref[u, tq:2 * tq, :] = jnp.where(lane >= hd, qs, zero)
        acc_refs[u][...] = jnp.zeros(acc_refs[u].shape, F32)
        m_refs[u][...] = jnp.full(m_refs[u].shape, NEG_BIG, F32)

    def scores(j, u, dst):
        rows = pl.ds(pl.multiple_of(j * tq, tq), tq)
        kb = k_ref[rows, u * vd:(u + 1) * vd]
        dst[...] = lax.dot_general(kb, qq_ref[u], (((1,), (1,)), ((), ())), preferred_element_type=F32)

    def softmax_pv(j, u, src, masked):
        st = src[...]
        if masked:
            r = lax.broadcasted_iota(jnp.int32, st.shape, 0)
            c = lax.broadcasted_iota(jnp.int32, st.shape, 1)
            st = jnp.where(r <= jnp.where(c >= tq, c - tq, c), st, NEG_BIG)
        m_old = m_refs[u][...]
        m_new = jnp.maximum(m_old, jnp.max(st, axis=0, keepdims=True))
        alpha = jnp.exp(m_old - m_new)
        pt = jnp.exp(st - m_new).astype(vxt_ref.dtype)
        pv = jnp.dot(vxt_ref[u, j], pt, preferred_element_type=F32)
        acc_refs[u][...] = alpha * acc_refs[u][...] + pv
        m_refs[u][...] = m_new

    def half(j, cur, nxt):
        for u in range(hp):
            scores(j + 1, u, nxt[u])
            softmax_pv(j, u, cur[u], False)

    for u in range(hp):
        scores(0, u, sa_refs[u])

    def pair(t, carry):
        half(2 * t, sa_refs, sb_refs)
        half(2 * t + 1, sb_refs, sa_refs)
        return carry

    lax.fori_loop(0, i // 2, pair, 0)

    @pl.when(i % 2 == 1)
    def _():
        half(i - 1, sa_refs, sb_refs)
        for u in range(hp):
            softmax_pv(i, u, sb_refs[u], True)

    @pl.when(i % 2 == 0)
    def _():
        for u in range(hp):
            softmax_pv(i, u, sa_refs[u], True)

    for u in range(hp):
        acc = acc_refs[u][...]
        o12 = acc[0:vd] * (1.0 / acc[vd:vd + 1])
        ot = o12[:, 0:tq] - lam * o12[:, tq:2 * tq]
        msq = jnp.mean(ot * ot, axis=0, keepdims=True)
        o = (ot * lax.rsqrt(msq + eps)).T
        o_ref[:, u * vd:(u + 1) * vd] = (o * sw_ref[...] * (1.0 - lam_init)).astype(o_ref.dtype)


def _diff_attn(proj, lq1, lk1, lq2, lk2, subln_w, B, S, d_conv, d_attn, lam_init):
    T = proj.shape[0]
    vd = subln_w.shape[0]
    hd = lq1.shape[0]
    assert vd == LANES and 2 * hd == vd
    H = d_attn // vd
    hp = 4
    ones_rows = 16
    tq = min(256, S)
    nq = S // tq
    assert H % hp == 0
    qc = 2 * d_conv // (hp * vd)
    kc = qc + H // hp
    vc = kc + H // hp
    lspec = pl.BlockSpec((1, hd), lambda b, h, i: (0, 0))
    return pl.pallas_call(
        functools.partial(_attn_kernel, tq=tq, hd=hd, hp=hp, ones_rows=ones_rows, lam_init=lam_init, eps=SUBLN_EPS),
        grid=(B, H // hp, nq),
        in_specs=[lspec, lspec, lspec, lspec,
                  pl.BlockSpec((tq, hp * vd), lambda b, h, i: (b * nq + i, qc + h)),
                  pl.BlockSpec((S, hp * vd), lambda b, h, i: (b, kc + h)),
                  pl.BlockSpec((S, hp * vd), lambda b, h, i: (b, vc + h)),
                  pl.BlockSpec((1, vd), lambda b, h, i: (0, 0))],
        out_specs=pl.BlockSpec((tq, hp * vd), lambda b, h, i: (b * nq + i, h)),
        out_shape=jax.ShapeDtypeStruct((T, d_attn), BF16),
        scratch_shapes=[pltpu.VMEM((hp, S // tq, vd + ones_rows, tq), BF16),
                        pltpu.VMEM((hp, 2 * tq, vd), BF16)]
                       + [pltpu.VMEM((vd + ones_rows, 2 * tq), F32) for _ in range(hp)]
                       + [pltpu.VMEM((1, 2 * tq), F32) for _ in range(hp)]
                       + [pltpu.VMEM((tq, 2 * tq), F32) for _ in range(2 * hp)],
        compiler_params=_cparams(("arbitrary", "arbitrary", "arbitrary"), 40),
        name="diff_attn",
    )(lq1.reshape(1, hd), lk1.reshape(1, hd), lq2.reshape(1, hd), lk2.reshape(1, hd),
      proj, proj, proj, subln_w.reshape(1, vd))


def _pack_bf16_pairs(v):
    n = v.shape[1] // 2
    bits = lax.bitcast_convert_type(v.astype(BF16).astype(F32), jnp.uint32)
    return (bits[:, 0:n] >> 16) | (bits[:, n:2 * n] & jnp.uint32(0xFFFF0000))


def _unpack_bf16_pairs(w):
    lo = lax.bitcast_convert_type(w << 16, F32)
    hi = lax.bitcast_convert_type(w & jnp.uint32(0xFFFF0000), F32)
    return lo, hi


def _store_row_tiles(ref, row0, words):
    m, w = words.shape
    assert w == SUBLANES * LANES
    for s in range(SUBLANES):
        ref[pl.ds(row0 * SUBLANES + s, m, stride=SUBLANES), :] = words[:, s * LANES:(s + 1) * LANES]


def _load_row_tiles(ref, row0, m):
    return jnp.concatenate([ref[pl.ds(row0 * SUBLANES + s, m, stride=SUBLANES), :] for s in range(SUBLANES)], axis=1)


def _outproj_kernel(x_ref, yc_ref, ya_ref, wo_ref, fw_ref, wr_ref, br_ref, h_ref, lg_ref, ug_ref, *, eps, nsub):
    dc = yc_ref.shape[1]
    sub = x_ref.shape[0] // nsub
    mixes = []
    for t in range(nsub):
        rows = slice(t * sub, (t + 1) * sub)
        mixes.append(jnp.dot(yc_ref[rows, :], wo_ref[0:dc, :], preferred_element_type=F32)
                     + jnp.dot(ya_ref[rows, :], wo_ref[dc:, :], preferred_element_type=F32))
    for t in range(nsub):
        rows = slice(t * sub, (t + 1) * sub)
        h = x_ref[rows, :] + mixes[t]
        h_ref[rows, :] = h
        ms = jnp.mean(h * h, axis=-1, keepdims=True)
        un = h * lax.rsqrt(ms + eps) * fw_ref[...]
        hi = un.astype(BF16)
        lo = (un - hi.astype(F32)).astype(BF16)
        hh = jnp.dot(hi, wr_ref[...], preferred_element_type=F32)
        lh = jnp.dot(lo, wr_ref[:, 0:LANES], preferred_element_type=F32)
        lg_ref[rows, :] = hh[:, 0:LANES] + hh[:, LANES:2 * LANES] + lh + br_ref[...]
        _store_row_tiles(ug_ref, t * sub, _pack_bf16_pairs(un))


def _outproj(x2, y_conv, y_attn, wo_bf, ffn_w, wr_hl, br):
    T, D = x2.shape
    dc = y_conv.shape[1]
    da = y_attn.shape[1]
    tm = min(512, T)
    nsub = 2 if tm % 512 == 0 else 1
    return pl.pallas_call(
        functools.partial(_outproj_kernel, eps=RMS_EPS, nsub=nsub),
        grid=(T // tm,),
        in_specs=[pl.BlockSpec((tm, D), lambda i: (i, 0)),
                  pl.BlockSpec((tm, dc), lambda i: (i, 0)),
                  pl.BlockSpec((tm, da), lambda i: (i, 0)),
                  pl.BlockSpec((dc + da, D), lambda i: (0, 0), pipeline_mode=pl.Buffered(1)),
                  pl.BlockSpec((1, D), lambda i: (0, 0)),
                  pl.BlockSpec((D, 2 * LANES), lambda i: (0, 0)),
                  pl.BlockSpec((1, LANES), lambda i: (0, 0))],
        out_specs=[pl.BlockSpec((tm, D), lambda i: (i, 0)),
                   pl.BlockSpec((tm, LANES), lambda i: (i, 0)),
                   pl.BlockSpec((tm * SUBLANES, LANES), lambda i: (i, 0))],
        out_shape=[jax.ShapeDtypeStruct((T, D), F32), jax.ShapeDtypeStruct((T, LANES), F32),
                   jax.ShapeDtypeStruct((T * SUBLANES, LANES), jnp.uint32)],
        compiler_params=_cparams(("arbitrary",), 56),
        name="outproj",
    )(x2, y_conv, y_attn, wo_bf, ffn_w, wr_hl, br)


def _route_kernel(lg_ref, info_ref, cnt_ref, *, ng, epg):
    i = pl.program_id(0)
    lg = lg_ref[...]
    tm = lg.shape[0]
    lane = lax.broadcasted_iota(jnp.int32, lg.shape, 1)
    lanef = lane.astype(F32)
    ne = ng * epg

    def first_argmax(vals):
        mx = jnp.max(vals, axis=-1, keepdims=True)
        idx = jnp.min(jnp.where(vals == mx, lanef, float(LANES)), axis=-1, keepdims=True)
        return mx, idx

    gmask = lane < ng
    gl = jnp.where(gmask, lg, NEG_BIG)
    gmax, gsel = first_argmax(gl)
    gsum = jnp.sum(jnp.where(gmask, jnp.exp(gl - gmax), 0.0), axis=-1, keepdims=True)
    g_w = 1.0 / gsum
    lo = gsel * epg + ng
    emask = (lanef >= lo) & (lanef < lo + epg)
    el = jnp.where(emask, lg, NEG_BIG)
    v1, i1 = first_argmax(el)
    el2 = jnp.where(lanef == i1, NEG_BIG, el)
    v2, i2 = first_argmax(el2)
    e2 = jnp.exp(v2 - v1)
    p1 = 1.0 / (1.0 + e2)
    gate1 = g_w * p1
    gate2 = g_w * (e2 * p1)

    oh1 = lanef == i1
    oh2 = lanef == i2
    cmat = jnp.where(oh1 | oh2, 1.0, 0.0).astype(BF16)
    r = lax.broadcasted_iota(jnp.int32, (tm, tm), 0)
    c = lax.broadcasted_iota(jnp.int32, (tm, tm), 1)
    tri = jnp.where(c < r, 1.0, 0.0).astype(BF16)

    @pl.when(i == 0)
    def _():
        cnt_ref[...] = jnp.zeros(cnt_ref.shape, F32)

    carry = cnt_ref[0:1, :]
    prefix = jnp.dot(tri, cmat, preferred_element_type=F32) + carry
    rank1 = jnp.sum(jnp.where(oh1, prefix, 0.0), axis=-1, keepdims=True)
    rank2 = jnp.sum(jnp.where(oh2, prefix, 0.0), axis=-1, keepdims=True)
    cnt_ref[...] = jnp.broadcast_to(carry + jnp.sum(cmat.astype(F32), axis=0, keepdims=True), cnt_ref.shape)

    info = jnp.where(lane == 0, i1 - ng,
           jnp.where(lane == 1, i2 - ng,
           jnp.where(lane == 2, gate1,
           jnp.where(lane == 3, gate2,
           jnp.where(lane == 4, rank1,
           jnp.where(lane == 5, rank2, 0.0))))))
    info_ref[...] = info


def _route(logits, ng, epg):
    T = logits.shape[0]
    tm = min(512, T)
    return pl.pallas_call(
        functools.partial(_route_kernel, ng=ng, epg=epg),
        grid=(T // tm,),
        in_specs=[pl.BlockSpec((tm, LANES), lambda i: (i, 0))],
        out_specs=[pl.BlockSpec((tm, LANES), lambda i: (i, 0)),
                   pl.BlockSpec((8, LANES), lambda i: (0, 0))],
        out_shape=[jax.ShapeDtypeStruct((T, LANES), F32), jax.ShapeDtypeStruct((8, LANES), F32)],
        compiler_params=_cparams(("arbitrary",), 32),
        name="route",
    )(logits)


ROW_UNROLL = 8


def _dispatch_kernel(dest_ref, pend_ref, ug_ref, xs_hbm, ring, zbuf, sems, zsem, *, tm, blk, ne):
    i = pl.program_id(0)
    n = pl.num_programs(0)
    slot = i % 2

    def seg_tail(e):
        end = pend_ref[e]
        start = jnp.where(e == 0, 0, pend_ref[jnp.maximum(e - 1, 0)])
        tail = pl.multiple_of(jnp.maximum(end - blk, 0) * SUBLANES, blk * SUBLANES)
        return pltpu.make_async_copy(zbuf, xs_hbm.at[pl.ds(tail, blk * SUBLANES)], zsem), end > start

    @pl.when(i == 0)
    def _():
        zbuf[...] = jnp.zeros(zbuf.shape, zbuf.dtype)

        def zstart(e, carry):
            cp, nonempty = seg_tail(e)

            @pl.when(nonempty)
            def _():
                cp.start()
            return carry

        def zwait(e, carry):
            cp, nonempty = seg_tail(e)

            @pl.when(nonempty)
            def _():
                cp.wait()
            return carry

        def spare(b):
            row = pl.multiple_of(b * blk, blk)
            tile = pl.multiple_of(row * SUBLANES, blk * SUBLANES)
            return pltpu.make_async_copy(zbuf, xs_hbm.at[pl.ds(tile, blk * SUBLANES)], zsem), row >= pend_ref[ne - 1]

        def sstart(b, carry):
            cp, unused = spare(b)

            @pl.when(unused)
            def _():
                cp.start()
            return carry

        def swait(b, carry):
            cp, unused = spare(b)

            @pl.when(unused)
            def _():
                cp.wait()
            return carry

        nblk = xs_hbm.shape[0] // (blk * SUBLANES)
        lax.fori_loop(0, ne, zstart, 0)
        lax.fori_loop(0, nblk, sstart, 0)
        lax.fori_loop(0, ne, zwait, 0)
        lax.fori_loop(0, nblk, swait, 0)

    def drain(s):
        for _ in range(TOP_K):
            pltpu.make_async_copy(ring.at[s], xs_hbm.at[pl.ds(0, tm * SUBLANES)], sems.at[s]).wait()

    @pl.when(i >= 2)
    def _():
        drain(slot)

    ring[slot] = ug_ref[...]

    def issue(c, carry):
        r0 = pl.multiple_of(c * ROW_UNROLL, ROW_UNROLL)
        for s in range(ROW_UNROLL):
            for k in range(TOP_K):
                d = dest_ref[(i * tm + r0 + s) * TOP_K + k]
                src = ring.at[slot, pl.ds(pl.multiple_of((r0 + s) * SUBLANES, SUBLANES), SUBLANES)]
                dst = xs_hbm.at[pl.ds(pl.multiple_of(d * SUBLANES, SUBLANES), SUBLANES)]
                pltpu.make_async_copy(src, dst, sems.at[slot]).start(priority=k % 2)
        return carry

    lax.fori_loop(0, tm // ROW_UNROLL, issue, 0)

    @pl.when(i == n - 1)
    def _():
        drain(slot)

        @pl.when(n >= 2)
        def _():
            drain(1 - slot)


def _dispatch(dest_flat, pad_ends, ug, nrows, tm, blk):
    T = ug.shape[0] // SUBLANES
    ne = pad_ends.shape[0]
    grid_spec = pltpu.PrefetchScalarGridSpec(
        num_scalar_prefetch=2,
        grid=(T // tm,),
        in_specs=[pl.BlockSpec((tm * SUBLANES, LANES), lambda i, d, p: (i, 0))],
        out_specs=pl.BlockSpec(memory_space=pl.ANY),
        scratch_shapes=[pltpu.VMEM((2, tm * SUBLANES, LANES), ug.dtype),
                        pltpu.VMEM((blk * SUBLANES, LANES), ug.dtype),
                        pltpu.SemaphoreType.DMA((2,)),
                        pltpu.SemaphoreType.DMA(())],
    )
    return pl.pallas_call(
        functools.partial(_dispatch_kernel, tm=tm, blk=blk, ne=ne),
        grid_spec=grid_spec,
        out_shape=jax.ShapeDtypeStruct((nrows * SUBLANES, LANES), ug.dtype),
        compiler_params=_cparams(("arbitrary",), 32),
        name="dispatch",
    )(dest_flat, pad_ends, ug)


WEIGHT_DMA_PRIORITY = (0, 1, 1)
WEIGHT_SLOTS = 3


def _experts_kernel(bexp_ref, first_ref, slot_ref, next_ref, lead_ref, nused_ref, x_ref, w1_hbm, w3_hbm, w2_hbm,
                    y_ref, w1b, w3b, w2b, wsems):
    i = pl.program_id(0)
    nused = nused_ref[0]

    def weight_copies(e, s):
        return (pltpu.make_async_copy(w1_hbm.at[e], w1b.at[s], wsems.at[s, 0]),
                pltpu.make_async_copy(w3_hbm.at[e], w3b.at[s], wsems.at[s, 1]),
                pltpu.make_async_copy(w2_hbm.at[e], w2b.at[s], wsems.at[s, 2]))

    def start_weights(e, s):
        for n, cp in enumerate(weight_copies(e, s)):
            cp.start(priority=WEIGHT_DMA_PRIORITY[n])

    @pl.when(i == 0)
    def _():
        for n in range(WEIGHT_SLOTS - 1):
            @pl.when(lead_ref[n] >= 0)
            def _():
                start_weights(lead_ref[n], n)

    @pl.when(i < nused)
    def _():
        s = slot_ref[i]

        @pl.when(first_ref[i] == 1)
        def _():
            @pl.when(next_ref[i] >= 0)
            def _():
                start_weights(next_ref[i], (s + WEIGHT_SLOTS - 1) % WEIGHT_SLOTS)
            for cp in weight_copies(bexp_ref[i], s):
                cp.wait()

        blk = x_ref.shape[0] // SUBLANES
        x_lo, x_hi = _unpack_bf16_pairs(_load_row_tiles(x_ref, 0, blk))
        half = x_lo.shape[1]
        a = (jnp.dot(x_lo, w1b[s, 0:half, :], preferred_element_type=F32)
             + jnp.dot(x_hi, w1b[s, half:2 * half, :], preferred_element_type=F32))
        b = (jnp.dot(x_lo, w3b[s, 0:half, :], preferred_element_type=F32)
             + jnp.dot(x_hi, w3b[s, half:2 * half, :], preferred_element_type=F32))
        hdn = a * jax.nn.sigmoid(a) * b
        y = jnp.dot(hdn, w2b[s], preferred_element_type=F32)
        _store_row_tiles(y_ref, 0, _pack_bf16_pairs(y))

    @pl.when(i >= nused)
    def _():
        y_ref[...] = jnp.zeros(y_ref.shape, y_ref.dtype)


def _experts(bexp, first, slot, nxt, lead, nused, xs, w1, w3, w2, blk):
    P = xs.shape[0] // SUBLANES
    E, D, Fh = w1.shape
    nblk = P // blk
    smap = lambda i, *_: (i, 0)
    grid_spec = pltpu.PrefetchScalarGridSpec(
        num_scalar_prefetch=6,
        grid=(nblk,),
        in_specs=[pl.BlockSpec((blk * SUBLANES, LANES), smap),
                  pl.BlockSpec(memory_space=pl.ANY),
                  pl.BlockSpec(memory_space=pl.ANY),
                  pl.BlockSpec(memory_space=pl.ANY)],
        out_specs=pl.BlockSpec((blk * SUBLANES, LANES), smap),
        scratch_shapes=[pltpu.VMEM((WEIGHT_SLOTS, D, Fh), w1.dtype),
                        pltpu.VMEM((WEIGHT_SLOTS, D, Fh), w3.dtype),
                        pltpu.VMEM((WEIGHT_SLOTS, Fh, D), w2.dtype),
                        pltpu.SemaphoreType.DMA((WEIGHT_SLOTS, 3))],
    )
    return pl.pallas_call(
        _experts_kernel,
        grid_spec=grid_spec,
        out_shape=jax.ShapeDtypeStruct(xs.shape, xs.dtype),
        compiler_params=_cparams(("arbitrary",), 56),
        name="experts",
    )(bexp, first, slot, nxt, lead, nused, xs, w1, w3, w2)


def _row_gather(src_hbm, idx_ref, base, dst, sem, n):
    def body(c, carry):
        r0 = pl.multiple_of(c * ROW_UNROLL, ROW_UNROLL)
        for s in range(ROW_UNROLL):
            d = idx_ref[base + r0 + s]
            pltpu.make_async_copy(src_hbm.at[pl.ds(pl.multiple_of(d * SUBLANES, SUBLANES), SUBLANES)],
                                  dst.at[pl.ds(pl.multiple_of((r0 + s) * SUBLANES, SUBLANES), SUBLANES)],
                                  sem).start(priority=s % 2)
        return carry
    lax.fori_loop(0, n // ROW_UNROLL, body, 0)


def _row_gather_wait(src_hbm, dst, sem, n):
    pltpu.make_async_copy(src_hbm.at[pl.ds(0, n * SUBLANES)], dst, sem).wait()


def _combine_kernel(dest_ref, h_ref, info_ref, y_hbm, fw_ref, o_ref, ybuf, sems, *, tm, eps):
    i = pl.program_id(0)
    n = pl.num_programs(0)
    slot = i % 2
    nrow = TOP_K * tm

    @pl.when(i == 0)
    def _():
        _row_gather(y_hbm, dest_ref, 0, ybuf.at[0], sems.at[0], nrow)

    @pl.when(i + 1 < n)
    def _():
        _row_gather(y_hbm, dest_ref, (i + 1) * nrow, ybuf.at[1 - slot], sems.at[1 - slot], nrow)

    _row_gather_wait(y_hbm, ybuf.at[slot], sems.at[slot], nrow)
    half = o_ref.shape[1] // 2
    rc = tm

    def rows(ci, carry):
        r0 = pl.multiple_of(ci * rc, rc)
        rs = pl.ds(r0, rc)
        info = info_ref[rs, :]
        g1 = info[:, 2:3]
        g2 = info[:, 3:4]
        y1_lo, y1_hi = _unpack_bf16_pairs(_load_row_tiles(ybuf.at[slot], r0, rc))
        y2_lo, y2_hi = _unpack_bf16_pairs(_load_row_tiles(ybuf.at[slot], tm + r0, rc))
        h_lo = h_ref[rs, 0:half] + (g1 * y1_lo + g2 * y2_lo)
        h_hi = h_ref[rs, half:2 * half] + (g1 * y1_hi + g2 * y2_hi)
        ms = (jnp.sum(h_lo * h_lo, axis=-1, keepdims=True)
              + jnp.sum(h_hi * h_hi, axis=-1, keepdims=True)) / (2 * half)
        r = lax.rsqrt(ms + eps)
        o_ref[rs, 0:half] = h_lo * r * fw_ref[:, 0:half]
        o_ref[rs, half:2 * half] = h_hi * r * fw_ref[:, half:2 * half]
        return carry

    lax.fori_loop(0, tm // rc, rows, 0)


def _combine(dest_km, h, info, yb, final_w, tm):
    T, D = h.shape
    grid_spec = pltpu.PrefetchScalarGridSpec(
        num_scalar_prefetch=1,
        grid=(T // tm,),
        in_specs=[pl.BlockSpec((tm, D), lambda i, d: (i, 0)),
                  pl.BlockSpec((tm, LANES), lambda i, d: (i, 0)),
                  pl.BlockSpec(memory_space=pl.ANY),
                  pl.BlockSpec((1, D), lambda i, d: (0, 0))],
        out_specs=pl.BlockSpec((tm, D), lambda i, d: (i, 0)),
        scratch_shapes=[pltpu.VMEM((2, TOP_K * tm * SUBLANES, LANES), yb.dtype),
                        pltpu.SemaphoreType.DMA((2,))],
    )
    return pl.pallas_call(
        functools.partial(_combine_kernel, tm=tm, eps=RMS_EPS),
        grid_spec=grid_spec,
        out_shape=jax.ShapeDtypeStruct((T, D), F32),
        compiler_params=_cparams(("arbitrary",), 48),
        name="combine",
    )(dest_km, h, info, yb, final_w)


def _layer(h_in, l, B, S, mix_norm_w, w_in, conv_dw_w, conv_dw_b, conv_ln_w, conv_ln_b,
           lam_q1, lam_k1, lam_q2, lam_k2, attn_subln_w, w_out, ffn_norm_w,
           w_group, b_group, w_expert_gate, b_expert_gate, w1, w3, w2):
    T, D = h_in.shape
    d_conv = conv_dw_w.shape[1]
    d_attn = (w_in.shape[1] - 2 * d_conv) // 3
    ng = w_group.shape[1]
    ne = w_expert_gate.shape[1]
    epg = ne // ng
    assert ng + ne <= LANES
    lam_init = 0.8 - 0.6 * math.exp(-0.3 * l)

    proj = _norm_inproj(h_in, mix_norm_w.reshape(1, D), w_in.astype(BF16))
    y_conv = _conformer(proj, conv_dw_w, conv_dw_b, conv_ln_w, conv_ln_b, B, S)
    y_attn = _diff_attn(proj, lam_q1, lam_k1, lam_q2, lam_k2, attn_subln_w, B, S, d_conv, d_attn, lam_init)

    wr = jnp.concatenate([w_group, w_expert_gate, jnp.zeros((D, LANES - ng - ne), F32)], axis=1)
    wr_hi = wr.astype(BF16)
    wr_lo = (wr - wr_hi.astype(F32)).astype(BF16)
    br = jnp.concatenate([b_group, b_expert_gate.reshape(-1), jnp.zeros((LANES - ng - ne,), F32)]).reshape(1, LANES)
    wr_hl = jnp.concatenate([wr_hi, wr_lo], axis=1)
    h, logits, ug = _outproj(h_in, y_conv, y_attn, w_out.astype(BF16), ffn_norm_w.reshape(1, D), wr_hl, br)

    info, cnt = _route(logits, ng, epg)

    blk = 256
    tmd = min(256, T)
    A = T * TOP_K
    nblk = (A + ne * (blk - 1) + blk - 1) // blk
    i32 = jnp.int32
    counts = cnt[0, ng:ng + ne].astype(i32)
    padded = (counts + blk - 1) // blk * blk
    pad_ends = jnp.cumsum(padded).astype(i32)
    pad_starts = pad_ends - padded
    eid = info[:, 0:TOP_K].astype(i32)
    rank = info[:, 4:4 + TOP_K].astype(i32)
    onehot = eid[:, :, None] == jnp.arange(ne, dtype=i32)[None, None, :]
    dest = jnp.sum(jnp.where(onehot, pad_starts[None, None, :], 0), axis=-1).astype(i32) + rank
    nused = (pad_ends[-1] // blk).astype(i32)
    bpos = jnp.arange(nblk, dtype=i32)
    bexp = jnp.minimum(jnp.searchsorted(pad_ends, jnp.minimum(bpos, nused - 1) * blk, side="right"), ne - 1).astype(i32)
    first = ((bpos < nused) & ((bpos == 0) | (bexp != jnp.roll(bexp, 1)))).astype(i32)
    slot = ((jnp.cumsum(first) - 1) % WEIGHT_SLOTS).astype(i32)
    used_idx = jnp.where(padded > 0, jnp.arange(ne, dtype=i32), ne)
    suffix_min = lax.cummin(used_idx, reverse=True)
    next_used = jnp.concatenate([suffix_min[1:], jnp.full((2,), ne, i32)])
    ahead = jnp.arange(ne, dtype=i32)
    for _ in range(WEIGHT_SLOTS - 1):
        ahead = next_used[ahead]
    ahead = jnp.where(ahead >= ne, -1, ahead)
    nxt = ahead[bexp].astype(i32)
    lead = [suffix_min[0]]
    for _ in range(WEIGHT_SLOTS - 2):
        lead.append(next_used[lead[-1]])
    lead = jnp.stack([jnp.where(e >= ne, -1, e) for e in lead]).astype(i32)

    xs = _dispatch(dest.reshape(-1), pad_ends, ug, nblk * blk, tmd, blk)
    yb = _experts(bexp, first, slot, nxt, lead, nused.reshape(1), xs, w1, w3, w2, blk)
    return h, info, dest, yb


def kernel(x, mix_norm_w, w_in, conv_dw_w, conv_dw_b, conv_ln_w, conv_ln_b, lam_q1, lam_k1, lam_q2, lam_k2,
           attn_subln_w, w_out, ffn_norm_w, w_group, b_group, w_expert_gate, b_expert_gate, w1, w3, w2,
           final_norm_w):
    B, S, D = x.shape
    depth = w_in.shape[0]
    assert depth == 1
    T = B * S
    tmc = min(256, T)
    h = x.reshape(T, D)
    for l in range(depth):
        h, info, dest, yb = _layer(
            h, l, B, S, mix_norm_w[l], w_in[l], conv_dw_w[l], conv_dw_b[l], conv_ln_w[l], conv_ln_b[l],
            lam_q1[l], lam_k1[l], lam_q2[l], lam_k2[l], attn_subln_w[l], w_out[l], ffn_norm_w[l],
            w_group[l], b_group[l], w_expert_gate[l], b_expert_gate[l], w1[l], w3[l], w2[l])
        dest_km = dest.reshape(T // tmc, tmc, TOP_K).transpose(0, 2, 1).reshape(-1)
        h = _combine(dest_km, h, info, yb, final_norm_w.reshape(1, D), tmc)
    return h.reshape(B, S, D)
```

```python
import functools
import math

import jax
import jax.numpy as jnp
from jax import lax
from jax.experimental import pallas as pl
from jax.experimental.pallas import tpu as pltpu

F32 = jnp.float32
BF16 = jnp.bfloat16

RMS_EPS = 1e-6
SUBLN_EPS = 1e-5
LN_EPS = 1e-5
TOP_K = 2
LANES = 128
SUBLANES = 8
NEG_BIG = -1e30
MIB = 1024 * 1024


def _cparams(sem, vmem_mib):
    return pltpu.CompilerParams(dimension_semantics=sem, vmem_limit_bytes=vmem_mib * MIB)


def _norm_inproj_kernel(x_ref, nw_ref, w_ref, o_ref, u_ref, *, eps, rc):
    @pl.when(pl.program_id(1) == 0)
    def _():
        for r0 in range(0, x_ref.shape[0], rc):
            rows = slice(r0, r0 + rc)
            x = x_ref[rows, :]
            ms = jnp.mean(x * x, axis=-1, keepdims=True)
            u = (x * lax.rsqrt(ms + eps) * nw_ref[...]).astype(u_ref.dtype)
            u_ref[rows, :] = u
            o_ref[rows, :] = jnp.dot(u, w_ref[...], preferred_element_type=F32).astype(o_ref.dtype)

    @pl.when(pl.program_id(1) > 0)
    def _():
        o_ref[...] = jnp.dot(u_ref[...], w_ref[...], preferred_element_type=F32).astype(o_ref.dtype)


def _norm_inproj(x2, nw, w_bf):
    T, D = x2.shape
    N = w_bf.shape[1]
    tm = min(1024, T)
    tn = 1024
    return pl.pallas_call(
        functools.partial(_norm_inproj_kernel, eps=RMS_EPS, rc=256),
        grid=(T // tm, N // tn),
        in_specs=[pl.BlockSpec((tm, D), lambda i, j: (i, 0)),
                  pl.BlockSpec((1, D), lambda i, j: (0, 0)),
                  pl.BlockSpec((D, tn), lambda i, j: (0, j))],
        out_specs=pl.BlockSpec((tm, tn), lambda i, j: (i, j)),
        out_shape=jax.ShapeDtypeStruct((T, N), BF16),
        scratch_shapes=[pltpu.VMEM((tm, D), BF16)],
        compiler_params=_cparams(("arbitrary", "arbitrary"), 48),
        name="norm_inproj",
    )(x2, nw, w_bf)


def _conv_kernel(a_ref, g_ref, w_ref, b_ref, lw_ref, lb_ref, o_ref, ubuf, cbuf, sh, *, ts, kw, halo, eps):
    s = pl.program_id(1)
    C = a_ref.shape[1]

    @pl.when(s == 0)
    def _():
        ubuf[0:halo, :] = jnp.zeros((halo, C), F32)

    @pl.when(s > 0)
    def _():
        ubuf[0:halo, :] = ubuf[ts:ts + halo, :]

    rg = 64
    for r0 in range(0, ts, rg):
        a = a_ref[r0:r0 + rg, :].astype(F32)
        g = g_ref[r0:r0 + rg, :].astype(F32)
        ubuf[halo + r0:halo + r0 + rg, :] = a * jax.nn.sigmoid(g)

    nsh = sh.shape[1]
    for r in range(1, SUBLANES):
        for i0 in range(0, nsh, rg):
            n = min(rg, nsh - i0)
            sh[r - 1, i0:i0 + n, :] = ubuf[i0 + r:i0 + r + n, :]

    off = halo - (kw - 1)
    rc = 64

    def conv_ln_rows(ri, carry):
        r0 = pl.multiple_of(ri * rc, rc)
        for c in range(C // LANES):
            cs = slice(c * LANES, (c + 1) * LANES)
            acc = jnp.broadcast_to(b_ref[0:1, cs], (rc // SUBLANES, SUBLANES, LANES))
            for k in range(kw):
                q, r = divmod(off + k, SUBLANES)
                rows = pl.ds(r0 + q * SUBLANES, rc)
                tap = ubuf[rows, cs] if r == 0 else sh[r - 1, rows, cs]
                acc = acc + w_ref[k, :, cs] * tap.reshape(rc // SUBLANES, SUBLANES, LANES)
            cbuf[pl.ds(r0, rc), cs] = acc.reshape(rc, LANES)
        cv = cbuf[pl.ds(r0, rc), :]
        mu = jnp.mean(cv, axis=-1, keepdims=True)
        d = cv - mu
        var = jnp.mean(d * d, axis=-1, keepdims=True)
        un = d * lax.rsqrt(var + eps) * lw_ref[...] + lb_ref[...]
        o_ref[pl.ds(r0, rc), :] = (un * jax.nn.sigmoid(un)).astype(o_ref.dtype)
        return carry

    lax.fori_loop(0, ts // rc, conv_ln_rows, 0)


def _conformer(proj, dw_w, dw_b, ln_w, ln_b, B, S):
    T = proj.shape[0]
    kw, C = dw_w.shape
    ts = min(256, S)
    halo = 32
    assert kw - 1 <= halo and S % ts == 0 and ts >= halo
    ns = S // ts
    wp = jnp.broadcast_to(dw_w[:, None, :], (kw, SUBLANES, C))
    return pl.pallas_call(
        functools.partial(_conv_kernel, ts=ts, kw=kw, halo=halo, eps=LN_EPS),
        grid=(B, ns),
        in_specs=[pl.BlockSpec((ts, C), lambda b, s: (b * ns + s, 0)),
                  pl.BlockSpec((ts, C), lambda b, s: (b * ns + s, 1)),
                  pl.BlockSpec((kw, SUBLANES, C), lambda b, s: (0, 0, 0)),
                  pl.BlockSpec((1, C), lambda b, s: (0, 0)),
                  pl.BlockSpec((1, C), lambda b, s: (0, 0)),
                  pl.BlockSpec((1, C), lambda b, s: (0, 0))],
        out_specs=pl.BlockSpec((ts, C), lambda b, s: (b * ns + s, 0)),
        out_shape=jax.ShapeDtypeStruct((T, C), BF16),
        scratch_shapes=[pltpu.VMEM((ts + halo, C), F32), pltpu.VMEM((ts, C), F32),
                        pltpu.VMEM((SUBLANES - 1, ts + halo - SUBLANES, C), F32)],
        compiler_params=_cparams(("arbitrary", "arbitrary"), 32),
        name="conformer",
    )(proj, proj, wp, dw_b.reshape(1, C), ln_w.reshape(1, C), ln_b.reshape(1, C))


def _attn_kernel(lq1_ref, lk1_ref, lq2_ref, lk2_ref, q_ref, k_ref, v_ref, sw_ref, o_ref,
                 vxt_ref, qq_ref, *scr, tq, hd, hp, ones_rows, lam_init, eps):
    i = pl.program_id(2)
    vd = 2 * hd
    nkb = v_ref.shape[0] // tq
    acc_refs, m_refs, sa_refs, sb_refs = (scr[n * hp:(n + 1) * hp] for n in range(4))

    @pl.when(i == 0)
    def _():
        for u in range(hp):
            for jb in range(nkb):
                vblk = v_ref[jb * tq:(jb + 1) * tq, u * vd:(u + 1) * vd].astype(F32)
                vxt_ref[u, jb, 0:vd, :] = vblk.T.astype(vxt_ref.dtype)
                vxt_ref[u, jb, vd:vd + ones_rows, :] = jnp.ones((ones_rows, tq), vxt_ref.dtype)

    lam = (jnp.exp(jnp.sum(lq1_ref[...] * lk1_ref[...], axis=-1, keepdims=True))
           - jnp.exp(jnp.sum(lq2_ref[...] * lk2_ref[...], axis=-1, keepdims=True)) + lam_init)

    for u in range(hp):
        q = q_ref[:, u * vd:(u + 1) * vd]
        qs = q * jnp.asarray(hd ** -0.5, q.dtype)
        lane = lax.broadcasted_iota(jnp.int32, q.shape, 1)
        zero = jnp.zeros_like(qs)
        qq_ref[u, 0:tq, :] = jnp.where(lane < hd, qs, zero)
        qq_ref[u, tq:2 * tq, :] = jnp.where(lane >= hd, qs, zero)
        acc_refs[u][...] = jnp.zeros(acc_refs[u].shape, F32)
        m_refs[u][...] = jnp.full(m_refs[u].shape, NEG_BIG, F32)

    def scores(j, u, dst):
        rows = pl.ds(pl.multiple_of(j * tq, tq), tq)
        kb = k_ref[rows, u * vd:(u + 1) * vd]
        dst[...] = lax.dot_general(kb, qq_ref[u], (((1,), (1,)), ((), ())), preferred_element_type=F32)

    def softmax_pv(j, u, src, masked):
        st = src[...]
        if masked:
            r = lax.broadcasted_iota(jnp.int32, st.shape, 0)
            c = lax.broadcasted_iota(jnp.int32, st.shape, 1)
            st = jnp.where(r <= jnp.where(c >= tq, c - tq, c), st, NEG_BIG)
        m_old = m_refs[u][...]
        m_new = jnp.maximum(m_old, jnp.max(st, axis=0, keepdims=True))
        alpha = jnp.exp(m_old - m_new)
        pt = jnp.exp(st - m_new).astype(vxt_ref.dtype)
        pv = jnp.dot(vxt_ref[u, j], pt, preferred_element_type=F32)
        acc_refs[u][...] = alpha * acc_refs[u][...] + pv
        m_refs[u][...] = m_new

    def half(j, cur, nxt):
        for u in range(hp):
            scores(j + 1, u, nxt[u])
            softmax_pv(j, u, cur[u], False)

    for u in range(hp):
        scores(0, u, sa_refs[u])

    def pair(t, carry):
        half(2 * t, sa_refs, sb_refs)
        half(2 * t + 1, sb_refs, sa_refs)
        return carry

    lax.fori_loop(0, i // 2, pair, 0)

    @pl.when(i % 2 == 1)
    def _():
        half(i - 1, sa_refs, sb_refs)
        for u in range(hp):
            softmax_pv(i, u, sb_refs[u], True)

    @pl.when(i % 2 == 0)
    def _():
        for u in range(hp):
            softmax_pv(i, u, sa_refs[u], True)

    for u in range(hp):
        acc = acc_refs[u][...]
        o12 = acc[0:vd] * (1.0 / acc[vd:vd + 1])
        ot = o12[:, 0:tq] - lam * o12[:, tq:2 * tq]
        msq = jnp.mean(ot * ot, axis=0, keepdims=True)
        o = (ot * lax.rsqrt(msq + eps)).T
        o_ref[:, u * vd:(u + 1) * vd] = (o * sw_ref[...] * (1.0 - lam_init)).astype(o_ref.dtype)


def _diff_attn(proj, lq1, lk1, lq2, lk2, subln_w, B, S, d_conv, d_attn, lam_init):
    T = proj.shape[0]
    vd = subln_w.shape[0]
    hd = lq1.shape[0]
    assert vd == LANES and 2 * hd == vd
    H = d_attn // vd
    hp = 8
    ones_rows = 16
    tq = min(256, S)
    nq = S // tq
    assert H % hp == 0
    qc = 2 * d_conv // (hp * vd)
    kc = qc + H // hp
    vc = kc + H // hp
    lspec = pl.BlockSpec((1, hd), lambda b, h, i: (0, 0))
    return pl.pallas_call(
        functools.partial(_attn_kernel, tq=tq, hd=hd, hp=hp, ones_rows=ones_rows, lam_init=lam_init, eps=SUBLN_EPS),
        grid=(B, H // hp, nq),
        in_specs=[lspec, lspec, lspec, lspec,
                  pl.BlockSpec((tq, hp * vd), lambda b, h, i: (b * nq + i, qc + h)),
                  pl.BlockSpec((S, hp * vd), lambda b, h, i: (b, kc + h)),
                  pl.BlockSpec((S, hp * vd), lambda b, h, i: (b, vc + h)),
                  pl.BlockSpec((1, vd), lambda b, h, i: (0, 0))],
        out_specs=pl.BlockSpec((tq, hp * vd), lambda b, h, i: (b * nq + i, h)),
        out_shape=jax.ShapeDtypeStruct((T, d_attn), BF16),
        scratch_shapes=[pltpu.VMEM((hp, S // tq, vd + ones_rows, tq), BF16),
                        pltpu.VMEM((hp, 2 * tq, vd), BF16)]
                       + [pltpu.VMEM((vd + ones_rows, 2 * tq), F32) for _ in range(hp)]
                       + [pltpu.VMEM((1, 2 * tq), F32) for _ in range(hp)]
                       + [pltpu.VMEM((tq, 2 * tq), F32) for _ in range(2 * hp)],
        compiler_params=_cparams(("arbitrary", "arbitrary", "arbitrary"), 52),
        name="diff_attn",
    )(lq1.reshape(1, hd), lk1.reshape(1, hd), lq2.reshape(1, hd), lk2.reshape(1, hd),
      proj, proj, proj, subln_w.reshape(1, vd))


def _pack_bf16_pairs(v):
    n = v.shape[1] // 2
    return _pack_bf16_words(v[:, 0:n], v[:, n:2 * n])


def _pack_bf16_words(lo, hi):
    lo_bits = lax.bitcast_convert_type(lo.astype(BF16).astype(F32), jnp.uint32)
    hi_bits = lax.bitcast_convert_type(hi.astype(BF16).astype(F32), jnp.uint32)
    return (lo_bits >> 16) | (hi_bits & jnp.uint32(0xFFFF0000))


def _unpack_bf16_pairs(w):
    lo = lax.bitcast_convert_type(w << 16, F32)
    hi = lax.bitcast_convert_type(w & jnp.uint32(0xFFFF0000), F32)
    return lo, hi


def _store_row_tiles(ref, row0, words):
    m, w = words.shape
    assert w == SUBLANES * LANES
    for s in range(SUBLANES):
        ref[pl.ds(row0 * SUBLANES + s, m, stride=SUBLANES), :] = words[:, s * LANES:(s + 1) * LANES]


def _load_row_tiles(ref, row0, m):
    return jnp.concatenate([ref[pl.ds(row0 * SUBLANES + s, m, stride=SUBLANES), :] for s in range(SUBLANES)], axis=1)


def _outproj_kernel(x_ref, yc_ref, ya_ref, wo_ref, fw_ref, wr_ref, br_ref, h_ref, lg_ref, ug_ref, *, eps, nsub):
    dc = yc_ref.shape[1]
    sub = x_ref.shape[0] // nsub
    mixes = []
    for t in range(nsub):
        rows = slice(t * sub, (t + 1) * sub)
        mixes.append(jnp.dot(yc_ref[rows, :], wo_ref[0:dc, :], preferred_element_type=F32)
                     + jnp.dot(ya_ref[rows, :], wo_ref[dc:, :], preferred_element_type=F32))
    for t in range(nsub):
        rows = slice(t * sub, (t + 1) * sub)
        h = x_ref[rows, :] + mixes[t]
        h_ref[rows, :] = h
        ms = jnp.mean(h * h, axis=-1, keepdims=True)
        un = h * lax.rsqrt(ms + eps) * fw_ref[...]
        hi = un.astype(BF16)
        lo = (un - hi.astype(F32)).astype(BF16)
        hh = jnp.dot(hi, wr_ref[...], preferred_element_type=F32)
        lh = jnp.dot(lo, wr_ref[:, 0:LANES], preferred_element_type=F32)
        lg_ref[rows, :] = hh[:, 0:LANES] + hh[:, LANES:2 * LANES] + lh + br_ref[...]
        _store_row_tiles(ug_ref, t * sub, _pack_bf16_pairs(un))


def _outproj(x2, y_conv, y_attn, wo_bf, ffn_w, wr_hl, br):
    T, D = x2.shape
    dc = y_conv.shape[1]
    da = y_attn.shape[1]
    tm = min(512, T)
    nsub = 2 if tm % 512 == 0 else 1
    return pl.pallas_call(
        functools.partial(_outproj_kernel, eps=RMS_EPS, nsub=nsub),
        grid=(T // tm,),
        in_specs=[pl.BlockSpec((tm, D), lambda i: (i, 0)),
                  pl.BlockSpec((tm, dc), lambda i: (i, 0)),
                  pl.BlockSpec((tm, da), lambda i: (i, 0)),
                  pl.BlockSpec((dc + da, D), lambda i: (0, 0), pipeline_mode=pl.Buffered(1)),
                  pl.BlockSpec((1, D), lambda i: (0, 0)),
                  pl.BlockSpec((D, 2 * LANES), lambda i: (0, 0)),
                  pl.BlockSpec((1, LANES), lambda i: (0, 0))],
        out_specs=[pl.BlockSpec((tm, D), lambda i: (i, 0)),
                   pl.BlockSpec((tm, LANES), lambda i: (i, 0)),
                   pl.BlockSpec((tm * SUBLANES, LANES), lambda i: (i, 0))],
        out_shape=[jax.ShapeDtypeStruct((T, D), F32), jax.ShapeDtypeStruct((T, LANES), F32),
                   jax.ShapeDtypeStruct((T * SUBLANES, LANES), jnp.uint32)],
        compiler_params=_cparams(("arbitrary",), 56),
        name="outproj",
    )(x2, y_conv, y_attn, wo_bf, ffn_w, wr_hl, br)


def _route_kernel(lg_ref, info_ref, cnt_ref, *, ng, epg):
    i = pl.program_id(0)
    lg = lg_ref[...]
    tm = lg.shape[0]
    lane = lax.broadcasted_iota(jnp.int32, lg.shape, 1)
    lanef = lane.astype(F32)
    ne = ng * epg

    def first_argmax(vals):
        mx = jnp.max(vals, axis=-1, keepdims=True)
        idx = jnp.min(jnp.where(vals == mx, lanef, float(LANES)), axis=-1, keepdims=True)
        return mx, idx

    gmask = lane < ng
    gl = jnp.where(gmask, lg, NEG_BIG)
    gmax, gsel = first_argmax(gl)
    gsum = jnp.sum(jnp.where(gmask, jnp.exp(gl - gmax), 0.0), axis=-1, keepdims=True)
    g_w = 1.0 / gsum
    lo = gsel * epg + ng
    emask = (lanef >= lo) & (lanef < lo + epg)
    el = jnp.where(emask, lg, NEG_BIG)
    v1, i1 = first_argmax(el)
    el2 = jnp.where(lanef == i1, NEG_BIG, el)
    v2, i2 = first_argmax(el2)
    e2 = jnp.exp(v2 - v1)
    p1 = 1.0 / (1.0 + e2)
    gate1 = g_w * p1
    gate2 = g_w * (e2 * p1)

    oh1 = lanef == i1
    oh2 = lanef == i2
    cmat = jnp.where(oh1 | oh2, 1.0, 0.0).astype(BF16)
    r = lax.broadcasted_iota(jnp.int32, (tm, tm), 0)
    c = lax.broadcasted_iota(jnp.int32, (tm, tm), 1)
    tri = jnp.where(c < r, 1.0, 0.0).astype(BF16)

    @pl.when(i == 0)
    def _():
        cnt_ref[...] = jnp.zeros(cnt_ref.shape, F32)

    carry = cnt_ref[0:1, :]
    prefix = jnp.dot(tri, cmat, preferred_element_type=F32) + carry
    rank1 = jnp.sum(jnp.where(oh1, prefix, 0.0), axis=-1, keepdims=True)
    rank2 = jnp.sum(jnp.where(oh2, prefix, 0.0), axis=-1, keepdims=True)
    cnt_ref[...] = jnp.broadcast_to(carry + jnp.sum(cmat.astype(F32), axis=0, keepdims=True), cnt_ref.shape)

    info = jnp.where(lane == 0, i1 - ng,
           jnp.where(lane == 1, i2 - ng,
           jnp.where(lane == 2, gate1,
           jnp.where(lane == 3, gate2,
           jnp.where(lane == 4, rank1,
           jnp.where(lane == 5, rank2, 0.0))))))
    info_ref[...] = info


def _route(logits, ng, epg):
    T = logits.shape[0]
    tm = min(512, T)
    return pl.pallas_call(
        functools.partial(_route_kernel, ng=ng, epg=epg),
        grid=(T // tm,),
        in_specs=[pl.BlockSpec((tm, LANES), lambda i: (i, 0))],
        out_specs=[pl.BlockSpec((tm, LANES), lambda i: (i, 0)),
                   pl.BlockSpec((8, LANES), lambda i: (0, 0))],
        out_shape=[jax.ShapeDtypeStruct((T, LANES), F32), jax.ShapeDtypeStruct((8, LANES), F32)],
        compiler_params=_cparams(("arbitrary",), 32),
        name="route",
    )(logits)


ROW_UNROLL = 8


def _dispatch_kernel(dest_ref, pend_ref, ug_ref, xs_hbm, ring, zbuf, sems, zsem, *, tm, blk, ne):
    i = pl.program_id(0)
    n = pl.num_programs(0)
    slot = i % 2

    def seg_tail(e):
        end = pend_ref[e]
        start = jnp.where(e == 0, 0, pend_ref[jnp.maximum(e - 1, 0)])
        tail = pl.multiple_of(jnp.maximum(end - blk, 0) * SUBLANES, blk * SUBLANES)
        return pltpu.make_async_copy(zbuf, xs_hbm.at[pl.ds(tail, blk * SUBLANES)], zsem), end > start

    @pl.when(i == 0)
    def _():
        zbuf[...] = jnp.zeros(zbuf.shape, zbuf.dtype)

        def zstart(e, carry):
            cp, nonempty = seg_tail(e)

            @pl.when(nonempty)
            def _():
                cp.start()
            return carry

        def zwait(e, carry):
            cp, nonempty = seg_tail(e)

            @pl.when(nonempty)
            def _():
                cp.wait()
            return carry

        def spare(b):
            row = pl.multiple_of(b * blk, blk)
            tile = pl.multiple_of(row * SUBLANES, blk * SUBLANES)
            return pltpu.make_async_copy(zbuf, xs_hbm.at[pl.ds(tile, blk * SUBLANES)], zsem), row >= pend_ref[ne - 1]

        def sstart(b, carry):
            cp, unused = spare(b)

            @pl.when(unused)
            def _():
                cp.start()
            return carry

        def swait(b, carry):
            cp, unused = spare(b)

            @pl.when(unused)
            def _():
                cp.wait()
            return carry

        nblk = xs_hbm.shape[0] // (blk * SUBLANES)
        lax.fori_loop(0, ne, zstart, 0)
        lax.fori_loop(0, nblk, sstart, 0)
        lax.fori_loop(0, ne, zwait, 0)
        lax.fori_loop(0, nblk, swait, 0)

    def drain(s):
        for _ in range(TOP_K):
            pltpu.make_async_copy(ring.at[s], xs_hbm.at[pl.ds(0, tm * SUBLANES)], sems.at[s]).wait()

    @pl.when(i >= 2)
    def _():
        drain(slot)

    ring[slot] = ug_ref[...]

    def issue(c, carry):
        r0 = pl.multiple_of(c * ROW_UNROLL, ROW_UNROLL)
        for s in range(ROW_UNROLL):
            for k in range(TOP_K):
                d = dest_ref[(i * tm + r0 + s) * TOP_K + k]
                src = ring.at[slot, pl.ds(pl.multiple_of((r0 + s) * SUBLANES, SUBLANES), SUBLANES)]
                dst = xs_hbm.at[pl.ds(pl.multiple_of(d * SUBLANES, SUBLANES), SUBLANES)]
                pltpu.make_async_copy(src, dst, sems.at[slot]).start(priority=k % 2)
        return carry

    lax.fori_loop(0, tm // ROW_UNROLL, issue, 0)

    @pl.when(i == n - 1)
    def _():
        drain(slot)

        @pl.when(n >= 2)
        def _():
            drain(1 - slot)


def _dispatch(dest_flat, pad_ends, ug, nrows, tm, blk):
    T = ug.shape[0] // SUBLANES
    ne = pad_ends.shape[0]
    grid_spec = pltpu.PrefetchScalarGridSpec(
        num_scalar_prefetch=2,
        grid=(T // tm,),
        in_specs=[pl.BlockSpec((tm * SUBLANES, LANES), lambda i, d, p: (i, 0))],
        out_specs=pl.BlockSpec(memory_space=pl.ANY),
        scratch_shapes=[pltpu.VMEM((2, tm * SUBLANES, LANES), ug.dtype),
                        pltpu.VMEM((blk * SUBLANES, LANES), ug.dtype),
                        pltpu.SemaphoreType.DMA((2,)),
                        pltpu.SemaphoreType.DMA(())],
    )
    return pl.pallas_call(
        functools.partial(_dispatch_kernel, tm=tm, blk=blk, ne=ne),
        grid_spec=grid_spec,
        out_shape=jax.ShapeDtypeStruct((nrows * SUBLANES, LANES), ug.dtype),
        compiler_params=_cparams(("arbitrary",), 32),
        name="dispatch",
    )(dest_flat, pad_ends, ug)


WEIGHT_DMA_PRIORITY = (0, 1, 1)
WEIGHT_SLOTS = 3


def _experts_kernel(bexp_ref, first_ref, slot_ref, next_ref, lead_ref, nused_ref, x_ref, w1_hbm, w3_hbm, w2_hbm,
                    y_ref, w1b, w3b, w2b, wsems):
    i = pl.program_id(0)
    nused = nused_ref[0]

    def weight_copies(e, s):
        return (pltpu.make_async_copy(w1_hbm.at[e], w1b.at[s], wsems.at[s, 0]),
                pltpu.make_async_copy(w3_hbm.at[e], w3b.at[s], wsems.at[s, 1]),
                pltpu.make_async_copy(w2_hbm.at[e], w2b.at[s], wsems.at[s, 2]))

    def start_weights(e, s):
        for n, cp in enumerate(weight_copies(e, s)):
            cp.start(priority=WEIGHT_DMA_PRIORITY[n])

    @pl.when(i == 0)
    def _():
        for n in range(WEIGHT_SLOTS - 1):
            @pl.when(lead_ref[n] >= 0)
            def _():
                start_weights(lead_ref[n], n)

    @pl.when(i < nused)
    def _():
        s = slot_ref[i]

        @pl.when(first_ref[i] == 1)
        def _():
            @pl.when(next_ref[i] >= 0)
            def _():
                start_weights(next_ref[i], (s + WEIGHT_SLOTS - 1) % WEIGHT_SLOTS)
            for cp in weight_copies(bexp_ref[i], s):
                cp.wait()

        blk = x_ref.shape[0] // SUBLANES
        x_lo, x_hi = _unpack_bf16_pairs(_load_row_tiles(x_ref, 0, blk))
        half = x_lo.shape[1]
        a = (jnp.dot(x_lo, w1b[s, 0:half, :], preferred_element_type=F32)
             + jnp.dot(x_hi, w1b[s, half:2 * half, :], preferred_element_type=F32))
        b = (jnp.dot(x_lo, w3b[s, 0:half, :], preferred_element_type=F32)
             + jnp.dot(x_hi, w3b[s, half:2 * half, :], preferred_element_type=F32))
        hdn = a * jax.nn.sigmoid(a) * b
        cw = 2 * LANES
        for c0 in range(0, half, cw):
            y_lo = jnp.dot(hdn, w2b[s, :, c0:c0 + cw], preferred_element_type=F32)
            y_hi = jnp.dot(hdn, w2b[s, :, half + c0:half + c0 + cw], preferred_element_type=F32)
            words = _pack_bf16_words(y_lo, y_hi)
            for g in range(cw // LANES):
                y_ref[pl.ds(c0 // LANES + g, blk, stride=SUBLANES), :] = words[:, g * LANES:(g + 1) * LANES]

    @pl.when(i >= nused)
    def _():
        y_ref[...] = jnp.zeros(y_ref.shape, y_ref.dtype)


def _experts(bexp, first, slot, nxt, lead, nused, xs, w1, w3, w2, blk):
    P = xs.shape[0] // SUBLANES
    E, D, Fh = w1.shape
    nblk = P // blk
    smap = lambda i, *_: (i, 0)
    grid_spec = pltpu.PrefetchScalarGridSpec(
        num_scalar_prefetch=6,
        grid=(nblk,),
        in_specs=[pl.BlockSpec((blk * SUBLANES, LANES), smap),
                  pl.BlockSpec(memory_space=pl.ANY),
                  pl.BlockSpec(memory_space=pl.ANY),
                  pl.BlockSpec(memory_space=pl.ANY)],
        out_specs=pl.BlockSpec((blk * SUBLANES, LANES), smap),
        scratch_shapes=[pltpu.VMEM((WEIGHT_SLOTS, D, Fh), w1.dtype),
                        pltpu.VMEM((WEIGHT_SLOTS, D, Fh), w3.dtype),
                        pltpu.VMEM((WEIGHT_SLOTS, Fh, D), w2.dtype),
                        pltpu.SemaphoreType.DMA((WEIGHT_SLOTS, 3))],
    )
    return pl.pallas_call(
        _experts_kernel,
        grid_spec=grid_spec,
        out_shape=jax.ShapeDtypeStruct(xs.shape, xs.dtype),
        compiler_params=_cparams(("arbitrary",), 56),
        name="experts",
    )(bexp, first, slot, nxt, lead, nused, xs, w1, w3, w2)


def _row_gather(src_hbm, idx_ref, base, dst, sem, n):
    def body(c, carry):
        r0 = pl.multiple_of(c * ROW_UNROLL, ROW_UNROLL)
        for s in range(ROW_UNROLL):
            d = idx_ref[base + r0 + s]
            pltpu.make_async_copy(src_hbm.at[pl.ds(pl.multiple_of(d * SUBLANES, SUBLANES), SUBLANES)],
                                  dst.at[pl.ds(pl.multiple_of((r0 + s) * SUBLANES, SUBLANES), SUBLANES)],
                                  sem).start(priority=s % 2)
        return carry
    lax.fori_loop(0, n // ROW_UNROLL, body, 0)


def _row_gather_wait(src_hbm, dst, sem, n):
    pltpu.make_async_copy(src_hbm.at[pl.ds(0, n * SUBLANES)], dst, sem).wait()


def _combine_kernel(dest_ref, h_ref, info_ref, y_hbm, fw_ref, o_ref, ybuf, sems, *, tm, eps):
    i = pl.program_id(0)
    n = pl.num_programs(0)
    slot = i % 2
    nrow = TOP_K * tm

    @pl.when(i == 0)
    def _():
        _row_gather(y_hbm, dest_ref, 0, ybuf.at[0], sems.at[0], nrow)

    @pl.when(i + 1 < n)
    def _():
        _row_gather(y_hbm, dest_ref, (i + 1) * nrow, ybuf.at[1 - slot], sems.at[1 - slot], nrow)

    _row_gather_wait(y_hbm, ybuf.at[slot], sems.at[slot], nrow)
    half = o_ref.shape[1] // 2
    rc = tm

    def rows(ci, carry):
        r0 = pl.multiple_of(ci * rc, rc)
        rs = pl.ds(r0, rc)
        info = info_ref[rs, :]
        g1 = info[:, 2:3]
        g2 = info[:, 3:4]
        y1_lo, y1_hi = _unpack_bf16_pairs(_load_row_tiles(ybuf.at[slot], r0, rc))
        y2_lo, y2_hi = _unpack_bf16_pairs(_load_row_tiles(ybuf.at[slot], tm + r0, rc))
        h_lo = h_ref[rs, 0:half] + (g1 * y1_lo + g2 * y2_lo)
        h_hi = h_ref[rs, half:2 * half] + (g1 * y1_hi + g2 * y2_hi)
        ms = (jnp.sum(h_lo * h_lo, axis=-1, keepdims=True)
              + jnp.sum(h_hi * h_hi, axis=-1, keepdims=True)) / (2 * half)
        r = lax.rsqrt(ms + eps)
        o_ref[rs, 0:half] = h_lo * r * fw_ref[:, 0:half]
        o_ref[rs, half:2 * half] = h_hi * r * fw_ref[:, half:2 * half]
        return carry

    lax.fori_loop(0, tm // rc, rows, 0)


def _combine(dest_km, h, info, yb, final_w, tm):
    T, D = h.shape
    grid_spec = pltpu.PrefetchScalarGridSpec(
        num_scalar_prefetch=1,
        grid=(T // tm,),
        in_specs=[pl.BlockSpec((tm, D), lambda i, d: (i, 0)),
                  pl.BlockSpec((tm, LANES), lambda i, d: (i, 0)),
                  pl.BlockSpec(memory_space=pl.ANY),
                  pl.BlockSpec((1, D), lambda i, d: (0, 0))],
        out_specs=pl.BlockSpec((tm, D), lambda i, d: (i, 0)),
        scratch_shapes=[pltpu.VMEM((2, TOP_K * tm * SUBLANES, LANES), yb.dtype),
                        pltpu.SemaphoreType.DMA((2,))],
    )
    return pl.pallas_call(
        functools.partial(_combine_kernel, tm=tm, eps=RMS_EPS),
        grid_spec=grid_spec,
        out_shape=jax.ShapeDtypeStruct((T, D), F32),
        compiler_params=_cparams(("arbitrary",), 48),
        name="combine",
    )(dest_km, h, info, yb, final_w)


def _layer(h_in, l, B, S, mix_norm_w, w_in, conv_dw_w, conv_dw_b, conv_ln_w, conv_ln_b,
           lam_q1, lam_k1, lam_q2, lam_k2, attn_subln_w, w_out, ffn_norm_w,
           w_group, b_group, w_expert_gate, b_expert_gate, w1, w3, w2):
    T, D = h_in.shape
    d_conv = conv_dw_w.shape[1]
    d_attn = (w_in.shape[1] - 2 * d_conv) // 3
    ng = w_group.shape[1]
    ne = w_expert_gate.shape[1]
    epg = ne // ng
    assert ng + ne <= LANES
    lam_init = 0.8 - 0.6 * math.exp(-0.3 * l)

    proj = _norm_inproj(h_in, mix_norm_w.reshape(1, D), w_in.astype(BF16))
    y_conv = _conformer(proj, conv_dw_w, conv_dw_b, conv_ln_w, conv_ln_b, B, S)
    y_attn = _diff_attn(proj, lam_q1, lam_k1, lam_q2, lam_k2, attn_subln_w, B, S, d_conv, d_attn, lam_init)

    wr = jnp.concatenate([w_group, w_expert_gate, jnp.zeros((D, LANES - ng - ne), F32)], axis=1)
    wr_hi = wr.astype(BF16)
    wr_lo = (wr - wr_hi.astype(F32)).astype(BF16)
    br = jnp.concatenate([b_group, b_expert_gate.reshape(-1), jnp.zeros((LANES - ng - ne,), F32)]).reshape(1, LANES)
    wr_hl = jnp.concatenate([wr_hi, wr_lo], axis=1)
    h, logits, ug = _outproj(h_in, y_conv, y_attn, w_out.astype(BF16), ffn_norm_w.reshape(1, D), wr_hl, br)

    info, cnt = _route(logits, ng, epg)

    blk = 256
    tmd = min(256, T)
    A = T * TOP_K
    nblk = (A + ne * (blk - 1) + blk - 1) // blk
    i32 = jnp.int32
    counts = cnt[0, ng:ng + ne].astype(i32)
    padded = (counts + blk - 1) // blk * blk
    pad_ends = jnp.cumsum(padded).astype(i32)
    pad_starts = pad_ends - padded
    eid = info[:, 0:TOP_K].astype(i32)
    rank = info[:, 4:4 + TOP_K].astype(i32)
    onehot = eid[:, :, None] == jnp.arange(ne, dtype=i32)[None, None, :]
    dest = jnp.sum(jnp.where(onehot, pad_starts[None, None, :], 0), axis=-1).astype(i32) + rank
    nused = (pad_ends[-1] // blk).astype(i32)
    bpos = jnp.arange(nblk, dtype=i32)
    brow = jnp.minimum(bpos, nused - 1) * blk
    bexp = jnp.minimum(jnp.sum((pad_ends[None, :] <= brow[:, None]).astype(i32), axis=1), ne - 1)
    first = ((bpos < nused) & ((bpos == 0) | (bexp != jnp.roll(bexp, 1)))).astype(i32)
    slot = ((jnp.cumsum(first) - 1) % WEIGHT_SLOTS).astype(i32)
    used_idx = jnp.where(padded > 0, jnp.arange(ne, dtype=i32), ne)
    suffix_min = lax.cummin(used_idx, reverse=True)
    next_used = jnp.concatenate([suffix_min[1:], jnp.full((2,), ne, i32)])
    ahead = jnp.arange(ne, dtype=i32)
    for _ in range(WEIGHT_SLOTS - 1):
        ahead = next_used[ahead]
    ahead = jnp.where(ahead >= ne, -1, ahead)
    nxt = ahead[bexp].astype(i32)
    lead = [suffix_min[0]]
    for _ in range(WEIGHT_SLOTS - 2):
        lead.append(next_used[lead[-1]])
    lead = jnp.stack([jnp.where(e >= ne, -1, e) for e in lead]).astype(i32)

    xs = _dispatch(dest.reshape(-1), pad_ends, ug, nblk * blk, tmd, blk)
    yb = _experts(bexp, first, slot, nxt, lead, nused.reshape(1), xs, w1, w3, w2, blk)
    return h, info, dest, yb


def kernel(x, mix_norm_w, w_in, conv_dw_w, conv_dw_b, conv_ln_w, conv_ln_b, lam_q1, lam_k1, lam_q2, lam_k2,
           attn_subln_w, w_out, ffn_norm_w, w_group, b_group, w_expert_gate, b_expert_gate, w1, w3, w2,
           final_norm_w):
    B, S, D = x.shape
    depth = w_in.shape[0]
    assert depth == 1
    T = B * S
    tmc = min(256, T)
    h = x.reshape(T, D)
    for l in range(depth):
        h, info, dest, yb = _layer(
            h, l, B, S, mix_norm_w[l], w_in[l], conv_dw_w[l], conv_dw_b[l], conv_ln_w[l], conv_ln_b[l],
            lam_q1[l], lam_k1[l], lam_q2[l], lam_k2[l], attn_subln_w[l], w_out[l], ffn_norm_w[l],
            w_group[l], b_group[l], w_expert_gate[l], b_expert_gate[l], w1[l], w3[l], w2[l])
        dest_km = dest.reshape(T // tmc, tmc, TOP_K).transpose(0, 2, 1).reshape(-1)
        h = _combine(dest_km, h, info, yb, final_norm_w.reshape(1, D), tmc)
    return h.reshape(B, S, D)
```

```python
import functools
import math

import jax
import jax.numpy as jnp
from jax import lax
from jax.experimental import pallas as pl
from jax.experimental.pallas import tpu as pltpu

F32 = jnp.float32
BF16 = jnp.bfloat16

RMS_EPS = 1e-6
SUBLN_EPS = 1e-5
LN_EPS = 1e-5
TOP_K = 2
LANES = 128
SUBLANES = 8
NEG_BIG = -1e30
MIB = 1024 * 1024


def _cparams(sem, vmem_mib):
    return pltpu.CompilerParams(dimension_semantics=sem, vmem_limit_bytes=vmem_mib * MIB)


def _norm_inproj_kernel(x_ref, nw_ref, w_ref, o_ref, u_ref, *, eps, rc):
    @pl.when(pl.program_id(1) == 0)
    def _():
        for r0 in range(0, x_ref.shape[0], rc):
            rows = slice(r0, r0 + rc)
            x = x_ref[rows, :]
            ms = jnp.mean(x * x, axis=-1, keepdims=True)
            u = (x * lax.rsqrt(ms + eps) * nw_ref[...]).astype(u_ref.dtype)
            u_ref[rows, :] = u
            o_ref[rows, :] = jnp.dot(u, w_ref[...], preferred_element_type=F32).astype(o_ref.dtype)

    @pl.when(pl.program_id(1) > 0)
    def _():
        o_ref[...] = jnp.dot(u_ref[...], w_ref[...], preferred_element_type=F32).astype(o_ref.dtype)


def _norm_inproj(x2, nw, w_bf):
    T, D = x2.shape
    N = w_bf.shape[1]
    tm = min(1024, T)
    tn = 1024
    return pl.pallas_call(
        functools.partial(_norm_inproj_kernel, eps=RMS_EPS, rc=256),
        grid=(T // tm, N // tn),
        in_specs=[pl.BlockSpec((tm, D), lambda i, j: (i, 0)),
                  pl.BlockSpec((1, D), lambda i, j: (0, 0)),
                  pl.BlockSpec((D, tn), lambda i, j: (0, j))],
        out_specs=pl.BlockSpec((tm, tn), lambda i, j: (i, j)),
        out_shape=jax.ShapeDtypeStruct((T, N), BF16),
        scratch_shapes=[pltpu.VMEM((tm, D), BF16)],
        compiler_params=_cparams(("arbitrary", "arbitrary"), 48),
        name="norm_inproj",
    )(x2, nw, w_bf)


def _conv_kernel(a_ref, g_ref, w_ref, b_ref, lw_ref, lb_ref, o_ref, ubuf, cbuf, sh, *, ts, kw, halo, eps):
    s = pl.program_id(1)
    C = a_ref.shape[1]

    @pl.when(s == 0)
    def _():
        ubuf[0:halo, :] = jnp.zeros((halo, C), F32)

    @pl.when(s > 0)
    def _():
        ubuf[0:halo, :] = ubuf[ts:ts + halo, :]

    rg = 64
    for r0 in range(0, ts, rg):
        a = a_ref[r0:r0 + rg, :].astype(F32)
        g = g_ref[r0:r0 + rg, :].astype(F32)
        ubuf[halo + r0:halo + r0 + rg, :] = a * jax.nn.sigmoid(g)

    nsh = sh.shape[1]
    for r in range(1, SUBLANES):
        for i0 in range(0, nsh, rg):
            n = min(rg, nsh - i0)
            sh[r - 1, i0:i0 + n, :] = ubuf[i0 + r:i0 + r + n, :]

    off = halo - (kw - 1)
    rc = 64

    def conv_ln_rows(ri, carry):
        r0 = pl.multiple_of(ri * rc, rc)
        for c in range(C // LANES):
            cs = slice(c * LANES, (c + 1) * LANES)
            acc = jnp.broadcast_to(b_ref[0:1, cs], (rc // SUBLANES, SUBLANES, LANES))
            for k in range(kw):
                q, r = divmod(off + k, SUBLANES)
                rows = pl.ds(r0 + q * SUBLANES, rc)
                tap = ubuf[rows, cs] if r == 0 else sh[r - 1, rows, cs]
                acc = acc + w_ref[k, :, cs] * tap.reshape(rc // SUBLANES, SUBLANES, LANES)
            cbuf[pl.ds(r0, rc), cs] = acc.reshape(rc, LANES)
        cv = cbuf[pl.ds(r0, rc), :]
        mu = jnp.mean(cv, axis=-1, keepdims=True)
        d = cv - mu
        var = jnp.mean(d * d, axis=-1, keepdims=True)
        un = d * lax.rsqrt(var + eps) * lw_ref[...] + lb_ref[...]
        o_ref[pl.ds(r0, rc), :] = (un * jax.nn.sigmoid(un)).astype(o_ref.dtype)
        return carry

    lax.fori_loop(0, ts // rc, conv_ln_rows, 0)


def _conformer(proj, dw_w, dw_b, ln_w, ln_b, B, S):
    T = proj.shape[0]
    kw, C = dw_w.shape
    ts = min(256, S)
    halo = 32
    assert kw - 1 <= halo and S % ts == 0 and ts >= halo
    ns = S // ts
    wp = jnp.broadcast_to(dw_w[:, None, :], (kw, SUBLANES, C))
    return pl.pallas_call(
        functools.partial(_conv_kernel, ts=ts, kw=kw, halo=halo, eps=LN_EPS),
        grid=(B, ns),
        in_specs=[pl.BlockSpec((ts, C), lambda b, s: (b * ns + s, 0)),
                  pl.BlockSpec((ts, C), lambda b, s: (b * ns + s, 1)),
                  pl.BlockSpec((kw, SUBLANES, C), lambda b, s: (0, 0, 0)),
                  pl.BlockSpec((1, C), lambda b, s: (0, 0)),
                  pl.BlockSpec((1, C), lambda b, s: (0, 0)),
                  pl.BlockSpec((1, C), lambda b, s: (0, 0))],
        out_specs=pl.BlockSpec((ts, C), lambda b, s: (b * ns + s, 0)),
        out_shape=jax.ShapeDtypeStruct((T, C), BF16),
        scratch_shapes=[pltpu.VMEM((ts + halo, C), F32), pltpu.VMEM((ts, C), F32),
                        pltpu.VMEM((SUBLANES - 1, ts + halo - SUBLANES, C), F32)],
        compiler_params=_cparams(("arbitrary", "arbitrary"), 32),
        name="conformer",
    )(proj, proj, wp, dw_b.reshape(1, C), ln_w.reshape(1, C), ln_b.reshape(1, C))


def _attn_kernel(lq1_ref, lk1_ref, lq2_ref, lk2_ref, q_ref, k_ref, v_ref, sw_ref, o_ref,
                 vxt_ref, qq_ref, *scr, tq, hd, hp, ones_rows, lam_init, eps):
    i = pl.program_id(2)
    vd = 2 * hd
    nkb = v_ref.shape[0] // tq
    acc_refs, m_refs, sa_refs, sb_refs = (scr[n * hp:(n + 1) * hp] for n in range(4))

    @pl.when(i == 0)
    def _():
        for u in range(hp):
            for jb in range(nkb):
                vblk = v_ref[jb * tq:(jb + 1) * tq, u * vd:(u + 1) * vd].astype(F32)
                vxt_ref[u, jb, 0:vd, :] = vblk.T.astype(vxt_ref.dtype)
                vxt_ref[u, jb, vd:vd + ones_rows, :] = jnp.ones((ones_rows, tq), vxt_ref.dtype)

    lam = (jnp.exp(jnp.sum(lq1_ref[...] * lk1_ref[...], axis=-1, keepdims=True))
           - jnp.exp(jnp.sum(lq2_ref[...] * lk2_ref[...], axis=-1, keepdims=True)) + lam_init)

    for u in range(hp):
        q = q_ref[:, u * vd:(u + 1) * vd]
        qs = q * jnp.asarray(hd ** -0.5, q.dtype)
        lane = lax.broadcasted_iota(jnp.int32, q.shape, 1)
        zero = jnp.zeros_like(qs)
        qq_ref[u, 0:tq, :] = jnp.where(lane < hd, qs, zero)
        qq_ref[u, tq:2 * tq, :] = jnp.where(lane >= hd, qs, zero)
        acc_refs[u][...] = jnp.zeros(acc_refs[u].shape, F32)
        m_refs[u][...] = jnp.full(m_refs[u].shape, NEG_BIG, F32)

    def scores(j, u, dst):
        rows = pl.ds(pl.multiple_of(j * tq, tq), tq)
        kb = k_ref[rows, u * vd:(u + 1) * vd]
        dst[...] = lax.dot_general(kb, qq_ref[u], (((1,), (1,)), ((), ())), preferred_element_type=F32)

    def softmax_pv(j, u, src, masked):
        st = src[...]
        if masked:
            r = lax.broadcasted_iota(jnp.int32, st.shape, 0)
            c = lax.broadcasted_iota(jnp.int32, st.shape, 1)
            st = jnp.where(r <= jnp.where(c >= tq, c - tq, c), st, NEG_BIG)
        m_old = m_refs[u][...]
        m_new = jnp.maximum(m_old, jnp.max(st, axis=0, keepdims=True))
        alpha = jnp.exp(m_old - m_new)
        pt = jnp.exp(st - m_new).astype(vxt_ref.dtype)
        pv = jnp.dot(vxt_ref[u, j], pt, preferred_element_type=F32)
        acc_refs[u][...] = alpha * acc_refs[u][...] + pv
        m_refs[u][...] = m_new

    def half(j, cur, nxt):
        for u in range(hp):
            scores(j + 1, u, nxt[u])
            softmax_pv(j, u, cur[u], False)

    for u in range(hp):
        scores(0, u, sa_refs[u])

    def pair(t, carry):
        half(2 * t, sa_refs, sb_refs)
        half(2 * t + 1, sb_refs, sa_refs)
        return carry

    lax.fori_loop(0, i // 2, pair, 0)

    @pl.when(i % 2 == 1)
    def _():
        half(i - 1, sa_refs, sb_refs)
        for u in range(hp):
            softmax_pv(i, u, sb_refs[u], True)

    @pl.when(i % 2 == 0)
    def _():
        for u in range(hp):
            softmax_pv(i, u, sa_refs[u], True)

    for u in range(hp):
        acc = acc_refs[u][...]
        o12 = acc[0:vd] * (1.0 / acc[vd:vd + 1])
        ot = o12[:, 0:tq] - lam * o12[:, tq:2 * tq]
        msq = jnp.mean(ot * ot, axis=0, keepdims=True)
        o = (ot * lax.rsqrt(msq + eps)).T
        o_ref[:, u * vd:(u + 1) * vd] = (o * sw_ref[...] * (1.0 - lam_init)).astype(o_ref.dtype)


def _diff_attn(proj, lq1, lk1, lq2, lk2, subln_w, B, S, d_conv, d_attn, lam_init):
    T = proj.shape[0]
    vd = subln_w.shape[0]
    hd = lq1.shape[0]
    assert vd == LANES and 2 * hd == vd
    H = d_attn // vd
    hp = 8
    ones_rows = 16
    tq = min(256, S)
    nq = S // tq
    assert H % hp == 0
    qc = 2 * d_conv // (hp * vd)
    kc = qc + H // hp
    vc = kc + H // hp
    lspec = pl.BlockSpec((1, hd), lambda b, h, i: (0, 0))
    return pl.pallas_call(
        functools.partial(_attn_kernel, tq=tq, hd=hd, hp=hp, ones_rows=ones_rows, lam_init=lam_init, eps=SUBLN_EPS),
        grid=(B, H // hp, nq),
        in_specs=[lspec, lspec, lspec, lspec,
                  pl.BlockSpec((tq, hp * vd), lambda b, h, i: (b * nq + i, qc + h)),
                  pl.BlockSpec((S, hp * vd), lambda b, h, i: (b, kc + h)),
                  pl.BlockSpec((S, hp * vd), lambda b, h, i: (b, vc + h)),
                  pl.BlockSpec((1, vd), lambda b, h, i: (0, 0))],
        out_specs=pl.BlockSpec((tq, hp * vd), lambda b, h, i: (b * nq + i, h)),
        out_shape=jax.ShapeDtypeStruct((T, d_attn), BF16),
        scratch_shapes=[pltpu.VMEM((hp, S // tq, vd + ones_rows, tq), BF16),
                        pltpu.VMEM((hp, 2 * tq, vd), BF16)]
                       + [pltpu.VMEM((vd + ones_rows, 2 * tq), F32) for _ in range(hp)]
                       + [pltpu.VMEM((1, 2 * tq), F32) for _ in range(hp)]
                       + [pltpu.VMEM((tq, 2 * tq), F32) for _ in range(2 * hp)],
        compiler_params=_cparams(("arbitrary", "arbitrary", "arbitrary"), 52),
        name="diff_attn",
    )(lq1.reshape(1, hd), lk1.reshape(1, hd), lq2.reshape(1, hd), lk2.reshape(1, hd),
      proj, proj, proj, subln_w.reshape(1, vd))


def _pack_bf16_pairs(v):
    n = v.shape[1] // 2
    return _pack_bf16_words(v[:, 0:n], v[:, n:2 * n])


def _pack_bf16_words(lo, hi):
    lo_bits = lax.bitcast_convert_type(lo.astype(BF16).astype(F32), jnp.uint32)
    hi_bits = lax.bitcast_convert_type(hi.astype(BF16).astype(F32), jnp.uint32)
    return (lo_bits >> 16) | (hi_bits & jnp.uint32(0xFFFF0000))


def _unpack_bf16_pairs(w):
    lo = lax.bitcast_convert_type(w << 16, F32)
    hi = lax.bitcast_convert_type(w & jnp.uint32(0xFFFF0000), F32)
    return lo, hi


def _store_row_tiles(ref, row0, words):
    m, w = words.shape
    assert w == SUBLANES * LANES
    for s in range(SUBLANES):
        ref[pl.ds(row0 * SUBLANES + s, m, stride=SUBLANES), :] = words[:, s * LANES:(s + 1) * LANES]


def _load_row_tiles(ref, row0, m):
    return jnp.concatenate([ref[pl.ds(row0 * SUBLANES + s, m, stride=SUBLANES), :] for s in range(SUBLANES)], axis=1)


def _outproj_kernel(x_ref, yc_ref, ya_ref, wo_ref, fw_ref, wr_ref, br_ref, h_ref, lg_ref, ug_ref, *, eps, nsub):
    dc = yc_ref.shape[1]
    sub = x_ref.shape[0] // nsub
    mixes = []
    for t in range(nsub):
        rows = slice(t * sub, (t + 1) * sub)
        mixes.append(jnp.dot(yc_ref[rows, :], wo_ref[0:dc, :], preferred_element_type=F32)
                     + jnp.dot(ya_ref[rows, :], wo_ref[dc:, :], preferred_element_type=F32))
    for t in range(nsub):
        rows = slice(t * sub, (t + 1) * sub)
        h = x_ref[rows, :] + mixes[t]
        h_ref[rows, :] = h
        ms = jnp.mean(h * h, axis=-1, keepdims=True)
        un = h * lax.rsqrt(ms + eps) * fw_ref[...]
        hi = un.astype(BF16)
        lo = (un - hi.astype(F32)).astype(BF16)
        hh = jnp.dot(hi, wr_ref[...], preferred_element_type=F32)
        lh = jnp.dot(lo, wr_ref[:, 0:LANES], preferred_element_type=F32)
        lg_ref[rows, :] = hh[:, 0:LANES] + hh[:, LANES:2 * LANES] + lh + br_ref[...]
        _store_row_tiles(ug_ref, t * sub, _pack_bf16_pairs(un))


def _outproj(x2, y_conv, y_attn, wo_bf, ffn_w, wr_hl, br):
    T, D = x2.shape
    dc = y_conv.shape[1]
    da = y_attn.shape[1]
    tm = min(512, T)
    nsub = 2 if tm % 512 == 0 else 1
    return pl.pallas_call(
        functools.partial(_outproj_kernel, eps=RMS_EPS, nsub=nsub),
        grid=(T // tm,),
        in_specs=[pl.BlockSpec((tm, D), lambda i: (i, 0)),
                  pl.BlockSpec((tm, dc), lambda i: (i, 0)),
                  pl.BlockSpec((tm, da), lambda i: (i, 0)),
                  pl.BlockSpec((dc + da, D), lambda i: (0, 0), pipeline_mode=pl.Buffered(1)),
                  pl.BlockSpec((1, D), lambda i: (0, 0)),
                  pl.BlockSpec((D, 2 * LANES), lambda i: (0, 0)),
                  pl.BlockSpec((1, LANES), lambda i: (0, 0))],
        out_specs=[pl.BlockSpec((tm, D), lambda i: (i, 0)),
                   pl.BlockSpec((tm, LANES), lambda i: (i, 0)),
                   pl.BlockSpec((tm * SUBLANES, LANES), lambda i: (i, 0))],
        out_shape=[jax.ShapeDtypeStruct((T, D), F32), jax.ShapeDtypeStruct((T, LANES), F32),
                   jax.ShapeDtypeStruct((T * SUBLANES, LANES), jnp.uint32)],
        compiler_params=_cparams(("arbitrary",), 56),
        name="outproj",
    )(x2, y_conv, y_attn, wo_bf, ffn_w, wr_hl, br)


def _route_kernel(lg_ref, info_ref, cnt_ref, *, ng, epg):
    i = pl.program_id(0)
    lg = lg_ref[...]
    tm = lg.shape[0]
    lane = lax.broadcasted_iota(jnp.int32, lg.shape, 1)
    lanef = lane.astype(F32)
    ne = ng * epg

    def first_argmax(vals):
        mx = jnp.max(vals, axis=-1, keepdims=True)
        idx = jnp.min(jnp.where(vals == mx, lanef, float(LANES)), axis=-1, keepdims=True)
        return mx, idx

    gmask = lane < ng
    gl = jnp.where(gmask, lg, NEG_BIG)
    gmax, gsel = first_argmax(gl)
    gsum = jnp.sum(jnp.where(gmask, jnp.exp(gl - gmax), 0.0), axis=-1, keepdims=True)
    g_w = 1.0 / gsum
    lo = gsel * epg + ng
    emask = (lanef >= lo) & (lanef < lo + epg)
    el = jnp.where(emask, lg, NEG_BIG)
    v1, i1 = first_argmax(el)
    el2 = jnp.where(lanef == i1, NEG_BIG, el)
    v2, i2 = first_argmax(el2)
    e2 = jnp.exp(v2 - v1)
    p1 = 1.0 / (1.0 + e2)
    gate1 = g_w * p1
    gate2 = g_w * (e2 * p1)

    oh1 = lanef == i1
    oh2 = lanef == i2
    cmat = jnp.where(oh1 | oh2, 1.0, 0.0).astype(BF16)
    r = lax.broadcasted_iota(jnp.int32, (tm, tm), 0)
    c = lax.broadcasted_iota(jnp.int32, (tm, tm), 1)
    tri = jnp.where(c < r, 1.0, 0.0).astype(BF16)

    @pl.when(i == 0)
    def _():
        cnt_ref[...] = jnp.zeros(cnt_ref.shape, F32)

    carry = cnt_ref[0:1, :]
    prefix = jnp.dot(tri, cmat, preferred_element_type=F32) + carry
    rank1 = jnp.sum(jnp.where(oh1, prefix, 0.0), axis=-1, keepdims=True)
    rank2 = jnp.sum(jnp.where(oh2, prefix, 0.0), axis=-1, keepdims=True)
    cnt_ref[...] = jnp.broadcast_to(carry + jnp.sum(cmat.astype(F32), axis=0, keepdims=True), cnt_ref.shape)

    info = jnp.where(lane == 0, i1 - ng,
           jnp.where(lane == 1, i2 - ng,
           jnp.where(lane == 2, gate1,
           jnp.where(lane == 3, gate2,
           jnp.where(lane == 4, rank1,
           jnp.where(lane == 5, rank2, 0.0))))))
    info_ref[...] = info


def _route(logits, ng, epg):
    T = logits.shape[0]
    tm = min(512, T)
    return pl.pallas_call(
        functools.partial(_route_kernel, ng=ng, epg=epg),
        grid=(T // tm,),
        in_specs=[pl.BlockSpec((tm, LANES), lambda i: (i, 0))],
        out_specs=[pl.BlockSpec((tm, LANES), lambda i: (i, 0)),
                   pl.BlockSpec((8, LANES), lambda i: (0, 0))],
        out_shape=[jax.ShapeDtypeStruct((T, LANES), F32), jax.ShapeDtypeStruct((8, LANES), F32)],
        compiler_params=_cparams(("arbitrary",), 32),
        name="route",
    )(logits)


ROW_UNROLL = 8


def _dispatch_kernel(dest_ref, pend_ref, ug_ref, xs_hbm, ring, zbuf, sems, zsem, *, tm, blk, ne):
    i = pl.program_id(0)
    n = pl.num_programs(0)
    slot = i % 2

    def seg_tail(e):
        end = pend_ref[e]
        start = jnp.where(e == 0, 0, pend_ref[jnp.maximum(e - 1, 0)])
        tail = pl.multiple_of(jnp.maximum(end - blk, 0) * SUBLANES, blk * SUBLANES)
        return pltpu.make_async_copy(zbuf, xs_hbm.at[pl.ds(tail, blk * SUBLANES)], zsem), end > start

    @pl.when(i == 0)
    def _():
        zbuf[...] = jnp.zeros(zbuf.shape, zbuf.dtype)

        def zstart(e, carry):
            cp, nonempty = seg_tail(e)

            @pl.when(nonempty)
            def _():
                cp.start()
            return carry

        def zwait(e, carry):
            cp, nonempty = seg_tail(e)

            @pl.when(nonempty)
            def _():
                cp.wait()
            return carry

        def spare(b):
            row = pl.multiple_of(b * blk, blk)
            tile = pl.multiple_of(row * SUBLANES, blk * SUBLANES)
            return pltpu.make_async_copy(zbuf, xs_hbm.at[pl.ds(tile, blk * SUBLANES)], zsem), row >= pend_ref[ne - 1]

        def sstart(b, carry):
            cp, unused = spare(b)

            @pl.when(unused)
            def _():
                cp.start()
            return carry

        def swait(b, carry):
            cp, unused = spare(b)

            @pl.when(unused)
            def _():
                cp.wait()
            return carry

        nblk = xs_hbm.shape[0] // (blk * SUBLANES)
        lax.fori_loop(0, ne, zstart, 0)
        lax.fori_loop(0, nblk, sstart, 0)
        lax.fori_loop(0, ne, zwait, 0)
        lax.fori_loop(0, nblk, swait, 0)

    def drain(s):
        for _ in range(TOP_K):
            pltpu.make_async_copy(ring.at[s], xs_hbm.at[pl.ds(0, tm * SUBLANES)], sems.at[s]).wait()

    @pl.when(i >= 2)
    def _():
        drain(slot)

    ring[slot] = ug_ref[...]

    def issue(c, carry):
        r0 = pl.multiple_of(c * ROW_UNROLL, ROW_UNROLL)
        for s in range(ROW_UNROLL):
            for k in range(TOP_K):
                d = dest_ref[(i * tm + r0 + s) * TOP_K + k]
                src = ring.at[slot, pl.ds(pl.multiple_of((r0 + s) * SUBLANES, SUBLANES), SUBLANES)]
                dst = xs_hbm.at[pl.ds(pl.multiple_of(d * SUBLANES, SUBLANES), SUBLANES)]
                pltpu.make_async_copy(src, dst, sems.at[slot]).start(priority=k % 2)
        return carry

    lax.fori_loop(0, tm // ROW_UNROLL, issue, 0)

    @pl.when(i == n - 1)
    def _():
        drain(slot)

        @pl.when(n >= 2)
        def _():
            drain(1 - slot)


def _dispatch(dest_flat, pad_ends, ug, nrows, tm, blk):
    T = ug.shape[0] // SUBLANES
    ne = pad_ends.shape[0]
    grid_spec = pltpu.PrefetchScalarGridSpec(
        num_scalar_prefetch=2,
        grid=(T // tm,),
        in_specs=[pl.BlockSpec((tm * SUBLANES, LANES), lambda i, d, p: (i, 0))],
        out_specs=pl.BlockSpec(memory_space=pl.ANY),
        scratch_shapes=[pltpu.VMEM((2, tm * SUBLANES, LANES), ug.dtype),
                        pltpu.VMEM((blk * SUBLANES, LANES), ug.dtype),
                        pltpu.SemaphoreType.DMA((2,)),
                        pltpu.SemaphoreType.DMA(())],
    )
    return pl.pallas_call(
        functools.partial(_dispatch_kernel, tm=tm, blk=blk, ne=ne),
        grid_spec=grid_spec,
        out_shape=jax.ShapeDtypeStruct((nrows * SUBLANES, LANES), ug.dtype),
        compiler_params=_cparams(("arbitrary",), 32),
        name="dispatch",
    )(dest_flat, pad_ends, ug)


WEIGHT_DMA_PRIORITY = (0, 1, 1)
WEIGHT_SLOTS = 3


def _experts_kernel(bexp_ref, first_ref, slot_ref, next_ref, lead_ref, nused_ref, x_ref, w1_hbm, w3_hbm, w2_hbm,
                    y_ref, w1b, w3b, w2b, wsems):
    i = pl.program_id(0)
    nused = nused_ref[0]

    def weight_copies(e, s):
        return (pltpu.make_async_copy(w1_hbm.at[e], w1b.at[s], wsems.at[s, 0]),
                pltpu.make_async_copy(w3_hbm.at[e], w3b.at[s], wsems.at[s, 1]),
                pltpu.make_async_copy(w2_hbm.at[e], w2b.at[s], wsems.at[s, 2]))

    def start_weights(e, s):
        for n, cp in enumerate(weight_copies(e, s)):
            cp.start(priority=WEIGHT_DMA_PRIORITY[n])

    @pl.when(i == 0)
    def _():
        for n in range(WEIGHT_SLOTS - 1):
            @pl.when(lead_ref[n] >= 0)
            def _():
                start_weights(lead_ref[n], n)

    @pl.when(i < nused)
    def _():
        s = slot_ref[i]

        @pl.when(first_ref[i] == 1)
        def _():
            @pl.when(next_ref[i] >= 0)
            def _():
                start_weights(next_ref[i], (s + WEIGHT_SLOTS - 1) % WEIGHT_SLOTS)
            for cp in weight_copies(bexp_ref[i], s):
                cp.wait()

        blk = x_ref.shape[0] // SUBLANES
        x_lo, x_hi = _unpack_bf16_pairs(_load_row_tiles(x_ref, 0, blk))
        half = x_lo.shape[1]
        a = (jnp.dot(x_lo, w1b[s, 0:half, :], preferred_element_type=F32)
             + jnp.dot(x_hi, w1b[s, half:2 * half, :], preferred_element_type=F32))
        b = (jnp.dot(x_lo, w3b[s, 0:half, :], preferred_element_type=F32)
             + jnp.dot(x_hi, w3b[s, half:2 * half, :], preferred_element_type=F32))
        hdn = a * jax.nn.sigmoid(a) * b
        cw = 2 * LANES
        for c0 in range(0, half, cw):
            y_lo = jnp.dot(hdn, w2b[s, :, c0:c0 + cw], preferred_element_type=F32)
            y_hi = jnp.dot(hdn, w2b[s, :, half + c0:half + c0 + cw], preferred_element_type=F32)
            words = _pack_bf16_words(y_lo, y_hi)
            for g in range(cw // LANES):
                y_ref[pl.ds(c0 // LANES + g, blk, stride=SUBLANES), :] = words[:, g * LANES:(g + 1) * LANES]

    @pl.when(i >= nused)
    def _():
        y_ref[...] = jnp.zeros(y_ref.shape, y_ref.dtype)


def _experts(bexp, first, slot, nxt, lead, nused, xs, w1, w3, w2, blk):
    P = xs.shape[0] // SUBLANES
    E, D, Fh = w1.shape
    nblk = P // blk
    smap = lambda i, *_: (i, 0)
    grid_spec = pltpu.PrefetchScalarGridSpec(
        num_scalar_prefetch=6,
        grid=(nblk,),
        in_specs=[pl.BlockSpec((blk * SUBLANES, LANES), smap),
                  pl.BlockSpec(memory_space=pl.ANY),
                  pl.BlockSpec(memory_space=pl.ANY),
                  pl.BlockSpec(memory_space=pl.ANY)],
        out_specs=pl.BlockSpec((blk * SUBLANES, LANES), smap),
        scratch_shapes=[pltpu.VMEM((WEIGHT_SLOTS, D, Fh), w1.dtype),
                        pltpu.VMEM((WEIGHT_SLOTS, D, Fh), w3.dtype),
                        pltpu.VMEM((WEIGHT_SLOTS, Fh, D), w2.dtype),
                        pltpu.SemaphoreType.DMA((WEIGHT_SLOTS, 3))],
    )
    return pl.pallas_call(
        _experts_kernel,
        grid_spec=grid_spec,
        out_shape=jax.ShapeDtypeStruct(xs.shape, xs.dtype),
        compiler_params=_cparams(("arbitrary",), 56),
        name="experts",
    )(bexp, first, slot, nxt, lead, nused, xs, w1, w3, w2)


def _row_gather(src_hbm, idx_ref, base, dst, sem, n):
    def body(c, carry):
        r0 = pl.multiple_of(c * ROW_UNROLL, ROW_UNROLL)
        for s in range(ROW_UNROLL):
            d = idx_ref[base + r0 + s]
            pltpu.make_async_copy(src_hbm.at[pl.ds(pl.multiple_of(d * SUBLANES, SUBLANES), SUBLANES)],
                                  dst.at[pl.ds(pl.multiple_of((r0 + s) * SUBLANES, SUBLANES), SUBLANES)],
                                  sem).start(priority=s % 2)
        return carry
    lax.fori_loop(0, n // ROW_UNROLL, body, 0)


def _row_gather_wait(src_hbm, dst, sem, n):
    pltpu.make_async_copy(src_hbm.at[pl.ds(0, n * SUBLANES)], dst, sem).wait()


def _combine_kernel(dest_ref, h_ref, info_ref, y_hbm, fw_ref, o_ref, ybuf, sems, *, tm, eps):
    i = pl.program_id(0)
    n = pl.num_programs(0)
    slot = i % 2
    nrow = TOP_K * tm

    @pl.when(i == 0)
    def _():
        _row_gather(y_hbm, dest_ref, 0, ybuf.at[0], sems.at[0], nrow)

    _row_gather_wait(y_hbm, ybuf.at[slot], sems.at[slot], nrow)
    half = o_ref.shape[1] // 2
    nchunk = 4
    rc = tm // nchunk
    per = nrow // nchunk
    nbase = jnp.minimum(i + 1, n - 1) * nrow
    dst = ybuf.at[1 - slot]
    sem = sems.at[1 - slot]
    for ci in range(nchunk):
        for r in range(ci * per, (ci + 1) * per):
            d = dest_ref[nbase + r]
            pltpu.make_async_copy(y_hbm.at[pl.ds(pl.multiple_of(d * SUBLANES, SUBLANES), SUBLANES)],
                                  dst.at[pl.ds(r * SUBLANES, SUBLANES)], sem).start(priority=r % 2)
        r0 = ci * rc
        rs = slice(r0, r0 + rc)
        info = info_ref[rs, :]
        g1 = info[:, 2:3]
        g2 = info[:, 3:4]
        y1_lo, y1_hi = _unpack_bf16_pairs(_load_row_tiles(ybuf.at[slot], r0, rc))
        y2_lo, y2_hi = _unpack_bf16_pairs(_load_row_tiles(ybuf.at[slot], tm + r0, rc))
        h_lo = h_ref[rs, 0:half] + (g1 * y1_lo + g2 * y2_lo)
        h_hi = h_ref[rs, half:2 * half] + (g1 * y1_hi + g2 * y2_hi)
        ms = (jnp.sum(h_lo * h_lo, axis=-1, keepdims=True)
              + jnp.sum(h_hi * h_hi, axis=-1, keepdims=True)) / (2 * half)
        rn = lax.rsqrt(ms + eps)
        o_ref[rs, 0:half] = h_lo * rn * fw_ref[:, 0:half]
        o_ref[rs, half:2 * half] = h_hi * rn * fw_ref[:, half:2 * half]

    @pl.when(i == n - 1)
    def _():
        _row_gather_wait(y_hbm, ybuf.at[1 - slot], sems.at[1 - slot], nrow)


def _combine(dest_km, h, info, yb, final_w, tm):
    T, D = h.shape
    grid_spec = pltpu.PrefetchScalarGridSpec(
        num_scalar_prefetch=1,
        grid=(T // tm,),
        in_specs=[pl.BlockSpec((tm, D), lambda i, d: (i, 0)),
                  pl.BlockSpec((tm, LANES), lambda i, d: (i, 0)),
                  pl.BlockSpec(memory_space=pl.ANY),
                  pl.BlockSpec((1, D), lambda i, d: (0, 0))],
        out_specs=pl.BlockSpec((tm, D), lambda i, d: (i, 0)),
        scratch_shapes=[pltpu.VMEM((2, TOP_K * tm * SUBLANES, LANES), yb.dtype),
                        pltpu.SemaphoreType.DMA((2,))],
    )
    return pl.pallas_call(
        functools.partial(_combine_kernel, tm=tm, eps=RMS_EPS),
        grid_spec=grid_spec,
        out_shape=jax.ShapeDtypeStruct((T, D), F32),
        compiler_params=_cparams(("arbitrary",), 48),
        name="combine",
    )(dest_km, h, info, yb, final_w)


def _layer(h_in, l, B, S, mix_norm_w, w_in, conv_dw_w, conv_dw_b, conv_ln_w, conv_ln_b,
           lam_q1, lam_k1, lam_q2, lam_k2, attn_subln_w, w_out, ffn_norm_w,
           w_group, b_group, w_expert_gate, b_expert_gate, w1, w3, w2):
    T, D = h_in.shape
    d_conv = conv_dw_w.shape[1]
    d_attn = (w_in.shape[1] - 2 * d_conv) // 3
    ng = w_group.shape[1]
    ne = w_expert_gate.shape[1]
    epg = ne // ng
    assert ng + ne <= LANES
    lam_init = 0.8 - 0.6 * math.exp(-0.3 * l)

    proj = _norm_inproj(h_in, mix_norm_w.reshape(1, D), w_in.astype(BF16))
    y_conv = _conformer(proj, conv_dw_w, conv_dw_b, conv_ln_w, conv_ln_b, B, S)
    y_attn = _diff_attn(proj, lam_q1, lam_k1, lam_q2, lam_k2, attn_subln_w, B, S, d_conv, d_attn, lam_init)

    wr = jnp.concatenate([w_group, w_expert_gate, jnp.zeros((D, LANES - ng - ne), F32)], axis=1)
    wr_hi = wr.astype(BF16)
    wr_lo = (wr - wr_hi.astype(F32)).astype(BF16)
    br = jnp.concatenate([b_group, b_expert_gate.reshape(-1), jnp.zeros((LANES - ng - ne,), F32)]).reshape(1, LANES)
    wr_hl = jnp.concatenate([wr_hi, wr_lo], axis=1)
    h, logits, ug = _outproj(h_in, y_conv, y_attn, w_out.astype(BF16), ffn_norm_w.reshape(1, D), wr_hl, br)

    info, cnt = _route(logits, ng, epg)

    blk = 256
    tmd = min(256, T)
    A = T * TOP_K
    nblk = (A + ne * (blk - 1) + blk - 1) // blk
    i32 = jnp.int32
    counts = cnt[0, ng:ng + ne].astype(i32)
    padded = (counts + blk - 1) // blk * blk
    pad_ends = jnp.cumsum(padded).astype(i32)
    pad_starts = pad_ends - padded
    eid = info[:, 0:TOP_K].astype(i32)
    rank = info[:, 4:4 + TOP_K].astype(i32)
    onehot = eid[:, :, None] == jnp.arange(ne, dtype=i32)[None, None, :]
    dest = jnp.sum(jnp.where(onehot, pad_starts[None, None, :], 0), axis=-1).astype(i32) + rank
    nused = (pad_ends[-1] // blk).astype(i32)
    bpos = jnp.arange(nblk, dtype=i32)
    brow = jnp.minimum(bpos, nused - 1) * blk
    bexp = jnp.minimum(jnp.sum((pad_ends[None, :] <= brow[:, None]).astype(i32), axis=1), ne - 1)
    first = ((bpos < nused) & ((bpos == 0) | (bexp != jnp.roll(bexp, 1)))).astype(i32)
    slot = ((jnp.cumsum(first) - 1) % WEIGHT_SLOTS).astype(i32)
    used_idx = jnp.where(padded > 0, jnp.arange(ne, dtype=i32), ne)
    suffix_min = lax.cummin(used_idx, reverse=True)
    next_used = jnp.concatenate([suffix_min[1:], jnp.full((2,), ne, i32)])
    ahead = jnp.arange(ne, dtype=i32)
    for _ in range(WEIGHT_SLOTS - 1):
        ahead = next_used[ahead]
    ahead = jnp.where(ahead >= ne, -1, ahead)
    nxt = ahead[bexp].astype(i32)
    lead = [suffix_min[0]]
    for _ in range(WEIGHT_SLOTS - 2):
        lead.append(next_used[lead[-1]])
    lead = jnp.stack([jnp.where(e >= ne, -1, e) for e in lead]).astype(i32)

    xs = _dispatch(dest.reshape(-1), pad_ends, ug, nblk * blk, tmd, blk)
    yb = _experts(bexp, first, slot, nxt, lead, nused.reshape(1), xs, w1, w3, w2, blk)
    return h, info, dest, yb


def kernel(x, mix_norm_w, w_in, conv_dw_w, conv_dw_b, conv_ln_w, conv_ln_b, lam_q1, lam_k1, lam_q2, lam_k2,
           attn_subln_w, w_out, ffn_norm_w, w_group, b_group, w_expert_gate, b_expert_gate, w1, w3, w2,
           final_norm_w):
    B, S, D = x.shape
    depth = w_in.shape[0]
    assert depth == 1
    T = B * S
    tmc = min(256, T)
    h = x.reshape(T, D)
    for l in range(depth):
        h, info, dest, yb = _layer(
            h, l, B, S, mix_norm_w[l], w_in[l], conv_dw_w[l], conv_dw_b[l], conv_ln_w[l], conv_ln_b[l],
            lam_q1[l], lam_k1[l], lam_q2[l], lam_k2[l], attn_subln_w[l], w_out[l], ffn_norm_w[l],
            w_group[l], b_group[l], w_expert_gate[l], b_expert_gate[l], w1[l], w3[l], w2[l])
        dest_km = dest.reshape(T // tmc, tmc, TOP_K).transpose(0, 2, 1).reshape(-1)
        h = _combine(dest_km, h, info, yb, final_norm_w.reshape(1, D), tmc)
    return h.reshape(B, S, D)
```

```python
import functools
import math

import jax
import jax.numpy as jnp
from jax import lax
from jax.experimental import pallas as pl
from jax.experimental.pallas import tpu as pltpu

F32 = jnp.float32
BF16 = jnp.bfloat16

RMS_EPS = 1e-6
SUBLN_EPS = 1e-5
LN_EPS = 1e-5
TOP_K = 2
LANES = 128
SUBLANES = 8
NEG_BIG = -1e30
MIB = 1024 * 1024


def _cparams(sem, vmem_mib):
    return pltpu.CompilerParams(dimension_semantics=sem, vmem_limit_bytes=vmem_mib * MIB)


def _norm_inproj_kernel(x_ref, nw_ref, w_ref, o_ref, u_ref, *, eps, rc):
    @pl.when(pl.program_id(1) == 0)
    def _():
        for r0 in range(0, x_ref.shape[0], rc):
            rows = slice(r0, r0 + rc)
            x = x_ref[rows, :]
            ms = jnp.mean(x * x, axis=-1, keepdims=True)
            u = (x * lax.rsqrt(ms + eps) * nw_ref[...]).astype(u_ref.dtype)
            u_ref[rows, :] = u
            o_ref[rows, :] = jnp.dot(u, w_ref[...], preferred_element_type=F32).astype(o_ref.dtype)

    @pl.when(pl.program_id(1) > 0)
    def _():
        o_ref[...] = jnp.dot(u_ref[...], w_ref[...], preferred_element_type=F32).astype(o_ref.dtype)


def _norm_inproj(x2, nw, w_bf):
    T, D = x2.shape
    N = w_bf.shape[1]
    tm = min(1024, T)
    tn = 1024
    return pl.pallas_call(
        functools.partial(_norm_inproj_kernel, eps=RMS_EPS, rc=256),
        grid=(T // tm, N // tn),
        in_specs=[pl.BlockSpec((tm, D), lambda i, j: (i, 0)),
                  pl.BlockSpec((1, D), lambda i, j: (0, 0)),
                  pl.BlockSpec((D, tn), lambda i, j: (0, j))],
        out_specs=pl.BlockSpec((tm, tn), lambda i, j: (i, j)),
        out_shape=jax.ShapeDtypeStruct((T, N), BF16),
        scratch_shapes=[pltpu.VMEM((tm, D), BF16)],
        compiler_params=_cparams(("arbitrary", "arbitrary"), 48),
        name="norm_inproj",
    )(x2, nw, w_bf)


def _conv_kernel(a_ref, g_ref, w_ref, b_ref, lw_ref, lb_ref, o_ref, ubuf, cbuf, sh, *, ts, kw, halo, eps):
    s = pl.program_id(1)
    C = a_ref.shape[1]

    @pl.when(s == 0)
    def _():
        ubuf[0:halo, :] = jnp.zeros((halo, C), F32)

    @pl.when(s > 0)
    def _():
        ubuf[0:halo, :] = ubuf[ts:ts + halo, :]

    rg = 64
    for r0 in range(0, ts, rg):
        a = a_ref[r0:r0 + rg, :].astype(F32)
        g = g_ref[r0:r0 + rg, :].astype(F32)
        ubuf[halo + r0:halo + r0 + rg, :] = a * jax.nn.sigmoid(g)

    nsh = sh.shape[1]
    for r in range(1, SUBLANES):
        for i0 in range(0, nsh, rg):
            n = min(rg, nsh - i0)
            sh[r - 1, i0:i0 + n, :] = ubuf[i0 + r:i0 + r + n, :]

    off = halo - (kw - 1)
    rc = 64

    def conv_ln_rows(ri, carry):
        r0 = pl.multiple_of(ri * rc, rc)
        for c in range(C // LANES):
            cs = slice(c * LANES, (c + 1) * LANES)
            acc = jnp.broadcast_to(b_ref[0:1, cs], (rc // SUBLANES, SUBLANES, LANES))
            for k in range(kw):
                q, r = divmod(off + k, SUBLANES)
                rows = pl.ds(r0 + q * SUBLANES, rc)
                tap = ubuf[rows, cs] if r == 0 else sh[r - 1, rows, cs]
                acc = acc + w_ref[k, :, cs] * tap.reshape(rc // SUBLANES, SUBLANES, LANES)
            cbuf[pl.ds(r0, rc), cs] = acc.reshape(rc, LANES)
        cv = cbuf[pl.ds(r0, rc), :]
        mu = jnp.mean(cv, axis=-1, keepdims=True)
        d = cv - mu
        var = jnp.mean(d * d, axis=-1, keepdims=True)
        un = d * lax.rsqrt(var + eps) * lw_ref[...] + lb_ref[...]
        o_ref[pl.ds(r0, rc), :] = (un * jax.nn.sigmoid(un)).astype(o_ref.dtype)
        return carry

    lax.fori_loop(0, ts // rc, conv_ln_rows, 0)


def _conformer(proj, dw_w, dw_b, ln_w, ln_b, B, S):
    T = proj.shape[0]
    kw, C = dw_w.shape
    ts = min(256, S)
    halo = 32
    assert kw - 1 <= halo and S % ts == 0 and ts >= halo
    ns = S // ts
    wp = jnp.broadcast_to(dw_w[:, None, :], (kw, SUBLANES, C))
    return pl.pallas_call(
        functools.partial(_conv_kernel, ts=ts, kw=kw, halo=halo, eps=LN_EPS),
        grid=(B, ns),
        in_specs=[pl.BlockSpec((ts, C), lambda b, s: (b * ns + s, 0)),
                  pl.BlockSpec((ts, C), lambda b, s: (b * ns + s, 1)),
                  pl.BlockSpec((kw, SUBLANES, C), lambda b, s: (0, 0, 0)),
                  pl.BlockSpec((1, C), lambda b, s: (0, 0)),
                  pl.BlockSpec((1, C), lambda b, s: (0, 0)),
                  pl.BlockSpec((1, C), lambda b, s: (0, 0))],
        out_specs=pl.BlockSpec((ts, C), lambda b, s: (b * ns + s, 0)),
        out_shape=jax.ShapeDtypeStruct((T, C), BF16),
        scratch_shapes=[pltpu.VMEM((ts + halo, C), F32), pltpu.VMEM((ts, C), F32),
                        pltpu.VMEM((SUBLANES - 1, ts + halo - SUBLANES, C), F32)],
        compiler_params=_cparams(("arbitrary", "arbitrary"), 32),
        name="conformer",
    )(proj, proj, wp, dw_b.reshape(1, C), ln_w.reshape(1, C), ln_b.reshape(1, C))


def _attn_kernel(lq1_ref, lk1_ref, lq2_ref, lk2_ref, q_ref, k_ref, v_ref, sw_ref, o_ref,
                 vxt_ref, qq_ref, *scr, tq, hd, hp, ones_rows, lam_init, eps):
    i = pl.program_id(2)
    vd = 2 * hd
    nkb = v_ref.shape[0] // tq
    acc_refs, m_refs, sa_refs, sb_refs = (scr[n * hp:(n + 1) * hp] for n in range(4))

    @pl.when(i == 0)
    def _():
        for u in range(hp):
            for jb in range(nkb):
                vblk = v_ref[jb * tq:(jb + 1) * tq, u * vd:(u + 1) * vd].astype(F32)
                vxt_ref[u, jb, 0:vd, :] = vblk.T.astype(vxt_ref.dtype)
                vxt_ref[u, jb, vd:vd + ones_rows, :] = jnp.ones((ones_rows, tq), vxt_ref.dtype)

    lam = (jnp.exp(jnp.sum(lq1_ref[...] * lk1_ref[...], axis=-1, keepdims=True))
           - jnp.exp(jnp.sum(lq2_ref[...] * lk2_ref[...], axis=-1, keepdims=True)) + lam_init)

    for u in range(hp):
        q = q_ref[:, u * vd:(u + 1) * vd]
        qs = q * jnp.asarray(hd ** -0.5, q.dtype)
        lane = lax.broadcasted_iota(jnp.int32, q.shape, 1)
        zero = jnp.zeros_like(qs)
        qq_ref[u, 0:tq, :] = jnp.where(lane < hd, qs, zero)
        qq_ref[u, tq:2 * tq, :] = jnp.where(lane >= hd, qs, zero)
        acc_refs[u][...] = jnp.zeros(acc_refs[u].shape, F32)
        m_refs[u][...] = jnp.full(m_refs[u].shape, NEG_BIG, F32)

    def scores(j, u, dst):
        rows = pl.ds(pl.multiple_of(j * tq, tq), tq)
        kb = k_ref[rows, u * vd:(u + 1) * vd]
        dst[...] = lax.dot_general(kb, qq_ref[u], (((1,), (1,)), ((), ())), preferred_element_type=F32)

    def softmax_pv(j, u, src, masked):
        st = src[...]
        if masked:
            r = lax.broadcasted_iota(jnp.int32, st.shape, 0)
            c = lax.broadcasted_iota(jnp.int32, st.shape, 1)
            st = jnp.where(r <= jnp.where(c >= tq, c - tq, c), st, NEG_BIG)
        m_old = m_refs[u][...]
        m_new = jnp.maximum(m_old, jnp.max(st, axis=0, keepdims=True))
        alpha = jnp.exp(m_old - m_new)
        pt = jnp.exp(st - m_new).astype(vxt_ref.dtype)
        pv = jnp.dot(vxt_ref[u, j], pt, preferred_element_type=F32)
        acc_refs[u][...] = alpha * acc_refs[u][...] + pv
        m_refs[u][...] = m_new

    def half(j, cur, nxt):
        for u in range(hp):
            scores(j + 1, u, nxt[u])
            softmax_pv(j, u, cur[u], False)

    for u in range(hp):
        scores(0, u, sa_refs[u])

    def pair(t, carry):
        half(2 * t, sa_refs, sb_refs)
        half(2 * t + 1, sb_refs, sa_refs)
        return carry

    lax.fori_loop(0, i // 2, pair, 0)

    @pl.when(i % 2 == 1)
    def _():
        half(i - 1, sa_refs, sb_refs)
        for u in range(hp):
            softmax_pv(i, u, sb_refs[u], True)

    @pl.when(i % 2 == 0)
    def _():
        for u in range(hp):
            softmax_pv(i, u, sa_refs[u], True)

    for u in range(hp):
        acc = acc_refs[u][...]
        o12 = acc[0:vd] * (1.0 / acc[vd:vd + 1])
        ot = o12[:, 0:tq] - lam * o12[:, tq:2 * tq]
        msq = jnp.mean(ot * ot, axis=0, keepdims=True)
        o = (ot * lax.rsqrt(msq + eps)).T
        o_ref[:, u * vd:(u + 1) * vd] = (o * sw_ref[...] * (1.0 - lam_init)).astype(o_ref.dtype)


def _diff_attn(proj, lq1, lk1, lq2, lk2, subln_w, B, S, d_conv, d_attn, lam_init):
    T = proj.shape[0]
    vd = subln_w.shape[0]
    hd = lq1.shape[0]
    assert vd == LANES and 2 * hd == vd
    H = d_attn // vd
    hp = 8
    ones_rows = 16
    tq = min(256, S)
    nq = S // tq
    assert H % hp == 0
    qc = 2 * d_conv // (hp * vd)
    kc = qc + H // hp
    vc = kc + H // hp
    lspec = pl.BlockSpec((1, hd), lambda b, h, i: (0, 0))
    return pl.pallas_call(
        functools.partial(_attn_kernel, tq=tq, hd=hd, hp=hp, ones_rows=ones_rows, lam_init=lam_init, eps=SUBLN_EPS),
        grid=(B, H // hp, nq),
        in_specs=[lspec, lspec, lspec, lspec,
                  pl.BlockSpec((tq, hp * vd), lambda b, h, i: (b * nq + i, qc + h)),
                  pl.BlockSpec((S, hp * vd), lambda b, h, i: (b, kc + h)),
                  pl.BlockSpec((S, hp * vd), lambda b, h, i: (b, vc + h)),
                  pl.BlockSpec((1, vd), lambda b, h, i: (0, 0))],
        out_specs=pl.BlockSpec((tq, hp * vd), lambda b, h, i: (b * nq + i, h)),
        out_shape=jax.ShapeDtypeStruct((T, d_attn), BF16),
        scratch_shapes=[pltpu.VMEM((hp, S // tq, vd + ones_rows, tq), BF16),
                        pltpu.VMEM((hp, 2 * tq, vd), BF16)]
                       + [pltpu.VMEM((vd + ones_rows, 2 * tq), F32) for _ in range(hp)]
                       + [pltpu.VMEM((1, 2 * tq), F32) for _ in range(hp)]
                       + [pltpu.VMEM((tq, 2 * tq), F32) for _ in range(2 * hp)],
        compiler_params=_cparams(("arbitrary", "arbitrary", "arbitrary"), 52),
        name="diff_attn",
    )(lq1.reshape(1, hd), lk1.reshape(1, hd), lq2.reshape(1, hd), lk2.reshape(1, hd),
      proj, proj, proj, subln_w.reshape(1, vd))


def _pack_bf16_pairs(v):
    n = v.shape[1] // 2
    return _pack_bf16_words(v[:, 0:n], v[:, n:2 * n])


def _pack_bf16_words(lo, hi):
    lo_bits = lax.bitcast_convert_type(lo.astype(BF16).astype(F32), jnp.uint32)
    hi_bits = lax.bitcast_convert_type(hi.astype(BF16).astype(F32), jnp.uint32)
    return (lo_bits >> 16) | (hi_bits & jnp.uint32(0xFFFF0000))


def _unpack_bf16_pairs(w):
    lo = lax.bitcast_convert_type(w << 16, F32)
    hi = lax.bitcast_convert_type(w & jnp.uint32(0xFFFF0000), F32)
    return lo, hi


def _store_row_tiles(ref, row0, words):
    m, w = words.shape
    assert w == SUBLANES * LANES
    for s in range(SUBLANES):
        ref[pl.ds(row0 * SUBLANES + s, m, stride=SUBLANES), :] = words[:, s * LANES:(s + 1) * LANES]


def _load_row_tiles(ref, row0, m):
    return jnp.concatenate([ref[pl.ds(row0 * SUBLANES + s, m, stride=SUBLANES), :] for s in range(SUBLANES)], axis=1)


def _outproj_kernel(x_ref, yc_ref, ya_ref, wo_ref, fw_ref, wr_ref, br_ref, h_ref, lg_ref, ug_ref, *, eps, nsub):
    dc = yc_ref.shape[1]
    sub = x_ref.shape[0] // nsub
    mixes = []
    for t in range(nsub):
        rows = slice(t * sub, (t + 1) * sub)
        mixes.append(jnp.dot(yc_ref[rows, :], wo_ref[0:dc, :], preferred_element_type=F32)
                     + jnp.dot(ya_ref[rows, :], wo_ref[dc:, :], preferred_element_type=F32))
    for t in range(nsub):
        rows = slice(t * sub, (t + 1) * sub)
        h = x_ref[rows, :] + mixes[t]
        h_ref[rows, :] = h
        ms = jnp.mean(h * h, axis=-1, keepdims=True)
        un = h * lax.rsqrt(ms + eps) * fw_ref[...]
        hi = un.astype(BF16)
        lo = (un - hi.astype(F32)).astype(BF16)
        hh = jnp.dot(hi, wr_ref[...], preferred_element_type=F32)
        lh = jnp.dot(lo, wr_ref[:, 0:LANES], preferred_element_type=F32)
        lg_ref[rows, :] = hh[:, 0:LANES] + hh[:, LANES:2 * LANES] + lh + br_ref[...]
        _store_row_tiles(ug_ref, t * sub, _pack_bf16_pairs(un))


def _outproj(x2, y_conv, y_attn, wo_bf, ffn_w, wr_hl, br):
    T, D = x2.shape
    dc = y_conv.shape[1]
    da = y_attn.shape[1]
    tm = min(512, T)
    nsub = 2 if tm % 512 == 0 else 1
    return pl.pallas_call(
        functools.partial(_outproj_kernel, eps=RMS_EPS, nsub=nsub),
        grid=(T // tm,),
        in_specs=[pl.BlockSpec((tm, D), lambda i: (i, 0)),
                  pl.BlockSpec((tm, dc), lambda i: (i, 0)),
                  pl.BlockSpec((tm, da), lambda i: (i, 0)),
                  pl.BlockSpec((dc + da, D), lambda i: (0, 0), pipeline_mode=pl.Buffered(1)),
                  pl.BlockSpec((1, D), lambda i: (0, 0)),
                  pl.BlockSpec((D, 2 * LANES), lambda i: (0, 0)),
                  pl.BlockSpec((1, LANES), lambda i: (0, 0))],
        out_specs=[pl.BlockSpec((tm, D), lambda i: (i, 0)),
                   pl.BlockSpec((tm, LANES), lambda i: (i, 0)),
                   pl.BlockSpec((tm * SUBLANES, LANES), lambda i: (i, 0))],
        out_shape=[jax.ShapeDtypeStruct((T, D), F32), jax.ShapeDtypeStruct((T, LANES), F32),
                   jax.ShapeDtypeStruct((T * SUBLANES, LANES), jnp.uint32)],
        compiler_params=_cparams(("arbitrary",), 56),
        name="outproj",
    )(x2, y_conv, y_attn, wo_bf, ffn_w, wr_hl, br)


def _route_kernel(lg_ref, info_ref, cnt_ref, *, ng, epg):
    i = pl.program_id(0)
    lg = lg_ref[...]
    tm = lg.shape[0]
    lane = lax.broadcasted_iota(jnp.int32, lg.shape, 1)
    lanef = lane.astype(F32)
    ne = ng * epg

    def first_argmax(vals):
        mx = jnp.max(vals, axis=-1, keepdims=True)
        idx = jnp.min(jnp.where(vals == mx, lanef, float(LANES)), axis=-1, keepdims=True)
        return mx, idx

    gmask = lane < ng
    gl = jnp.where(gmask, lg, NEG_BIG)
    gmax, gsel = first_argmax(gl)
    gsum = jnp.sum(jnp.where(gmask, jnp.exp(gl - gmax), 0.0), axis=-1, keepdims=True)
    g_w = 1.0 / gsum
    lo = gsel * epg + ng
    emask = (lanef >= lo) & (lanef < lo + epg)
    el = jnp.where(emask, lg, NEG_BIG)
    v1, i1 = first_argmax(el)
    el2 = jnp.where(lanef == i1, NEG_BIG, el)
    v2, i2 = first_argmax(el2)
    e2 = jnp.exp(v2 - v1)
    p1 = 1.0 / (1.0 + e2)
    gate1 = g_w * p1
    gate2 = g_w * (e2 * p1)

    oh1 = lanef == i1
    oh2 = lanef == i2
    cmat = jnp.where(oh1 | oh2, 1.0, 0.0).astype(BF16)
    r = lax.broadcasted_iota(jnp.int32, (tm, tm), 0)
    c = lax.broadcasted_iota(jnp.int32, (tm, tm), 1)
    tri = jnp.where(c < r, 1.0, 0.0).astype(BF16)

    @pl.when(i == 0)
    def _():
        cnt_ref[...] = jnp.zeros(cnt_ref.shape, F32)

    carry = cnt_ref[0:1, :]
    prefix = jnp.dot(tri, cmat, preferred_element_type=F32) + carry
    rank1 = jnp.sum(jnp.where(oh1, prefix, 0.0), axis=-1, keepdims=True)
    rank2 = jnp.sum(jnp.where(oh2, prefix, 0.0), axis=-1, keepdims=True)
    cnt_ref[...] = jnp.broadcast_to(carry + jnp.sum(cmat.astype(F32), axis=0, keepdims=True), cnt_ref.shape)

    info = jnp.where(lane == 0, i1 - ng,
           jnp.where(lane == 1, i2 - ng,
           jnp.where(lane == 2, gate1,
           jnp.where(lane == 3, gate2,
           jnp.where(lane == 4, rank1,
           jnp.where(lane == 5, rank2, 0.0))))))
    info_ref[...] = info


def _route(logits, ng, epg):
    T = logits.shape[0]
    tm = min(512, T)
    return pl.pallas_call(
        functools.partial(_route_kernel, ng=ng, epg=epg),
        grid=(T // tm,),
        in_specs=[pl.BlockSpec((tm, LANES), lambda i: (i, 0))],
        out_specs=[pl.BlockSpec((tm, LANES), lambda i: (i, 0)),
                   pl.BlockSpec((8, LANES), lambda i: (0, 0))],
        out_shape=[jax.ShapeDtypeStruct((T, LANES), F32), jax.ShapeDtypeStruct((8, LANES), F32)],
        compiler_params=_cparams(("arbitrary",), 32),
        name="route",
    )(logits)


ROW_UNROLL = 8


def _rowtok_kernel(dest_ref, zeros_hbm, rt_hbm, rt_smem, sem, *, ntok):
    fill = pltpu.make_async_copy(zeros_hbm, rt_smem, sem)
    fill.start()
    fill.wait()

    def put(t, carry):
        for k in range(TOP_K):
            rt_smem[dest_ref[t * TOP_K + k]] = t
        return carry

    lax.fori_loop(0, ntok, put, 0, unroll=8)
    out = pltpu.make_async_copy(rt_smem, rt_hbm, sem)
    out.start()
    out.wait()


def _rowtok(dest_flat, nrows):
    ntok = dest_flat.shape[0] // TOP_K
    grid_spec = pltpu.PrefetchScalarGridSpec(
        num_scalar_prefetch=1,
        grid=(1,),
        in_specs=[pl.BlockSpec(memory_space=pl.ANY)],
        out_specs=pl.BlockSpec(memory_space=pl.ANY),
        scratch_shapes=[pltpu.SMEM((nrows,), jnp.int32), pltpu.SemaphoreType.DMA(())],
    )
    return pl.pallas_call(
        functools.partial(_rowtok_kernel, ntok=ntok),
        grid_spec=grid_spec,
        out_shape=jax.ShapeDtypeStruct((nrows,), jnp.int32),
        compiler_params=_cparams(("arbitrary",), 16),
        name="rowtok",
    )(dest_flat, jnp.zeros((nrows,), jnp.int32))


WEIGHT_DMA_PRIORITY = (0, 1, 1)
WEIGHT_SLOTS = 3
GATHER_GROUPS = 8


def _experts_kernel(bexp_ref, first_ref, slot_ref, next_ref, lead_ref, nused_ref, rtok_ref, ug_hbm,
                    w1_hbm, w3_hbm, w2_hbm, y_ref, xring, xsems, w1b, w3b, w2b, wsems, *, blk):
    i = pl.program_id(0)
    nused = nused_ref[0]
    xs = i % 2

    def weight_copies(e, s):
        return (pltpu.make_async_copy(w1_hbm.at[e], w1b.at[s], wsems.at[s, 0]),
                pltpu.make_async_copy(w3_hbm.at[e], w3b.at[s], wsems.at[s, 1]),
                pltpu.make_async_copy(w2_hbm.at[e], w2b.at[s], wsems.at[s, 2]))

    def start_weights(e, s):
        for n, cp in enumerate(weight_copies(e, s)):
            cp.start(priority=WEIGHT_DMA_PRIORITY[n])

    def start_row(block, r, dst_slot):
        t = rtok_ref[block * blk + r]
        pltpu.make_async_copy(ug_hbm.at[pl.ds(pl.multiple_of(t * SUBLANES, SUBLANES), SUBLANES)],
                              xring.at[dst_slot, pl.ds(r * SUBLANES, SUBLANES)], xsems.at[dst_slot]).start()

    def wait_rows(s):
        pltpu.make_async_copy(ug_hbm.at[pl.ds(0, blk * SUBLANES)], xring.at[s], xsems.at[s]).wait()

    @pl.when(i == 0)
    def _():
        for n in range(WEIGHT_SLOTS - 1):
            @pl.when(lead_ref[n] >= 0)
            def _():
                start_weights(lead_ref[n], n)

        def first_rows(c, carry):
            r0 = pl.multiple_of(c * ROW_UNROLL, ROW_UNROLL)
            for q in range(ROW_UNROLL):
                start_row(0, r0 + q, 0)
            return carry

        lax.fori_loop(0, blk // ROW_UNROLL, first_rows, 0)

    @pl.when(i < nused)
    def _():
        s = slot_ref[i]

        @pl.when(first_ref[i] == 1)
        def _():
            @pl.when(next_ref[i] >= 0)
            def _():
                start_weights(next_ref[i], (s + WEIGHT_SLOTS - 1) % WEIGHT_SLOTS)
            for cp in weight_copies(bexp_ref[i], s):
                cp.wait()

        wait_rows(xs)
        nb = jnp.minimum(i + 1, nused - 1)
        per = blk // GATHER_GROUPS

        def gather_group(g):
            for r in range(g * per, (g + 1) * per):
                start_row(nb, r, 1 - xs)

        gather_group(0)
        x_lo, x_hi = _unpack_bf16_pairs(_load_row_tiles(xring.at[xs], 0, blk))
        half = x_lo.shape[1]
        gather_group(1)
        a = (jnp.dot(x_lo, w1b[s, 0:half, :], preferred_element_type=F32)
             + jnp.dot(x_hi, w1b[s, half:2 * half, :], preferred_element_type=F32))
        gather_group(2)
        b = (jnp.dot(x_lo, w3b[s, 0:half, :], preferred_element_type=F32)
             + jnp.dot(x_hi, w3b[s, half:2 * half, :], preferred_element_type=F32))
        gather_group(3)
        hdn = a * jax.nn.sigmoid(a) * b
        cw = 2 * LANES
        for n, c0 in enumerate(range(0, half, cw)):
            gather_group(4 + n)
            y_lo = jnp.dot(hdn, w2b[s, :, c0:c0 + cw], preferred_element_type=F32)
            y_hi = jnp.dot(hdn, w2b[s, :, half + c0:half + c0 + cw], preferred_element_type=F32)
            words = _pack_bf16_words(y_lo, y_hi)
            for g in range(cw // LANES):
                y_ref[pl.ds(c0 // LANES + g, blk, stride=SUBLANES), :] = words[:, g * LANES:(g + 1) * LANES]

        @pl.when(i == nused - 1)
        def _():
            wait_rows(1 - xs)

    @pl.when(i >= nused)
    def _():
        y_ref[...] = jnp.zeros(y_ref.shape, y_ref.dtype)


def _experts(bexp, first, slot, nxt, lead, nused, row_tok, ug, w1, w3, w2, blk):
    P = row_tok.shape[0]
    E, D, Fh = w1.shape
    assert D // 2 == SUBLANES * LANES and D // 2 == (GATHER_GROUPS - 4) * 2 * LANES
    nblk = P // blk
    grid_spec = pltpu.PrefetchScalarGridSpec(
        num_scalar_prefetch=7,
        grid=(nblk,),
        in_specs=[pl.BlockSpec(memory_space=pl.ANY),
                  pl.BlockSpec(memory_space=pl.ANY),
                  pl.BlockSpec(memory_space=pl.ANY),
                  pl.BlockSpec(memory_space=pl.ANY)],
        out_specs=pl.BlockSpec((blk * SUBLANES, LANES), lambda i, *_: (i, 0)),
        scratch_shapes=[pltpu.VMEM((2, blk * SUBLANES, LANES), ug.dtype),
                        pltpu.SemaphoreType.DMA((2,)),
                        pltpu.VMEM((WEIGHT_SLOTS, D, Fh), w1.dtype),
                        pltpu.VMEM((WEIGHT_SLOTS, D, Fh), w3.dtype),
                        pltpu.VMEM((WEIGHT_SLOTS, Fh, D), w2.dtype),
                        pltpu.SemaphoreType.DMA((WEIGHT_SLOTS, 3))],
    )
    return pl.pallas_call(
        functools.partial(_experts_kernel, blk=blk),
        grid_spec=grid_spec,
        out_shape=jax.ShapeDtypeStruct((P * SUBLANES, LANES), ug.dtype),
        compiler_params=_cparams(("arbitrary",), 56),
        name="experts",
    )(bexp, first, slot, nxt, lead, nused, row_tok, ug, w1, w3, w2)


def _row_gather(src_hbm, idx_ref, base, dst, sem, n):
    def body(c, carry):
        r0 = pl.multiple_of(c * ROW_UNROLL, ROW_UNROLL)
        for s in range(ROW_UNROLL):
            d = idx_ref[base + r0 + s]
            pltpu.make_async_copy(src_hbm.at[pl.ds(pl.multiple_of(d * SUBLANES, SUBLANES), SUBLANES)],
                                  dst.at[pl.ds(pl.multiple_of((r0 + s) * SUBLANES, SUBLANES), SUBLANES)],
                                  sem).start(priority=s % 2)
        return carry
    lax.fori_loop(0, n // ROW_UNROLL, body, 0)


def _row_gather_wait(src_hbm, dst, sem, n):
    pltpu.make_async_copy(src_hbm.at[pl.ds(0, n * SUBLANES)], dst, sem).wait()


def _combine_kernel(dest_ref, h_ref, info_ref, y_hbm, fw_ref, o_ref, ybuf, sems, *, tm, eps):
    i = pl.program_id(0)
    n = pl.num_programs(0)
    slot = i % 2
    nrow = TOP_K * tm

    @pl.when(i == 0)
    def _():
        _row_gather(y_hbm, dest_ref, 0, ybuf.at[0], sems.at[0], nrow)

    @pl.when(i + 1 < n)
    def _():
        _row_gather(y_hbm, dest_ref, (i + 1) * nrow, ybuf.at[1 - slot], sems.at[1 - slot], nrow)

    _row_gather_wait(y_hbm, ybuf.at[slot], sems.at[slot], nrow)
    half = o_ref.shape[1] // 2
    rc = tm

    def rows(ci, carry):
        r0 = pl.multiple_of(ci * rc, rc)
        rs = pl.ds(r0, rc)
        info = info_ref[rs, :]
        g1 = info[:, 2:3]
        g2 = info[:, 3:4]
        y1_lo, y1_hi = _unpack_bf16_pairs(_load_row_tiles(ybuf.at[slot], r0, rc))
        y2_lo, y2_hi = _unpack_bf16_pairs(_load_row_tiles(ybuf.at[slot], tm + r0, rc))
        h_lo = h_ref[rs, 0:half] + (g1 * y1_lo + g2 * y2_lo)
        h_hi = h_ref[rs, half:2 * half] + (g1 * y1_hi + g2 * y2_hi)
        ms = (jnp.sum(h_lo * h_lo, axis=-1, keepdims=True)
              + jnp.sum(h_hi * h_hi, axis=-1, keepdims=True)) / (2 * half)
        r = lax.rsqrt(ms + eps)
        o_ref[rs, 0:half] = h_lo * r * fw_ref[:, 0:half]
        o_ref[rs, half:2 * half] = h_hi * r * fw_ref[:, half:2 * half]
        return carry

    lax.fori_loop(0, tm // rc, rows, 0)


def _combine(dest_km, h, info, yb, final_w, tm):
    T, D = h.shape
    grid_spec = pltpu.PrefetchScalarGridSpec(
        num_scalar_prefetch=1,
        grid=(T // tm,),
        in_specs=[pl.BlockSpec((tm, D), lambda i, d: (i, 0)),
                  pl.BlockSpec((tm, LANES), lambda i, d: (i, 0)),
                  pl.BlockSpec(memory_space=pl.ANY),
                  pl.BlockSpec((1, D), lambda i, d: (0, 0))],
        out_specs=pl.BlockSpec((tm, D), lambda i, d: (i, 0)),
        scratch_shapes=[pltpu.VMEM((2, TOP_K * tm * SUBLANES, LANES), yb.dtype),
                        pltpu.SemaphoreType.DMA((2,))],
    )
    return pl.pallas_call(
        functools.partial(_combine_kernel, tm=tm, eps=RMS_EPS),
        grid_spec=grid_spec,
        out_shape=jax.ShapeDtypeStruct((T, D), F32),
        compiler_params=_cparams(("arbitrary",), 48),
        name="combine",
    )(dest_km, h, info, yb, final_w)


def _layer(h_in, l, B, S, mix_norm_w, w_in, conv_dw_w, conv_dw_b, conv_ln_w, conv_ln_b,
           lam_q1, lam_k1, lam_q2, lam_k2, attn_subln_w, w_out, ffn_norm_w,
           w_group, b_group, w_expert_gate, b_expert_gate, w1, w3, w2):
    T, D = h_in.shape
    d_conv = conv_dw_w.shape[1]
    d_attn = (w_in.shape[1] - 2 * d_conv) // 3
    ng = w_group.shape[1]
    ne = w_expert_gate.shape[1]
    epg = ne // ng
    assert ng + ne <= LANES
    lam_init = 0.8 - 0.6 * math.exp(-0.3 * l)

    proj = _norm_inproj(h_in, mix_norm_w.reshape(1, D), w_in.astype(BF16))
    y_conv = _conformer(proj, conv_dw_w, conv_dw_b, conv_ln_w, conv_ln_b, B, S)
    y_attn = _diff_attn(proj, lam_q1, lam_k1, lam_q2, lam_k2, attn_subln_w, B, S, d_conv, d_attn, lam_init)

    wr = jnp.concatenate([w_group, w_expert_gate, jnp.zeros((D, LANES - ng - ne), F32)], axis=1)
    wr_hi = wr.astype(BF16)
    wr_lo = (wr - wr_hi.astype(F32)).astype(BF16)
    br = jnp.concatenate([b_group, b_expert_gate.reshape(-1), jnp.zeros((LANES - ng - ne,), F32)]).reshape(1, LANES)
    wr_hl = jnp.concatenate([wr_hi, wr_lo], axis=1)
    h, logits, ug = _outproj(h_in, y_conv, y_attn, w_out.astype(BF16), ffn_norm_w.reshape(1, D), wr_hl, br)

    info, cnt = _route(logits, ng, epg)

    blk = 256
    A = T * TOP_K
    nblk = (A + ne * (blk - 1) + blk - 1) // blk
    i32 = jnp.int32
    counts = cnt[0, ng:ng + ne].astype(i32)
    padded = (counts + blk - 1) // blk * blk
    pad_ends = jnp.cumsum(padded).astype(i32)
    pad_starts = pad_ends - padded
    eid = info[:, 0:TOP_K].astype(i32)
    rank = info[:, 4:4 + TOP_K].astype(i32)
    onehot = eid[:, :, None] == jnp.arange(ne, dtype=i32)[None, None, :]
    dest = jnp.sum(jnp.where(onehot, pad_starts[None, None, :], 0), axis=-1).astype(i32) + rank
    nused = (pad_ends[-1] // blk).astype(i32)
    bpos = jnp.arange(nblk, dtype=i32)
    brow = jnp.minimum(bpos, nused - 1) * blk
    bexp = jnp.minimum(jnp.sum((pad_ends[None, :] <= brow[:, None]).astype(i32), axis=1), ne - 1)
    first = ((bpos < nused) & ((bpos == 0) | (bexp != jnp.roll(bexp, 1)))).astype(i32)
    slot = ((jnp.cumsum(first) - 1) % WEIGHT_SLOTS).astype(i32)
    used_idx = jnp.where(padded > 0, jnp.arange(ne, dtype=i32), ne)
    suffix_min = lax.cummin(used_idx, reverse=True)
    next_used = jnp.concatenate([suffix_min[1:], jnp.full((2,), ne, i32)])
    ahead = jnp.arange(ne, dtype=i32)
    for _ in range(WEIGHT_SLOTS - 1):
        ahead = next_used[ahead]
    ahead = jnp.where(ahead >= ne, -1, ahead)
    nxt = ahead[bexp].astype(i32)
    lead = [suffix_min[0]]
    for _ in range(WEIGHT_SLOTS - 2):
        lead.append(next_used[lead[-1]])
    lead = jnp.stack([jnp.where(e >= ne, -1, e) for e in lead]).astype(i32)

    row_tok = _rowtok(dest.reshape(-1), nblk * blk)
    yb = _experts(bexp, first, slot, nxt, lead, nused.reshape(1), row_tok, ug, w1, w3, w2, blk)
    return h, info, dest, yb


def kernel(x, mix_norm_w, w_in, conv_dw_w, conv_dw_b, conv_ln_w, conv_ln_b, lam_q1, lam_k1, lam_q2, lam_k2,
           attn_subln_w, w_out, ffn_norm_w, w_group, b_group, w_expert_gate, b_expert_gate, w1, w3, w2,
           final_norm_w):
    B, S, D = x.shape
    depth = w_in.shape[0]
    assert depth == 1
    T = B * S
    tmc = min(256, T)
    h = x.reshape(T, D)
    for l in range(depth):
        h, info, dest, yb = _layer(
            h, l, B, S, mix_norm_w[l], w_in[l], conv_dw_w[l], conv_dw_b[l], conv_ln_w[l], conv_ln_b[l],
            lam_q1[l], lam_k1[l], lam_q2[l], lam_k2[l], attn_subln_w[l], w_out[l], ffn_norm_w[l],
            w_group[l], b_group[l], w_expert_gate[l], b_expert_gate[l], w1[l], w3[l], w2[l])
        dest_km = dest.reshape(T // tmc, tmc, TOP_K).transpose(0, 2, 1).reshape(-1)
        h = _combine(dest_km, h, info, yb, final_norm_w.reshape(1, D), tmc)
    return h.reshape(B, S, D)
```

```python
import functools
import math

import jax
import jax.numpy as jnp
from jax import lax
from jax.experimental import pallas as pl
from jax.experimental.pallas import tpu as pltpu

F32 = jnp.float32
BF16 = jnp.bfloat16

RMS_EPS = 1e-6
SUBLN_EPS = 1e-5
LN_EPS = 1e-5
TOP_K = 2
LANES = 128
SUBLANES = 8
NEG_BIG = -1e30
MIB = 1024 * 1024


def _cparams(sem, vmem_mib):
    return pltpu.CompilerParams(dimension_semantics=sem, vmem_limit_bytes=vmem_mib * MIB)


def _norm_inproj_kernel(x_ref, nw_ref, w_ref, o_ref, u_ref, *, eps, rc):
    @pl.when(pl.program_id(1) == 0)
    def _():
        for r0 in range(0, x_ref.shape[0], rc):
            rows = slice(r0, r0 + rc)
            x = x_ref[rows, :]
            ms = jnp.mean(x * x, axis=-1, keepdims=True)
            u = (x * lax.rsqrt(ms + eps) * nw_ref[...]).astype(u_ref.dtype)
            u_ref[rows, :] = u
            o_ref[rows, :] = jnp.dot(u, w_ref[...], preferred_element_type=F32).astype(o_ref.dtype)

    @pl.when(pl.program_id(1) > 0)
    def _():
        o_ref[...] = jnp.dot(u_ref[...], w_ref[...], preferred_element_type=F32).astype(o_ref.dtype)


def _norm_inproj(x2, nw, w_bf):
    T, D = x2.shape
    N = w_bf.shape[1]
    tm = min(1024, T)
    tn = 1024
    return pl.pallas_call(
        functools.partial(_norm_inproj_kernel, eps=RMS_EPS, rc=256),
        grid=(T // tm, N // tn),
        in_specs=[pl.BlockSpec((tm, D), lambda i, j: (i, 0)),
                  pl.BlockSpec((1, D), lambda i, j: (0, 0)),
                  pl.BlockSpec((D, tn), lambda i, j: (0, j))],
        out_specs=pl.BlockSpec((tm, tn), lambda i, j: (i, j)),
        out_shape=jax.ShapeDtypeStruct((T, N), BF16),
        scratch_shapes=[pltpu.VMEM((tm, D), BF16)],
        compiler_params=_cparams(("arbitrary", "arbitrary"), 48),
        name="norm_inproj",
    )(x2, nw, w_bf)


def _conv_kernel(a_ref, g_ref, w_ref, b_ref, lw_ref, lb_ref, o_ref, ubuf, cbuf, sh, *, ts, kw, halo, eps):
    s = pl.program_id(1)
    C = a_ref.shape[1]

    @pl.when(s == 0)
    def _():
        ubuf[0:halo, :] = jnp.zeros((halo, C), F32)

    @pl.when(s > 0)
    def _():
        ubuf[0:halo, :] = ubuf[ts:ts + halo, :]

    rg = 64
    for r0 in range(0, ts, rg):
        a = a_ref[r0:r0 + rg, :].astype(F32)
        g = g_ref[r0:r0 + rg, :].astype(F32)
        ubuf[halo + r0:halo + r0 + rg, :] = a * jax.nn.sigmoid(g)

    nsh = sh.shape[1]
    for r in range(1, SUBLANES):
        for i0 in range(0, nsh, rg):
            n = min(rg, nsh - i0)
            sh[r - 1, i0:i0 + n, :] = ubuf[i0 + r:i0 + r + n, :]

    off = halo - (kw - 1)
    rc = 64

    def conv_ln_rows(ri, carry):
        r0 = pl.multiple_of(ri * rc, rc)
        for c in range(C // LANES):
            cs = slice(c * LANES, (c + 1) * LANES)
            acc = jnp.broadcast_to(b_ref[0:1, cs], (rc // SUBLANES, SUBLANES, LANES))
            for k in range(kw):
                q, r = divmod(off + k, SUBLANES)
                rows = pl.ds(r0 + q * SUBLANES, rc)
                tap = ubuf[rows, cs] if r == 0 else sh[r - 1, rows, cs]
                acc = acc + w_ref[k, :, cs] * tap.reshape(rc // SUBLANES, SUBLANES, LANES)
            cbuf[pl.ds(r0, rc), cs] = acc.reshape(rc, LANES)
        cv = cbuf[pl.ds(r0, rc), :]
        mu = jnp.mean(cv, axis=-1, keepdims=True)
        d = cv - mu
        var = jnp.mean(d * d, axis=-1, keepdims=True)
        un = d * lax.rsqrt(var + eps) * lw_ref[...] + lb_ref[...]
        o_ref[pl.ds(r0, rc), :] = (un * jax.nn.sigmoid(un)).astype(o_ref.dtype)
        return carry

    lax.fori_loop(0, ts // rc, conv_ln_rows, 0)


def _conformer(proj, dw_w, dw_b, ln_w, ln_b, B, S):
    T = proj.shape[0]
    kw, C = dw_w.shape
    ts = min(256, S)
    halo = 32
    assert kw - 1 <= halo and S % ts == 0 and ts >= halo
    ns = S // ts
    wp = jnp.broadcast_to(dw_w[:, None, :], (kw, SUBLANES, C))
    return pl.pallas_call(
        functools.partial(_conv_kernel, ts=ts, kw=kw, halo=halo, eps=LN_EPS),
        grid=(B, ns),
        in_specs=[pl.BlockSpec((ts, C), lambda b, s: (b * ns + s, 0)),
                  pl.BlockSpec((ts, C), lambda b, s: (b * ns + s, 1)),
                  pl.BlockSpec((kw, SUBLANES, C), lambda b, s: (0, 0, 0)),
                  pl.BlockSpec((1, C), lambda b, s: (0, 0)),
                  pl.BlockSpec((1, C), lambda b, s: (0, 0)),
                  pl.BlockSpec((1, C), lambda b, s: (0, 0))],
        out_specs=pl.BlockSpec((ts, C), lambda b, s: (b * ns + s, 0)),
        out_shape=jax.ShapeDtypeStruct((T, C), BF16),
        scratch_shapes=[pltpu.VMEM((ts + halo, C), F32), pltpu.VMEM((ts, C), F32),
                        pltpu.VMEM((SUBLANES - 1, ts + halo - SUBLANES, C), F32)],
        compiler_params=_cparams(("arbitrary", "arbitrary"), 32),
        name="conformer",
    )(proj, proj, wp, dw_b.reshape(1, C), ln_w.reshape(1, C), ln_b.reshape(1, C))


def _attn_kernel(lq1_ref, lk1_ref, lq2_ref, lk2_ref, q_ref, k_ref, v_ref, sw_ref, o_ref,
                 vxt_ref, qq_ref, *scr, tq, hd, hp, ones_rows, lam_init, eps):
    i = pl.program_id(2)
    vd = 2 * hd
    nkb = v_ref.shape[0] // tq
    acc_refs, m_refs, sa_refs, sb_refs = (scr[n * hp:(n + 1) * hp] for n in range(4))

    @pl.when(i == 0)
    def _():
        for u in range(hp):
            for jb in range(nkb):
                vblk = v_ref[jb * tq:(jb + 1) * tq, u * vd:(u + 1) * vd].astype(F32)
                vxt_ref[u, jb, 0:vd, :] = vblk.T.astype(vxt_ref.dtype)
                vxt_ref[u, jb, vd:vd + ones_rows, :] = jnp.ones((ones_rows, tq), vxt_ref.dtype)

    lam = (jnp.exp(jnp.sum(lq1_ref[...] * lk1_ref[...], axis=-1, keepdims=True))
           - jnp.exp(jnp.sum(lq2_ref[...] * lk2_ref[...], axis=-1, keepdims=True)) + lam_init)

    for u in range(hp):
        q = q_ref[:, u * vd:(u + 1) * vd]
        qs = q * jnp.asarray(hd ** -0.5, q.dtype)
        lane = lax.broadcasted_iota(jnp.int32, q.shape, 1)
        zero = jnp.zeros_like(qs)
        qq_ref[u, 0:tq, :] = jnp.where(lane < hd, qs, zero)
        qq_ref[u, tq:2 * tq, :] = jnp.where(lane >= hd, qs, zero)
        acc_refs[u][...] = jnp.zeros(acc_refs[u].shape, F32)
        m_refs[u][...] = jnp.full(m_refs[u].shape, NEG_BIG, F32)

    def scores(j, u, dst):
        rows = pl.ds(pl.multiple_of(j * tq, tq), tq)
        kb = k_ref[rows, u * vd:(u + 1) * vd]
        dst[...] = lax.dot_general(kb, qq_ref[u], (((1,), (1,)), ((), ())), preferred_element_type=F32)

    def softmax_pv(j, u, src, masked):
        st = src[...]
        if masked:
            r = lax.broadcasted_iota(jnp.int32, st.shape, 0)
            c = lax.broadcasted_iota(jnp.int32, st.shape, 1)
            st = jnp.where(r <= jnp.where(c >= tq, c - tq, c), st, NEG_BIG)
        m_old = m_refs[u][...]
        m_new = jnp.maximum(m_old, jnp.max(st, axis=0, keepdims=True))
        alpha = jnp.exp(m_old - m_new)
        pt = jnp.exp(st - m_new).astype(vxt_ref.dtype)
        pv = jnp.dot(vxt_ref[u, j], pt, preferred_element_type=F32)
        acc_refs[u][...] = alpha * acc_refs[u][...] + pv
        m_refs[u][...] = m_new

    def half(j, cur, nxt):
        for u in range(hp):
            scores(j + 1, u, nxt[u])
            softmax_pv(j, u, cur[u], False)

    for u in range(hp):
        scores(0, u, sa_refs[u])

    def pair(t, carry):
        half(2 * t, sa_refs, sb_refs)
        half(2 * t + 1, sb_refs, sa_refs)
        return carry

    lax.fori_loop(0, i // 2, pair, 0)

    @pl.when(i % 2 == 1)
    def _():
        half(i - 1, sa_refs, sb_refs)
        for u in range(hp):
            softmax_pv(i, u, sb_refs[u], True)

    @pl.when(i % 2 == 0)
    def _():
        for u in range(hp):
            softmax_pv(i, u, sa_refs[u], True)

    for u in range(hp):
        acc = acc_refs[u][...]
        o12 = acc[0:vd] * (1.0 / acc[vd:vd + 1])
        ot = o12[:, 0:tq] - lam * o12[:, tq:2 * tq]
        msq = jnp.mean(ot * ot, axis=0, keepdims=True)
        o = (ot * lax.rsqrt(msq + eps)).T
        o_ref[:, u * vd:(u + 1) * vd] = (o * sw_ref[...] * (1.0 - lam_init)).astype(o_ref.dtype)


def _diff_attn(proj, lq1, lk1, lq2, lk2, subln_w, B, S, d_conv, d_attn, lam_init):
    T = proj.shape[0]
    vd = subln_w.shape[0]
    hd = lq1.shape[0]
    assert vd == LANES and 2 * hd == vd
    H = d_attn // vd
    hp = 8
    ones_rows = 16
    tq = min(256, S)
    nq = S // tq
    assert H % hp == 0
    qc = 2 * d_conv // (hp * vd)
    kc = qc + H // hp
    vc = kc + H // hp
    lspec = pl.BlockSpec((1, hd), lambda b, h, i: (0, 0))
    return pl.pallas_call(
        functools.partial(_attn_kernel, tq=tq, hd=hd, hp=hp, ones_rows=ones_rows, lam_init=lam_init, eps=SUBLN_EPS),
        grid=(B, H // hp, nq),
        in_specs=[lspec, lspec, lspec, lspec,
                  pl.BlockSpec((tq, hp * vd), lambda b, h, i: (b * nq + i, qc + h)),
                  pl.BlockSpec((S, hp * vd), lambda b, h, i: (b, kc + h)),
                  pl.BlockSpec((S, hp * vd), lambda b, h, i: (b, vc + h)),
                  pl.BlockSpec((1, vd), lambda b, h, i: (0, 0))],
        out_specs=pl.BlockSpec((tq, hp * vd), lambda b, h, i: (b * nq + i, h)),
        out_shape=jax.ShapeDtypeStruct((T, d_attn), BF16),
        scratch_shapes=[pltpu.VMEM((hp, S // tq, vd + ones_rows, tq), BF16),
                        pltpu.VMEM((hp, 2 * tq, vd), BF16)]
                       + [pltpu.VMEM((vd + ones_rows, 2 * tq), F32) for _ in range(hp)]
                       + [pltpu.VMEM((1, 2 * tq), F32) for _ in range(hp)]
                       + [pltpu.VMEM((tq, 2 * tq), F32) for _ in range(2 * hp)],
        compiler_params=_cparams(("arbitrary", "arbitrary", "arbitrary"), 52),
        name="diff_attn",
    )(lq1.reshape(1, hd), lk1.reshape(1, hd), lq2.reshape(1, hd), lk2.reshape(1, hd),
      proj, proj, proj, subln_w.reshape(1, vd))


def _pack_bf16_pairs(v):
    n = v.shape[1] // 2
    return _pack_bf16_words(v[:, 0:n], v[:, n:2 * n])


def _pack_bf16_words(lo, hi):
    lo_bits = lax.bitcast_convert_type(lo.astype(BF16).astype(F32), jnp.uint32)
    hi_bits = lax.bitcast_convert_type(hi.astype(BF16).astype(F32), jnp.uint32)
    return (lo_bits >> 16) | (hi_bits & jnp.uint32(0xFFFF0000))


def _unpack_bf16_pairs(w):
    lo = lax.bitcast_convert_type(w << 16, F32)
    hi = lax.bitcast_convert_type(w & jnp.uint32(0xFFFF0000), F32)
    return lo, hi


def _store_row_tiles(ref, row0, words):
    m, w = words.shape
    assert w == SUBLANES * LANES
    for s in range(SUBLANES):
        ref[pl.ds(row0 * SUBLANES + s, m, stride=SUBLANES), :] = words[:, s * LANES:(s + 1) * LANES]


def _load_row_tiles(ref, row0, m):
    return jnp.concatenate([ref[pl.ds(row0 * SUBLANES + s, m, stride=SUBLANES), :] for s in range(SUBLANES)], axis=1)


def _outproj_kernel(x_ref, yc_ref, ya_ref, wo_ref, fw_ref, wr_ref, br_ref, h_ref, lg_ref, ug_ref, *, eps, nsub):
    dc = yc_ref.shape[1]
    sub = x_ref.shape[0] // nsub
    mixes = []
    for t in range(nsub):
        rows = slice(t * sub, (t + 1) * sub)
        mixes.append(jnp.dot(yc_ref[rows, :], wo_ref[0:dc, :], preferred_element_type=F32)
                     + jnp.dot(ya_ref[rows, :], wo_ref[dc:, :], preferred_element_type=F32))
    for t in range(nsub):
        rows = slice(t * sub, (t + 1) * sub)
        h = x_ref[rows, :] + mixes[t]
        h_ref[rows, :] = h
        ms = jnp.mean(h * h, axis=-1, keepdims=True)
        un = h * lax.rsqrt(ms + eps) * fw_ref[...]
        hi = un.astype(BF16)
        lo = (un - hi.astype(F32)).astype(BF16)
        hh = jnp.dot(hi, wr_ref[...], preferred_element_type=F32)
        lh = jnp.dot(lo, wr_ref[:, 0:LANES], preferred_element_type=F32)
        lg_ref[rows, :] = hh[:, 0:LANES] + hh[:, LANES:2 * LANES] + lh + br_ref[...]
        _store_row_tiles(ug_ref, t * sub, _pack_bf16_pairs(un))


def _outproj(x2, y_conv, y_attn, wo_bf, ffn_w, wr_hl, br):
    T, D = x2.shape
    dc = y_conv.shape[1]
    da = y_attn.shape[1]
    tm = min(512, T)
    nsub = 2 if tm % 512 == 0 else 1
    return pl.pallas_call(
        functools.partial(_outproj_kernel, eps=RMS_EPS, nsub=nsub),
        grid=(T // tm,),
        in_specs=[pl.BlockSpec((tm, D), lambda i: (i, 0)),
                  pl.BlockSpec((tm, dc), lambda i: (i, 0)),
                  pl.BlockSpec((tm, da), lambda i: (i, 0)),
                  pl.BlockSpec((dc + da, D), lambda i: (0, 0), pipeline_mode=pl.Buffered(1)),
                  pl.BlockSpec((1, D), lambda i: (0, 0)),
                  pl.BlockSpec((D, 2 * LANES), lambda i: (0, 0)),
                  pl.BlockSpec((1, LANES), lambda i: (0, 0))],
        out_specs=[pl.BlockSpec((tm, D), lambda i: (i, 0)),
                   pl.BlockSpec((tm, LANES), lambda i: (i, 0)),
                   pl.BlockSpec((tm * SUBLANES, LANES), lambda i: (i, 0))],
        out_shape=[jax.ShapeDtypeStruct((T, D), F32), jax.ShapeDtypeStruct((T, LANES), F32),
                   jax.ShapeDtypeStruct((T * SUBLANES, LANES), jnp.uint32)],
        compiler_params=_cparams(("arbitrary",), 56),
        name="outproj",
    )(x2, y_conv, y_attn, wo_bf, ffn_w, wr_hl, br)


def _route_kernel(lg_ref, info_ref, cnt_ref, *, ng, epg):
    i = pl.program_id(0)
    lg = lg_ref[...]
    tm = lg.shape[0]
    lane = lax.broadcasted_iota(jnp.int32, lg.shape, 1)
    lanef = lane.astype(F32)
    ne = ng * epg

    def first_argmax(vals):
        mx = jnp.max(vals, axis=-1, keepdims=True)
        idx = jnp.min(jnp.where(vals == mx, lanef, float(LANES)), axis=-1, keepdims=True)
        return mx, idx

    gmask = lane < ng
    gl = jnp.where(gmask, lg, NEG_BIG)
    gmax, gsel = first_argmax(gl)
    gsum = jnp.sum(jnp.where(gmask, jnp.exp(gl - gmax), 0.0), axis=-1, keepdims=True)
    g_w = 1.0 / gsum
    lo = gsel * epg + ng
    emask = (lanef >= lo) & (lanef < lo + epg)
    el = jnp.where(emask, lg, NEG_BIG)
    v1, i1 = first_argmax(el)
    el2 = jnp.where(lanef == i1, NEG_BIG, el)
    v2, i2 = first_argmax(el2)
    e2 = jnp.exp(v2 - v1)
    p1 = 1.0 / (1.0 + e2)
    gate1 = g_w * p1
    gate2 = g_w * (e2 * p1)

    oh1 = lanef == i1
    oh2 = lanef == i2
    cmat = jnp.where(oh1 | oh2, 1.0, 0.0).astype(BF16)
    r = lax.broadcasted_iota(jnp.int32, (tm, tm), 0)
    c = lax.broadcasted_iota(jnp.int32, (tm, tm), 1)
    tri = jnp.where(c < r, 1.0, 0.0).astype(BF16)

    @pl.when(i == 0)
    def _():
        cnt_ref[...] = jnp.zeros(cnt_ref.shape, F32)

    carry = cnt_ref[0:1, :]
    prefix = jnp.dot(tri, cmat, preferred_element_type=F32) + carry
    rank1 = jnp.sum(jnp.where(oh1, prefix, 0.0), axis=-1, keepdims=True)
    rank2 = jnp.sum(jnp.where(oh2, prefix, 0.0), axis=-1, keepdims=True)
    cnt_ref[...] = jnp.broadcast_to(carry + jnp.sum(cmat.astype(F32), axis=0, keepdims=True), cnt_ref.shape)

    info = jnp.where(lane == 0, i1 - ng,
           jnp.where(lane == 1, i2 - ng,
           jnp.where(lane == 2, gate1,
           jnp.where(lane == 3, gate2,
           jnp.where(lane == 4, rank1,
           jnp.where(lane == 5, rank2, 0.0))))))
    info_ref[...] = info


def _route(logits, ng, epg):
    T = logits.shape[0]
    tm = min(512, T)
    return pl.pallas_call(
        functools.partial(_route_kernel, ng=ng, epg=epg),
        grid=(T // tm,),
        in_specs=[pl.BlockSpec((tm, LANES), lambda i: (i, 0))],
        out_specs=[pl.BlockSpec((tm, LANES), lambda i: (i, 0)),
                   pl.BlockSpec((8, LANES), lambda i: (0, 0))],
        out_shape=[jax.ShapeDtypeStruct((T, LANES), F32), jax.ShapeDtypeStruct((8, LANES), F32)],
        compiler_params=_cparams(("arbitrary",), 32),
        name="route",
    )(logits)


ROW_UNROLL = 8


def _rowtok_kernel(dest_ref, zeros_hbm, rt_hbm, rt_smem, sem, *, ntok):
    fill = pltpu.make_async_copy(zeros_hbm, rt_smem, sem)
    fill.start()
    fill.wait()

    def put(t, carry):
        for k in range(TOP_K):
            rt_smem[dest_ref[t * TOP_K + k]] = t
        return carry

    lax.fori_loop(0, ntok, put, 0, unroll=8)
    out = pltpu.make_async_copy(rt_smem, rt_hbm, sem)
    out.start()
    out.wait()


def _rowtok(dest_flat, nrows):
    ntok = dest_flat.shape[0] // TOP_K
    grid_spec = pltpu.PrefetchScalarGridSpec(
        num_scalar_prefetch=1,
        grid=(1,),
        in_specs=[pl.BlockSpec(memory_space=pl.ANY)],
        out_specs=pl.BlockSpec(memory_space=pl.ANY),
        scratch_shapes=[pltpu.SMEM((nrows,), jnp.int32), pltpu.SemaphoreType.DMA(())],
    )
    return pl.pallas_call(
        functools.partial(_rowtok_kernel, ntok=ntok),
        grid_spec=grid_spec,
        out_shape=jax.ShapeDtypeStruct((nrows,), jnp.int32),
        compiler_params=_cparams(("arbitrary",), 16),
        name="rowtok",
    )(dest_flat, jnp.zeros((nrows,), jnp.int32))


WEIGHT_DMA_PRIORITY = (1, 1, 1)
WEIGHT_SLOTS = 3
GATHER_GROUPS = 8


def _experts_kernel(bexp_ref, first_ref, slot_ref, next_ref, lead_ref, nused_ref, rtok_ref, ug_hbm,
                    w1_hbm, w3_hbm, w2_hbm, y_ref, xring, xsems, w1b, w3b, w2b, wsems, *, blk):
    i = pl.program_id(0)
    nused = nused_ref[0]
    xs = i % 2

    def weight_copies(e, s):
        return (pltpu.make_async_copy(w1_hbm.at[e], w1b.at[s], wsems.at[s, 0]),
                pltpu.make_async_copy(w3_hbm.at[e], w3b.at[s], wsems.at[s, 1]),
                pltpu.make_async_copy(w2_hbm.at[e], w2b.at[s], wsems.at[s, 2]))

    def start_weights(e, s):
        for n, cp in enumerate(weight_copies(e, s)):
            cp.start(priority=WEIGHT_DMA_PRIORITY[n])

    def start_row(block, r, dst_slot):
        t = rtok_ref[block * blk + r]
        pltpu.make_async_copy(ug_hbm.at[pl.ds(pl.multiple_of(t * SUBLANES, SUBLANES), SUBLANES)],
                              xring.at[dst_slot, pl.ds(r * SUBLANES, SUBLANES)], xsems.at[dst_slot]).start()

    def wait_rows(s):
        pltpu.make_async_copy(ug_hbm.at[pl.ds(0, blk * SUBLANES)], xring.at[s], xsems.at[s]).wait()

    @pl.when(i == 0)
    def _():
        for n in range(WEIGHT_SLOTS - 1):
            @pl.when(lead_ref[n] >= 0)
            def _():
                start_weights(lead_ref[n], n)

        def first_rows(c, carry):
            r0 = pl.multiple_of(c * ROW_UNROLL, ROW_UNROLL)
            for q in range(ROW_UNROLL):
                start_row(0, r0 + q, 0)
            return carry

        lax.fori_loop(0, blk // ROW_UNROLL, first_rows, 0)

    @pl.when(i < nused)
    def _():
        s = slot_ref[i]

        @pl.when(first_ref[i] == 1)
        def _():
            @pl.when(next_ref[i] >= 0)
            def _():
                start_weights(next_ref[i], (s + WEIGHT_SLOTS - 1) % WEIGHT_SLOTS)
            for cp in weight_copies(bexp_ref[i], s):
                cp.wait()

        wait_rows(xs)
        nb = jnp.minimum(i + 1, nused - 1)
        per = blk // GATHER_GROUPS

        def gather_group(g):
            for r in range(g * per, (g + 1) * per):
                start_row(nb, r, 1 - xs)

        gather_group(0)
        x_lo, x_hi = _unpack_bf16_pairs(_load_row_tiles(xring.at[xs], 0, blk))
        half = x_lo.shape[1]
        gather_group(1)
        a = (jnp.dot(x_lo, w1b[s, 0:half, :], preferred_element_type=F32)
             + jnp.dot(x_hi, w1b[s, half:2 * half, :], preferred_element_type=F32))
        gather_group(2)
        b = (jnp.dot(x_lo, w3b[s, 0:half, :], preferred_element_type=F32)
             + jnp.dot(x_hi, w3b[s, half:2 * half, :], preferred_element_type=F32))
        gather_group(3)
        hdn = a * jax.nn.sigmoid(a) * b
        cw = 2 * LANES
        for n, c0 in enumerate(range(0, half, cw)):
            gather_group(4 + n)
            y_lo = jnp.dot(hdn, w2b[s, :, c0:c0 + cw], preferred_element_type=F32)
            y_hi = jnp.dot(hdn, w2b[s, :, half + c0:half + c0 + cw], preferred_element_type=F32)
            words = _pack_bf16_words(y_lo, y_hi)
            for g in range(cw // LANES):
                y_ref[pl.ds(c0 // LANES + g, blk, stride=SUBLANES), :] = words[:, g * LANES:(g + 1) * LANES]

        @pl.when(i == nused - 1)
        def _():
            wait_rows(1 - xs)

    @pl.when(i >= nused)
    def _():
        y_ref[...] = jnp.zeros(y_ref.shape, y_ref.dtype)


def _experts(bexp, first, slot, nxt, lead, nused, row_tok, ug, w1, w3, w2, blk):
    P = row_tok.shape[0]
    E, D, Fh = w1.shape
    assert D // 2 == SUBLANES * LANES and D // 2 == (GATHER_GROUPS - 4) * 2 * LANES
    nblk = P // blk
    grid_spec = pltpu.PrefetchScalarGridSpec(
        num_scalar_prefetch=7,
        grid=(nblk,),
        in_specs=[pl.BlockSpec(memory_space=pl.ANY),
                  pl.BlockSpec(memory_space=pl.ANY),
                  pl.BlockSpec(memory_space=pl.ANY),
                  pl.BlockSpec(memory_space=pl.ANY)],
        out_specs=pl.BlockSpec((blk * SUBLANES, LANES), lambda i, *_: (i, 0)),
        scratch_shapes=[pltpu.VMEM((2, blk * SUBLANES, LANES), ug.dtype),
                        pltpu.SemaphoreType.DMA((2,)),
                        pltpu.VMEM((WEIGHT_SLOTS, D, Fh), w1.dtype),
                        pltpu.VMEM((WEIGHT_SLOTS, D, Fh), w3.dtype),
                        pltpu.VMEM((WEIGHT_SLOTS, Fh, D), w2.dtype),
                        pltpu.SemaphoreType.DMA((WEIGHT_SLOTS, 3))],
    )
    return pl.pallas_call(
        functools.partial(_experts_kernel, blk=blk),
        grid_spec=grid_spec,
        out_shape=jax.ShapeDtypeStruct((P * SUBLANES, LANES), ug.dtype),
        compiler_params=_cparams(("arbitrary",), 56),
        name="experts",
    )(bexp, first, slot, nxt, lead, nused, row_tok, ug, w1, w3, w2)


def _row_gather(src_hbm, idx_ref, base, dst, sem, n):
    def body(c, carry):
        r0 = pl.multiple_of(c * ROW_UNROLL, ROW_UNROLL)
        for s in range(ROW_UNROLL):
            d = idx_ref[base + r0 + s]
            pltpu.make_async_copy(src_hbm.at[pl.ds(pl.multiple_of(d * SUBLANES, SUBLANES), SUBLANES)],
                                  dst.at[pl.ds(pl.multiple_of((r0 + s) * SUBLANES, SUBLANES), SUBLANES)],
                                  sem).start(priority=s % 2)
        return carry
    lax.fori_loop(0, n // ROW_UNROLL, body, 0)


def _row_gather_wait(src_hbm, dst, sem, n):
    pltpu.make_async_copy(src_hbm.at[pl.ds(0, n * SUBLANES)], dst, sem).wait()


def _combine_kernel(dest_ref, h_ref, info_ref, y_hbm, fw_ref, o_ref, ybuf, sems, *, tm, eps):
    i = pl.program_id(0)
    n = pl.num_programs(0)
    slot = i % 2
    nrow = TOP_K * tm

    @pl.when(i == 0)
    def _():
        _row_gather(y_hbm, dest_ref, 0, ybuf.at[0], sems.at[0], nrow)

    @pl.when(i + 1 < n)
    def _():
        _row_gather(y_hbm, dest_ref, (i + 1) * nrow, ybuf.at[1 - slot], sems.at[1 - slot], nrow)

    _row_gather_wait(y_hbm, ybuf.at[slot], sems.at[slot], nrow)
    half = o_ref.shape[1] // 2
    rc = tm

    def rows(ci, carry):
        r0 = pl.multiple_of(ci * rc, rc)
        rs = pl.ds(r0, rc)
        info = info_ref[rs, :]
        g1 = info[:, 2:3]
        g2 = info[:, 3:4]
        y1_lo, y1_hi = _unpack_bf16_pairs(_load_row_tiles(ybuf.at[slot], r0, rc))
        y2_lo, y2_hi = _unpack_bf16_pairs(_load_row_tiles(ybuf.at[slot], tm + r0, rc))
        h_lo = h_ref[rs, 0:half] + (g1 * y1_lo + g2 * y2_lo)
        h_hi = h_ref[rs, half:2 * half] + (g1 * y1_hi + g2 * y2_hi)
        ms = (jnp.sum(h_lo * h_lo, axis=-1, keepdims=True)
              + jnp.sum(h_hi * h_hi, axis=-1, keepdims=True)) / (2 * half)
        r = lax.rsqrt(ms + eps)
        o_ref[rs, 0:half] = h_lo * r * fw_ref[:, 0:half]
        o_ref[rs, half:2 * half] = h_hi * r * fw_ref[:, half:2 * half]
        return carry

    lax.fori_loop(0, tm // rc, rows, 0)


def _combine(dest_km, h, info, yb, final_w, tm):
    T, D = h.shape
    grid_spec = pltpu.PrefetchScalarGridSpec(
        num_scalar_prefetch=1,
        grid=(T // tm,),
        in_specs=[pl.BlockSpec((tm, D), lambda i, d: (i, 0)),
                  pl.BlockSpec((tm, LANES), lambda i, d: (i, 0)),
                  pl.BlockSpec(memory_space=pl.ANY),
                  pl.BlockSpec((1, D), lambda i, d: (0, 0))],
        out_specs=pl.BlockSpec((tm, D), lambda i, d: (i, 0)),
        scratch_shapes=[pltpu.VMEM((2, TOP_K * tm * SUBLANES, LANES), yb.dtype),
                        pltpu.SemaphoreType.DMA((2,))],
    )
    return pl.pallas_call(
        functools.partial(_combine_kernel, tm=tm, eps=RMS_EPS),
        grid_spec=grid_spec,
        out_shape=jax.ShapeDtypeStruct((T, D), F32),
        compiler_params=_cparams(("arbitrary",), 48),
        name="combine",
    )(dest_km, h, info, yb, final_w)


def _layer(h_in, l, B, S, mix_norm_w, w_in, conv_dw_w, conv_dw_b, conv_ln_w, conv_ln_b,
           lam_q1, lam_k1, lam_q2, lam_k2, attn_subln_w, w_out, ffn_norm_w,
           w_group, b_group, w_expert_gate, b_expert_gate, w1, w3, w2):
    T, D = h_in.shape
    d_conv = conv_dw_w.shape[1]
    d_attn = (w_in.shape[1] - 2 * d_conv) // 3
    ng = w_group.shape[1]
    ne = w_expert_gate.shape[1]
    epg = ne // ng
    assert ng + ne <= LANES
    lam_init = 0.8 - 0.6 * math.exp(-0.3 * l)

    proj = _norm_inproj(h_in, mix_norm_w.reshape(1, D), w_in.astype(BF16))
    y_conv = _conformer(proj, conv_dw_w, conv_dw_b, conv_ln_w, conv_ln_b, B, S)
    y_attn = _diff_attn(proj, lam_q1, lam_k1, lam_q2, lam_k2, attn_subln_w, B, S, d_conv, d_attn, lam_init)

    wr = jnp.concatenate([w_group, w_expert_gate, jnp.zeros((D, LANES - ng - ne), F32)], axis=1)
    wr_hi = wr.astype(BF16)
    wr_lo = (wr - wr_hi.astype(F32)).astype(BF16)
    br = jnp.concatenate([b_group, b_expert_gate.reshape(-1), jnp.zeros((LANES - ng - ne,), F32)]).reshape(1, LANES)
    wr_hl = jnp.concatenate([wr_hi, wr_lo], axis=1)
    h, logits, ug = _outproj(h_in, y_conv, y_attn, w_out.astype(BF16), ffn_norm_w.reshape(1, D), wr_hl, br)

    info, cnt = _route(logits, ng, epg)

    blk = 256
    A = T * TOP_K
    nblk = (A + ne * (blk - 1) + blk - 1) // blk
    i32 = jnp.int32
    counts = cnt[0, ng:ng + ne].astype(i32)
    padded = (counts + blk - 1) // blk * blk
    pad_ends = jnp.cumsum(padded).astype(i32)
    pad_starts = pad_ends - padded
    eid = info[:, 0:TOP_K].astype(i32)
    rank = info[:, 4:4 + TOP_K].astype(i32)
    onehot = eid[:, :, None] == jnp.arange(ne, dtype=i32)[None, None, :]
    dest = jnp.sum(jnp.where(onehot, pad_starts[None, None, :], 0), axis=-1).astype(i32) + rank
    nused = (pad_ends[-1] // blk).astype(i32)
    bpos = jnp.arange(nblk, dtype=i32)
    brow = jnp.minimum(bpos, nused - 1) * blk
    bexp = jnp.minimum(jnp.sum((pad_ends[None, :] <= brow[:, None]).astype(i32), axis=1), ne - 1)
    first = ((bpos < nused) & ((bpos == 0) | (bexp != jnp.roll(bexp, 1)))).astype(i32)
    slot = ((jnp.cumsum(first) - 1) % WEIGHT_SLOTS).astype(i32)
    used_idx = jnp.where(padded > 0, jnp.arange(ne, dtype=i32), ne)
    suffix_min = lax.cummin(used_idx, reverse=True)
    next_used = jnp.concatenate([suffix_min[1:], jnp.full((2,), ne, i32)])
    ahead = jnp.arange(ne, dtype=i32)
    for _ in range(WEIGHT_SLOTS - 1):
        ahead = next_used[ahead]
    ahead = jnp.where(ahead >= ne, -1, ahead)
    nxt = ahead[bexp].astype(i32)
    lead = [suffix_min[0]]
    for _ in range(WEIGHT_SLOTS - 2):
        lead.append(next_used[lead[-1]])
    lead = jnp.stack([jnp.where(e >= ne, -1, e) for e in lead]).astype(i32)

    row_tok = _rowtok(dest.reshape(-1), nblk * blk)
    yb = _experts(bexp, first, slot, nxt, lead, nused.reshape(1), row_tok, ug, w1, w3, w2, blk)
    return h, info, dest, yb


def kernel(x, mix_norm_w, w_in, conv_dw_w, conv_dw_b, conv_ln_w, conv_ln_b, lam_q1, lam_k1, lam_q2, lam_k2,
           attn_subln_w, w_out, ffn_norm_w, w_group, b_group, w_expert_gate, b_expert_gate, w1, w3, w2,
           final_norm_w):
    B, S, D = x.shape
    depth = w_in.shape[0]
    assert depth == 1
    T = B * S
    tmc = min(256, T)
    h = x.reshape(T, D)
    for l in range(depth):
        h, info, dest, yb = _layer(
            h, l, B, S, mix_norm_w[l], w_in[l], conv_dw_w[l], conv_dw_b[l], conv_ln_w[l], conv_ln_b[l],
            lam_q1[l], lam_k1[l], lam_q2[l], lam_k2[l], attn_subln_w[l], w_out[l], ffn_norm_w[l],
            w_group[l], b_group[l], w_expert_gate[l], b_expert_gate[l], w1[l], w3[l], w2[l])
        dest_km = dest.reshape(T // tmc, tmc, TOP_K).transpose(0, 2, 1).reshape(-1)
        h = _combine(dest_km, h, info, yb, final_norm_w.reshape(1, D), tmc)
    return h.reshape(B, S, D)
```

```python
import functools
import math

import jax
import jax.numpy as jnp
from jax import lax
from jax.experimental import pallas as pl
from jax.experimental.pallas import tpu as pltpu

F32 = jnp.float32
BF16 = jnp.bfloat16

RMS_EPS = 1e-6
SUBLN_EPS = 1e-5
LN_EPS = 1e-5
TOP_K = 2
LANES = 128
SUBLANES = 8
NEG_BIG = -1e30
MIB = 1024 * 1024


def _cparams(sem, vmem_mib):
    return pltpu.CompilerParams(dimension_semantics=sem, vmem_limit_bytes=vmem_mib * MIB)


def _norm_inproj_kernel(x_ref, nw_ref, w_ref, o_ref, u_ref, *, eps, rc):
    @pl.when(pl.program_id(1) == 0)
    def _():
        for r0 in range(0, x_ref.shape[0], rc):
            rows = slice(r0, r0 + rc)
            x = x_ref[rows, :]
            ms = jnp.mean(x * x, axis=-1, keepdims=True)
            u = (x * lax.rsqrt(ms + eps) * nw_ref[...]).astype(u_ref.dtype)
            u_ref[rows, :] = u
            o_ref[rows, :] = jnp.dot(u, w_ref[...], preferred_element_type=F32).astype(o_ref.dtype)

    @pl.when(pl.program_id(1) > 0)
    def _():
        o_ref[...] = jnp.dot(u_ref[...], w_ref[...], preferred_element_type=F32).astype(o_ref.dtype)


def _norm_inproj(x2, nw, w_bf):
    T, D = x2.shape
    N = w_bf.shape[1]
    tm = min(1024, T)
    tn = 1024
    return pl.pallas_call(
        functools.partial(_norm_inproj_kernel, eps=RMS_EPS, rc=256),
        grid=(T // tm, N // tn),
        in_specs=[pl.BlockSpec((tm, D), lambda i, j: (i, 0)),
                  pl.BlockSpec((1, D), lambda i, j: (0, 0)),
                  pl.BlockSpec((D, tn), lambda i, j: (0, j))],
        out_specs=pl.BlockSpec((tm, tn), lambda i, j: (i, j)),
        out_shape=jax.ShapeDtypeStruct((T, N), BF16),
        scratch_shapes=[pltpu.VMEM((tm, D), BF16)],
        compiler_params=_cparams(("arbitrary", "arbitrary"), 48),
        name="norm_inproj",
    )(x2, nw, w_bf)


def _conv_kernel(a_ref, g_ref, w_ref, b_ref, lw_ref, lb_ref, o_ref, ubuf, cbuf, sh, *, ts, kw, halo, eps):
    s = pl.program_id(1)
    C = a_ref.shape[1]

    @pl.when(s == 0)
    def _():
        ubuf[0:halo, :] = jnp.zeros((halo, C), F32)

    @pl.when(s > 0)
    def _():
        ubuf[0:halo, :] = ubuf[ts:ts + halo, :]

    rg = 64
    for r0 in range(0, ts, rg):
        a = a_ref[r0:r0 + rg, :].astype(F32)
        g = g_ref[r0:r0 + rg, :].astype(F32)
        ubuf[halo + r0:halo + r0 + rg, :] = a * jax.nn.sigmoid(g)

    nsh = sh.shape[1]
    for r in range(1, SUBLANES):
        for i0 in range(0, nsh, rg):
            n = min(rg, nsh - i0)
            sh[r - 1, i0:i0 + n, :] = ubuf[i0 + r:i0 + r + n, :]

    off = halo - (kw - 1)
    rc = 128

    def conv_ln_rows(ri, carry):
        r0 = pl.multiple_of(ri * rc, rc)
        for c in range(C // LANES):
            cs = slice(c * LANES, (c + 1) * LANES)
            acc = jnp.broadcast_to(b_ref[0:1, cs], (rc // SUBLANES, SUBLANES, LANES))
            for k in range(kw):
                q, r = divmod(off + k, SUBLANES)
                rows = pl.ds(r0 + q * SUBLANES, rc)
                tap = ubuf[rows, cs] if r == 0 else sh[r - 1, rows, cs]
                acc = acc + w_ref[k, :, cs] * tap.reshape(rc // SUBLANES, SUBLANES, LANES)
            cbuf[pl.ds(r0, rc), cs] = acc.reshape(rc, LANES)
        cv = cbuf[pl.ds(r0, rc), :]
        mu = jnp.mean(cv, axis=-1, keepdims=True)
        d = cv - mu
        var = jnp.mean(d * d, axis=-1, keepdims=True)
        un = d * lax.rsqrt(var + eps) * lw_ref[...] + lb_ref[...]
        o_ref[pl.ds(r0, rc), :] = (un * jax.nn.sigmoid(un)).astype(o_ref.dtype)
        return carry

    lax.fori_loop(0, ts // rc, conv_ln_rows, 0)


def _conformer(proj, dw_w, dw_b, ln_w, ln_b, B, S):
    T = proj.shape[0]
    kw, C = dw_w.shape
    ts = min(256, S)
    halo = 32
    assert kw - 1 <= halo and S % ts == 0 and ts >= halo
    ns = S // ts
    wp = jnp.broadcast_to(dw_w[:, None, :], (kw, SUBLANES, C))
    return pl.pallas_call(
        functools.partial(_conv_kernel, ts=ts, kw=kw, halo=halo, eps=LN_EPS),
        grid=(B, ns),
        in_specs=[pl.BlockSpec((ts, C), lambda b, s: (b * ns + s, 0)),
                  pl.BlockSpec((ts, C), lambda b, s: (b * ns + s, 1)),
                  pl.BlockSpec((kw, SUBLANES, C), lambda b, s: (0, 0, 0)),
                  pl.BlockSpec((1, C), lambda b, s: (0, 0)),
                  pl.BlockSpec((1, C), lambda b, s: (0, 0)),
                  pl.BlockSpec((1, C), lambda b, s: (0, 0))],
        out_specs=pl.BlockSpec((ts, C), lambda b, s: (b * ns + s, 0)),
        out_shape=jax.ShapeDtypeStruct((T, C), BF16),
        scratch_shapes=[pltpu.VMEM((ts + halo, C), F32), pltpu.VMEM((ts, C), F32),
                        pltpu.VMEM((SUBLANES - 1, ts + halo - SUBLANES, C), F32)],
        compiler_params=_cparams(("arbitrary", "arbitrary"), 32),
        name="conformer",
    )(proj, proj, wp, dw_b.reshape(1, C), ln_w.reshape(1, C), ln_b.reshape(1, C))


def _attn_kernel(lq1_ref, lk1_ref, lq2_ref, lk2_ref, q_ref, k_ref, v_ref, sw_ref, o_ref,
                 vxt_ref, qq_ref, *scr, tq, hd, hp, ones_rows, lam_init, eps):
    i = pl.program_id(2)
    vd = 2 * hd
    nkb = v_ref.shape[0] // tq
    acc_refs, m_refs, sa_refs, sb_refs = (scr[n * hp:(n + 1) * hp] for n in range(4))

    @pl.when(i == 0)
    def _():
        for u in range(hp):
            for jb in range(nkb):
                vblk = v_ref[jb * tq:(jb + 1) * tq, u * vd:(u + 1) * vd].astype(F32)
                vxt_ref[u, jb, 0:vd, :] = vblk.T.astype(vxt_ref.dtype)
                vxt_ref[u, jb, vd:vd + ones_rows, :] = jnp.ones((ones_rows, tq), vxt_ref.dtype)

    lam = (jnp.exp(jnp.sum(lq1_ref[...] * lk1_ref[...], axis=-1, keepdims=True))
           - jnp.exp(jnp.sum(lq2_ref[...] * lk2_ref[...], axis=-1, keepdims=True)) + lam_init)

    for u in range(hp):
        q = q_ref[:, u * vd:(u + 1) * vd]
        qs = q * jnp.asarray(hd ** -0.5, q.dtype)
        lane = lax.broadcasted_iota(jnp.int32, q.shape, 1)
        zero = jnp.zeros_like(qs)
        qq_ref[u, 0:tq, :] = jnp.where(lane < hd, qs, zero)
        qq_ref[u, tq:2 * tq, :] = jnp.where(lane >= hd, qs, zero)
        acc_refs[u][...] = jnp.zeros(acc_refs[u].shape, F32)
        m_refs[u][...] = jnp.full(m_refs[u].shape, NEG_BIG, F32)

    def scores(j, u, dst):
        rows = pl.ds(pl.multiple_of(j * tq, tq), tq)
        kb = k_ref[rows, u * vd:(u + 1) * vd]
        dst[...] = lax.dot_general(kb, qq_ref[u], (((1,), (1,)), ((), ())), preferred_element_type=F32)

    def softmax_pv(j, u, src, masked):
        st = src[...]
        if masked:
            r = lax.broadcasted_iota(jnp.int32, st.shape, 0)
            c = lax.broadcasted_iota(jnp.int32, st.shape, 1)
            st = jnp.where(r <= jnp.where(c >= tq, c - tq, c), st, NEG_BIG)
        m_old = m_refs[u][...]
        m_new = jnp.maximum(m_old, jnp.max(st, axis=0, keepdims=True))
        alpha = jnp.exp(m_old - m_new)
        pt = jnp.exp(st - m_new).astype(vxt_ref.dtype)
        pv = jnp.dot(vxt_ref[u, j], pt, preferred_element_type=F32)
        acc_refs[u][...] = alpha * acc_refs[u][...] + pv
        m_refs[u][...] = m_new

    def half(j, cur, nxt):
        for u in range(hp):
            scores(j + 1, u, nxt[u])
            softmax_pv(j, u, cur[u], False)

    for u in range(hp):
        scores(0, u, sa_refs[u])

    def pair(t, carry):
        half(2 * t, sa_refs, sb_refs)
        half(2 * t + 1, sb_refs, sa_refs)
        return carry

    lax.fori_loop(0, i // 2, pair, 0)

    @pl.when(i % 2 == 1)
    def _():
        half(i - 1, sa_refs, sb_refs)
        for u in range(hp):
            softmax_pv(i, u, sb_refs[u], True)

    @pl.when(i % 2 == 0)
    def _():
        for u in range(hp):
            softmax_pv(i, u, sa_refs[u], True)

    for u in range(hp):
        acc = acc_refs[u][...]
        o12 = acc[0:vd] * (1.0 / acc[vd:vd + 1])
        ot = o12[:, 0:tq] - lam * o12[:, tq:2 * tq]
        msq = jnp.mean(ot * ot, axis=0, keepdims=True)
        o = (ot * lax.rsqrt(msq + eps)).T
        o_ref[:, u * vd:(u + 1) * vd] = (o * sw_ref[...] * (1.0 - lam_init)).astype(o_ref.dtype)


def _diff_attn(proj, lq1, lk1, lq2, lk2, subln_w, B, S, d_conv, d_attn, lam_init):
    T = proj.shape[0]
    vd = subln_w.shape[0]
    hd = lq1.shape[0]
    assert vd == LANES and 2 * hd == vd
    H = d_attn // vd
    hp = 8
    ones_rows = 16
    tq = min(256, S)
    nq = S // tq
    assert H % hp == 0
    qc = 2 * d_conv // (hp * vd)
    kc = qc + H // hp
    vc = kc + H // hp
    lspec = pl.BlockSpec((1, hd), lambda b, h, i: (0, 0))
    return pl.pallas_call(
        functools.partial(_attn_kernel, tq=tq, hd=hd, hp=hp, ones_rows=ones_rows, lam_init=lam_init, eps=SUBLN_EPS),
        grid=(B, H // hp, nq),
        in_specs=[lspec, lspec, lspec, lspec,
                  pl.BlockSpec((tq, hp * vd), lambda b, h, i: (b * nq + i, qc + h)),
                  pl.BlockSpec((S, hp * vd), lambda b, h, i: (b, kc + h)),
                  pl.BlockSpec((S, hp * vd), lambda b, h, i: (b, vc + h)),
                  pl.BlockSpec((1, vd), lambda b, h, i: (0, 0))],
        out_specs=pl.BlockSpec((tq, hp * vd), lambda b, h, i: (b * nq + i, h)),
        out_shape=jax.ShapeDtypeStruct((T, d_attn), BF16),
        scratch_shapes=[pltpu.VMEM((hp, S // tq, vd + ones_rows, tq), BF16),
                        pltpu.VMEM((hp, 2 * tq, vd), BF16)]
                       + [pltpu.VMEM((vd + ones_rows, 2 * tq), F32) for _ in range(hp)]
                       + [pltpu.VMEM((1, 2 * tq), F32) for _ in range(hp)]
                       + [pltpu.VMEM((tq, 2 * tq), F32) for _ in range(2 * hp)],
        compiler_params=_cparams(("arbitrary", "arbitrary", "arbitrary"), 52),
        name="diff_attn",
    )(lq1.reshape(1, hd), lk1.reshape(1, hd), lq2.reshape(1, hd), lk2.reshape(1, hd),
      proj, proj, proj, subln_w.reshape(1, vd))


def _pack_bf16_pairs(v):
    n = v.shape[1] // 2
    return _pack_bf16_words(v[:, 0:n], v[:, n:2 * n])


def _pack_bf16_words(lo, hi):
    lo_bits = lax.bitcast_convert_type(lo.astype(BF16).astype(F32), jnp.uint32)
    hi_bits = lax.bitcast_convert_type(hi.astype(BF16).astype(F32), jnp.uint32)
    return (lo_bits >> 16) | (hi_bits & jnp.uint32(0xFFFF0000))


def _unpack_bf16_pairs(w):
    lo = lax.bitcast_convert_type(w << 16, F32)
    hi = lax.bitcast_convert_type(w & jnp.uint32(0xFFFF0000), F32)
    return lo, hi


def _store_row_tiles(ref, row0, words):
    m, w = words.shape
    assert w == SUBLANES * LANES
    for s in range(SUBLANES):
        ref[pl.ds(row0 * SUBLANES + s, m, stride=SUBLANES), :] = words[:, s * LANES:(s + 1) * LANES]


def _load_row_tiles(ref, row0, m):
    return jnp.concatenate([ref[pl.ds(row0 * SUBLANES + s, m, stride=SUBLANES), :] for s in range(SUBLANES)], axis=1)


def _outproj_kernel(x_ref, yc_ref, ya_ref, wo_ref, fw_ref, wr_ref, br_ref, h_ref, lg_ref, ug_ref, *, eps, nsub):
    dc = yc_ref.shape[1]
    sub = x_ref.shape[0] // nsub
    mixes = []
    for t in range(nsub):
        rows = slice(t * sub, (t + 1) * sub)
        mixes.append(jnp.dot(yc_ref[rows, :], wo_ref[0:dc, :], preferred_element_type=F32)
                     + jnp.dot(ya_ref[rows, :], wo_ref[dc:, :], preferred_element_type=F32))
    for t in range(nsub):
        rows = slice(t * sub, (t + 1) * sub)
        h = x_ref[rows, :] + mixes[t]
        h_ref[rows, :] = h
        ms = jnp.mean(h * h, axis=-1, keepdims=True)
        un = h * lax.rsqrt(ms + eps) * fw_ref[...]
        hi = un.astype(BF16)
        lo = (un - hi.astype(F32)).astype(BF16)
        hh = jnp.dot(hi, wr_ref[...], preferred_element_type=F32)
        lh = jnp.dot(lo, wr_ref[:, 0:LANES], preferred_element_type=F32)
        lg_ref[rows, :] = hh[:, 0:LANES] + hh[:, LANES:2 * LANES] + lh + br_ref[...]
        _store_row_tiles(ug_ref, t * sub, _pack_bf16_pairs(un))


def _outproj(x2, y_conv, y_attn, wo_bf, ffn_w, wr_hl, br):
    T, D = x2.shape
    dc = y_conv.shape[1]
    da = y_attn.shape[1]
    tm = min(512, T)
    nsub = 2 if tm % 512 == 0 else 1
    return pl.pallas_call(
        functools.partial(_outproj_kernel, eps=RMS_EPS, nsub=nsub),
        grid=(T // tm,),
        in_specs=[pl.BlockSpec((tm, D), lambda i: (i, 0)),
                  pl.BlockSpec((tm, dc), lambda i: (i, 0)),
                  pl.BlockSpec((tm, da), lambda i: (i, 0)),
                  pl.BlockSpec((dc + da, D), lambda i: (0, 0), pipeline_mode=pl.Buffered(1)),
                  pl.BlockSpec((1, D), lambda i: (0, 0)),
                  pl.BlockSpec((D, 2 * LANES), lambda i: (0, 0)),
                  pl.BlockSpec((1, LANES), lambda i: (0, 0))],
        out_specs=[pl.BlockSpec((tm, D), lambda i: (i, 0)),
                   pl.BlockSpec((tm, LANES), lambda i: (i, 0)),
                   pl.BlockSpec((tm * SUBLANES, LANES), lambda i: (i, 0))],
        out_shape=[jax.ShapeDtypeStruct((T, D), F32), jax.ShapeDtypeStruct((T, LANES), F32),
                   jax.ShapeDtypeStruct((T * SUBLANES, LANES), jnp.uint32)],
        compiler_params=_cparams(("arbitrary",), 56),
        name="outproj",
    )(x2, y_conv, y_attn, wo_bf, ffn_w, wr_hl, br)


def _route_kernel(lg_ref, info_ref, cnt_ref, *, ng, epg):
    i = pl.program_id(0)
    lg = lg_ref[...]
    tm = lg.shape[0]
    lane = lax.broadcasted_iota(jnp.int32, lg.shape, 1)
    lanef = lane.astype(F32)
    ne = ng * epg

    def first_argmax(vals):
        mx = jnp.max(vals, axis=-1, keepdims=True)
        idx = jnp.min(jnp.where(vals == mx, lanef, float(LANES)), axis=-1, keepdims=True)
        return mx, idx

    gmask = lane < ng
    gl = jnp.where(gmask, lg, NEG_BIG)
    gmax, gsel = first_argmax(gl)
    gsum = jnp.sum(jnp.where(gmask, jnp.exp(gl - gmax), 0.0), axis=-1, keepdims=True)
    g_w = 1.0 / gsum
    lo = gsel * epg + ng
    emask = (lanef >= lo) & (lanef < lo + epg)
    el = jnp.where(emask, lg, NEG_BIG)
    v1, i1 = first_argmax(el)
    el2 = jnp.where(lanef == i1, NEG_BIG, el)
    v2, i2 = first_argmax(el2)
    e2 = jnp.exp(v2 - v1)
    p1 = 1.0 / (1.0 + e2)
    gate1 = g_w * p1
    gate2 = g_w * (e2 * p1)

    oh1 = lanef == i1
    oh2 = lanef == i2
    cmat = jnp.where(oh1 | oh2, 1.0, 0.0).astype(BF16)
    r = lax.broadcasted_iota(jnp.int32, (tm, tm), 0)
    c = lax.broadcasted_iota(jnp.int32, (tm, tm), 1)
    tri = jnp.where(c < r, 1.0, 0.0).astype(BF16)

    @pl.when(i == 0)
    def _():
        cnt_ref[...] = jnp.zeros(cnt_ref.shape, F32)

    carry = cnt_ref[0:1, :]
    prefix = jnp.dot(tri, cmat, preferred_element_type=F32) + carry
    rank1 = jnp.sum(jnp.where(oh1, prefix, 0.0), axis=-1, keepdims=True)
    rank2 = jnp.sum(jnp.where(oh2, prefix, 0.0), axis=-1, keepdims=True)
    cnt_ref[...] = jnp.broadcast_to(carry + jnp.sum(cmat.astype(F32), axis=0, keepdims=True), cnt_ref.shape)

    info = jnp.where(lane == 0, i1 - ng,
           jnp.where(lane == 1, i2 - ng,
           jnp.where(lane == 2, gate1,
           jnp.where(lane == 3, gate2,
           jnp.where(lane == 4, rank1,
           jnp.where(lane == 5, rank2, 0.0))))))
    info_ref[...] = info


def _route(logits, ng, epg):
    T = logits.shape[0]
    tm = min(512, T)
    return pl.pallas_call(
        functools.partial(_route_kernel, ng=ng, epg=epg),
        grid=(T // tm,),
        in_specs=[pl.BlockSpec((tm, LANES), lambda i: (i, 0))],
        out_specs=[pl.BlockSpec((tm, LANES), lambda i: (i, 0)),
                   pl.BlockSpec((8, LANES), lambda i: (0, 0))],
        out_shape=[jax.ShapeDtypeStruct((T, LANES), F32), jax.ShapeDtypeStruct((8, LANES), F32)],
        compiler_params=_cparams(("arbitrary",), 32),
        name="route",
    )(logits)


ROW_UNROLL = 8


def _dispatch_kernel(dest_ref, pend_ref, ug_ref, xs_hbm, ring, zbuf, sems, zsem, *, tm, blk, ne):
    i = pl.program_id(0)
    n = pl.num_programs(0)
    slot = i % 2

    def seg_tail(e):
        end = pend_ref[e]
        start = jnp.where(e == 0, 0, pend_ref[jnp.maximum(e - 1, 0)])
        tail = pl.multiple_of(jnp.maximum(end - blk, 0) * SUBLANES, blk * SUBLANES)
        return pltpu.make_async_copy(zbuf, xs_hbm.at[pl.ds(tail, blk * SUBLANES)], zsem), end > start

    @pl.when(i == 0)
    def _():
        zbuf[...] = jnp.zeros(zbuf.shape, zbuf.dtype)

        def zstart(e, carry):
            cp, nonempty = seg_tail(e)

            @pl.when(nonempty)
            def _():
                cp.start()
            return carry

        def zwait(e, carry):
            cp, nonempty = seg_tail(e)

            @pl.when(nonempty)
            def _():
                cp.wait()
            return carry

        def spare(b):
            row = pl.multiple_of(b * blk, blk)
            tile = pl.multiple_of(row * SUBLANES, blk * SUBLANES)
            return pltpu.make_async_copy(zbuf, xs_hbm.at[pl.ds(tile, blk * SUBLANES)], zsem), row >= pend_ref[ne - 1]

        def sstart(b, carry):
            cp, unused = spare(b)

            @pl.when(unused)
            def _():
                cp.start()
            return carry

        def swait(b, carry):
            cp, unused = spare(b)

            @pl.when(unused)
            def _():
                cp.wait()
            return carry

        nblk = xs_hbm.shape[0] // (blk * SUBLANES)
        lax.fori_loop(0, ne, zstart, 0)
        lax.fori_loop(0, nblk, sstart, 0)
        lax.fori_loop(0, ne, zwait, 0)
        lax.fori_loop(0, nblk, swait, 0)

    def drain(s):
        for _ in range(TOP_K):
            pltpu.make_async_copy(ring.at[s], xs_hbm.at[pl.ds(0, tm * SUBLANES)], sems.at[s]).wait()

    @pl.when(i >= 2)
    def _():
        drain(slot)

    ring[slot] = ug_ref[...]

    def issue(c, carry):
        r0 = pl.multiple_of(c * ROW_UNROLL, ROW_UNROLL)
        for s in range(ROW_UNROLL):
            for k in range(TOP_K):
                d = dest_ref[(i * tm + r0 + s) * TOP_K + k]
                src = ring.at[slot, pl.ds(pl.multiple_of((r0 + s) * SUBLANES, SUBLANES), SUBLANES)]
                dst = xs_hbm.at[pl.ds(pl.multiple_of(d * SUBLANES, SUBLANES), SUBLANES)]
                pltpu.make_async_copy(src, dst, sems.at[slot]).start(priority=k % 2)
        return carry

    lax.fori_loop(0, tm // ROW_UNROLL, issue, 0)

    @pl.when(i == n - 1)
    def _():
        drain(slot)

        @pl.when(n >= 2)
        def _():
            drain(1 - slot)


def _dispatch(dest_flat, pad_ends, ug, nrows, tm, blk):
    T = ug.shape[0] // SUBLANES
    ne = pad_ends.shape[0]
    grid_spec = pltpu.PrefetchScalarGridSpec(
        num_scalar_prefetch=2,
        grid=(T // tm,),
        in_specs=[pl.BlockSpec((tm * SUBLANES, LANES), lambda i, d, p: (i, 0))],
        out_specs=pl.BlockSpec(memory_space=pl.ANY),
        scratch_shapes=[pltpu.VMEM((2, tm * SUBLANES, LANES), ug.dtype),
                        pltpu.VMEM((blk * SUBLANES, LANES), ug.dtype),
                        pltpu.SemaphoreType.DMA((2,)),
                        pltpu.SemaphoreType.DMA(())],
    )
    return pl.pallas_call(
        functools.partial(_dispatch_kernel, tm=tm, blk=blk, ne=ne),
        grid_spec=grid_spec,
        out_shape=jax.ShapeDtypeStruct((nrows * SUBLANES, LANES), ug.dtype),
        compiler_params=_cparams(("arbitrary",), 32),
        name="dispatch",
    )(dest_flat, pad_ends, ug)


WEIGHT_DMA_PRIORITY = (0, 1, 1)
WEIGHT_SLOTS = 3


def _experts_kernel(bexp_ref, first_ref, slot_ref, next_ref, lead_ref, nused_ref, x_ref, w1_hbm, w3_hbm, w2_hbm,
                    y_ref, w1b, w3b, w2b, wsems):
    i = pl.program_id(0)
    nused = nused_ref[0]

    def weight_copies(e, s):
        return (pltpu.make_async_copy(w1_hbm.at[e], w1b.at[s], wsems.at[s, 0]),
                pltpu.make_async_copy(w3_hbm.at[e], w3b.at[s], wsems.at[s, 1]),
                pltpu.make_async_copy(w2_hbm.at[e], w2b.at[s], wsems.at[s, 2]))

    def start_weights(e, s):
        for n, cp in enumerate(weight_copies(e, s)):
            cp.start(priority=WEIGHT_DMA_PRIORITY[n])

    @pl.when(i == 0)
    def _():
        for n in range(WEIGHT_SLOTS - 1):
            @pl.when(lead_ref[n] >= 0)
            def _():
                start_weights(lead_ref[n], n)

    @pl.when(i < nused)
    def _():
        s = slot_ref[i]

        @pl.when(first_ref[i] == 1)
        def _():
            @pl.when(next_ref[i] >= 0)
            def _():
                start_weights(next_ref[i], (s + WEIGHT_SLOTS - 1) % WEIGHT_SLOTS)
            for cp in weight_copies(bexp_ref[i], s):
                cp.wait()

        blk = x_ref.shape[0] // SUBLANES
        x_lo, x_hi = _unpack_bf16_pairs(_load_row_tiles(x_ref, 0, blk))
        half = x_lo.shape[1]
        a = (jnp.dot(x_lo, w1b[s, 0:half, :], preferred_element_type=F32)
             + jnp.dot(x_hi, w1b[s, half:2 * half, :], preferred_element_type=F32))
        b = (jnp.dot(x_lo, w3b[s, 0:half, :], preferred_element_type=F32)
             + jnp.dot(x_hi, w3b[s, half:2 * half, :], preferred_element_type=F32))
        hdn = a * jax.nn.sigmoid(a) * b
        cw = 2 * LANES
        for c0 in range(0, half, cw):
            y_lo = jnp.dot(hdn, w2b[s, :, c0:c0 + cw], preferred_element_type=F32)
            y_hi = jnp.dot(hdn, w2b[s, :, half + c0:half + c0 + cw], preferred_element_type=F32)
            words = _pack_bf16_words(y_lo, y_hi)
            for g in range(cw // LANES):
                y_ref[pl.ds(c0 // LANES + g, blk, stride=SUBLANES), :] = words[:, g * LANES:(g + 1) * LANES]

    @pl.when(i >= nused)
    def _():
        y_ref[...] = jnp.zeros(y_ref.shape, y_ref.dtype)


def _experts(bexp, first, slot, nxt, lead, nused, xs, w1, w3, w2, blk):
    P = xs.shape[0] // SUBLANES
    E, D, Fh = w1.shape
    nblk = P // blk
    smap = lambda i, *_: (i, 0)
    xmap = lambda i, be, fi, sl, nx, ld, nu: (jnp.minimum(i, nu[0] - 1), 0)
    grid_spec = pltpu.PrefetchScalarGridSpec(
        num_scalar_prefetch=6,
        grid=(nblk,),
        in_specs=[pl.BlockSpec((blk * SUBLANES, LANES), xmap),
                  pl.BlockSpec(memory_space=pl.ANY),
                  pl.BlockSpec(memory_space=pl.ANY),
                  pl.BlockSpec(memory_space=pl.ANY)],
        out_specs=pl.BlockSpec((blk * SUBLANES, LANES), smap),
        scratch_shapes=[pltpu.VMEM((WEIGHT_SLOTS, D, Fh), w1.dtype),
                        pltpu.VMEM((WEIGHT_SLOTS, D, Fh), w3.dtype),
                        pltpu.VMEM((WEIGHT_SLOTS, Fh, D), w2.dtype),
                        pltpu.SemaphoreType.DMA((WEIGHT_SLOTS, 3))],
    )
    return pl.pallas_call(
        _experts_kernel,
        grid_spec=grid_spec,
        out_shape=jax.ShapeDtypeStruct(xs.shape, xs.dtype),
        compiler_params=_cparams(("arbitrary",), 56),
        name="experts",
    )(bexp, first, slot, nxt, lead, nused, xs, w1, w3, w2)


def _row_gather(src_hbm, idx_ref, base, dst, sem, n):
    def body(c, carry):
        r0 = pl.multiple_of(c * ROW_UNROLL, ROW_UNROLL)
        for s in range(ROW_UNROLL):
            d = idx_ref[base + r0 + s]
            pltpu.make_async_copy(src_hbm.at[pl.ds(pl.multiple_of(d * SUBLANES, SUBLANES), SUBLANES)],
                                  dst.at[pl.ds(pl.multiple_of((r0 + s) * SUBLANES, SUBLANES), SUBLANES)],
                                  sem).start(priority=s % 2)
        return carry
    lax.fori_loop(0, n // ROW_UNROLL, body, 0)


def _row_gather_wait(src_hbm, dst, sem, n):
    pltpu.make_async_copy(src_hbm.at[pl.ds(0, n * SUBLANES)], dst, sem).wait()


def _combine_kernel(dest_ref, h_ref, info_ref, y_hbm, fw_ref, o_ref, ybuf, sems, *, tm, eps):
    i = pl.program_id(0)
    n = pl.num_programs(0)
    slot = i % 2
    nrow = TOP_K * tm

    @pl.when(i == 0)
    def _():
        _row_gather(y_hbm, dest_ref, 0, ybuf.at[0], sems.at[0], nrow)

    @pl.when(i + 1 < n)
    def _():
        _row_gather(y_hbm, dest_ref, (i + 1) * nrow, ybuf.at[1 - slot], sems.at[1 - slot], nrow)

    _row_gather_wait(y_hbm, ybuf.at[slot], sems.at[slot], nrow)
    half = o_ref.shape[1] // 2
    rc = tm

    def rows(ci, carry):
        r0 = pl.multiple_of(ci * rc, rc)
        rs = pl.ds(r0, rc)
        info = info_ref[rs, :]
        g1 = info[:, 2:3]
        g2 = info[:, 3:4]
        y1_lo, y1_hi = _unpack_bf16_pairs(_load_row_tiles(ybuf.at[slot], r0, rc))
        y2_lo, y2_hi = _unpack_bf16_pairs(_load_row_tiles(ybuf.at[slot], tm + r0, rc))
        h_lo = h_ref[rs, 0:half] + (g1 * y1_lo + g2 * y2_lo)
        h_hi = h_ref[rs, half:2 * half] + (g1 * y1_hi + g2 * y2_hi)
        ms = (jnp.sum(h_lo * h_lo, axis=-1, keepdims=True)
              + jnp.sum(h_hi * h_hi, axis=-1, keepdims=True)) / (2 * half)
        r = lax.rsqrt(ms + eps)
        o_ref[rs, 0:half] = h_lo * r * fw_ref[:, 0:half]
        o_ref[rs, half:2 * half] = h_hi * r * fw_ref[:, half:2 * half]
        return carry

    lax.fori_loop(0, tm // rc, rows, 0)


def _combine(dest_km, h, info, yb, final_w, tm):
    T, D = h.shape
    grid_spec = pltpu.PrefetchScalarGridSpec(
        num_scalar_prefetch=1,
        grid=(T // tm,),
        in_specs=[pl.BlockSpec((tm, D), lambda i, d: (i, 0)),
                  pl.BlockSpec((tm, LANES), lambda i, d: (i, 0)),
                  pl.BlockSpec(memory_space=pl.ANY),
                  pl.BlockSpec((1, D), lambda i, d: (0, 0))],
        out_specs=pl.BlockSpec((tm, D), lambda i, d: (i, 0)),
        scratch_shapes=[pltpu.VMEM((2, TOP_K * tm * SUBLANES, LANES), yb.dtype),
                        pltpu.SemaphoreType.DMA((2,))],
    )
    return pl.pallas_call(
        functools.partial(_combine_kernel, tm=tm, eps=RMS_EPS),
        grid_spec=grid_spec,
        out_shape=jax.ShapeDtypeStruct((T, D), F32),
        compiler_params=_cparams(("arbitrary",), 48),
        name="combine",
    )(dest_km, h, info, yb, final_w)


def _layer(h_in, l, B, S, mix_norm_w, w_in, conv_dw_w, conv_dw_b, conv_ln_w, conv_ln_b,
           lam_q1, lam_k1, lam_q2, lam_k2, attn_subln_w, w_out, ffn_norm_w,
           w_group, b_group, w_expert_gate, b_expert_gate, w1, w3, w2):
    T, D = h_in.shape
    d_conv = conv_dw_w.shape[1]
    d_attn = (w_in.shape[1] - 2 * d_conv) // 3
    ng = w_group.shape[1]
    ne = w_expert_gate.shape[1]
    epg = ne // ng
    assert ng + ne <= LANES
    lam_init = 0.8 - 0.6 * math.exp(-0.3 * l)

    proj = _norm_inproj(h_in, mix_norm_w.reshape(1, D), w_in.astype(BF16))
    y_conv = _conformer(proj, conv_dw_w, conv_dw_b, conv_ln_w, conv_ln_b, B, S)
    y_attn = _diff_attn(proj, lam_q1, lam_k1, lam_q2, lam_k2, attn_subln_w, B, S, d_conv, d_attn, lam_init)

    wr = jnp.concatenate([w_group, w_expert_gate, jnp.zeros((D, LANES - ng - ne), F32)], axis=1)
    wr_hi = wr.astype(BF16)
    wr_lo = (wr - wr_hi.astype(F32)).astype(BF16)
    br = jnp.concatenate([b_group, b_expert_gate.reshape(-1), jnp.zeros((LANES - ng - ne,), F32)]).reshape(1, LANES)
    wr_hl = jnp.concatenate([wr_hi, wr_lo], axis=1)
    h, logits, ug = _outproj(h_in, y_conv, y_attn, w_out.astype(BF16), ffn_norm_w.reshape(1, D), wr_hl, br)

    info, cnt = _route(logits, ng, epg)

    blk = 256
    tmd = min(256, T)
    A = T * TOP_K
    nblk = (A + ne * (blk - 1) + blk - 1) // blk
    i32 = jnp.int32
    counts = cnt[0, ng:ng + ne].astype(i32)
    padded = (counts + blk - 1) // blk * blk
    pad_ends = jnp.cumsum(padded).astype(i32)
    pad_starts = pad_ends - padded
    eid = info[:, 0:TOP_K].astype(i32)
    rank = info[:, 4:4 + TOP_K].astype(i32)
    onehot = eid[:, :, None] == jnp.arange(ne, dtype=i32)[None, None, :]
    dest = jnp.sum(jnp.where(onehot, pad_starts[None, None, :], 0), axis=-1).astype(i32) + rank
    nused = (pad_ends[-1] // blk).astype(i32)
    bpos = jnp.arange(nblk, dtype=i32)
    brow = jnp.minimum(bpos, nused - 1) * blk
    bexp = jnp.minimum(jnp.sum((pad_ends[None, :] <= brow[:, None]).astype(i32), axis=1), ne - 1)
    first = ((bpos < nused) & ((bpos == 0) | (bexp != jnp.roll(bexp, 1)))).astype(i32)
    slot = ((jnp.cumsum(first) - 1) % WEIGHT_SLOTS).astype(i32)
    used_idx = jnp.where(padded > 0, jnp.arange(ne, dtype=i32), ne)
    suffix_min = lax.cummin(used_idx, reverse=True)
    next_used = jnp.concatenate([suffix_min[1:], jnp.full((2,), ne, i32)])
    ahead = jnp.arange(ne, dtype=i32)
    for _ in range(WEIGHT_SLOTS - 1):
        ahead = next_used[ahead]
    ahead = jnp.where(ahead >= ne, -1, ahead)
    nxt = ahead[bexp].astype(i32)
    lead = [suffix_min[0]]
    for _ in range(WEIGHT_SLOTS - 2):
        lead.append(next_used[lead[-1]])
    lead = jnp.stack([jnp.where(e >= ne, -1, e) for e in lead]).astype(i32)

    xs = _dispatch(dest.reshape(-1), pad_ends, ug, nblk * blk, tmd, blk)
    yb = _experts(bexp, first, slot, nxt, lead, nused.reshape(1), xs, w1, w3, w2, blk)
    return h, info, dest, yb


def kernel(x, mix_norm_w, w_in, conv_dw_w, conv_dw_b, conv_ln_w, conv_ln_b, lam_q1, lam_k1, lam_q2, lam_k2,
           attn_subln_w, w_out, ffn_norm_w, w_group, b_group, w_expert_gate, b_expert_gate, w1, w3, w2,
           final_norm_w):
    B, S, D = x.shape
    depth = w_in.shape[0]
    assert depth == 1
    T = B * S
    tmc = min(256, T)
    h = x.reshape(T, D)
    for l in range(depth):
        h, info, dest, yb = _layer(
            h, l, B, S, mix_norm_w[l], w_in[l], conv_dw_w[l], conv_dw_b[l], conv_ln_w[l], conv_ln_b[l],
            lam_q1[l], lam_k1[l], lam_q2[l], lam_k2[l], attn_subln_w[l], w_out[l], ffn_norm_w[l],
            w_group[l], b_group[l], w_expert_gate[l], b_expert_gate[l], w1[l], w3[l], w2[l])
        dest_km = dest.reshape(T // tmc, tmc, TOP_K).transpose(0, 2, 1).reshape(-1)
        h = _combine(dest_km, h, info, yb, final_norm_w.reshape(1, D), tmc)
    return h.reshape(B, S, D)
```

```python
import functools
import math

import jax
import jax.numpy as jnp
from jax import lax
from jax.experimental import pallas as pl
from jax.experimental.pallas import tpu as pltpu

F32 = jnp.float32
BF16 = jnp.bfloat16

RMS_EPS = 1e-6
SUBLN_EPS = 1e-5
LN_EPS = 1e-5
TOP_K = 2
LANES = 128
SUBLANES = 8
NEG_BIG = -1e30
MIB = 1024 * 1024


def _cparams(sem, vmem_mib):
    return pltpu.CompilerParams(dimension_semantics=sem, vmem_limit_bytes=vmem_mib * MIB)


def _norm_inproj_kernel(x_ref, nw_ref, w_ref, o_ref, u_ref, *, eps, rc):
    @pl.when(pl.program_id(1) == 0)
    def _():
        for r0 in range(0, x_ref.shape[0], rc):
            rows = slice(r0, r0 + rc)
            x = x_ref[rows, :]
            ms = jnp.mean(x * x, axis=-1, keepdims=True)
            u = (x * lax.rsqrt(ms + eps) * nw_ref[...]).astype(u_ref.dtype)
            u_ref[rows, :] = u
            o_ref[rows, :] = jnp.dot(u, w_ref[...], preferred_element_type=F32).astype(o_ref.dtype)

    @pl.when(pl.program_id(1) > 0)
    def _():
        o_ref[...] = jnp.dot(u_ref[...], w_ref[...], preferred_element_type=F32).astype(o_ref.dtype)


def _norm_inproj(x2, nw, w_in):
    T, D = x2.shape
    N = w_in.shape[1]
    tm = min(1024, T)
    tn = 1024
    return pl.pallas_call(
        functools.partial(_norm_inproj_kernel, eps=RMS_EPS, rc=256),
        grid=(T // tm, N // tn),
        in_specs=[pl.BlockSpec((tm, D), lambda i, j: (i, 0)),
                  pl.BlockSpec((1, D), lambda i, j: (0, 0)),
                  pl.BlockSpec((D, tn), lambda i, j: (0, j))],
        out_specs=pl.BlockSpec((tm, tn), lambda i, j: (i, j)),
        out_shape=jax.ShapeDtypeStruct((T, N), BF16),
        scratch_shapes=[pltpu.VMEM((tm, D), BF16)],
        compiler_params=_cparams(("arbitrary", "arbitrary"), 48),
        name="norm_inproj",
    )(x2, nw, w_in)


def _conv_kernel(a_ref, g_ref, w_ref, b_ref, lw_ref, lb_ref, o_ref, ubuf, cbuf, sh, *, ts, kw, halo, eps):
    s = pl.program_id(1)
    C = a_ref.shape[1]

    @pl.when(s == 0)
    def _():
        ubuf[0:halo, :] = jnp.zeros((halo, C), F32)

    @pl.when(s > 0)
    def _():
        ubuf[0:halo, :] = ubuf[ts:ts + halo, :]

    rg = 64
    for r0 in range(0, ts, rg):
        a = a_ref[r0:r0 + rg, :].astype(F32)
        g = g_ref[r0:r0 + rg, :].astype(F32)
        ubuf[halo + r0:halo + r0 + rg, :] = a * jax.nn.sigmoid(g)

    nsh = sh.shape[1]
    for r in range(1, SUBLANES):
        for i0 in range(0, nsh, rg):
            n = min(rg, nsh - i0)
            sh[r - 1, i0:i0 + n, :] = ubuf[i0 + r:i0 + r + n, :]

    off = halo - (kw - 1)
    rc = 128

    def conv_ln_rows(ri, carry):
        r0 = pl.multiple_of(ri * rc, rc)
        for c in range(C // LANES):
            cs = slice(c * LANES, (c + 1) * LANES)
            acc = jnp.broadcast_to(b_ref[0:1, cs], (rc // SUBLANES, SUBLANES, LANES))
            for k in range(kw):
                q, r = divmod(off + k, SUBLANES)
                rows = pl.ds(r0 + q * SUBLANES, rc)
                tap = ubuf[rows, cs] if r == 0 else sh[r - 1, rows, cs]
                acc = acc + w_ref[k, :, cs] * tap.reshape(rc // SUBLANES, SUBLANES, LANES)
            cbuf[pl.ds(r0, rc), cs] = acc.reshape(rc, LANES)
        cv = cbuf[pl.ds(r0, rc), :]
        mu = jnp.mean(cv, axis=-1, keepdims=True)
        d = cv - mu
        var = jnp.mean(d * d, axis=-1, keepdims=True)
        un = d * lax.rsqrt(var + eps) * lw_ref[...] + lb_ref[...]
        o_ref[pl.ds(r0, rc), :] = (un * jax.nn.sigmoid(un)).astype(o_ref.dtype)
        return carry

    lax.fori_loop(0, ts // rc, conv_ln_rows, 0)


def _conformer(proj, dw_w, dw_b, ln_w, ln_b, B, S):
    T = proj.shape[0]
    kw, C = dw_w.shape
    ts = min(256, S)
    halo = 32
    assert kw - 1 <= halo and S % ts == 0 and ts >= halo
    ns = S // ts
    wp = jnp.broadcast_to(dw_w[:, None, :], (kw, SUBLANES, C))
    return pl.pallas_call(
        functools.partial(_conv_kernel, ts=ts, kw=kw, halo=halo, eps=LN_EPS),
        grid=(B, ns),
        in_specs=[pl.BlockSpec((ts, C), lambda b, s: (b * ns + s, 0)),
                  pl.BlockSpec((ts, C), lambda b, s: (b * ns + s, 1)),
                  pl.BlockSpec((kw, SUBLANES, C), lambda b, s: (0, 0, 0)),
                  pl.BlockSpec((1, C), lambda b, s: (0, 0)),
                  pl.BlockSpec((1, C), lambda b, s: (0, 0)),
                  pl.BlockSpec((1, C), lambda b, s: (0, 0))],
        out_specs=pl.BlockSpec((ts, C), lambda b, s: (b * ns + s, 0)),
        out_shape=jax.ShapeDtypeStruct((T, C), BF16),
        scratch_shapes=[pltpu.VMEM((ts + halo, C), F32), pltpu.VMEM((ts, C), F32),
                        pltpu.VMEM((SUBLANES - 1, ts + halo - SUBLANES, C), F32)],
        compiler_params=_cparams(("arbitrary", "arbitrary"), 32),
        name="conformer",
    )(proj, proj, wp, dw_b.reshape(1, C), ln_w.reshape(1, C), ln_b.reshape(1, C))


def _attn_kernel(lq1_ref, lk1_ref, lq2_ref, lk2_ref, q_ref, k_ref, v_ref, sw_ref, o_ref,
                 vxt_ref, qq_ref, *scr, tq, hd, hp, ones_rows, lam_init, eps):
    i = pl.program_id(2)
    vd = 2 * hd
    nkb = v_ref.shape[0] // tq
    acc_refs, m_refs, sa_refs, sb_refs = (scr[n * hp:(n + 1) * hp] for n in range(4))

    @pl.when(i == 0)
    def _():
        for u in range(hp):
            for jb in range(nkb):
                vblk = v_ref[jb * tq:(jb + 1) * tq, u * vd:(u + 1) * vd].astype(F32)
                vxt_ref[u, jb, 0:vd, :] = vblk.T.astype(vxt_ref.dtype)
                vxt_ref[u, jb, vd:vd + ones_rows, :] = jnp.ones((ones_rows, tq), vxt_ref.dtype)

    lam = (jnp.exp(jnp.sum(lq1_ref[...] * lk1_ref[...], axis=-1, keepdims=True))
           - jnp.exp(jnp.sum(lq2_ref[...] * lk2_ref[...], axis=-1, keepdims=True)) + lam_init)

    for u in range(hp):
        q = q_ref[:, u * vd:(u + 1) * vd]
        qs = q * jnp.asarray(hd ** -0.5, q.dtype)
        lane = lax.broadcasted_iota(jnp.int32, q.shape, 1)
        zero = jnp.zeros_like(qs)
        qq_ref[u, 0:tq, :] = jnp.where(lane < hd, qs, zero)
        qq_ref[u, tq:2 * tq, :] = jnp.where(lane >= hd, qs, zero)
        acc_refs[u][...] = jnp.zeros(acc_refs[u].shape, F32)
        m_refs[u][...] = jnp.full(m_refs[u].shape, NEG_BIG, F32)

    def scores(j, u, dst):
        rows = pl.ds(pl.multiple_of(j * tq, tq), tq)
        kb = k_ref[rows, u * vd:(u + 1) * vd]
        dst[...] = lax.dot_general(kb, qq_ref[u], (((1,), (1,)), ((), ())), preferred_element_type=F32)

    def softmax_pv(j, u, src, masked):
        st = src[...]
        if masked:
            r = lax.broadcasted_iota(jnp.int32, st.shape, 0)
            c = lax.broadcasted_iota(jnp.int32, st.shape, 1)
            st = jnp.where(r <= jnp.where(c >= tq, c - tq, c), st, NEG_BIG)
        m_old = m_refs[u][...]
        m_new = jnp.maximum(m_old, jnp.max(st, axis=0, keepdims=True))
        alpha = jnp.exp(m_old - m_new)
        pt = jnp.exp(st - m_new).astype(vxt_ref.dtype)
        pv = jnp.dot(vxt_ref[u, j], pt, preferred_element_type=F32)
        acc_refs[u][...] = alpha * acc_refs[u][...] + pv
        m_refs[u][...] = m_new

    def half(j, cur, nxt):
        for u in range(hp):
            scores(j + 1, u, nxt[u])
            softmax_pv(j, u, cur[u], False)

    for u in range(hp):
        scores(0, u, sa_refs[u])

    def pair(t, carry):
        half(2 * t, sa_refs, sb_refs)
        half(2 * t + 1, sb_refs, sa_refs)
        return carry

    lax.fori_loop(0, i // 2, pair, 0)

    @pl.when(i % 2 == 1)
    def _():
        half(i - 1, sa_refs, sb_refs)
        for u in range(hp):
            softmax_pv(i, u, sb_refs[u], True)

    @pl.when(i % 2 == 0)
    def _():
        for u in range(hp):
            softmax_pv(i, u, sa_refs[u], True)

    for u in range(hp):
        acc = acc_refs[u][...]
        o12 = acc[0:vd] * (1.0 / acc[vd:vd + 1])
        ot = o12[:, 0:tq] - lam * o12[:, tq:2 * tq]
        msq = jnp.mean(ot * ot, axis=0, keepdims=True)
        o = (ot * lax.rsqrt(msq + eps)).T
        o_ref[:, u * vd:(u + 1) * vd] = (o * sw_ref[...] * (1.0 - lam_init)).astype(o_ref.dtype)


def _diff_attn(proj, lq1, lk1, lq2, lk2, subln_w, B, S, d_conv, d_attn, lam_init):
    T = proj.shape[0]
    vd = subln_w.shape[0]
    hd = lq1.shape[0]
    assert vd == LANES and 2 * hd == vd
    H = d_attn // vd
    hp = 8
    ones_rows = 16
    tq = min(256, S)
    nq = S // tq
    assert H % hp == 0
    qc = 2 * d_conv // (hp * vd)
    kc = qc + H // hp
    vc = kc + H // hp
    lspec = pl.BlockSpec((1, hd), lambda b, h, i: (0, 0))
    return pl.pallas_call(
        functools.partial(_attn_kernel, tq=tq, hd=hd, hp=hp, ones_rows=ones_rows, lam_init=lam_init, eps=SUBLN_EPS),
        grid=(B, H // hp, nq),
        in_specs=[lspec, lspec, lspec, lspec,
                  pl.BlockSpec((tq, hp * vd), lambda b, h, i: (b * nq + i, qc + h)),
                  pl.BlockSpec((S, hp * vd), lambda b, h, i: (b, kc + h)),
                  pl.BlockSpec((S, hp * vd), lambda b, h, i: (b, vc + h)),
                  pl.BlockSpec((1, vd), lambda b, h, i: (0, 0))],
        out_specs=pl.BlockSpec((tq, hp * vd), lambda b, h, i: (b * nq + i, h)),
        out_shape=jax.ShapeDtypeStruct((T, d_attn), BF16),
        scratch_shapes=[pltpu.VMEM((hp, S // tq, vd + ones_rows, tq), BF16),
                        pltpu.VMEM((hp, 2 * tq, vd), BF16)]
                       + [pltpu.VMEM((vd + ones_rows, 2 * tq), F32) for _ in range(hp)]
                       + [pltpu.VMEM((1, 2 * tq), F32) for _ in range(hp)]
                       + [pltpu.VMEM((tq, 2 * tq), F32) for _ in range(2 * hp)],
        compiler_params=_cparams(("arbitrary", "arbitrary", "arbitrary"), 52),
        name="diff_attn",
    )(lq1.reshape(1, hd), lk1.reshape(1, hd), lq2.reshape(1, hd), lk2.reshape(1, hd),
      proj, proj, proj, subln_w.reshape(1, vd))


def _pack_bf16_pairs(v):
    n = v.shape[1] // 2
    return _pack_bf16_words(v[:, 0:n], v[:, n:2 * n])


def _pack_bf16_words(lo, hi):
    lo_bits = lax.bitcast_convert_type(lo.astype(BF16).astype(F32), jnp.uint32)
    hi_bits = lax.bitcast_convert_type(hi.astype(BF16).astype(F32), jnp.uint32)
    return (lo_bits >> 16) | (hi_bits & jnp.uint32(0xFFFF0000))


def _unpack_bf16_pairs(w):
    lo = lax.bitcast_convert_type(w << 16, F32)
    hi = lax.bitcast_convert_type(w & jnp.uint32(0xFFFF0000), F32)
    return lo, hi


def _store_row_tiles(ref, row0, words):
    m, w = words.shape
    assert w == SUBLANES * LANES
    for s in range(SUBLANES):
        ref[pl.ds(row0 * SUBLANES + s, m, stride=SUBLANES), :] = words[:, s * LANES:(s + 1) * LANES]


def _load_row_tiles(ref, row0, m):
    return jnp.concatenate([ref[pl.ds(row0 * SUBLANES + s, m, stride=SUBLANES), :] for s in range(SUBLANES)], axis=1)


def _outproj_kernel(x_ref, yc_ref, ya_ref, wo_ref, fw_ref, wr_ref, br_ref, h_ref, lg_ref, ug_ref, *, eps, nsub):
    dc = yc_ref.shape[1]
    sub = x_ref.shape[0] // nsub
    mixes = []
    for t in range(nsub):
        rows = slice(t * sub, (t + 1) * sub)
        mixes.append(jnp.dot(yc_ref[rows, :], wo_ref[0:dc, :], preferred_element_type=F32)
                     + jnp.dot(ya_ref[rows, :], wo_ref[dc:, :], preferred_element_type=F32))
    for t in range(nsub):
        rows = slice(t * sub, (t + 1) * sub)
        h = x_ref[rows, :] + mixes[t]
        h_ref[rows, :] = h
        ms = jnp.mean(h * h, axis=-1, keepdims=True)
        un = h * lax.rsqrt(ms + eps) * fw_ref[...]
        hi = un.astype(BF16)
        lo = (un - hi.astype(F32)).astype(BF16)
        hh = jnp.dot(hi, wr_ref[...], preferred_element_type=F32)
        lh = jnp.dot(lo, wr_ref[:, 0:LANES], preferred_element_type=F32)
        lg_ref[rows, :] = hh[:, 0:LANES] + hh[:, LANES:2 * LANES] + lh + br_ref[...]
        _store_row_tiles(ug_ref, t * sub, _pack_bf16_pairs(un))


def _outproj(x2, y_conv, y_attn, w_out, ffn_w, wr_hl, br):
    T, D = x2.shape
    dc = y_conv.shape[1]
    da = y_attn.shape[1]
    tm = min(512, T)
    nsub = 2 if tm % 512 == 0 else 1
    return pl.pallas_call(
        functools.partial(_outproj_kernel, eps=RMS_EPS, nsub=nsub),
        grid=(T // tm,),
        in_specs=[pl.BlockSpec((tm, D), lambda i: (i, 0)),
                  pl.BlockSpec((tm, dc), lambda i: (i, 0)),
                  pl.BlockSpec((tm, da), lambda i: (i, 0)),
                  pl.BlockSpec((dc + da, D), lambda i: (0, 0), pipeline_mode=pl.Buffered(1)),
                  pl.BlockSpec((1, D), lambda i: (0, 0)),
                  pl.BlockSpec((D, 2 * LANES), lambda i: (0, 0)),
                  pl.BlockSpec((1, LANES), lambda i: (0, 0))],
        out_specs=[pl.BlockSpec((tm, D), lambda i: (i, 0)),
                   pl.BlockSpec((tm, LANES), lambda i: (i, 0)),
                   pl.BlockSpec((tm * SUBLANES, LANES), lambda i: (i, 0))],
        out_shape=[jax.ShapeDtypeStruct((T, D), F32), jax.ShapeDtypeStruct((T, LANES), F32),
                   jax.ShapeDtypeStruct((T * SUBLANES, LANES), jnp.uint32)],
        compiler_params=_cparams(("arbitrary",), 56),
        name="outproj",
    )(x2, y_conv, y_attn, w_out, ffn_w, wr_hl, br)


def _route_kernel(lg_ref, info_ref, cnt_ref, *, ng, epg):
    i = pl.program_id(0)
    lg = lg_ref[...]
    tm = lg.shape[0]
    lane = lax.broadcasted_iota(jnp.int32, lg.shape, 1)
    lanef = lane.astype(F32)
    ne = ng * epg

    def first_argmax(vals):
        mx = jnp.max(vals, axis=-1, keepdims=True)
        idx = jnp.min(jnp.where(vals == mx, lanef, float(LANES)), axis=-1, keepdims=True)
        return mx, idx

    gmask = lane < ng
    gl = jnp.where(gmask, lg, NEG_BIG)
    gmax, gsel = first_argmax(gl)
    gsum = jnp.sum(jnp.where(gmask, jnp.exp(gl - gmax), 0.0), axis=-1, keepdims=True)
    g_w = 1.0 / gsum
    lo = gsel * epg + ng
    emask = (lanef >= lo) & (lanef < lo + epg)
    el = jnp.where(emask, lg, NEG_BIG)
    v1, i1 = first_argmax(el)
    el2 = jnp.where(lanef == i1, NEG_BIG, el)
    v2, i2 = first_argmax(el2)
    e2 = jnp.exp(v2 - v1)
    p1 = 1.0 / (1.0 + e2)
    gate1 = g_w * p1
    gate2 = g_w * (e2 * p1)

    oh1 = lanef == i1
    oh2 = lanef == i2
    cmat = jnp.where(oh1 | oh2, 1.0, 0.0).astype(BF16)
    r = lax.broadcasted_iota(jnp.int32, (tm, tm), 0)
    c = lax.broadcasted_iota(jnp.int32, (tm, tm), 1)
    tri = jnp.where(c < r, 1.0, 0.0).astype(BF16)

    @pl.when(i == 0)
    def _():
        cnt_ref[...] = jnp.zeros(cnt_ref.shape, F32)

    carry = cnt_ref[0:1, :]
    prefix = jnp.dot(tri, cmat, preferred_element_type=F32) + carry
    rank1 = jnp.sum(jnp.where(oh1, prefix, 0.0), axis=-1, keepdims=True)
    rank2 = jnp.sum(jnp.where(oh2, prefix, 0.0), axis=-1, keepdims=True)
    cnt_ref[...] = jnp.broadcast_to(carry + jnp.sum(cmat.astype(F32), axis=0, keepdims=True), cnt_ref.shape)

    info = jnp.where(lane == 0, i1 - ng,
           jnp.where(lane == 1, i2 - ng,
           jnp.where(lane == 2, gate1,
           jnp.where(lane == 3, gate2,
           jnp.where(lane == 4, rank1,
           jnp.where(lane == 5, rank2, 0.0))))))
    info_ref[...] = info


def _route(logits, ng, epg):
    T = logits.shape[0]
    tm = min(512, T)
    return pl.pallas_call(
        functools.partial(_route_kernel, ng=ng, epg=epg),
        grid=(T // tm,),
        in_specs=[pl.BlockSpec((tm, LANES), lambda i: (i, 0))],
        out_specs=[pl.BlockSpec((tm, LANES), lambda i: (i, 0)),
                   pl.BlockSpec((8, LANES), lambda i: (0, 0))],
        out_shape=[jax.ShapeDtypeStruct((T, LANES), F32), jax.ShapeDtypeStruct((8, LANES), F32)],
        compiler_params=_cparams(("arbitrary",), 32),
        name="route",
    )(logits)


ROW_UNROLL = 8


def _dispatch_kernel(dest_ref, pend_ref, ug_ref, xs_hbm, ring, zbuf, sems, zsem, *, tm, blk, ne):
    i = pl.program_id(0)
    n = pl.num_programs(0)
    slot = i % 2

    def seg_tail(e):
        end = pend_ref[e]
        start = jnp.where(e == 0, 0, pend_ref[jnp.maximum(e - 1, 0)])
        tail = pl.multiple_of(jnp.maximum(end - blk, 0) * SUBLANES, blk * SUBLANES)
        return pltpu.make_async_copy(zbuf, xs_hbm.at[pl.ds(tail, blk * SUBLANES)], zsem), end > start

    @pl.when(i == 0)
    def _():
        zbuf[...] = jnp.zeros(zbuf.shape, zbuf.dtype)

        def zstart(e, carry):
            cp, nonempty = seg_tail(e)

            @pl.when(nonempty)
            def _():
                cp.start()
            return carry

        def zwait(e, carry):
            cp, nonempty = seg_tail(e)

            @pl.when(nonempty)
            def _():
                cp.wait()
            return carry

        def spare(b):
            row = pl.multiple_of(b * blk, blk)
            tile = pl.multiple_of(row * SUBLANES, blk * SUBLANES)
            return pltpu.make_async_copy(zbuf, xs_hbm.at[pl.ds(tile, blk * SUBLANES)], zsem), row >= pend_ref[ne - 1]

        def sstart(b, carry):
            cp, unused = spare(b)

            @pl.when(unused)
            def _():
                cp.start()
            return carry

        def swait(b, carry):
            cp, unused = spare(b)

            @pl.when(unused)
            def _():
                cp.wait()
            return carry

        nblk = xs_hbm.shape[0] // (blk * SUBLANES)
        lax.fori_loop(0, ne, zstart, 0)
        lax.fori_loop(0, nblk, sstart, 0)
        lax.fori_loop(0, ne, zwait, 0)
        lax.fori_loop(0, nblk, swait, 0)

    def drain(s):
        for _ in range(TOP_K):
            pltpu.make_async_copy(ring.at[s], xs_hbm.at[pl.ds(0, tm * SUBLANES)], sems.at[s]).wait()

    @pl.when(i >= 2)
    def _():
        drain(slot)

    ring[slot] = ug_ref[...]

    def issue(c, carry):
        r0 = pl.multiple_of(c * ROW_UNROLL, ROW_UNROLL)
        for s in range(ROW_UNROLL):
            for k in range(TOP_K):
                d = dest_ref[(i * tm + r0 + s) * TOP_K + k]
                src = ring.at[slot, pl.ds(pl.multiple_of((r0 + s) * SUBLANES, SUBLANES), SUBLANES)]
                dst = xs_hbm.at[pl.ds(pl.multiple_of(d * SUBLANES, SUBLANES), SUBLANES)]
                pltpu.make_async_copy(src, dst, sems.at[slot]).start(priority=k % 2)
        return carry

    lax.fori_loop(0, tm // ROW_UNROLL, issue, 0)

    @pl.when(i == n - 1)
    def _():
        drain(slot)

        @pl.when(n >= 2)
        def _():
            drain(1 - slot)


def _dispatch(dest_flat, pad_ends, ug, nrows, tm, blk):
    T = ug.shape[0] // SUBLANES
    ne = pad_ends.shape[0]
    grid_spec = pltpu.PrefetchScalarGridSpec(
        num_scalar_prefetch=2,
        grid=(T // tm,),
        in_specs=[pl.BlockSpec((tm * SUBLANES, LANES), lambda i, d, p: (i, 0))],
        out_specs=pl.BlockSpec(memory_space=pl.ANY),
        scratch_shapes=[pltpu.VMEM((2, tm * SUBLANES, LANES), ug.dtype),
                        pltpu.VMEM((blk * SUBLANES, LANES), ug.dtype),
                        pltpu.SemaphoreType.DMA((2,)),
                        pltpu.SemaphoreType.DMA(())],
    )
    return pl.pallas_call(
        functools.partial(_dispatch_kernel, tm=tm, blk=blk, ne=ne),
        grid_spec=grid_spec,
        out_shape=jax.ShapeDtypeStruct((nrows * SUBLANES, LANES), ug.dtype),
        compiler_params=_cparams(("arbitrary",), 32),
        name="dispatch",
    )(dest_flat, pad_ends, ug)


WEIGHT_DMA_PRIORITY = (0, 1, 1)
WEIGHT_SLOTS = 3


def _experts_kernel(bexp_ref, first_ref, slot_ref, next_ref, lead_ref, nused_ref, x_ref, w1_hbm, w3_hbm, w2_hbm,
                    y_ref, w1b, w3b, w2b, wsems):
    i = pl.program_id(0)
    nused = nused_ref[0]

    def weight_copies(e, s):
        return (pltpu.make_async_copy(w1_hbm.at[e], w1b.at[s], wsems.at[s, 0]),
                pltpu.make_async_copy(w3_hbm.at[e], w3b.at[s], wsems.at[s, 1]),
                pltpu.make_async_copy(w2_hbm.at[e], w2b.at[s], wsems.at[s, 2]))

    def start_weights(e, s):
        for n, cp in enumerate(weight_copies(e, s)):
            cp.start(priority=WEIGHT_DMA_PRIORITY[n])

    @pl.when(i == 0)
    def _():
        for n in range(WEIGHT_SLOTS - 1):
            @pl.when(lead_ref[n] >= 0)
            def _():
                start_weights(lead_ref[n], n)

    @pl.when(i < nused)
    def _():
        s = slot_ref[i]

        @pl.when(first_ref[i] == 1)
        def _():
            @pl.when(next_ref[i] >= 0)
            def _():
                start_weights(next_ref[i], (s + WEIGHT_SLOTS - 1) % WEIGHT_SLOTS)
            for cp in weight_copies(bexp_ref[i], s):
                cp.wait()

        blk = x_ref.shape[0] // SUBLANES
        x_lo, x_hi = _unpack_bf16_pairs(_load_row_tiles(x_ref, 0, blk))
        half = x_lo.shape[1]
        a = (jnp.dot(x_lo, w1b[s, 0:half, :], preferred_element_type=F32)
             + jnp.dot(x_hi, w1b[s, half:2 * half, :], preferred_element_type=F32))
        b = (jnp.dot(x_lo, w3b[s, 0:half, :], preferred_element_type=F32)
             + jnp.dot(x_hi, w3b[s, half:2 * half, :], preferred_element_type=F32))
        hdn = a * jax.nn.sigmoid(a) * b
        cw = 2 * LANES
        for c0 in range(0, half, cw):
            y_lo = jnp.dot(hdn, w2b[s, :, c0:c0 + cw], preferred_element_type=F32)
            y_hi = jnp.dot(hdn, w2b[s, :, half + c0:half + c0 + cw], preferred_element_type=F32)
            words = _pack_bf16_words(y_lo, y_hi)
            for g in range(cw // LANES):
                y_ref[pl.ds(c0 // LANES + g, blk, stride=SUBLANES), :] = words[:, g * LANES:(g + 1) * LANES]

    @pl.when(i >= nused)
    def _():
        y_ref[...] = jnp.zeros(y_ref.shape, y_ref.dtype)


def _experts(bexp, first, slot, nxt, lead, nused, xs, w1, w3, w2, blk):
    P = xs.shape[0] // SUBLANES
    E, D, Fh = w1.shape
    nblk = P // blk
    smap = lambda i, *_: (i, 0)
    xmap = lambda i, be, fi, sl, nx, ld, nu: (jnp.minimum(i, nu[0] - 1), 0)
    grid_spec = pltpu.PrefetchScalarGridSpec(
        num_scalar_prefetch=6,
        grid=(nblk,),
        in_specs=[pl.BlockSpec((blk * SUBLANES, LANES), xmap),
                  pl.BlockSpec(memory_space=pl.ANY),
                  pl.BlockSpec(memory_space=pl.ANY),
                  pl.BlockSpec(memory_space=pl.ANY)],
        out_specs=pl.BlockSpec((blk * SUBLANES, LANES), smap),
        scratch_shapes=[pltpu.VMEM((WEIGHT_SLOTS, D, Fh), w1.dtype),
                        pltpu.VMEM((WEIGHT_SLOTS, D, Fh), w3.dtype),
                        pltpu.VMEM((WEIGHT_SLOTS, Fh, D), w2.dtype),
                        pltpu.SemaphoreType.DMA((WEIGHT_SLOTS, 3))],
    )
    return pl.pallas_call(
        _experts_kernel,
        grid_spec=grid_spec,
        out_shape=jax.ShapeDtypeStruct(xs.shape, xs.dtype),
        compiler_params=_cparams(("arbitrary",), 56),
        name="experts",
    )(bexp, first, slot, nxt, lead, nused, xs, w1, w3, w2)


def _row_gather(src_hbm, idx_ref, base, dst, sem, n):
    def body(c, carry):
        r0 = pl.multiple_of(c * ROW_UNROLL, ROW_UNROLL)
        for s in range(ROW_UNROLL):
            d = idx_ref[base + r0 + s]
            pltpu.make_async_copy(src_hbm.at[pl.ds(pl.multiple_of(d * SUBLANES, SUBLANES), SUBLANES)],
                                  dst.at[pl.ds(pl.multiple_of((r0 + s) * SUBLANES, SUBLANES), SUBLANES)],
                                  sem).start(priority=s % 2)
        return carry
    lax.fori_loop(0, n // ROW_UNROLL, body, 0)


def _row_gather_wait(src_hbm, dst, sem, n):
    pltpu.make_async_copy(src_hbm.at[pl.ds(0, n * SUBLANES)], dst, sem).wait()


def _combine_kernel(dest_ref, h_ref, info_ref, y_hbm, fw_ref, o_ref, ybuf, sems, *, tm, eps):
    i = pl.program_id(0)
    n = pl.num_programs(0)
    slot = i % 2
    nrow = TOP_K * tm

    @pl.when(i == 0)
    def _():
        _row_gather(y_hbm, dest_ref, 0, ybuf.at[0], sems.at[0], nrow)

    @pl.when(i + 1 < n)
    def _():
        _row_gather(y_hbm, dest_ref, (i + 1) * nrow, ybuf.at[1 - slot], sems.at[1 - slot], nrow)

    _row_gather_wait(y_hbm, ybuf.at[slot], sems.at[slot], nrow)
    half = o_ref.shape[1] // 2
    rc = tm

    def rows(ci, carry):
        r0 = pl.multiple_of(ci * rc, rc)
        rs = pl.ds(r0, rc)
        info = info_ref[rs, :]
        g1 = info[:, 2:3]
        g2 = info[:, 3:4]
        y1_lo, y1_hi = _unpack_bf16_pairs(_load_row_tiles(ybuf.at[slot], r0, rc))
        y2_lo, y2_hi = _unpack_bf16_pairs(_load_row_tiles(ybuf.at[slot], tm + r0, rc))
        h_lo = h_ref[rs, 0:half] + (g1 * y1_lo + g2 * y2_lo)
        h_hi = h_ref[rs, half:2 * half] + (g1 * y1_hi + g2 * y2_hi)
        ms = (jnp.sum(h_lo * h_lo, axis=-1, keepdims=True)
              + jnp.sum(h_hi * h_hi, axis=-1, keepdims=True)) / (2 * half)
        r = lax.rsqrt(ms + eps)
        o_ref[rs, 0:half] = h_lo * r * fw_ref[:, 0:half]
        o_ref[rs, half:2 * half] = h_hi * r * fw_ref[:, half:2 * half]
        return carry

    lax.fori_loop(0, tm // rc, rows, 0)


def _combine(dest_km, h, info, yb, final_w, tm):
    T, D = h.shape
    grid_spec = pltpu.PrefetchScalarGridSpec(
        num_scalar_prefetch=1,
        grid=(T // tm,),
        in_specs=[pl.BlockSpec((tm, D), lambda i, d: (i, 0)),
                  pl.BlockSpec((tm, LANES), lambda i, d: (i, 0)),
                  pl.BlockSpec(memory_space=pl.ANY),
                  pl.BlockSpec((1, D), lambda i, d: (0, 0))],
        out_specs=pl.BlockSpec((tm, D), lambda i, d: (i, 0)),
        scratch_shapes=[pltpu.VMEM((2, TOP_K * tm * SUBLANES, LANES), yb.dtype),
                        pltpu.SemaphoreType.DMA((2,))],
    )
    return pl.pallas_call(
        functools.partial(_combine_kernel, tm=tm, eps=RMS_EPS),
        grid_spec=grid_spec,
        out_shape=jax.ShapeDtypeStruct((T, D), F32),
        compiler_params=_cparams(("arbitrary",), 48),
        name="combine",
    )(dest_km, h, info, yb, final_w)


def _layer(h_in, l, B, S, mix_norm_w, w_in, conv_dw_w, conv_dw_b, conv_ln_w, conv_ln_b,
           lam_q1, lam_k1, lam_q2, lam_k2, attn_subln_w, w_out, ffn_norm_w,
           w_group, b_group, w_expert_gate, b_expert_gate, w1, w3, w2):
    T, D = h_in.shape
    d_conv = conv_dw_w.shape[1]
    d_attn = (w_in.shape[1] - 2 * d_conv) // 3
    ng = w_group.shape[1]
    ne = w_expert_gate.shape[1]
    epg = ne // ng
    assert ng + ne <= LANES
    lam_init = 0.8 - 0.6 * math.exp(-0.3 * l)

    proj = _norm_inproj(h_in, mix_norm_w.reshape(1, D), w_in)
    y_conv = _conformer(proj, conv_dw_w, conv_dw_b, conv_ln_w, conv_ln_b, B, S)
    y_attn = _diff_attn(proj, lam_q1, lam_k1, lam_q2, lam_k2, attn_subln_w, B, S, d_conv, d_attn, lam_init)

    wr = jnp.concatenate([w_group, w_expert_gate, jnp.zeros((D, LANES - ng - ne), F32)], axis=1)
    wr_hi = wr.astype(BF16)
    wr_lo = (wr - wr_hi.astype(F32)).astype(BF16)
    br = jnp.concatenate([b_group, b_expert_gate.reshape(-1), jnp.zeros((LANES - ng - ne,), F32)]).reshape(1, LANES)
    wr_hl = jnp.concatenate([wr_hi, wr_lo], axis=1)
    h, logits, ug = _outproj(h_in, y_conv, y_attn, w_out, ffn_norm_w.reshape(1, D), wr_hl, br)

    info, cnt = _route(logits, ng, epg)

    blk = 256
    tmd = min(256, T)
    A = T * TOP_K
    nblk = (A + ne * (blk - 1) + blk - 1) // blk
    i32 = jnp.int32
    counts = cnt[0, ng:ng + ne].astype(i32)
    padded = (counts + blk - 1) // blk * blk
    pad_ends = jnp.cumsum(padded).astype(i32)
    pad_starts = pad_ends - padded
    eid = info[:, 0:TOP_K].astype(i32)
    rank = info[:, 4:4 + TOP_K].astype(i32)
    onehot = eid[:, :, None] == jnp.arange(ne, dtype=i32)[None, None, :]
    dest = jnp.sum(jnp.where(onehot, pad_starts[None, None, :], 0), axis=-1).astype(i32) + rank
    nused = (pad_ends[-1] // blk).astype(i32)
    bpos = jnp.arange(nblk, dtype=i32)
    brow = jnp.minimum(bpos, nused - 1) * blk
    bexp = jnp.minimum(jnp.sum((pad_ends[None, :] <= brow[:, None]).astype(i32), axis=1), ne - 1)
    first = ((bpos < nused) & ((bpos == 0) | (bexp != jnp.roll(bexp, 1)))).astype(i32)
    slot = ((jnp.cumsum(first) - 1) % WEIGHT_SLOTS).astype(i32)
    used_idx = jnp.where(padded > 0, jnp.arange(ne, dtype=i32), ne)
    suffix_min = lax.cummin(used_idx, reverse=True)
    next_used = jnp.concatenate([suffix_min[1:], jnp.full((2,), ne, i32)])
    ahead = jnp.arange(ne, dtype=i32)
    for _ in range(WEIGHT_SLOTS - 1):
        ahead = next_used[ahead]
    ahead = jnp.where(ahead >= ne, -1, ahead)
    nxt = ahead[bexp].astype(i32)
    lead = [suffix_min[0]]
    for _ in range(WEIGHT_SLOTS - 2):
        lead.append(next_used[lead[-1]])
    lead = jnp.stack([jnp.where(e >= ne, -1, e) for e in lead]).astype(i32)

    xs = _dispatch(dest.reshape(-1), pad_ends, ug, nblk * blk, tmd, blk)
    yb = _experts(bexp, first, slot, nxt, lead, nused.reshape(1), xs, w1, w3, w2, blk)
    return h, info, dest, yb


def kernel(x, mix_norm_w, w_in, conv_dw_w, conv_dw_b, conv_ln_w, conv_ln_b, lam_q1, lam_k1, lam_q2, lam_k2,
           attn_subln_w, w_out, ffn_norm_w, w_group, b_group, w_expert_gate, b_expert_gate, w1, w3, w2,
           final_norm_w):
    B, S, D = x.shape
    depth = w_in.shape[0]
    assert depth == 1
    T = B * S
    tmc = min(256, T)
    h = x.reshape(T, D)
    for l in range(depth):
        h, info, dest, yb = _layer(
            h, l, B, S, mix_norm_w[l], w_in[l], conv_dw_w[l], conv_dw_b[l], conv_ln_w[l], conv_ln_b[l],
            lam_q1[l], lam_k1[l], lam_q2[l], lam_k2[l], attn_subln_w[l], w_out[l], ffn_norm_w[l],
            w_group[l], b_group[l], w_expert_gate[l], b_expert_gate[l], w1[l], w3[l], w2[l])
        dest_km = dest.reshape(T // tmc, tmc, TOP_K).transpose(0, 2, 1).reshape(-1)
        h = _combine(dest_km, h, info, yb, final_norm_w.reshape(1, D), tmc)
    return h.reshape(B, S, D)
```

```python
import functools
import math

import jax
import jax.numpy as jnp
from jax import lax
from jax.experimental import pallas as pl
from jax.experimental.pallas import tpu as pltpu

F32 = jnp.float32
BF16 = jnp.bfloat16

RMS_EPS = 1e-6
SUBLN_EPS = 1e-5
LN_EPS = 1e-5
TOP_K = 2
LANES = 128
SUBLANES = 8
NEG_BIG = -1e30
MIB = 1024 * 1024


def _cparams(sem, vmem_mib):
    return pltpu.CompilerParams(dimension_semantics=sem, vmem_limit_bytes=vmem_mib * MIB)


def _norm_inproj_kernel(x_ref, nw_ref, w_ref, o_ref, u_ref, *, eps, rc):
    @pl.when(pl.program_id(1) == 0)
    def _():
        for r0 in range(0, x_ref.shape[0], rc):
            rows = slice(r0, r0 + rc)
            x = x_ref[rows, :]
            ms = jnp.mean(x * x, axis=-1, keepdims=True)
            u = (x * lax.rsqrt(ms + eps) * nw_ref[...]).astype(u_ref.dtype)
            u_ref[rows, :] = u
            o_ref[rows, :] = jnp.dot(u, w_ref[...], preferred_element_type=F32).astype(o_ref.dtype)

    @pl.when(pl.program_id(1) > 0)
    def _():
        o_ref[...] = jnp.dot(u_ref[...], w_ref[...], preferred_element_type=F32).astype(o_ref.dtype)


def _norm_inproj(x2, nw, w_bf):
    T, D = x2.shape
    N = w_bf.shape[1]
    tm = min(1024, T)
    tn = 1024
    return pl.pallas_call(
        functools.partial(_norm_inproj_kernel, eps=RMS_EPS, rc=256),
        grid=(T // tm, N // tn),
        in_specs=[pl.BlockSpec((tm, D), lambda i, j: (i, 0)),
                  pl.BlockSpec((1, D), lambda i, j: (0, 0)),
                  pl.BlockSpec((D, tn), lambda i, j: (0, j))],
        out_specs=pl.BlockSpec((tm, tn), lambda i, j: (i, j)),
        out_shape=jax.ShapeDtypeStruct((T, N), BF16),
        scratch_shapes=[pltpu.VMEM((tm, D), BF16)],
        compiler_params=_cparams(("arbitrary", "arbitrary"), 48),
        name="norm_inproj",
    )(x2, nw, w_bf)


def _conv_kernel(a_ref, g_ref, w_ref, b_ref, lw_ref, lb_ref, o_ref, ubuf, cbuf, sh, *, ts, kw, halo, eps):
    s = pl.program_id(1)
    C = a_ref.shape[1]

    @pl.when(s == 0)
    def _():
        ubuf[0:halo, :] = jnp.zeros((halo, C), F32)

    @pl.when(s > 0)
    def _():
        ubuf[0:halo, :] = ubuf[ts:ts + halo, :]

    rg = 64
    for r0 in range(0, ts, rg):
        a = a_ref[r0:r0 + rg, :].astype(F32)
        g = g_ref[r0:r0 + rg, :].astype(F32)
        ubuf[halo + r0:halo + r0 + rg, :] = a * jax.nn.sigmoid(g)

    nsh = sh.shape[1]
    for r in range(1, SUBLANES):
        for i0 in range(0, nsh, rg):
            n = min(rg, nsh - i0)
            sh[r - 1, i0:i0 + n, :] = ubuf[i0 + r:i0 + r + n, :]

    off = halo - (kw - 1)
    rc = 128

    def conv_ln_rows(ri, carry):
        r0 = pl.multiple_of(ri * rc, rc)
        for c in range(C // LANES):
            cs = slice(c * LANES, (c + 1) * LANES)
            acc = jnp.broadcast_to(b_ref[0:1, cs], (rc // SUBLANES, SUBLANES, LANES))
            for k in range(kw):
                q, r = divmod(off + k, SUBLANES)
                rows = pl.ds(r0 + q * SUBLANES, rc)
                tap = ubuf[rows, cs] if r == 0 else sh[r - 1, rows, cs]
                acc = acc + w_ref[k, :, cs] * tap.reshape(rc // SUBLANES, SUBLANES, LANES)
            cbuf[pl.ds(r0, rc), cs] = acc.reshape(rc, LANES)
        cv = cbuf[pl.ds(r0, rc), :]
        mu = jnp.mean(cv, axis=-1, keepdims=True)
        d = cv - mu
        var = jnp.mean(d * d, axis=-1, keepdims=True)
        un = d * lax.rsqrt(var + eps) * lw_ref[...] + lb_ref[...]
        o_ref[pl.ds(r0, rc), :] = (un * jax.nn.sigmoid(un)).astype(o_ref.dtype)
        return carry

    lax.fori_loop(0, ts // rc, conv_ln_rows, 0)


def _conformer(proj, dw_w, dw_b, ln_w, ln_b, B, S):
    T = proj.shape[0]
    kw, C = dw_w.shape
    ts = min(256, S)
    halo = 32
    assert kw - 1 <= halo and S % ts == 0 and ts >= halo
    ns = S // ts
    wp = jnp.broadcast_to(dw_w[:, None, :], (kw, SUBLANES, C))
    return pl.pallas_call(
        functools.partial(_conv_kernel, ts=ts, kw=kw, halo=halo, eps=LN_EPS),
        grid=(B, ns),
        in_specs=[pl.BlockSpec((ts, C), lambda b, s: (b * ns + s, 0)),
                  pl.BlockSpec((ts, C), lambda b, s: (b * ns + s, 1)),
                  pl.BlockSpec((kw, SUBLANES, C), lambda b, s: (0, 0, 0)),
                  pl.BlockSpec((1, C), lambda b, s: (0, 0)),
                  pl.BlockSpec((1, C), lambda b, s: (0, 0)),
                  pl.BlockSpec((1, C), lambda b, s: (0, 0))],
        out_specs=pl.BlockSpec((ts, C), lambda b, s: (b * ns + s, 0)),
        out_shape=jax.ShapeDtypeStruct((T, C), BF16),
        scratch_shapes=[pltpu.VMEM((ts + halo, C), F32), pltpu.VMEM((ts, C), F32),
                        pltpu.VMEM((SUBLANES - 1, ts + halo - SUBLANES, C), F32)],
        compiler_params=_cparams(("arbitrary", "arbitrary"), 32),
        name="conformer",
    )(proj, proj, wp, dw_b.reshape(1, C), ln_w.reshape(1, C), ln_b.reshape(1, C))


def _attn_kernel(lq1_ref, lk1_ref, lq2_ref, lk2_ref, q_ref, k_ref, v_ref, sw_ref, o_ref,
                 vxt_ref, qq_ref, *scr, tq, hd, hp, ones_rows, lam_init, eps):
    i = pl.program_id(2)
    vd = 2 * hd
    nkb = v_ref.shape[0] // tq
    acc_refs, m_refs, sa_refs, sb_refs = (scr[n * hp:(n + 1) * hp] for n in range(4))

    @pl.when(i == 0)
    def _():
        for u in range(hp):
            for jb in range(nkb):
                vblk = v_ref[jb * tq:(jb + 1) * tq, u * vd:(u + 1) * vd].astype(F32)
                vxt_ref[u, jb, 0:vd, :] = vblk.T.astype(vxt_ref.dtype)
                vxt_ref[u, jb, vd:vd + ones_rows, :] = jnp.ones((ones_rows, tq), vxt_ref.dtype)

    lam = (jnp.exp(jnp.sum(lq1_ref[...] * lk1_ref[...], axis=-1, keepdims=True))
           - jnp.exp(jnp.sum(lq2_ref[...] * lk2_ref[...], axis=-1, keepdims=True)) + lam_init)

    for u in range(hp):
        q = q_ref[:, u * vd:(u + 1) * vd]
        qs = q * jnp.asarray(hd ** -0.5, q.dtype)
        lane = lax.broadcasted_iota(jnp.int32, q.shape, 1)
        zero = jnp.zeros_like(qs)
        qq_ref[u, 0:tq, :] = jnp.where(lane < hd, qs, zero)
        qq_ref[u, tq:2 * tq, :] = jnp.where(lane >= hd, qs, zero)
        acc_refs[u][...] = jnp.zeros(acc_refs[u].shape, F32)
        m_refs[u][...] = jnp.full(m_refs[u].shape, NEG_BIG, F32)

    def scores(j, u, dst):
        rows = pl.ds(pl.multiple_of(j * tq, tq), tq)
        kb = k_ref[rows, u * vd:(u + 1) * vd]
        dst[...] = lax.dot_general(kb, qq_ref[u], (((1,), (1,)), ((), ())), preferred_element_type=F32)

    def softmax_pv(j, u, src, masked):
        st = src[...]
        if masked:
            r = lax.broadcasted_iota(jnp.int32, st.shape, 0)
            c = lax.broadcasted_iota(jnp.int32, st.shape, 1)
            st = jnp.where(r <= jnp.where(c >= tq, c - tq, c), st, NEG_BIG)
        m_old = m_refs[u][...]
        m_new = jnp.maximum(m_old, jnp.max(st, axis=0, keepdims=True))
        alpha = jnp.exp(m_old - m_new)
        pt = jnp.exp(st - m_new).astype(vxt_ref.dtype)
        pv = jnp.dot(vxt_ref[u, j], pt, preferred_element_type=F32)
        acc_refs[u][...] = alpha * acc_refs[u][...] + pv
        m_refs[u][...] = m_new

    def half(j, cur, nxt):
        for u in range(hp):
            scores(j + 1, u, nxt[u])
            softmax_pv(j, u, cur[u], False)

    for u in range(hp):
        scores(0, u, sa_refs[u])

    def pair(t, carry):
        half(2 * t, sa_refs, sb_refs)
        half(2 * t + 1, sb_refs, sa_refs)
        return carry

    lax.fori_loop(0, i // 2, pair, 0)

    @pl.when(i % 2 == 1)
    def _():
        half(i - 1, sa_refs, sb_refs)
        for u in range(hp):
            softmax_pv(i, u, sb_refs[u], True)

    @pl.when(i % 2 == 0)
    def _():
        for u in range(hp):
            softmax_pv(i, u, sa_refs[u], True)

    for u in range(hp):
        acc = acc_refs[u][...]
        o12 = acc[0:vd] * (1.0 / acc[vd:vd + 1])
        ot = o12[:, 0:tq] - lam * o12[:, tq:2 * tq]
        msq = jnp.mean(ot * ot, axis=0, keepdims=True)
        o = (ot * lax.rsqrt(msq + eps)).T
        o_ref[:, u * vd:(u + 1) * vd] = (o * sw_ref[...] * (1.0 - lam_init)).astype(o_ref.dtype)


def _diff_attn(proj, lq1, lk1, lq2, lk2, subln_w, B, S, d_conv, d_attn, lam_init):
    T = proj.shape[0]
    vd = subln_w.shape[0]
    hd = lq1.shape[0]
    assert vd == LANES and 2 * hd == vd
    H = d_attn // vd
    hp = 8
    ones_rows = 16
    tq = min(256, S)
    nq = S // tq
    assert H % hp == 0
    qc = 2 * d_conv // (hp * vd)
    kc = qc + H // hp
    vc = kc + H // hp
    lspec = pl.BlockSpec((1, hd), lambda b, h, i: (0, 0))
    return pl.pallas_call(
        functools.partial(_attn_kernel, tq=tq, hd=hd, hp=hp, ones_rows=ones_rows, lam_init=lam_init, eps=SUBLN_EPS),
        grid=(B, H // hp, nq),
        in_specs=[lspec, lspec, lspec, lspec,
                  pl.BlockSpec((tq, hp * vd), lambda b, h, i: (b * nq + i, qc + h)),
                  pl.BlockSpec((S, hp * vd), lambda b, h, i: (b, kc + h)),
                  pl.BlockSpec((S, hp * vd), lambda b, h, i: (b, vc + h)),
                  pl.BlockSpec((1, vd), lambda b, h, i: (0, 0))],
        out_specs=pl.BlockSpec((tq, hp * vd), lambda b, h, i: (b * nq + i, h)),
        out_shape=jax.ShapeDtypeStruct((T, d_attn), BF16),
        scratch_shapes=[pltpu.VMEM((hp, S // tq, vd + ones_rows, tq), BF16),
                        pltpu.VMEM((hp, 2 * tq, vd), BF16)]
                       + [pltpu.VMEM((vd + ones_rows, 2 * tq), F32) for _ in range(hp)]
                       + [pltpu.VMEM((1, 2 * tq), F32) for _ in range(hp)]
                       + [pltpu.VMEM((tq, 2 * tq), F32) for _ in range(2 * hp)],
        compiler_params=_cparams(("arbitrary", "arbitrary", "arbitrary"), 52),
        name="diff_attn",
    )(lq1.reshape(1, hd), lk1.reshape(1, hd), lq2.reshape(1, hd), lk2.reshape(1, hd),
      proj, proj, proj, subln_w.reshape(1, vd))


def _pack_bf16_pairs(v):
    n = v.shape[1] // 2
    return _pack_bf16_words(v[:, 0:n], v[:, n:2 * n])


def _pack_bf16_words(lo, hi):
    lo_bits = lax.bitcast_convert_type(lo.astype(BF16).astype(F32), jnp.uint32)
    hi_bits = lax.bitcast_convert_type(hi.astype(BF16).astype(F32), jnp.uint32)
    return (lo_bits >> 16) | (hi_bits & jnp.uint32(0xFFFF0000))


def _unpack_bf16_pairs(w):
    lo = lax.bitcast_convert_type(w << 16, F32)
    hi = lax.bitcast_convert_type(w & jnp.uint32(0xFFFF0000), F32)
    return lo, hi


def _store_row_tiles(ref, row0, words):
    m, w = words.shape
    assert w == SUBLANES * LANES
    for s in range(SUBLANES):
        ref[pl.ds(row0 * SUBLANES + s, m, stride=SUBLANES), :] = words[:, s * LANES:(s + 1) * LANES]


def _load_row_tiles(ref, row0, m):
    return jnp.concatenate([ref[pl.ds(row0 * SUBLANES + s, m, stride=SUBLANES), :] for s in range(SUBLANES)], axis=1)


def _outproj_kernel(x_ref, yc_ref, ya_ref, wo_ref, fw_ref, wr_ref, br_ref, h_ref, lg_ref, ug_ref, *, eps, nsub):
    dc = yc_ref.shape[1]
    sub = x_ref.shape[0] // nsub
    mixes = []
    for t in range(nsub):
        rows = slice(t * sub, (t + 1) * sub)
        mixes.append(jnp.dot(yc_ref[rows, :], wo_ref[0:dc, :], preferred_element_type=F32)
                     + jnp.dot(ya_ref[rows, :], wo_ref[dc:, :], preferred_element_type=F32))
    for t in range(nsub):
        rows = slice(t * sub, (t + 1) * sub)
        h = x_ref[rows, :] + mixes[t]
        h_ref[rows, :] = h
        ms = jnp.mean(h * h, axis=-1, keepdims=True)
        un = h * lax.rsqrt(ms + eps) * fw_ref[...]
        hi = un.astype(BF16)
        lo = (un - hi.astype(F32)).astype(BF16)
        hh = jnp.dot(hi, wr_ref[...], preferred_element_type=F32)
        lh = jnp.dot(lo, wr_ref[:, 0:LANES], preferred_element_type=F32)
        lg_ref[rows, :] = hh[:, 0:LANES] + hh[:, LANES:2 * LANES] + lh + br_ref[...]
        _store_row_tiles(ug_ref, t * sub, _pack_bf16_pairs(un))


def _outproj(x2, y_conv, y_attn, wo_bf, ffn_w, wr_hl, br):
    T, D = x2.shape
    dc = y_conv.shape[1]
    da = y_attn.shape[1]
    tm = min(512, T)
    nsub = 2 if tm % 512 == 0 else 1
    return pl.pallas_call(
        functools.partial(_outproj_kernel, eps=RMS_EPS, nsub=nsub),
        grid=(T // tm,),
        in_specs=[pl.BlockSpec((tm, D), lambda i: (i, 0)),
                  pl.BlockSpec((tm, dc), lambda i: (i, 0)),
                  pl.BlockSpec((tm, da), lambda i: (i, 0)),
                  pl.BlockSpec((dc + da, D), lambda i: (0, 0), pipeline_mode=pl.Buffered(1)),
                  pl.BlockSpec((1, D), lambda i: (0, 0)),
                  pl.BlockSpec((D, 2 * LANES), lambda i: (0, 0)),
                  pl.BlockSpec((1, LANES), lambda i: (0, 0))],
        out_specs=[pl.BlockSpec((tm, D), lambda i: (i, 0)),
                   pl.BlockSpec((tm, LANES), lambda i: (i, 0)),
                   pl.BlockSpec((tm * SUBLANES, LANES), lambda i: (i, 0))],
        out_shape=[jax.ShapeDtypeStruct((T, D), F32), jax.ShapeDtypeStruct((T, LANES), F32),
                   jax.ShapeDtypeStruct((T * SUBLANES, LANES), jnp.uint32)],
        compiler_params=_cparams(("arbitrary",), 56),
        name="outproj",
    )(x2, y_conv, y_attn, wo_bf, ffn_w, wr_hl, br)


def _route_kernel(lg_ref, info_ref, cols_ref, cnt_ref, *, ng, epg):
    i = pl.program_id(0)
    lg = lg_ref[...]
    tm = lg.shape[0]
    lane = lax.broadcasted_iota(jnp.int32, lg.shape, 1)
    lanef = lane.astype(F32)
    ne = ng * epg

    def first_argmax(vals):
        mx = jnp.max(vals, axis=-1, keepdims=True)
        idx = jnp.min(jnp.where(vals == mx, lanef, float(LANES)), axis=-1, keepdims=True)
        return mx, idx

    gmask = lane < ng
    gl = jnp.where(gmask, lg, NEG_BIG)
    gmax, gsel = first_argmax(gl)
    gsum = jnp.sum(jnp.where(gmask, jnp.exp(gl - gmax), 0.0), axis=-1, keepdims=True)
    g_w = 1.0 / gsum
    lo = gsel * epg + ng
    emask = (lanef >= lo) & (lanef < lo + epg)
    el = jnp.where(emask, lg, NEG_BIG)
    v1, i1 = first_argmax(el)
    el2 = jnp.where(lanef == i1, NEG_BIG, el)
    v2, i2 = first_argmax(el2)
    e2 = jnp.exp(v2 - v1)
    p1 = 1.0 / (1.0 + e2)
    gate1 = g_w * p1
    gate2 = g_w * (e2 * p1)

    oh1 = lanef == i1
    oh2 = lanef == i2
    cmat = jnp.where(oh1 | oh2, 1.0, 0.0).astype(BF16)
    r = lax.broadcasted_iota(jnp.int32, (tm, tm), 0)
    c = lax.broadcasted_iota(jnp.int32, (tm, tm), 1)
    tri = jnp.where(c < r, 1.0, 0.0).astype(BF16)

    @pl.when(i == 0)
    def _():
        cnt_ref[...] = jnp.zeros(cnt_ref.shape, F32)

    carry = cnt_ref[0:1, :]
    prefix = jnp.dot(tri, cmat, preferred_element_type=F32) + carry
    rank1 = jnp.sum(jnp.where(oh1, prefix, 0.0), axis=-1, keepdims=True)
    rank2 = jnp.sum(jnp.where(oh2, prefix, 0.0), axis=-1, keepdims=True)
    cnt_ref[...] = jnp.broadcast_to(carry + jnp.sum(cmat.astype(F32), axis=0, keepdims=True), cnt_ref.shape)

    info = jnp.where(lane == 0, i1 - ng,
           jnp.where(lane == 1, i2 - ng,
           jnp.where(lane == 2, gate1,
           jnp.where(lane == 3, gate2,
           jnp.where(lane == 4, rank1,
           jnp.where(lane == 5, rank2, 0.0))))))
    info_ref[...] = info
    cols_ref[...] = info.T[0:SUBLANES, :]


def _route(logits, ng, epg):
    T = logits.shape[0]
    tm = min(512, T)
    return pl.pallas_call(
        functools.partial(_route_kernel, ng=ng, epg=epg),
        grid=(T // tm,),
        in_specs=[pl.BlockSpec((tm, LANES), lambda i: (i, 0))],
        out_specs=[pl.BlockSpec((tm, LANES), lambda i: (i, 0)),
                   pl.BlockSpec((SUBLANES, tm), lambda i: (0, i)),
                   pl.BlockSpec((8, LANES), lambda i: (0, 0))],
        out_shape=[jax.ShapeDtypeStruct((T, LANES), F32), jax.ShapeDtypeStruct((SUBLANES, T), F32),
                   jax.ShapeDtypeStruct((8, LANES), F32)],
        compiler_params=_cparams(("arbitrary",), 32),
        name="route",
    )(logits)


def _dest_kernel(pstart_ref, cols_ref, dest_ref, *, ne):
    eid = cols_ref[0:TOP_K, :].astype(jnp.int32)
    rank = cols_ref[4:4 + TOP_K, :].astype(jnp.int32)

    def body(e, acc):
        return acc + jnp.where(eid == e, pstart_ref[e], 0)

    start = lax.fori_loop(0, ne, body, jnp.zeros(eid.shape, jnp.int32))
    dest_ref[...] = jnp.zeros(dest_ref.shape, jnp.int32)
    dest_ref[0:TOP_K, :] = start + rank


def _dest(pad_starts, cols):
    R, T = cols.shape
    grid_spec = pltpu.PrefetchScalarGridSpec(
        num_scalar_prefetch=1,
        grid=(1,),
        in_specs=[pl.BlockSpec((R, T), lambda i, p: (0, 0))],
        out_specs=pl.BlockSpec((R, T), lambda i, p: (0, 0)),
    )
    return pl.pallas_call(
        functools.partial(_dest_kernel, ne=pad_starts.shape[0]),
        grid_spec=grid_spec,
        out_shape=jax.ShapeDtypeStruct((R, T), jnp.int32),
        compiler_params=_cparams(("arbitrary",), 16),
        name="dest",
    )(pad_starts, cols)


ROW_UNROLL = 8


def _dispatch_kernel(dest_ref, pend_ref, ug_ref, xs_hbm, ring, zbuf, sems, zsem, *, tm, blk, ne, ntok):
    i = pl.program_id(0)
    n = pl.num_programs(0)
    slot = i % 2

    def seg_tail(e):
        end = pend_ref[e]
        start = jnp.where(e == 0, 0, pend_ref[jnp.maximum(e - 1, 0)])
        tail = pl.multiple_of(jnp.maximum(end - blk, 0) * SUBLANES, blk * SUBLANES)
        return pltpu.make_async_copy(zbuf, xs_hbm.at[pl.ds(tail, blk * SUBLANES)], zsem), end > start

    @pl.when(i == 0)
    def _():
        zbuf[...] = jnp.zeros(zbuf.shape, zbuf.dtype)

        def zstart(e, carry):
            cp, nonempty = seg_tail(e)

            @pl.when(nonempty)
            def _():
                cp.start()
            return carry

        def zwait(e, carry):
            cp, nonempty = seg_tail(e)

            @pl.when(nonempty)
            def _():
                cp.wait()
            return carry

        def spare(b):
            row = pl.multiple_of(b * blk, blk)
            tile = pl.multiple_of(row * SUBLANES, blk * SUBLANES)
            return pltpu.make_async_copy(zbuf, xs_hbm.at[pl.ds(tile, blk * SUBLANES)], zsem), row >= pend_ref[ne - 1]

        def sstart(b, carry):
            cp, unused = spare(b)

            @pl.when(unused)
            def _():
                cp.start()
            return carry

        def swait(b, carry):
            cp, unused = spare(b)

            @pl.when(unused)
            def _():
                cp.wait()
            return carry

        nblk = xs_hbm.shape[0] // (blk * SUBLANES)
        lax.fori_loop(0, ne, zstart, 0)
        lax.fori_loop(0, nblk, sstart, 0)
        lax.fori_loop(0, ne, zwait, 0)
        lax.fori_loop(0, nblk, swait, 0)

    def drain(s):
        for _ in range(TOP_K):
            pltpu.make_async_copy(ring.at[s], xs_hbm.at[pl.ds(0, tm * SUBLANES)], sems.at[s]).wait()

    @pl.when(i >= 2)
    def _():
        drain(slot)

    ring[slot] = ug_ref[...]

    def issue(c, carry):
        r0 = pl.multiple_of(c * ROW_UNROLL, ROW_UNROLL)
        for s in range(ROW_UNROLL):
            for k in range(TOP_K):
                d = dest_ref[k * ntok + i * tm + r0 + s]
                src = ring.at[slot, pl.ds(pl.multiple_of((r0 + s) * SUBLANES, SUBLANES), SUBLANES)]
                dst = xs_hbm.at[pl.ds(pl.multiple_of(d * SUBLANES, SUBLANES), SUBLANES)]
                pltpu.make_async_copy(src, dst, sems.at[slot]).start(priority=k % 2)
        return carry

    lax.fori_loop(0, tm // ROW_UNROLL, issue, 0)

    @pl.when(i == n - 1)
    def _():
        drain(slot)

        @pl.when(n >= 2)
        def _():
            drain(1 - slot)


def _dispatch(dest_flat, pad_ends, ug, nrows, tm, blk):
    T = ug.shape[0] // SUBLANES
    ne = pad_ends.shape[0]
    grid_spec = pltpu.PrefetchScalarGridSpec(
        num_scalar_prefetch=2,
        grid=(T // tm,),
        in_specs=[pl.BlockSpec((tm * SUBLANES, LANES), lambda i, d, p: (i, 0))],
        out_specs=pl.BlockSpec(memory_space=pl.ANY),
        scratch_shapes=[pltpu.VMEM((2, tm * SUBLANES, LANES), ug.dtype),
                        pltpu.VMEM((blk * SUBLANES, LANES), ug.dtype),
                        pltpu.SemaphoreType.DMA((2,)),
                        pltpu.SemaphoreType.DMA(())],
    )
    return pl.pallas_call(
        functools.partial(_dispatch_kernel, tm=tm, blk=blk, ne=ne, ntok=T),
        grid_spec=grid_spec,
        out_shape=jax.ShapeDtypeStruct((nrows * SUBLANES, LANES), ug.dtype),
        compiler_params=_cparams(("arbitrary",), 32),
        name="dispatch",
    )(dest_flat, pad_ends, ug)


WEIGHT_DMA_PRIORITY = (0, 1, 1)
WEIGHT_SLOTS = 3


def _experts_kernel(bexp_ref, first_ref, slot_ref, next_ref, lead_ref, nused_ref, x_ref, w1_hbm, w3_hbm, w2_hbm,
                    y_ref, w1b, w3b, w2b, wsems):
    i = pl.program_id(0)
    nused = nused_ref[0]

    def weight_copies(e, s):
        return (pltpu.make_async_copy(w1_hbm.at[e], w1b.at[s], wsems.at[s, 0]),
                pltpu.make_async_copy(w3_hbm.at[e], w3b.at[s], wsems.at[s, 1]),
                pltpu.make_async_copy(w2_hbm.at[e], w2b.at[s], wsems.at[s, 2]))

    def start_weights(e, s):
        for n, cp in enumerate(weight_copies(e, s)):
            cp.start(priority=WEIGHT_DMA_PRIORITY[n])

    @pl.when(i == 0)
    def _():
        for n in range(WEIGHT_SLOTS - 1):
            @pl.when(lead_ref[n] >= 0)
            def _():
                start_weights(lead_ref[n], n)

    @pl.when(i < nused)
    def _():
        s = slot_ref[i]

        @pl.when(first_ref[i] == 1)
        def _():
            @pl.when(next_ref[i] >= 0)
            def _():
                start_weights(next_ref[i], (s + WEIGHT_SLOTS - 1) % WEIGHT_SLOTS)
            for cp in weight_copies(bexp_ref[i], s):
                cp.wait()

        blk = x_ref.shape[0] // SUBLANES
        x_lo, x_hi = _unpack_bf16_pairs(_load_row_tiles(x_ref, 0, blk))
        half = x_lo.shape[1]
        a = (jnp.dot(x_lo, w1b[s, 0:half, :], preferred_element_type=F32)
             + jnp.dot(x_hi, w1b[s, half:2 * half, :], preferred_element_type=F32))
        b = (jnp.dot(x_lo, w3b[s, 0:half, :], preferred_element_type=F32)
             + jnp.dot(x_hi, w3b[s, half:2 * half, :], preferred_element_type=F32))
        hdn = a * jax.nn.sigmoid(a) * b
        cw = 2 * LANES
        for c0 in range(0, half, cw):
            y_lo = jnp.dot(hdn, w2b[s, :, c0:c0 + cw], preferred_element_type=F32)
            y_hi = jnp.dot(hdn, w2b[s, :, half + c0:half + c0 + cw], preferred_element_type=F32)
            words = _pack_bf16_words(y_lo, y_hi)
            for g in range(cw // LANES):
                y_ref[pl.ds(c0 // LANES + g, blk, stride=SUBLANES), :] = words[:, g * LANES:(g + 1) * LANES]

    @pl.when(i >= nused)
    def _():
        y_ref[...] = jnp.zeros(y_ref.shape, y_ref.dtype)


def _experts(bexp, first, slot, nxt, lead, nused, xs, w1, w3, w2, blk):
    P = xs.shape[0] // SUBLANES
    E, D, Fh = w1.shape
    nblk = P // blk
    smap = lambda i, *_: (i, 0)
    xmap = lambda i, be, fi, sl, nx, ld, nu: (jnp.minimum(i, nu[0] - 1), 0)
    grid_spec = pltpu.PrefetchScalarGridSpec(
        num_scalar_prefetch=6,
        grid=(nblk,),
        in_specs=[pl.BlockSpec((blk * SUBLANES, LANES), xmap),
                  pl.BlockSpec(memory_space=pl.ANY),
                  pl.BlockSpec(memory_space=pl.ANY),
                  pl.BlockSpec(memory_space=pl.ANY)],
        out_specs=pl.BlockSpec((blk * SUBLANES, LANES), smap),
        scratch_shapes=[pltpu.VMEM((WEIGHT_SLOTS, D, Fh), w1.dtype),
                        pltpu.VMEM((WEIGHT_SLOTS, D, Fh), w3.dtype),
                        pltpu.VMEM((WEIGHT_SLOTS, Fh, D), w2.dtype),
                        pltpu.SemaphoreType.DMA((WEIGHT_SLOTS, 3))],
    )
    return pl.pallas_call(
        _experts_kernel,
        grid_spec=grid_spec,
        out_shape=jax.ShapeDtypeStruct(xs.shape, xs.dtype),
        compiler_params=_cparams(("arbitrary",), 56),
        name="experts",
    )(bexp, first, slot, nxt, lead, nused, xs, w1, w3, w2)


def _row_gather(src_hbm, idx_ref, base, dst, sem, n):
    def body(c, carry):
        r0 = pl.multiple_of(c * ROW_UNROLL, ROW_UNROLL)
        for s in range(ROW_UNROLL):
            d = idx_ref[base + r0 + s]
            pltpu.make_async_copy(src_hbm.at[pl.ds(pl.multiple_of(d * SUBLANES, SUBLANES), SUBLANES)],
                                  dst.at[pl.ds(pl.multiple_of((r0 + s) * SUBLANES, SUBLANES), SUBLANES)],
                                  sem).start(priority=s % 2)
        return carry
    lax.fori_loop(0, n // ROW_UNROLL, body, 0)


def _row_gather_wait(src_hbm, dst, sem, n):
    pltpu.make_async_copy(src_hbm.at[pl.ds(0, n * SUBLANES)], dst, sem).wait()


def _combine_kernel(dest_ref, h_ref, info_ref, y_hbm, fw_ref, o_ref, ybuf, sems, *, tm, eps, ntok):
    i = pl.program_id(0)
    n = pl.num_programs(0)
    slot = i % 2
    nrow = TOP_K * tm

    def gather_step(step, s):
        for k in range(TOP_K):
            _row_gather(y_hbm, dest_ref, k * ntok + step * tm,
                        ybuf.at[s, pl.ds(k * tm * SUBLANES, tm * SUBLANES)], sems.at[s], tm)

    @pl.when(i == 0)
    def _():
        gather_step(0, 0)

    @pl.when(i + 1 < n)
    def _():
        gather_step(i + 1, 1 - slot)

    _row_gather_wait(y_hbm, ybuf.at[slot], sems.at[slot], nrow)
    half = o_ref.shape[1] // 2
    rc = tm

    def rows(ci, carry):
        r0 = pl.multiple_of(ci * rc, rc)
        rs = pl.ds(r0, rc)
        info = info_ref[rs, :]
        g1 = info[:, 2:3]
        g2 = info[:, 3:4]
        y1_lo, y1_hi = _unpack_bf16_pairs(_load_row_tiles(ybuf.at[slot], r0, rc))
        y2_lo, y2_hi = _unpack_bf16_pairs(_load_row_tiles(ybuf.at[slot], tm + r0, rc))
        h_lo = h_ref[rs, 0:half] + (g1 * y1_lo + g2 * y2_lo)
        h_hi = h_ref[rs, half:2 * half] + (g1 * y1_hi + g2 * y2_hi)
        ms = (jnp.sum(h_lo * h_lo, axis=-1, keepdims=True)
              + jnp.sum(h_hi * h_hi, axis=-1, keepdims=True)) / (2 * half)
        r = lax.rsqrt(ms + eps)
        o_ref[rs, 0:half] = h_lo * r * fw_ref[:, 0:half]
        o_ref[rs, half:2 * half] = h_hi * r * fw_ref[:, half:2 * half]
        return carry

    lax.fori_loop(0, tm // rc, rows, 0)


def _combine(dest, h, info, yb, final_w, tm):
    T, D = h.shape
    grid_spec = pltpu.PrefetchScalarGridSpec(
        num_scalar_prefetch=1,
        grid=(T // tm,),
        in_specs=[pl.BlockSpec((tm, D), lambda i, d: (i, 0)),
                  pl.BlockSpec((tm, LANES), lambda i, d: (i, 0)),
                  pl.BlockSpec(memory_space=pl.ANY),
                  pl.BlockSpec((1, D), lambda i, d: (0, 0))],
        out_specs=pl.BlockSpec((tm, D), lambda i, d: (i, 0)),
        scratch_shapes=[pltpu.VMEM((2, TOP_K * tm * SUBLANES, LANES), yb.dtype),
                        pltpu.SemaphoreType.DMA((2,))],
    )
    return pl.pallas_call(
        functools.partial(_combine_kernel, tm=tm, eps=RMS_EPS, ntok=T),
        grid_spec=grid_spec,
        out_shape=jax.ShapeDtypeStruct((T, D), F32),
        compiler_params=_cparams(("arbitrary",), 48),
        name="combine",
    )(dest, h, info, yb, final_w)


def _layer(h_in, l, B, S, mix_norm_w, w_in, conv_dw_w, conv_dw_b, conv_ln_w, conv_ln_b,
           lam_q1, lam_k1, lam_q2, lam_k2, attn_subln_w, w_out, ffn_norm_w,
           w_group, b_group, w_expert_gate, b_expert_gate, w1, w3, w2):
    T, D = h_in.shape
    d_conv = conv_dw_w.shape[1]
    d_attn = (w_in.shape[1] - 2 * d_conv) // 3
    ng = w_group.shape[1]
    ne = w_expert_gate.shape[1]
    epg = ne // ng
    assert ng + ne <= LANES
    lam_init = 0.8 - 0.6 * math.exp(-0.3 * l)

    proj = _norm_inproj(h_in, mix_norm_w.reshape(1, D), w_in.astype(BF16))
    y_conv = _conformer(proj, conv_dw_w, conv_dw_b, conv_ln_w, conv_ln_b, B, S)
    y_attn = _diff_attn(proj, lam_q1, lam_k1, lam_q2, lam_k2, attn_subln_w, B, S, d_conv, d_attn, lam_init)

    wr = jnp.concatenate([w_group, w_expert_gate, jnp.zeros((D, LANES - ng - ne), F32)], axis=1)
    wr_hi = wr.astype(BF16)
    wr_lo = (wr - wr_hi.astype(F32)).astype(BF16)
    br = jnp.concatenate([b_group, b_expert_gate.reshape(-1), jnp.zeros((LANES - ng - ne,), F32)]).reshape(1, LANES)
    wr_hl = jnp.concatenate([wr_hi, wr_lo], axis=1)
    h, logits, ug = _outproj(h_in, y_conv, y_attn, w_out.astype(BF16), ffn_norm_w.reshape(1, D), wr_hl, br)

    info, cols, cnt = _route(logits, ng, epg)

    blk = 256
    tmd = min(256, T)
    A = T * TOP_K
    nblk = (A + ne * (blk - 1) + blk - 1) // blk
    i32 = jnp.int32
    counts = cnt[0, ng:ng + ne].astype(i32)
    padded = (counts + blk - 1) // blk * blk
    pad_ends = jnp.cumsum(padded).astype(i32)
    pad_starts = pad_ends - padded
    dest = _dest(pad_starts, cols)[0:TOP_K].reshape(-1)
    nused = (pad_ends[-1] // blk).astype(i32)
    bpos = jnp.arange(nblk, dtype=i32)
    brow = jnp.minimum(bpos, nused - 1) * blk
    bexp = jnp.minimum(jnp.sum((pad_ends[None, :] <= brow[:, None]).astype(i32), axis=1), ne - 1)
    first = ((bpos < nused) & ((bpos == 0) | (bexp != jnp.roll(bexp, 1)))).astype(i32)
    slot = ((jnp.cumsum(first) - 1) % WEIGHT_SLOTS).astype(i32)
    used_idx = jnp.where(padded > 0, jnp.arange(ne, dtype=i32), ne)
    suffix_min = lax.cummin(used_idx, reverse=True)
    next_used = jnp.concatenate([suffix_min[1:], jnp.full((2,), ne, i32)])
    ahead = jnp.arange(ne, dtype=i32)
    for _ in range(WEIGHT_SLOTS - 1):
        ahead = next_used[ahead]
    ahead = jnp.where(ahead >= ne, -1, ahead)
    nxt = ahead[bexp].astype(i32)
    lead = [suffix_min[0]]
    for _ in range(WEIGHT_SLOTS - 2):
        lead.append(next_used[lead[-1]])
    lead = jnp.stack([jnp.where(e >= ne, -1, e) for e in lead]).astype(i32)

    xs = _dispatch(dest, pad_ends, ug, nblk * blk, tmd, blk)
    yb = _experts(bexp, first, slot, nxt, lead, nused.reshape(1), xs, w1, w3, w2, blk)
    return h, info, dest, yb


def kernel(x, mix_norm_w, w_in, conv_dw_w, conv_dw_b, conv_ln_w, conv_ln_b, lam_q1, lam_k1, lam_q2, lam_k2,
           attn_subln_w, w_out, ffn_norm_w, w_group, b_group, w_expert_gate, b_expert_gate, w1, w3, w2,
           final_norm_w):
    B, S, D = x.shape
    depth = w_in.shape[0]
    assert depth == 1
    T = B * S
    tmc = min(256, T)
    h = x.reshape(T, D)
    for l in range(depth):
        h, info, dest, yb = _layer(
            h, l, B, S, mix_norm_w[l], w_in[l], conv_dw_w[l], conv_dw_b[l], conv_ln_w[l], conv_ln_b[l],
            lam_q1[l], lam_k1[l], lam_q2[l], lam_k2[l], attn_subln_w[l], w_out[l], ffn_norm_w[l],
            w_group[l], b_group[l], w_expert_gate[l], b_expert_gate[l], w1[l], w3[l], w2[l])
        h = _combine(dest, h, info, yb, final_norm_w.reshape(1, D), tmc)
    return h.reshape(B, S, D)
```

```python
import functools
import math

import jax
import jax.numpy as jnp
from jax import lax
from jax.experimental import pallas as pl
from jax.experimental.pallas import tpu as pltpu

F32 = jnp.float32
BF16 = jnp.bfloat16

RMS_EPS = 1e-6
SUBLN_EPS = 1e-5
LN_EPS = 1e-5
TOP_K = 2
LANES = 128
SUBLANES = 8
NEG_BIG = -1e30
MIB = 1024 * 1024


def _cparams(sem, vmem_mib):
    return pltpu.CompilerParams(dimension_semantics=sem, vmem_limit_bytes=vmem_mib * MIB)


def _norm_inproj_kernel(x_ref, nw_ref, w_ref, o_ref, u_ref, *, eps, rc):
    @pl.when(pl.program_id(1) == 0)
    def _():
        for r0 in range(0, x_ref.shape[0], rc):
            rows = slice(r0, r0 + rc)
            x = x_ref[rows, :]
            ms = jnp.mean(x * x, axis=-1, keepdims=True)
            u = (x * lax.rsqrt(ms + eps) * nw_ref[...]).astype(u_ref.dtype)
            u_ref[rows, :] = u
            o_ref[rows, :] = jnp.dot(u, w_ref[...], preferred_element_type=F32).astype(o_ref.dtype)

    @pl.when(pl.program_id(1) > 0)
    def _():
        o_ref[...] = jnp.dot(u_ref[...], w_ref[...], preferred_element_type=F32).astype(o_ref.dtype)


def _norm_inproj(x2, nw, w_bf):
    T, D = x2.shape
    N = w_bf.shape[1]
    tm = min(1024, T)
    tn = 1024
    return pl.pallas_call(
        functools.partial(_norm_inproj_kernel, eps=RMS_EPS, rc=256),
        grid=(T // tm, N // tn),
        in_specs=[pl.BlockSpec((tm, D), lambda i, j: (i, 0)),
                  pl.BlockSpec((1, D), lambda i, j: (0, 0)),
                  pl.BlockSpec((D, tn), lambda i, j: (0, j))],
        out_specs=pl.BlockSpec((tm, tn), lambda i, j: (i, j)),
        out_shape=jax.ShapeDtypeStruct((T, N), BF16),
        scratch_shapes=[pltpu.VMEM((tm, D), BF16)],
        compiler_params=_cparams(("arbitrary", "arbitrary"), 48),
        name="norm_inproj",
    )(x2, nw, w_bf)


def _conv_kernel(a_ref, g_ref, w_ref, b_ref, lw_ref, lb_ref, o_ref, ubuf, cbuf, sh, *, ts, kw, halo, eps):
    s = pl.program_id(1)
    C = a_ref.shape[1]

    @pl.when(s == 0)
    def _():
        ubuf[0:halo, :] = jnp.zeros((halo, C), F32)

    @pl.when(s > 0)
    def _():
        ubuf[0:halo, :] = ubuf[ts:ts + halo, :]

    rg = 64
    for r0 in range(0, ts, rg):
        a = a_ref[r0:r0 + rg, :].astype(F32)
        g = g_ref[r0:r0 + rg, :].astype(F32)
        ubuf[halo + r0:halo + r0 + rg, :] = a * jax.nn.sigmoid(g)

    nsh = sh.shape[1]
    for r in range(1, SUBLANES):
        for i0 in range(0, nsh, rg):
            n = min(rg, nsh - i0)
            sh[r - 1, i0:i0 + n, :] = ubuf[i0 + r:i0 + r + n, :]

    off = halo - (kw - 1)
    rc = 128

    def conv_ln_rows(ri, carry):
        r0 = pl.multiple_of(ri * rc, rc)
        for c in range(C // LANES):
            cs = slice(c * LANES, (c + 1) * LANES)
            acc = jnp.broadcast_to(b_ref[0:1, cs], (rc // SUBLANES, SUBLANES, LANES))
            for k in range(kw):
                q, r = divmod(off + k, SUBLANES)
                rows = pl.ds(r0 + q * SUBLANES, rc)
                tap = ubuf[rows, cs] if r == 0 else sh[r - 1, rows, cs]
                acc = acc + w_ref[k, :, cs] * tap.reshape(rc // SUBLANES, SUBLANES, LANES)
            cbuf[pl.ds(r0, rc), cs] = acc.reshape(rc, LANES)
        cv = cbuf[pl.ds(r0, rc), :]
        mu = jnp.mean(cv, axis=-1, keepdims=True)
        d = cv - mu
        var = jnp.mean(d * d, axis=-1, keepdims=True)
        un = d * lax.rsqrt(var + eps) * lw_ref[...] + lb_ref[...]
        o_ref[pl.ds(r0, rc), :] = (un * jax.nn.sigmoid(un)).astype(o_ref.dtype)
        return carry

    lax.fori_loop(0, ts // rc, conv_ln_rows, 0)


def _conformer(proj, dw_w, dw_b, ln_w, ln_b, B, S):
    T = proj.shape[0]
    kw, C = dw_w.shape
    ts = min(256, S)
    halo = 32
    assert kw - 1 <= halo and S % ts == 0 and ts >= halo
    ns = S // ts
    wp = jnp.broadcast_to(dw_w[:, None, :], (kw, SUBLANES, C))
    return pl.pallas_call(
        functools.partial(_conv_kernel, ts=ts, kw=kw, halo=halo, eps=LN_EPS),
        grid=(B, ns),
        in_specs=[pl.BlockSpec((ts, C), lambda b, s: (b * ns + s, 0)),
                  pl.BlockSpec((ts, C), lambda b, s: (b * ns + s, 1)),
                  pl.BlockSpec((kw, SUBLANES, C), lambda b, s: (0, 0, 0)),
                  pl.BlockSpec((1, C), lambda b, s: (0, 0)),
                  pl.BlockSpec((1, C), lambda b, s: (0, 0)),
                  pl.BlockSpec((1, C), lambda b, s: (0, 0))],
        out_specs=pl.BlockSpec((ts, C), lambda b, s: (b * ns + s, 0)),
        out_shape=jax.ShapeDtypeStruct((T, C), BF16),
        scratch_shapes=[pltpu.VMEM((ts + halo, C), F32), pltpu.VMEM((ts, C), F32),
                        pltpu.VMEM((SUBLANES - 1, ts + halo - SUBLANES, C), F32)],
        compiler_params=_cparams(("arbitrary", "arbitrary"), 32),
        name="conformer",
    )(proj, proj, wp, dw_b.reshape(1, C), ln_w.reshape(1, C), ln_b.reshape(1, C))


def _attn_kernel(lq1_ref, lk1_ref, lq2_ref, lk2_ref, q_ref, k_ref, v_ref, sw_ref, o_ref,
                 vxt_ref, qq_ref, *scr, tq, hd, hp, ones_rows, lam_init, eps):
    i = pl.program_id(2)
    vd = 2 * hd
    nkb = v_ref.shape[0] // tq
    acc_refs, m_refs, sa_refs, sb_refs = (scr[n * hp:(n + 1) * hp] for n in range(4))

    @pl.when(i == 0)
    def _():
        for u in range(hp):
            for jb in range(nkb):
                vblk = v_ref[jb * tq:(jb + 1) * tq, u * vd:(u + 1) * vd].astype(F32)
                vxt_ref[u, jb, 0:vd, :] = vblk.T.astype(vxt_ref.dtype)
                vxt_ref[u, jb, vd:vd + ones_rows, :] = jnp.ones((ones_rows, tq), vxt_ref.dtype)

    lam = (jnp.exp(jnp.sum(lq1_ref[...] * lk1_ref[...], axis=-1, keepdims=True))
           - jnp.exp(jnp.sum(lq2_ref[...] * lk2_ref[...], axis=-1, keepdims=True)) + lam_init)

    for u in range(hp):
        q = q_ref[:, u * vd:(u + 1) * vd]
        qs = q * jnp.asarray(hd ** -0.5, q.dtype)
        lane = lax.broadcasted_iota(jnp.int32, q.shape, 1)
        zero = jnp.zeros_like(qs)
        qq_ref[u, 0:tq, :] = jnp.where(lane < hd, qs, zero)
        qq_ref[u, tq:2 * tq, :] = jnp.where(lane >= hd, qs, zero)
        acc_refs[u][...] = jnp.zeros(acc_refs[u].shape, F32)
        m_refs[u][...] = jnp.full(m_refs[u].shape, NEG_BIG, F32)

    def scores(j, u, dst):
        rows = pl.ds(pl.multiple_of(j * tq, tq), tq)
        kb = k_ref[rows, u * vd:(u + 1) * vd]
        dst[...] = lax.dot_general(kb, qq_ref[u], (((1,), (1,)), ((), ())), preferred_element_type=F32)

    def softmax_pv(j, u, src, masked):
        st = src[...]
        if masked:
            r = lax.broadcasted_iota(jnp.int32, st.shape, 0)
            c = lax.broadcasted_iota(jnp.int32, st.shape, 1)
            st = jnp.where(r <= jnp.where(c >= tq, c - tq, c), st, NEG_BIG)
        m_old = m_refs[u][...]
        m_new = jnp.maximum(m_old, jnp.max(st, axis=0, keepdims=True))
        alpha = jnp.exp(m_old - m_new)
        pt = jnp.exp(st - m_new).astype(vxt_ref.dtype)
        pv = jnp.dot(vxt_ref[u, j], pt, preferred_element_type=F32)
        acc_refs[u][...] = alpha * acc_refs[u][...] + pv
        m_refs[u][...] = m_new

    def half(j, cur, nxt):
        for u in range(hp):
            scores(j + 1, u, nxt[u])
            softmax_pv(j, u, cur[u], False)

    for u in range(hp):
        scores(0, u, sa_refs[u])

    def pair(t, carry):
        half(2 * t, sa_refs, sb_refs)
        half(2 * t + 1, sb_refs, sa_refs)
        return carry

    lax.fori_loop(0, i // 2, pair, 0)

    @pl.when(i % 2 == 1)
    def _():
        half(i - 1, sa_refs, sb_refs)
        for u in range(hp):
            softmax_pv(i, u, sb_refs[u], True)

    @pl.when(i % 2 == 0)
    def _():
        for u in range(hp):
            softmax_pv(i, u, sa_refs[u], True)

    for u in range(hp):
        acc = acc_refs[u][...]
        o12 = acc[0:vd] * (1.0 / acc[vd:vd + 1])
        ot = o12[:, 0:tq] - lam * o12[:, tq:2 * tq]
        msq = jnp.mean(ot * ot, axis=0, keepdims=True)
        o = (ot * lax.rsqrt(msq + eps)).T
        o_ref[:, u * vd:(u + 1) * vd] = (o * sw_ref[...] * (1.0 - lam_init)).astype(o_ref.dtype)


def _diff_attn(proj, lq1, lk1, lq2, lk2, subln_w, B, S, d_conv, d_attn, lam_init):
    T = proj.shape[0]
    vd = subln_w.shape[0]
    hd = lq1.shape[0]
    assert vd == LANES and 2 * hd == vd
    H = d_attn // vd
    hp = 8
    ones_rows = 16
    tq = min(256, S)
    nq = S // tq
    assert H % hp == 0
    qc = 2 * d_conv // (hp * vd)
    kc = qc + H // hp
    vc = kc + H // hp
    lspec = pl.BlockSpec((1, hd), lambda b, h, i: (0, 0))
    return pl.pallas_call(
        functools.partial(_attn_kernel, tq=tq, hd=hd, hp=hp, ones_rows=ones_rows, lam_init=lam_init, eps=SUBLN_EPS),
        grid=(B, H // hp, nq),
        in_specs=[lspec, lspec, lspec, lspec,
                  pl.BlockSpec((tq, hp * vd), lambda b, h, i: (b * nq + i, qc + h)),
                  pl.BlockSpec((S, hp * vd), lambda b, h, i: (b, kc + h)),
                  pl.BlockSpec((S, hp * vd), lambda b, h, i: (b, vc + h)),
                  pl.BlockSpec((1, vd), lambda b, h, i: (0, 0))],
        out_specs=pl.BlockSpec((tq, hp * vd), lambda b, h, i: (b * nq + i, h)),
        out_shape=jax.ShapeDtypeStruct((T, d_attn), BF16),
        scratch_shapes=[pltpu.VMEM((hp, S // tq, vd + ones_rows, tq), BF16),
                        pltpu.VMEM((hp, 2 * tq, vd), BF16)]
                       + [pltpu.VMEM((vd + ones_rows, 2 * tq), F32) for _ in range(hp)]
                       + [pltpu.VMEM((1, 2 * tq), F32) for _ in range(hp)]
                       + [pltpu.VMEM((tq, 2 * tq), F32) for _ in range(2 * hp)],
        compiler_params=_cparams(("arbitrary", "arbitrary", "arbitrary"), 52),
        name="diff_attn",
    )(lq1.reshape(1, hd), lk1.reshape(1, hd), lq2.reshape(1, hd), lk2.reshape(1, hd),
      proj, proj, proj, subln_w.reshape(1, vd))


def _pack_bf16_pairs(v):
    n = v.shape[1] // 2
    return _pack_bf16_words(v[:, 0:n], v[:, n:2 * n])


def _pack_bf16_words(lo, hi):
    lo_bits = lax.bitcast_convert_type(lo.astype(BF16).astype(F32), jnp.uint32)
    hi_bits = lax.bitcast_convert_type(hi.astype(BF16).astype(F32), jnp.uint32)
    return (lo_bits >> 16) | (hi_bits & jnp.uint32(0xFFFF0000))


def _unpack_bf16_pairs(w):
    lo = lax.bitcast_convert_type(w << 16, F32)
    hi = lax.bitcast_convert_type(w & jnp.uint32(0xFFFF0000), F32)
    return lo, hi


def _store_row_tiles(ref, row0, words):
    m, w = words.shape
    assert w == SUBLANES * LANES
    for s in range(SUBLANES):
        ref[pl.ds(row0 * SUBLANES + s, m, stride=SUBLANES), :] = words[:, s * LANES:(s + 1) * LANES]


def _load_row_tiles(ref, row0, m):
    return jnp.concatenate([ref[pl.ds(row0 * SUBLANES + s, m, stride=SUBLANES), :] for s in range(SUBLANES)], axis=1)


def _outproj_kernel(x_ref, yc_ref, ya_ref, wo_ref, fw_ref, wr_ref, br_ref, h_ref, lg_ref, ug_ref, *, eps, nsub):
    dc = yc_ref.shape[1]
    sub = x_ref.shape[0] // nsub
    mixes = []
    for t in range(nsub):
        rows = slice(t * sub, (t + 1) * sub)
        mixes.append(jnp.dot(yc_ref[rows, :], wo_ref[0:dc, :], preferred_element_type=F32)
                     + jnp.dot(ya_ref[rows, :], wo_ref[dc:, :], preferred_element_type=F32))
    for t in range(nsub):
        rows = slice(t * sub, (t + 1) * sub)
        h = x_ref[rows, :] + mixes[t]
        h_ref[rows, :] = h
        ms = jnp.mean(h * h, axis=-1, keepdims=True)
        un = h * lax.rsqrt(ms + eps) * fw_ref[...]
        hi = un.astype(BF16)
        lo = (un - hi.astype(F32)).astype(BF16)
        hh = jnp.dot(hi, wr_ref[...], preferred_element_type=F32)
        lh = jnp.dot(lo, wr_ref[:, 0:LANES], preferred_element_type=F32)
        lg_ref[rows, :] = hh[:, 0:LANES] + hh[:, LANES:2 * LANES] + lh + br_ref[...]
        _store_row_tiles(ug_ref, t * sub, _pack_bf16_pairs(un))


def _outproj(x2, y_conv, y_attn, wo_bf, ffn_w, wr_hl, br):
    T, D = x2.shape
    dc = y_conv.shape[1]
    da = y_attn.shape[1]
    tm = min(512, T)
    nsub = 2 if tm % 512 == 0 else 1
    return pl.pallas_call(
        functools.partial(_outproj_kernel, eps=RMS_EPS, nsub=nsub),
        grid=(T // tm,),
        in_specs=[pl.BlockSpec((tm, D), lambda i: (i, 0)),
                  pl.BlockSpec((tm, dc), lambda i: (i, 0)),
                  pl.BlockSpec((tm, da), lambda i: (i, 0)),
                  pl.BlockSpec((dc + da, D), lambda i: (0, 0), pipeline_mode=pl.Buffered(1)),
                  pl.BlockSpec((1, D), lambda i: (0, 0)),
                  pl.BlockSpec((D, 2 * LANES), lambda i: (0, 0)),
                  pl.BlockSpec((1, LANES), lambda i: (0, 0))],
        out_specs=[pl.BlockSpec((tm, D), lambda i: (i, 0)),
                   pl.BlockSpec((tm, LANES), lambda i: (i, 0)),
                   pl.BlockSpec((tm * SUBLANES, LANES), lambda i: (i, 0))],
        out_shape=[jax.ShapeDtypeStruct((T, D), F32), jax.ShapeDtypeStruct((T, LANES), F32),
                   jax.ShapeDtypeStruct((T * SUBLANES, LANES), jnp.uint32)],
        compiler_params=_cparams(("arbitrary",), 56),
        name="outproj",
    )(x2, y_conv, y_attn, wo_bf, ffn_w, wr_hl, br)


def _route_kernel(lg_ref, info_ref, cols_ref, cnt_ref, *, ng, epg):
    i = pl.program_id(0)
    lg = lg_ref[...]
    tm = lg.shape[0]
    lane = lax.broadcasted_iota(jnp.int32, lg.shape, 1)
    lanef = lane.astype(F32)
    ne = ng * epg

    def first_argmax(vals):
        mx = jnp.max(vals, axis=-1, keepdims=True)
        idx = jnp.min(jnp.where(vals == mx, lanef, float(LANES)), axis=-1, keepdims=True)
        return mx, idx

    gmask = lane < ng
    gl = jnp.where(gmask, lg, NEG_BIG)
    gmax, gsel = first_argmax(gl)
    gsum = jnp.sum(jnp.where(gmask, jnp.exp(gl - gmax), 0.0), axis=-1, keepdims=True)
    g_w = 1.0 / gsum
    lo = gsel * epg + ng
    emask = (lanef >= lo) & (lanef < lo + epg)
    el = jnp.where(emask, lg, NEG_BIG)
    v1, i1 = first_argmax(el)
    el2 = jnp.where(lanef == i1, NEG_BIG, el)
    v2, i2 = first_argmax(el2)
    e2 = jnp.exp(v2 - v1)
    p1 = 1.0 / (1.0 + e2)
    gate1 = g_w * p1
    gate2 = g_w * (e2 * p1)

    oh1 = lanef == i1
    oh2 = lanef == i2
    cmat = jnp.where(oh1 | oh2, 1.0, 0.0).astype(BF16)
    r = lax.broadcasted_iota(jnp.int32, (tm, tm), 0)
    c = lax.broadcasted_iota(jnp.int32, (tm, tm), 1)
    tri = jnp.where(c < r, 1.0, 0.0).astype(BF16)

    @pl.when(i == 0)
    def _():
        cnt_ref[...] = jnp.zeros(cnt_ref.shape, F32)

    carry = cnt_ref[0:1, :]
    prefix = jnp.dot(tri, cmat, preferred_element_type=F32) + carry
    rank1 = jnp.sum(jnp.where(oh1, prefix, 0.0), axis=-1, keepdims=True)
    rank2 = jnp.sum(jnp.where(oh2, prefix, 0.0), axis=-1, keepdims=True)
    cnt_ref[...] = jnp.broadcast_to(carry + jnp.sum(cmat.astype(F32), axis=0, keepdims=True), cnt_ref.shape)

    info = jnp.where(lane == 0, i1 - ng,
           jnp.where(lane == 1, i2 - ng,
           jnp.where(lane == 2, gate1,
           jnp.where(lane == 3, gate2,
           jnp.where(lane == 4, rank1,
           jnp.where(lane == 5, rank2, 0.0))))))
    info_ref[...] = info
    cols_ref[...] = info.T[0:SUBLANES, :]


def _route(logits, ng, epg):
    T = logits.shape[0]
    tm = min(512, T)
    return pl.pallas_call(
        functools.partial(_route_kernel, ng=ng, epg=epg),
        grid=(T // tm,),
        in_specs=[pl.BlockSpec((tm, LANES), lambda i: (i, 0))],
        out_specs=[pl.BlockSpec((tm, LANES), lambda i: (i, 0)),
                   pl.BlockSpec((SUBLANES, tm), lambda i: (0, i)),
                   pl.BlockSpec((8, LANES), lambda i: (0, 0))],
        out_shape=[jax.ShapeDtypeStruct((T, LANES), F32), jax.ShapeDtypeStruct((SUBLANES, T), F32),
                   jax.ShapeDtypeStruct((8, LANES), F32)],
        compiler_params=_cparams(("arbitrary",), 32),
        name="route",
    )(logits)


def _dest_kernel(pstart_ref, cols_ref, dest_ref, *, ne):
    eid = cols_ref[0:TOP_K, :].astype(jnp.int32)
    rank = cols_ref[4:4 + TOP_K, :].astype(jnp.int32)

    def body(e, acc):
        return acc + jnp.where(eid == e, pstart_ref[e], 0)

    start = lax.fori_loop(0, ne, body, jnp.zeros(eid.shape, jnp.int32))
    dest_ref[...] = jnp.zeros(dest_ref.shape, jnp.int32)
    dest_ref[0:TOP_K, :] = start + rank


def _dest(pad_starts, cols):
    R, T = cols.shape
    grid_spec = pltpu.PrefetchScalarGridSpec(
        num_scalar_prefetch=1,
        grid=(1,),
        in_specs=[pl.BlockSpec((R, T), lambda i, p: (0, 0))],
        out_specs=pl.BlockSpec((R, T), lambda i, p: (0, 0)),
    )
    return pl.pallas_call(
        functools.partial(_dest_kernel, ne=pad_starts.shape[0]),
        grid_spec=grid_spec,
        out_shape=jax.ShapeDtypeStruct((R, T), jnp.int32),
        compiler_params=_cparams(("arbitrary",), 16),
        name="dest",
    )(pad_starts, cols)


ROW_UNROLL = 8


def _dispatch_kernel(dest_ref, pend_ref, ug_ref, xs_hbm, ring, zbuf, sems, zsem, *, tm, blk, ne, ntok):
    i = pl.program_id(0)
    n = pl.num_programs(0)
    slot = i % 2

    def seg_tail(e):
        end = pend_ref[e]
        start = jnp.where(e == 0, 0, pend_ref[jnp.maximum(e - 1, 0)])
        tail = pl.multiple_of(jnp.maximum(end - blk, 0) * SUBLANES, blk * SUBLANES)
        return pltpu.make_async_copy(zbuf, xs_hbm.at[pl.ds(tail, blk * SUBLANES)], zsem), end > start

    @pl.when(i == 0)
    def _():
        zbuf[...] = jnp.zeros(zbuf.shape, zbuf.dtype)

        def zstart(p, carry):
            for q in range(2):
                cp, nonempty = seg_tail(2 * p + q)

                @pl.when(nonempty)
                def _():
                    cp.start(priority=q)
            return carry

        def zwait(e, carry):
            cp, nonempty = seg_tail(e)

            @pl.when(nonempty)
            def _():
                cp.wait()
            return carry

        def spare(b):
            row = pl.multiple_of(b * blk, blk)
            tile = pl.multiple_of(row * SUBLANES, blk * SUBLANES)
            return pltpu.make_async_copy(zbuf, xs_hbm.at[pl.ds(tile, blk * SUBLANES)], zsem), row >= pend_ref[ne - 1]

        def sstart(p, carry):
            for q in range(2):
                cp, unused = spare(2 * p + q)

                @pl.when(unused)
                def _():
                    cp.start(priority=q)
            return carry

        def swait(b, carry):
            cp, unused = spare(b)

            @pl.when(unused)
            def _():
                cp.wait()
            return carry

        nblk = xs_hbm.shape[0] // (blk * SUBLANES)
        assert ne % 2 == 0 and nblk % 2 == 0
        lax.fori_loop(0, ne // 2, zstart, 0)
        lax.fori_loop(0, nblk // 2, sstart, 0)
        lax.fori_loop(0, ne, zwait, 0)
        lax.fori_loop(0, nblk, swait, 0)

    def drain(s):
        for _ in range(TOP_K):
            pltpu.make_async_copy(ring.at[s], xs_hbm.at[pl.ds(0, tm * SUBLANES)], sems.at[s]).wait()

    @pl.when(i >= 2)
    def _():
        drain(slot)

    ring[slot] = ug_ref[...]

    def issue(c, carry):
        r0 = pl.multiple_of(c * ROW_UNROLL, ROW_UNROLL)
        for s in range(ROW_UNROLL):
            for k in range(TOP_K):
                d = dest_ref[k * ntok + i * tm + r0 + s]
                src = ring.at[slot, pl.ds(pl.multiple_of((r0 + s) * SUBLANES, SUBLANES), SUBLANES)]
                dst = xs_hbm.at[pl.ds(pl.multiple_of(d * SUBLANES, SUBLANES), SUBLANES)]
                pltpu.make_async_copy(src, dst, sems.at[slot]).start(priority=k % 2)
        return carry

    lax.fori_loop(0, tm // ROW_UNROLL, issue, 0)

    @pl.when(i == n - 1)
    def _():
        drain(slot)

        @pl.when(n >= 2)
        def _():
            drain(1 - slot)


def _dispatch(dest_flat, pad_ends, ug, nrows, tm, blk):
    T = ug.shape[0] // SUBLANES
    ne = pad_ends.shape[0]
    grid_spec = pltpu.PrefetchScalarGridSpec(
        num_scalar_prefetch=2,
        grid=(T // tm,),
        in_specs=[pl.BlockSpec((tm * SUBLANES, LANES), lambda i, d, p: (i, 0))],
        out_specs=pl.BlockSpec(memory_space=pl.ANY),
        scratch_shapes=[pltpu.VMEM((2, tm * SUBLANES, LANES), ug.dtype),
                        pltpu.VMEM((blk * SUBLANES, LANES), ug.dtype),
                        pltpu.SemaphoreType.DMA((2,)),
                        pltpu.SemaphoreType.DMA(())],
    )
    return pl.pallas_call(
        functools.partial(_dispatch_kernel, tm=tm, blk=blk, ne=ne, ntok=T),
        grid_spec=grid_spec,
        out_shape=jax.ShapeDtypeStruct((nrows * SUBLANES, LANES), ug.dtype),
        compiler_params=_cparams(("arbitrary",), 32),
        name="dispatch",
    )(dest_flat, pad_ends, ug)


WEIGHT_DMA_PRIORITY = (0, 1, 1)
WEIGHT_SLOTS = 3


def _experts_kernel(bexp_ref, first_ref, slot_ref, next_ref, lead_ref, nused_ref, x_ref, w1_hbm, w3_hbm, w2_hbm,
                    y_ref, w1b, w3b, w2b, wsems):
    i = pl.program_id(0)
    nused = nused_ref[0]

    def weight_copies(e, s):
        return (pltpu.make_async_copy(w1_hbm.at[e], w1b.at[s], wsems.at[s, 0]),
                pltpu.make_async_copy(w3_hbm.at[e], w3b.at[s], wsems.at[s, 1]),
                pltpu.make_async_copy(w2_hbm.at[e], w2b.at[s], wsems.at[s, 2]))

    def start_weights(e, s):
        for n, cp in enumerate(weight_copies(e, s)):
            cp.start(priority=WEIGHT_DMA_PRIORITY[n])

    @pl.when(i == 0)
    def _():
        for n in range(WEIGHT_SLOTS - 1):
            @pl.when(lead_ref[n] >= 0)
            def _():
                start_weights(lead_ref[n], n)

    @pl.when(i < nused)
    def _():
        s = slot_ref[i]

        @pl.when(first_ref[i] == 1)
        def _():
            @pl.when(next_ref[i] >= 0)
            def _():
                start_weights(next_ref[i], (s + WEIGHT_SLOTS - 1) % WEIGHT_SLOTS)
            for cp in weight_copies(bexp_ref[i], s):
                cp.wait()

        blk = x_ref.shape[0] // SUBLANES
        x_lo, x_hi = _unpack_bf16_pairs(_load_row_tiles(x_ref, 0, blk))
        half = x_lo.shape[1]
        a = (jnp.dot(x_lo, w1b[s, 0:half, :], preferred_element_type=F32)
             + jnp.dot(x_hi, w1b[s, half:2 * half, :], preferred_element_type=F32))
        b = (jnp.dot(x_lo, w3b[s, 0:half, :], preferred_element_type=F32)
             + jnp.dot(x_hi, w3b[s, half:2 * half, :], preferred_element_type=F32))
        hdn = a * jax.nn.sigmoid(a) * b
        cw = 2 * LANES
        for c0 in range(0, half, cw):
            y_lo = jnp.dot(hdn, w2b[s, :, c0:c0 + cw], preferred_element_type=F32)
            y_hi = jnp.dot(hdn, w2b[s, :, half + c0:half + c0 + cw], preferred_element_type=F32)
            words = _pack_bf16_words(y_lo, y_hi)
            for g in range(cw // LANES):
                y_ref[pl.ds(c0 // LANES + g, blk, stride=SUBLANES), :] = words[:, g * LANES:(g + 1) * LANES]

    @pl.when(i >= nused)
    def _():
        y_ref[...] = jnp.zeros(y_ref.shape, y_ref.dtype)


def _experts(bexp, first, slot, nxt, lead, nused, xs, w1, w3, w2, blk):
    P = xs.shape[0] // SUBLANES
    E, D, Fh = w1.shape
    nblk = P // blk
    smap = lambda i, *_: (i, 0)
    xmap = lambda i, be, fi, sl, nx, ld, nu: (jnp.minimum(i, nu[0] - 1), 0)
    grid_spec = pltpu.PrefetchScalarGridSpec(
        num_scalar_prefetch=6,
        grid=(nblk,),
        in_specs=[pl.BlockSpec((blk * SUBLANES, LANES), xmap),
                  pl.BlockSpec(memory_space=pl.ANY),
                  pl.BlockSpec(memory_space=pl.ANY),
                  pl.BlockSpec(memory_space=pl.ANY)],
        out_specs=pl.BlockSpec((blk * SUBLANES, LANES), smap),
        scratch_shapes=[pltpu.VMEM((WEIGHT_SLOTS, D, Fh), w1.dtype),
                        pltpu.VMEM((WEIGHT_SLOTS, D, Fh), w3.dtype),
                        pltpu.VMEM((WEIGHT_SLOTS, Fh, D), w2.dtype),
                        pltpu.SemaphoreType.DMA((WEIGHT_SLOTS, 3))],
    )
    return pl.pallas_call(
        _experts_kernel,
        grid_spec=grid_spec,
        out_shape=jax.ShapeDtypeStruct(xs.shape, xs.dtype),
        compiler_params=_cparams(("arbitrary",), 56),
        name="experts",
    )(bexp, first, slot, nxt, lead, nused, xs, w1, w3, w2)


def _row_gather(src_hbm, idx_ref, base, dst, sem, n):
    def body(c, carry):
        r0 = pl.multiple_of(c * ROW_UNROLL, ROW_UNROLL)
        for s in range(ROW_UNROLL):
            d = idx_ref[base + r0 + s]
            pltpu.make_async_copy(src_hbm.at[pl.ds(pl.multiple_of(d * SUBLANES, SUBLANES), SUBLANES)],
                                  dst.at[pl.ds(pl.multiple_of((r0 + s) * SUBLANES, SUBLANES), SUBLANES)],
                                  sem).start(priority=s % 2)
        return carry
    lax.fori_loop(0, n // ROW_UNROLL, body, 0)


def _row_gather_wait(src_hbm, dst, sem, n):
    pltpu.make_async_copy(src_hbm.at[pl.ds(0, n * SUBLANES)], dst, sem).wait()


def _combine_kernel(dest_ref, h_ref, info_ref, y_hbm, fw_ref, o_ref, ybuf, sems, *, tm, eps, ntok):
    i = pl.program_id(0)
    n = pl.num_programs(0)
    slot = i % 2
    nrow = TOP_K * tm

    def gather_step(step, s):
        for k in range(TOP_K):
            _row_gather(y_hbm, dest_ref, k * ntok + step * tm,
                        ybuf.at[s, pl.ds(k * tm * SUBLANES, tm * SUBLANES)], sems.at[s], tm)

    @pl.when(i == 0)
    def _():
        gather_step(0, 0)

    @pl.when(i + 1 < n)
    def _():
        gather_step(i + 1, 1 - slot)

    _row_gather_wait(y_hbm, ybuf.at[slot], sems.at[slot], nrow)
    half = o_ref.shape[1] // 2
    rc = tm

    def rows(ci, carry):
        r0 = pl.multiple_of(ci * rc, rc)
        rs = pl.ds(r0, rc)
        info = info_ref[rs, :]
        g1 = info[:, 2:3]
        g2 = info[:, 3:4]
        y1_lo, y1_hi = _unpack_bf16_pairs(_load_row_tiles(ybuf.at[slot], r0, rc))
        y2_lo, y2_hi = _unpack_bf16_pairs(_load_row_tiles(ybuf.at[slot], tm + r0, rc))
        h_lo = h_ref[rs, 0:half] + (g1 * y1_lo + g2 * y2_lo)
        h_hi = h_ref[rs, half:2 * half] + (g1 * y1_hi + g2 * y2_hi)
        ms = (jnp.sum(h_lo * h_lo, axis=-1, keepdims=True)
              + jnp.sum(h_hi * h_hi, axis=-1, keepdims=True)) / (2 * half)
        r = lax.rsqrt(ms + eps)
        o_ref[rs, 0:half] = h_lo * r * fw_ref[:, 0:half]
        o_ref[rs, half:2 * half] = h_hi * r * fw_ref[:, half:2 * half]
        return carry

    lax.fori_loop(0, tm // rc, rows, 0)


def _combine(dest, h, info, yb, final_w, tm):
    T, D = h.shape
    grid_spec = pltpu.PrefetchScalarGridSpec(
        num_scalar_prefetch=1,
        grid=(T // tm,),
        in_specs=[pl.BlockSpec((tm, D), lambda i, d: (i, 0)),
                  pl.BlockSpec((tm, LANES), lambda i, d: (i, 0)),
                  pl.BlockSpec(memory_space=pl.ANY),
                  pl.BlockSpec((1, D), lambda i, d: (0, 0))],
        out_specs=pl.BlockSpec((tm, D), lambda i, d: (i, 0)),
        scratch_shapes=[pltpu.VMEM((2, TOP_K * tm * SUBLANES, LANES), yb.dtype),
                        pltpu.SemaphoreType.DMA((2,))],
    )
    return pl.pallas_call(
        functools.partial(_combine_kernel, tm=tm, eps=RMS_EPS, ntok=T),
        grid_spec=grid_spec,
        out_shape=jax.ShapeDtypeStruct((T, D), F32),
        compiler_params=_cparams(("arbitrary",), 48),
        name="combine",
    )(dest, h, info, yb, final_w)


def _layer(h_in, l, B, S, mix_norm_w, w_in, conv_dw_w, conv_dw_b, conv_ln_w, conv_ln_b,
           lam_q1, lam_k1, lam_q2, lam_k2, attn_subln_w, w_out, ffn_norm_w,
           w_group, b_group, w_expert_gate, b_expert_gate, w1, w3, w2):
    T, D = h_in.shape
    d_conv = conv_dw_w.shape[1]
    d_attn = (w_in.shape[1] - 2 * d_conv) // 3
    ng = w_group.shape[1]
    ne = w_expert_gate.shape[1]
    epg = ne // ng
    assert ng + ne <= LANES
    lam_init = 0.8 - 0.6 * math.exp(-0.3 * l)

    proj = _norm_inproj(h_in, mix_norm_w.reshape(1, D), w_in.astype(BF16))
    y_conv = _conformer(proj, conv_dw_w, conv_dw_b, conv_ln_w, conv_ln_b, B, S)
    y_attn = _diff_attn(proj, lam_q1, lam_k1, lam_q2, lam_k2, attn_subln_w, B, S, d_conv, d_attn, lam_init)

    wr = jnp.concatenate([w_group, w_expert_gate, jnp.zeros((D, LANES - ng - ne), F32)], axis=1)
    wr_hi = wr.astype(BF16)
    wr_lo = (wr - wr_hi.astype(F32)).astype(BF16)
    br = jnp.concatenate([b_group, b_expert_gate.reshape(-1), jnp.zeros((LANES - ng - ne,), F32)]).reshape(1, LANES)
    wr_hl = jnp.concatenate([wr_hi, wr_lo], axis=1)
    h, logits, ug = _outproj(h_in, y_conv, y_attn, w_out.astype(BF16), ffn_norm_w.reshape(1, D), wr_hl, br)

    info, cols, cnt = _route(logits, ng, epg)

    blk = 256
    tmd = min(256, T)
    A = T * TOP_K
    nblk = (A + ne * (blk - 1) + blk - 1) // blk
    i32 = jnp.int32
    counts = cnt[0, ng:ng + ne].astype(i32)
    padded = (counts + blk - 1) // blk * blk
    pad_ends = jnp.cumsum(padded).astype(i32)
    pad_starts = pad_ends - padded
    dest = _dest(pad_starts, cols)[0:TOP_K].reshape(-1)
    nused = (pad_ends[-1] // blk).astype(i32)
    bpos = jnp.arange(nblk, dtype=i32)
    brow = jnp.minimum(bpos, nused - 1) * blk
    bexp = jnp.minimum(jnp.sum((pad_ends[None, :] <= brow[:, None]).astype(i32), axis=1), ne - 1)
    first = ((bpos < nused) & ((bpos == 0) | (bexp != jnp.roll(bexp, 1)))).astype(i32)
    slot = ((jnp.cumsum(first) - 1) % WEIGHT_SLOTS).astype(i32)
    used_idx = jnp.where(padded > 0, jnp.arange(ne, dtype=i32), ne)
    suffix_min = lax.cummin(used_idx, reverse=True)
    next_used = jnp.concatenate([suffix_min[1:], jnp.full((2,), ne, i32)])
    ahead = jnp.arange(ne, dtype=i32)
    for _ in range(WEIGHT_SLOTS - 1):
        ahead = next_used[ahead]
    ahead = jnp.where(ahead >= ne, -1, ahead)
    nxt = ahead[bexp].astype(i32)
    lead = [suffix_min[0]]
    for _ in range(WEIGHT_SLOTS - 2):
        lead.append(next_used[lead[-1]])
    lead = jnp.stack([jnp.where(e >= ne, -1, e) for e in lead]).astype(i32)

    xs = _dispatch(dest, pad_ends, ug, nblk * blk, tmd, blk)
    yb = _experts(bexp, first, slot, nxt, lead, nused.reshape(1), xs, w1, w3, w2, blk)
    return h, info, dest, yb


def kernel(x, mix_norm_w, w_in, conv_dw_w, conv_dw_b, conv_ln_w, conv_ln_b, lam_q1, lam_k1, lam_q2, lam_k2,
           attn_subln_w, w_out, ffn_norm_w, w_group, b_group, w_expert_gate, b_expert_gate, w1, w3, w2,
           final_norm_w):
    B, S, D = x.shape
    depth = w_in.shape[0]
    assert depth == 1
    T = B * S
    tmc = min(256, T)
    h = x.reshape(T, D)
    for l in range(depth):
        h, info, dest, yb = _layer(
            h, l, B, S, mix_norm_w[l], w_in[l], conv_dw_w[l], conv_dw_b[l], conv_ln_w[l], conv_ln_b[l],
            lam_q1[l], lam_k1[l], lam_q2[l], lam_k2[l], attn_subln_w[l], w_out[l], ffn_norm_w[l],
            w_group[l], b_group[l], w_expert_gate[l], b_expert_gate[l], w1[l], w3[l], w2[l])
        h = _combine(dest, h, info, yb, final_norm_w.reshape(1, D), tmc)
    return h.reshape(B, S, D)
```

```python
import functools
import math

import jax
import jax.numpy as jnp
from jax import lax
from jax.experimental import pallas as pl
from jax.experimental.pallas import tpu as pltpu

F32 = jnp.float32
BF16 = jnp.bfloat16

RMS_EPS = 1e-6
SUBLN_EPS = 1e-5
LN_EPS = 1e-5
TOP_K = 2
LANES = 128
SUBLANES = 8
NEG_BIG = -1e30
MIB = 1024 * 1024


def _cparams(sem, vmem_mib):
    return pltpu.CompilerParams(dimension_semantics=sem, vmem_limit_bytes=vmem_mib * MIB)


def _norm_inproj_kernel(x_ref, nw_ref, w_ref, o_ref, u_ref, *, eps, rc):
    @pl.when(pl.program_id(1) == 0)
    def _():
        for r0 in range(0, x_ref.shape[0], rc):
            rows = slice(r0, r0 + rc)
            x = x_ref[rows, :]
            ms = jnp.mean(x * x, axis=-1, keepdims=True)
            u = (x * lax.rsqrt(ms + eps) * nw_ref[...]).astype(u_ref.dtype)
            u_ref[rows, :] = u
            o_ref[rows, :] = jnp.dot(u, w_ref[...], preferred_element_type=F32).astype(o_ref.dtype)

    @pl.when(pl.program_id(1) > 0)
    def _():
        o_ref[...] = jnp.dot(u_ref[...], w_ref[...], preferred_element_type=F32).astype(o_ref.dtype)


def _norm_inproj(x2, nw, w_bf):
    T, D = x2.shape
    N = w_bf.shape[1]
    tm = min(1024, T)
    tn = 1024
    return pl.pallas_call(
        functools.partial(_norm_inproj_kernel, eps=RMS_EPS, rc=256),
        grid=(T // tm, N // tn),
        in_specs=[pl.BlockSpec((tm, D), lambda i, j: (i, 0)),
                  pl.BlockSpec((1, D), lambda i, j: (0, 0)),
                  pl.BlockSpec((D, tn), lambda i, j: (0, j))],
        out_specs=[pl.BlockSpec((tm, tn), lambda i, j: (i, j)),
                   pl.BlockSpec((tm, D), lambda i, j: (i, 0))],
        out_shape=[jax.ShapeDtypeStruct((T, N), BF16), jax.ShapeDtypeStruct((T, D), BF16)],
        compiler_params=_cparams(("arbitrary", "arbitrary"), 48),
        name="norm_inproj",
    )(x2, nw, w_bf)


QKV_CHUNK = 2 * LANES
QKV_LEAD = 4


def _conv_qkv_kernel(a_ref, g_ref, w_ref, b_ref, lw_ref, lb_ref, u_ref, wq_ref, o_ref, qkv_ref,
                     ubuf, cbuf, sh, lhs, *, ts, kw, halo, eps):
    s = pl.program_id(1)
    C = a_ref.shape[1]
    nmm = wq_ref.shape[1] // QKV_CHUNK

    @pl.when(s == 0)
    def _():
        ubuf[0:halo, :] = jnp.zeros((halo, C), F32)

    @pl.when(s > 0)
    def _():
        ubuf[0:halo, :] = ubuf[ts:ts + halo, :]

    lhs[...] = u_ref[...]

    def project(n, after=None):
        if after is not None:
            allrows = jnp.sum(after, axis=0)
            bits = lax.bitcast_convert_type(jnp.concatenate([allrows, allrows], axis=0), jnp.uint32)
            zero = lax.bitcast_convert_type((bits >> 16) >> 16, F32).astype(lhs.dtype)
            lhs[0:2 * SUBLANES, 0:LANES] = lhs[0:2 * SUBLANES, 0:LANES] + zero
        cs = slice(n * QKV_CHUNK, (n + 1) * QKV_CHUNK)
        qkv_ref[:, cs] = jnp.dot(lhs[...], wq_ref[:, cs], preferred_element_type=F32).astype(qkv_ref.dtype)

    for n in range(QKV_LEAD):
        project(n)
    nproj = QKV_LEAD

    rg = 64
    for r0 in range(0, ts, rg):
        a = a_ref[r0:r0 + rg, :].astype(F32)
        g = g_ref[r0:r0 + rg, :].astype(F32)
        ubuf[halo + r0:halo + r0 + rg, :] = a * jax.nn.sigmoid(g)

    nsh = sh.shape[1]
    for r in range(1, SUBLANES):
        for i0 in range(0, nsh, rg):
            n = min(rg, nsh - i0)
            sh[r - 1, i0:i0 + n, :] = ubuf[i0 + r:i0 + r + n, :]

    off = halo - (kw - 1)
    rc = 128
    nunit = (ts // rc) * (C // LANES)
    every = max(1, nunit // max(1, nmm - QKV_LEAD))
    unit = 0
    for r0 in range(0, ts, rc):
        for c in range(C // LANES):
            cs = slice(c * LANES, (c + 1) * LANES)
            acc = jnp.broadcast_to(b_ref[0:1, cs], (rc // SUBLANES, SUBLANES, LANES))
            for k in range(kw):
                q, r = divmod(off + k, SUBLANES)
                a0 = r0 + q * SUBLANES
                tap = ubuf[a0:a0 + rc, cs] if r == 0 else sh[r - 1, a0:a0 + rc, cs]
                acc = acc + w_ref[k, :, cs] * tap.reshape(rc // SUBLANES, SUBLANES, LANES)
            cbuf[r0:r0 + rc, cs] = acc.reshape(rc, LANES)
            unit += 1
            if unit % every == 0 and nproj < nmm:
                project(nproj, after=acc)
                nproj += 1
        cv = cbuf[r0:r0 + rc, :]
        mu = jnp.mean(cv, axis=-1, keepdims=True)
        d = cv - mu
        var = jnp.mean(d * d, axis=-1, keepdims=True)
        un = d * lax.rsqrt(var + eps) * lw_ref[...] + lb_ref[...]
        o_ref[r0:r0 + rc, :] = (un * jax.nn.sigmoid(un)).astype(o_ref.dtype)
    while nproj < nmm:
        project(nproj)
        nproj += 1


def _conformer_qkv(proj_ag, u, w_qkv, dw_w, dw_b, ln_w, ln_b, B, S):
    T = proj_ag.shape[0]
    kw, C = dw_w.shape
    D, N = w_qkv.shape
    ts = min(256, S)
    halo = 32
    assert kw - 1 <= halo and S % ts == 0 and ts >= halo and N % QKV_CHUNK == 0
    ns = S // ts
    wp = jnp.broadcast_to(dw_w[:, None, :], (kw, SUBLANES, C))
    return pl.pallas_call(
        functools.partial(_conv_qkv_kernel, ts=ts, kw=kw, halo=halo, eps=LN_EPS),
        grid=(B, ns),
        in_specs=[pl.BlockSpec((ts, C), lambda b, s: (b * ns + s, 0)),
                  pl.BlockSpec((ts, C), lambda b, s: (b * ns + s, 1)),
                  pl.BlockSpec((kw, SUBLANES, C), lambda b, s: (0, 0, 0)),
                  pl.BlockSpec((1, C), lambda b, s: (0, 0)),
                  pl.BlockSpec((1, C), lambda b, s: (0, 0)),
                  pl.BlockSpec((1, C), lambda b, s: (0, 0)),
                  pl.BlockSpec((ts, D), lambda b, s: (b * ns + s, 0)),
                  pl.BlockSpec((D, N), lambda b, s: (0, 0), pipeline_mode=pl.Buffered(1))],
        out_specs=[pl.BlockSpec((ts, C), lambda b, s: (b * ns + s, 0)),
                   pl.BlockSpec((ts, N), lambda b, s: (b * ns + s, 0))],
        out_shape=[jax.ShapeDtypeStruct((T, C), BF16), jax.ShapeDtypeStruct((T, N), BF16)],
        scratch_shapes=[pltpu.VMEM((ts + halo, C), F32), pltpu.VMEM((ts, C), F32),
                        pltpu.VMEM((SUBLANES - 1, ts + halo - SUBLANES, C), F32),
                        pltpu.VMEM((ts, D), BF16)],
        compiler_params=_cparams(("arbitrary", "arbitrary"), 48),
        name="conformer_qkv",
    )(proj_ag, proj_ag, wp, dw_b.reshape(1, C), ln_w.reshape(1, C), ln_b.reshape(1, C), u, w_qkv)


def _attn_kernel(lq1_ref, lk1_ref, lq2_ref, lk2_ref, q_ref, k_ref, v_ref, sw_ref, o_ref,
                 vxt_ref, qq_ref, *scr, tq, hd, hp, ones_rows, lam_init, eps):
    i = pl.program_id(2)
    vd = 2 * hd
    nkb = v_ref.shape[0] // tq
    acc_refs, m_refs, sa_refs, sb_refs = (scr[n * hp:(n + 1) * hp] for n in range(4))

    @pl.when(i == 0)
    def _():
        for u in range(hp):
            for jb in range(nkb):
                vblk = v_ref[jb * tq:(jb + 1) * tq, u * vd:(u + 1) * vd].astype(F32)
                vxt_ref[u, jb, 0:vd, :] = vblk.T.astype(vxt_ref.dtype)
                vxt_ref[u, jb, vd:vd + ones_rows, :] = jnp.ones((ones_rows, tq), vxt_ref.dtype)

    lam = (jnp.exp(jnp.sum(lq1_ref[...] * lk1_ref[...], axis=-1, keepdims=True))
           - jnp.exp(jnp.sum(lq2_ref[...] * lk2_ref[...], axis=-1, keepdims=True)) + lam_init)

    for u in range(hp):
        q = q_ref[:, u * vd:(u + 1) * vd]
        qs = q * jnp.asarray(hd ** -0.5, q.dtype)
        lane = lax.broadcasted_iota(jnp.int32, q.shape, 1)
        zero = jnp.zeros_like(qs)
        qq_ref[u, 0:tq, :] = jnp.where(lane < hd, qs, zero)
        qq_ref[u, tq:2 * tq, :] = jnp.where(lane >= hd, qs, zero)
        acc_refs[u][...] = jnp.zeros(acc_refs[u].shape, F32)
        m_refs[u][...] = jnp.full(m_refs[u].shape, NEG_BIG, F32)

    def scores(j, u, dst):
        rows = pl.ds(pl.multiple_of(j * tq, tq), tq)
        kb = k_ref[rows, u * vd:(u + 1) * vd]
        dst[...] = lax.dot_general(kb, qq_ref[u], (((1,), (1,)), ((), ())), preferred_element_type=F32)

    def softmax_pv(j, u, src, masked):
        st = src[...]
        if masked:
            r = lax.broadcasted_iota(jnp.int32, st.shape, 0)
            c = lax.broadcasted_iota(jnp.int32, st.shape, 1)
            st = jnp.where(r <= jnp.where(c >= tq, c - tq, c), st, NEG_BIG)
        m_old = m_refs[u][...]
        m_new = jnp.maximum(m_old, jnp.max(st, axis=0, keepdims=True))
        alpha = jnp.exp(m_old - m_new)
        pt = jnp.exp(st - m_new).astype(vxt_ref.dtype)
        pv = jnp.dot(vxt_ref[u, j], pt, preferred_element_type=F32)
        acc_refs[u][...] = alpha * acc_refs[u][...] + pv
        m_refs[u][...] = m_new

    def half(j, cur, nxt):
        for u in range(hp):
            scores(j + 1, u, nxt[u])
            softmax_pv(j, u, cur[u], False)

    for u in range(hp):
        scores(0, u, sa_refs[u])

    def pair(t, carry):
        half(2 * t, sa_refs, sb_refs)
        half(2 * t + 1, sb_refs, sa_refs)
        return carry

    lax.fori_loop(0, i // 2, pair, 0)

    @pl.when(i % 2 == 1)
    def _():
        half(i - 1, sa_refs, sb_refs)
        for u in range(hp):
            softmax_pv(i, u, sb_refs[u], True)

    @pl.when(i % 2 == 0)
    def _():
        for u in range(hp):
            softmax_pv(i, u, sa_refs[u], True)

    for u in range(hp):
        acc = acc_refs[u][...]
        o12 = acc[0:vd] * (1.0 / acc[vd:vd + 1])
        ot = o12[:, 0:tq] - lam * o12[:, tq:2 * tq]
        msq = jnp.mean(ot * ot, axis=0, keepdims=True)
        o = (ot * lax.rsqrt(msq + eps)).T
        o_ref[:, u * vd:(u + 1) * vd] = (o * sw_ref[...] * (1.0 - lam_init)).astype(o_ref.dtype)


def _diff_attn(proj, lq1, lk1, lq2, lk2, subln_w, B, S, d_conv, d_attn, lam_init):
    T = proj.shape[0]
    vd = subln_w.shape[0]
    hd = lq1.shape[0]
    assert vd == LANES and 2 * hd == vd
    H = d_attn // vd
    hp = 8
    ones_rows = 16
    tq = min(256, S)
    nq = S // tq
    assert H % hp == 0
    qc = 2 * d_conv // (hp * vd)
    kc = qc + H // hp
    vc = kc + H // hp
    lspec = pl.BlockSpec((1, hd), lambda b, h, i: (0, 0))
    return pl.pallas_call(
        functools.partial(_attn_kernel, tq=tq, hd=hd, hp=hp, ones_rows=ones_rows, lam_init=lam_init, eps=SUBLN_EPS),
        grid=(B, H // hp, nq),
        in_specs=[lspec, lspec, lspec, lspec,
                  pl.BlockSpec((tq, hp * vd), lambda b, h, i: (b * nq + i, qc + h)),
                  pl.BlockSpec((S, hp * vd), lambda b, h, i: (b, kc + h)),
                  pl.BlockSpec((S, hp * vd), lambda b, h, i: (b, vc + h)),
                  pl.BlockSpec((1, vd), lambda b, h, i: (0, 0))],
        out_specs=pl.BlockSpec((tq, hp * vd), lambda b, h, i: (b * nq + i, h)),
        out_shape=jax.ShapeDtypeStruct((T, d_attn), BF16),
        scratch_shapes=[pltpu.VMEM((hp, S // tq, vd + ones_rows, tq), BF16),
                        pltpu.VMEM((hp, 2 * tq, vd), BF16)]
                       + [pltpu.VMEM((vd + ones_rows, 2 * tq), F32) for _ in range(hp)]
                       + [pltpu.VMEM((1, 2 * tq), F32) for _ in range(hp)]
                       + [pltpu.VMEM((tq, 2 * tq), F32) for _ in range(2 * hp)],
        compiler_params=_cparams(("arbitrary", "arbitrary", "arbitrary"), 52),
        name="diff_attn",
    )(lq1.reshape(1, hd), lk1.reshape(1, hd), lq2.reshape(1, hd), lk2.reshape(1, hd),
      proj, proj, proj, subln_w.reshape(1, vd))


def _pack_bf16_pairs(v):
    n = v.shape[1] // 2
    return _pack_bf16_words(v[:, 0:n], v[:, n:2 * n])


def _pack_bf16_words(lo, hi):
    lo_bits = lax.bitcast_convert_type(lo.astype(BF16).astype(F32), jnp.uint32)
    hi_bits = lax.bitcast_convert_type(hi.astype(BF16).astype(F32), jnp.uint32)
    return (lo_bits >> 16) | (hi_bits & jnp.uint32(0xFFFF0000))


def _unpack_bf16_pairs(w):
    lo = lax.bitcast_convert_type(w << 16, F32)
    hi = lax.bitcast_convert_type(w & jnp.uint32(0xFFFF0000), F32)
    return lo, hi


def _store_row_tiles(ref, row0, words):
    m, w = words.shape
    assert w == SUBLANES * LANES
    for s in range(SUBLANES):
        ref[pl.ds(row0 * SUBLANES + s, m, stride=SUBLANES), :] = words[:, s * LANES:(s + 1) * LANES]


def _load_row_tiles(ref, row0, m):
    return jnp.concatenate([ref[pl.ds(row0 * SUBLANES + s, m, stride=SUBLANES), :] for s in range(SUBLANES)], axis=1)


def _outproj_kernel(x_ref, yc_ref, ya_ref, wo_ref, fw_ref, wr_ref, br_ref, h_ref, lg_ref, ug_ref, *, eps, nsub):
    dc = yc_ref.shape[1]
    sub = x_ref.shape[0] // nsub
    mixes = []
    for t in range(nsub):
        rows = slice(t * sub, (t + 1) * sub)
        mixes.append(jnp.dot(yc_ref[rows, :], wo_ref[0:dc, :], preferred_element_type=F32)
                     + jnp.dot(ya_ref[rows, :], wo_ref[dc:, :], preferred_element_type=F32))
    for t in range(nsub):
        rows = slice(t * sub, (t + 1) * sub)
        h = x_ref[rows, :] + mixes[t]
        h_ref[rows, :] = h
        ms = jnp.mean(h * h, axis=-1, keepdims=True)
        un = h * lax.rsqrt(ms + eps) * fw_ref[...]
        hi = un.astype(BF16)
        lo = (un - hi.astype(F32)).astype(BF16)
        hh = jnp.dot(hi, wr_ref[...], preferred_element_type=F32)
        lh = jnp.dot(lo, wr_ref[:, 0:LANES], preferred_element_type=F32)
        lg_ref[rows, :] = hh[:, 0:LANES] + hh[:, LANES:2 * LANES] + lh + br_ref[...]
        _store_row_tiles(ug_ref, t * sub, _pack_bf16_pairs(un))


def _outproj(x2, y_conv, y_attn, wo_bf, ffn_w, wr_hl, br):
    T, D = x2.shape
    dc = y_conv.shape[1]
    da = y_attn.shape[1]
    tm = min(512, T)
    nsub = 2 if tm % 512 == 0 else 1
    return pl.pallas_call(
        functools.partial(_outproj_kernel, eps=RMS_EPS, nsub=nsub),
        grid=(T // tm,),
        in_specs=[pl.BlockSpec((tm, D), lambda i: (i, 0)),
                  pl.BlockSpec((tm, dc), lambda i: (i, 0)),
                  pl.BlockSpec((tm, da), lambda i: (i, 0)),
                  pl.BlockSpec((dc + da, D), lambda i: (0, 0), pipeline_mode=pl.Buffered(1)),
                  pl.BlockSpec((1, D), lambda i: (0, 0)),
                  pl.BlockSpec((D, 2 * LANES), lambda i: (0, 0)),
                  pl.BlockSpec((1, LANES), lambda i: (0, 0))],
        out_specs=[pl.BlockSpec((tm, D), lambda i: (i, 0)),
                   pl.BlockSpec((tm, LANES), lambda i: (i, 0)),
                   pl.BlockSpec((tm * SUBLANES, LANES), lambda i: (i, 0))],
        out_shape=[jax.ShapeDtypeStruct((T, D), F32), jax.ShapeDtypeStruct((T, LANES), F32),
                   jax.ShapeDtypeStruct((T * SUBLANES, LANES), jnp.uint32)],
        compiler_params=_cparams(("arbitrary",), 56),
        name="outproj",
    )(x2, y_conv, y_attn, wo_bf, ffn_w, wr_hl, br)


def _route_kernel(lg_ref, info_ref, cols_ref, cnt_ref, *, ng, epg):
    i = pl.program_id(0)
    lg = lg_ref[...]
    tm = lg.shape[0]
    lane = lax.broadcasted_iota(jnp.int32, lg.shape, 1)
    lanef = lane.astype(F32)
    ne = ng * epg

    def first_argmax(vals):
        mx = jnp.max(vals, axis=-1, keepdims=True)
        idx = jnp.min(jnp.where(vals == mx, lanef, float(LANES)), axis=-1, keepdims=True)
        return mx, idx

    gmask = lane < ng
    gl = jnp.where(gmask, lg, NEG_BIG)
    gmax, gsel = first_argmax(gl)
    gsum = jnp.sum(jnp.where(gmask, jnp.exp(gl - gmax), 0.0), axis=-1, keepdims=True)
    g_w = 1.0 / gsum
    lo = gsel * epg + ng
    emask = (lanef >= lo) & (lanef < lo + epg)
    el = jnp.where(emask, lg, NEG_BIG)
    v1, i1 = first_argmax(el)
    el2 = jnp.where(lanef == i1, NEG_BIG, el)
    v2, i2 = first_argmax(el2)
    e2 = jnp.exp(v2 - v1)
    p1 = 1.0 / (1.0 + e2)
    gate1 = g_w * p1
    gate2 = g_w * (e2 * p1)

    oh1 = lanef == i1
    oh2 = lanef == i2
    cmat = jnp.where(oh1 | oh2, 1.0, 0.0).astype(BF16)
    r = lax.broadcasted_iota(jnp.int32, (tm, tm), 0)
    c = lax.broadcasted_iota(jnp.int32, (tm, tm), 1)
    tri = jnp.where(c < r, 1.0, 0.0).astype(BF16)

    @pl.when(i == 0)
    def _():
        cnt_ref[...] = jnp.zeros(cnt_ref.shape, F32)

    carry = cnt_ref[0:1, :]
    prefix = jnp.dot(tri, cmat, preferred_element_type=F32) + carry
    rank1 = jnp.sum(jnp.where(oh1, prefix, 0.0), axis=-1, keepdims=True)
    rank2 = jnp.sum(jnp.where(oh2, prefix, 0.0), axis=-1, keepdims=True)
    cnt_ref[...] = jnp.broadcast_to(carry + jnp.sum(cmat.astype(F32), axis=0, keepdims=True), cnt_ref.shape)

    info = jnp.where(lane == 0, i1 - ng,
           jnp.where(lane == 1, i2 - ng,
           jnp.where(lane == 2, gate1,
           jnp.where(lane == 3, gate2,
           jnp.where(lane == 4, rank1,
           jnp.where(lane == 5, rank2, 0.0))))))
    info_ref[...] = info
    cols_ref[...] = info.T[0:SUBLANES, :]


def _route(logits, ng, epg):
    T = logits.shape[0]
    tm = min(512, T)
    return pl.pallas_call(
        functools.partial(_route_kernel, ng=ng, epg=epg),
        grid=(T // tm,),
        in_specs=[pl.BlockSpec((tm, LANES), lambda i: (i, 0))],
        out_specs=[pl.BlockSpec((tm, LANES), lambda i: (i, 0)),
                   pl.BlockSpec((SUBLANES, tm), lambda i: (0, i)),
                   pl.BlockSpec((8, LANES), lambda i: (0, 0))],
        out_shape=[jax.ShapeDtypeStruct((T, LANES), F32), jax.ShapeDtypeStruct((SUBLANES, T), F32),
                   jax.ShapeDtypeStruct((8, LANES), F32)],
        compiler_params=_cparams(("arbitrary",), 32),
        name="route",
    )(logits)


def _dest_kernel(pstart_ref, cols_ref, dest_ref, *, ne):
    eid = cols_ref[0:TOP_K, :].astype(jnp.int32)
    rank = cols_ref[4:4 + TOP_K, :].astype(jnp.int32)

    def body(e, acc):
        return acc + jnp.where(eid == e, pstart_ref[e], 0)

    start = lax.fori_loop(0, ne, body, jnp.zeros(eid.shape, jnp.int32))
    dest_ref[...] = jnp.zeros(dest_ref.shape, jnp.int32)
    dest_ref[0:TOP_K, :] = start + rank


def _dest(pad_starts, cols):
    R, T = cols.shape
    grid_spec = pltpu.PrefetchScalarGridSpec(
        num_scalar_prefetch=1,
        grid=(1,),
        in_specs=[pl.BlockSpec((R, T), lambda i, p: (0, 0))],
        out_specs=pl.BlockSpec((R, T), lambda i, p: (0, 0)),
    )
    return pl.pallas_call(
        functools.partial(_dest_kernel, ne=pad_starts.shape[0]),
        grid_spec=grid_spec,
        out_shape=jax.ShapeDtypeStruct((R, T), jnp.int32),
        compiler_params=_cparams(("arbitrary",), 16),
        name="dest",
    )(pad_starts, cols)


ROW_UNROLL = 8


def _dispatch_kernel(dest_ref, pend_ref, ug_ref, xs_hbm, ring, zbuf, sems, zsem, *, tm, blk, ne, ntok):
    i = pl.program_id(0)
    n = pl.num_programs(0)
    slot = i % 2

    def seg_tail(e):
        end = pend_ref[e]
        start = jnp.where(e == 0, 0, pend_ref[jnp.maximum(e - 1, 0)])
        tail = pl.multiple_of(jnp.maximum(end - blk, 0) * SUBLANES, blk * SUBLANES)
        return pltpu.make_async_copy(zbuf, xs_hbm.at[pl.ds(tail, blk * SUBLANES)], zsem), end > start

    @pl.when(i == 0)
    def _():
        zbuf[...] = jnp.zeros(zbuf.shape, zbuf.dtype)

        def zstart(p, carry):
            for q in range(2):
                cp, nonempty = seg_tail(2 * p + q)

                @pl.when(nonempty)
                def _():
                    cp.start(priority=q)
            return carry

        def zwait(e, carry):
            cp, nonempty = seg_tail(e)

            @pl.when(nonempty)
            def _():
                cp.wait()
            return carry

        def spare(b):
            row = pl.multiple_of(b * blk, blk)
            tile = pl.multiple_of(row * SUBLANES, blk * SUBLANES)
            return pltpu.make_async_copy(zbuf, xs_hbm.at[pl.ds(tile, blk * SUBLANES)], zsem), row >= pend_ref[ne - 1]

        def sstart(p, carry):
            for q in range(2):
                cp, unused = spare(2 * p + q)

                @pl.when(unused)
                def _():
                    cp.start(priority=q)
            return carry

        def swait(b, carry):
            cp, unused = spare(b)

            @pl.when(unused)
            def _():
                cp.wait()
            return carry

        nblk = xs_hbm.shape[0] // (blk * SUBLANES)
        assert ne % 2 == 0 and nblk % 2 == 0
        lax.fori_loop(0, ne // 2, zstart, 0)
        lax.fori_loop(0, nblk // 2, sstart, 0)
        lax.fori_loop(0, ne, zwait, 0)
        lax.fori_loop(0, nblk, swait, 0)

    def drain(s):
        for _ in range(TOP_K):
            pltpu.make_async_copy(ring.at[s], xs_hbm.at[pl.ds(0, tm * SUBLANES)], sems.at[s]).wait()

    @pl.when(i >= 2)
    def _():
        drain(slot)

    ring[slot] = ug_ref[...]

    def issue(c, carry):
        r0 = pl.multiple_of(c * ROW_UNROLL, ROW_UNROLL)
        for s in range(ROW_UNROLL):
            for k in range(TOP_K):
                d = dest_ref[k * ntok + i * tm + r0 + s]
                src = ring.at[slot, pl.ds(pl.multiple_of((r0 + s) * SUBLANES, SUBLANES), SUBLANES)]
                dst = xs_hbm.at[pl.ds(pl.multiple_of(d * SUBLANES, SUBLANES), SUBLANES)]
                pltpu.make_async_copy(src, dst, sems.at[slot]).start(priority=k % 2)
        return carry

    lax.fori_loop(0, tm // ROW_UNROLL, issue, 0)

    @pl.when(i == n - 1)
    def _():
        drain(slot)

        @pl.when(n >= 2)
        def _():
            drain(1 - slot)


def _dispatch(dest_flat, pad_ends, ug, nrows, tm, blk):
    T = ug.shape[0] // SUBLANES
    ne = pad_ends.shape[0]
    grid_spec = pltpu.PrefetchScalarGridSpec(
        num_scalar_prefetch=2,
        grid=(T // tm,),
        in_specs=[pl.BlockSpec((tm * SUBLANES, LANES), lambda i, d, p: (i, 0))],
        out_specs=pl.BlockSpec(memory_space=pl.ANY),
        scratch_shapes=[pltpu.VMEM((2, tm * SUBLANES, LANES), ug.dtype),
                        pltpu.VMEM((blk * SUBLANES, LANES), ug.dtype),
                        pltpu.SemaphoreType.DMA((2,)),
                        pltpu.SemaphoreType.DMA(())],
    )
    return pl.pallas_call(
        functools.partial(_dispatch_kernel, tm=tm, blk=blk, ne=ne, ntok=T),
        grid_spec=grid_spec,
        out_shape=jax.ShapeDtypeStruct((nrows * SUBLANES, LANES), ug.dtype),
        compiler_params=_cparams(("arbitrary",), 32),
        name="dispatch",
    )(dest_flat, pad_ends, ug)


WEIGHT_DMA_PRIORITY = (0, 1, 1)
WEIGHT_SLOTS = 3


def _experts_kernel(bexp_ref, first_ref, slot_ref, next_ref, lead_ref, nused_ref, x_ref, w1_hbm, w3_hbm, w2_hbm,
                    y_ref, w1b, w3b, w2b, wsems):
    i = pl.program_id(0)
    nused = nused_ref[0]

    def weight_copies(e, s):
        return (pltpu.make_async_copy(w1_hbm.at[e], w1b.at[s], wsems.at[s, 0]),
                pltpu.make_async_copy(w3_hbm.at[e], w3b.at[s], wsems.at[s, 1]),
                pltpu.make_async_copy(w2_hbm.at[e], w2b.at[s], wsems.at[s, 2]))

    def start_weights(e, s):
        for n, cp in enumerate(weight_copies(e, s)):
            cp.start(priority=WEIGHT_DMA_PRIORITY[n])

    @pl.when(i == 0)
    def _():
        for n in range(WEIGHT_SLOTS - 1):
            @pl.when(lead_ref[n] >= 0)
            def _():
                start_weights(lead_ref[n], n)

    @pl.when(i < nused)
    def _():
        s = slot_ref[i]

        @pl.when(first_ref[i] == 1)
        def _():
            @pl.when(next_ref[i] >= 0)
            def _():
                start_weights(next_ref[i], (s + WEIGHT_SLOTS - 1) % WEIGHT_SLOTS)
            for cp in weight_copies(bexp_ref[i], s):
                cp.wait()

        blk = x_ref.shape[0] // SUBLANES
        x_lo, x_hi = _unpack_bf16_pairs(_load_row_tiles(x_ref, 0, blk))
        half = x_lo.shape[1]
        a = (jnp.dot(x_lo, w1b[s, 0:half, :], preferred_element_type=F32)
             + jnp.dot(x_hi, w1b[s, half:2 * half, :], preferred_element_type=F32))
        b = (jnp.dot(x_lo, w3b[s, 0:half, :], preferred_element_type=F32)
             + jnp.dot(x_hi, w3b[s, half:2 * half, :], preferred_element_type=F32))
        hdn = a * jax.nn.sigmoid(a) * b
        cw = 2 * LANES
        for c0 in range(0, half, cw):
            y_lo = jnp.dot(hdn, w2b[s, :, c0:c0 + cw], preferred_element_type=F32)
            y_hi = jnp.dot(hdn, w2b[s, :, half + c0:half + c0 + cw], preferred_element_type=F32)
            words = _pack_bf16_words(y_lo, y_hi)
            for g in range(cw // LANES):
                y_ref[pl.ds(c0 // LANES + g, blk, stride=SUBLANES), :] = words[:, g * LANES:(g + 1) * LANES]

    @pl.when(i >= nused)
    def _():
        y_ref[...] = jnp.zeros(y_ref.shape, y_ref.dtype)


def _experts(bexp, first, slot, nxt, lead, nused, xs, w1, w3, w2, blk):
    P = xs.shape[0] // SUBLANES
    E, D, Fh = w1.shape
    nblk = P // blk
    smap = lambda i, *_: (i, 0)
    xmap = lambda i, be, fi, sl, nx, ld, nu: (jnp.minimum(i, nu[0] - 1), 0)
    grid_spec = pltpu.PrefetchScalarGridSpec(
        num_scalar_prefetch=6,
        grid=(nblk,),
        in_specs=[pl.BlockSpec((blk * SUBLANES, LANES), xmap),
                  pl.BlockSpec(memory_space=pl.ANY),
                  pl.BlockSpec(memory_space=pl.ANY),
                  pl.BlockSpec(memory_space=pl.ANY)],
        out_specs=pl.BlockSpec((blk * SUBLANES, LANES), smap),
        scratch_shapes=[pltpu.VMEM((WEIGHT_SLOTS, D, Fh), w1.dtype),
                        pltpu.VMEM((WEIGHT_SLOTS, D, Fh), w3.dtype),
                        pltpu.VMEM((WEIGHT_SLOTS, Fh, D), w2.dtype),
                        pltpu.SemaphoreType.DMA((WEIGHT_SLOTS, 3))],
    )
    return pl.pallas_call(
        _experts_kernel,
        grid_spec=grid_spec,
        out_shape=jax.ShapeDtypeStruct(xs.shape, xs.dtype),
        compiler_params=_cparams(("arbitrary",), 56),
        name="experts",
    )(bexp, first, slot, nxt, lead, nused, xs, w1, w3, w2)


def _row_gather(src_hbm, idx_ref, base, dst, sem, n):
    def body(c, carry):
        r0 = pl.multiple_of(c * ROW_UNROLL, ROW_UNROLL)
        for s in range(ROW_UNROLL):
            d = idx_ref[base + r0 + s]
            pltpu.make_async_copy(src_hbm.at[pl.ds(pl.multiple_of(d * SUBLANES, SUBLANES), SUBLANES)],
                                  dst.at[pl.ds(pl.multiple_of((r0 + s) * SUBLANES, SUBLANES), SUBLANES)],
                                  sem).start(priority=s % 2)
        return carry
    lax.fori_loop(0, n // ROW_UNROLL, body, 0)


def _row_gather_wait(src_hbm, dst, sem, n):
    pltpu.make_async_copy(src_hbm.at[pl.ds(0, n * SUBLANES)], dst, sem).wait()


def _combine_kernel(dest_ref, h_ref, info_ref, y_hbm, fw_ref, o_ref, ybuf, sems, *, tm, eps, ntok):
    i = pl.program_id(0)
    n = pl.num_programs(0)
    slot = i % 2
    nrow = TOP_K * tm

    def gather_step(step, s):
        for k in range(TOP_K):
            _row_gather(y_hbm, dest_ref, k * ntok + step * tm,
                        ybuf.at[s, pl.ds(k * tm * SUBLANES, tm * SUBLANES)], sems.at[s], tm)

    @pl.when(i == 0)
    def _():
        gather_step(0, 0)

    @pl.when(i + 1 < n)
    def _():
        gather_step(i + 1, 1 - slot)

    _row_gather_wait(y_hbm, ybuf.at[slot], sems.at[slot], nrow)
    half = o_ref.shape[1] // 2
    rc = tm

    def rows(ci, carry):
        r0 = pl.multiple_of(ci * rc, rc)
        rs = pl.ds(r0, rc)
        info = info_ref[rs, :]
        g1 = info[:, 2:3]
        g2 = info[:, 3:4]
        y1_lo, y1_hi = _unpack_bf16_pairs(_load_row_tiles(ybuf.at[slot], r0, rc))
        y2_lo, y2_hi = _unpack_bf16_pairs(_load_row_tiles(ybuf.at[slot], tm + r0, rc))
        h_lo = h_ref[rs, 0:half] + (g1 * y1_lo + g2 * y2_lo)
        h_hi = h_ref[rs, half:2 * half] + (g1 * y1_hi + g2 * y2_hi)
        ms = (jnp.sum(h_lo * h_lo, axis=-1, keepdims=True)
              + jnp.sum(h_hi * h_hi, axis=-1, keepdims=True)) / (2 * half)
        r = lax.rsqrt(ms + eps)
        o_ref[rs, 0:half] = h_lo * r * fw_ref[:, 0:half]
        o_ref[rs, half:2 * half] = h_hi * r * fw_ref[:, half:2 * half]
        return carry

    lax.fori_loop(0, tm // rc, rows, 0)


def _combine(dest, h, info, yb, final_w, tm):
    T, D = h.shape
    grid_spec = pltpu.PrefetchScalarGridSpec(
        num_scalar_prefetch=1,
        grid=(T // tm,),
        in_specs=[pl.BlockSpec((tm, D), lambda i, d: (i, 0)),
                  pl.BlockSpec((tm, LANES), lambda i, d: (i, 0)),
                  pl.BlockSpec(memory_space=pl.ANY),
                  pl.BlockSpec((1, D), lambda i, d: (0, 0))],
        out_specs=pl.BlockSpec((tm, D), lambda i, d: (i, 0)),
        scratch_shapes=[pltpu.VMEM((2, TOP_K * tm * SUBLANES, LANES), yb.dtype),
                        pltpu.SemaphoreType.DMA((2,))],
    )
    return pl.pallas_call(
        functools.partial(_combine_kernel, tm=tm, eps=RMS_EPS, ntok=T),
        grid_spec=grid_spec,
        out_shape=jax.ShapeDtypeStruct((T, D), F32),
        compiler_params=_cparams(("arbitrary",), 48),
        name="combine",
    )(dest, h, info, yb, final_w)


def _layer(h_in, l, B, S, mix_norm_w, w_in, conv_dw_w, conv_dw_b, conv_ln_w, conv_ln_b,
           lam_q1, lam_k1, lam_q2, lam_k2, attn_subln_w, w_out, ffn_norm_w,
           w_group, b_group, w_expert_gate, b_expert_gate, w1, w3, w2):
    T, D = h_in.shape
    d_conv = conv_dw_w.shape[1]
    d_attn = (w_in.shape[1] - 2 * d_conv) // 3
    ng = w_group.shape[1]
    ne = w_expert_gate.shape[1]
    epg = ne // ng
    assert ng + ne <= LANES
    lam_init = 0.8 - 0.6 * math.exp(-0.3 * l)

    w_bf = w_in.astype(BF16)
    proj_ag, u = _norm_inproj(h_in, mix_norm_w.reshape(1, D), w_bf[:, 0:2 * d_conv])
    y_conv, qkv = _conformer_qkv(proj_ag, u, w_bf[:, 2 * d_conv:], conv_dw_w, conv_dw_b, conv_ln_w, conv_ln_b, B, S)
    y_attn = _diff_attn(qkv, lam_q1, lam_k1, lam_q2, lam_k2, attn_subln_w, B, S, 0, d_attn, lam_init)

    wr = jnp.concatenate([w_group, w_expert_gate, jnp.zeros((D, LANES - ng - ne), F32)], axis=1)
    wr_hi = wr.astype(BF16)
    wr_lo = (wr - wr_hi.astype(F32)).astype(BF16)
    br = jnp.concatenate([b_group, b_expert_gate.reshape(-1), jnp.zeros((LANES - ng - ne,), F32)]).reshape(1, LANES)
    wr_hl = jnp.concatenate([wr_hi, wr_lo], axis=1)
    h, logits, ug = _outproj(h_in, y_conv, y_attn, w_out.astype(BF16), ffn_norm_w.reshape(1, D), wr_hl, br)

    info, cols, cnt = _route(logits, ng, epg)

    blk = 256
    tmd = min(256, T)
    A = T * TOP_K
    nblk = (A + ne * (blk - 1) + blk - 1) // blk
    i32 = jnp.int32
    counts = cnt[0, ng:ng + ne].astype(i32)
    padded = (counts + blk - 1) // blk * blk
    pad_ends = jnp.cumsum(padded).astype(i32)
    pad_starts = pad_ends - padded
    dest = _dest(pad_starts, cols)[0:TOP_K].reshape(-1)
    nused = (pad_ends[-1] // blk).astype(i32)
    bpos = jnp.arange(nblk, dtype=i32)
    brow = jnp.minimum(bpos, nused - 1) * blk
    bexp = jnp.minimum(jnp.sum((pad_ends[None, :] <= brow[:, None]).astype(i32), axis=1), ne - 1)
    first = ((bpos < nused) & ((bpos == 0) | (bexp != jnp.roll(bexp, 1)))).astype(i32)
    slot = ((jnp.cumsum(first) - 1) % WEIGHT_SLOTS).astype(i32)
    used_idx = jnp.where(padded > 0, jnp.arange(ne, dtype=i32), ne)
    suffix_min = lax.cummin(used_idx, reverse=True)
    next_used = jnp.concatenate([suffix_min[1:], jnp.full((2,), ne, i32)])
    ahead = jnp.arange(ne, dtype=i32)
    for _ in range(WEIGHT_SLOTS - 1):
        ahead = next_used[ahead]
    ahead = jnp.where(ahead >= ne, -1, ahead)
    nxt = ahead[bexp].astype(i32)
    lead = [suffix_min[0]]
    for _ in range(WEIGHT_SLOTS - 2):
        lead.append(next_used[lead[-1]])
    lead = jnp.stack([jnp.where(e >= ne, -1, e) for e in lead]).astype(i32)

    xs = _dispatch(dest, pad_ends, ug, nblk * blk, tmd, blk)
    yb = _experts(bexp, first, slot, nxt, lead, nused.reshape(1), xs, w1, w3, w2, blk)
    return h, info, dest, yb


def kernel(x, mix_norm_w, w_in, conv_dw_w, conv_dw_b, conv_ln_w, conv_ln_b, lam_q1, lam_k1, lam_q2, lam_k2,
           attn_subln_w, w_out, ffn_norm_w, w_group, b_group, w_expert_gate, b_expert_gate, w1, w3, w2,
           final_norm_w):
    B, S, D = x.shape
    depth = w_in.shape[0]
    assert depth == 1
    T = B * S
    tmc = min(256, T)
    h = x.reshape(T, D)
    for l in range(depth):
        h, info, dest, yb = _layer(
            h, l, B, S, mix_norm_w[l], w_in[l], conv_dw_w[l], conv_dw_b[l], conv_ln_w[l], conv_ln_b[l],
            lam_q1[l], lam_k1[l], lam_q2[l], lam_k2[l], attn_subln_w[l], w_out[l], ffn_norm_w[l],
            w_group[l], b_group[l], w_expert_gate[l], b_expert_gate[l], w1[l], w3[l], w2[l])
        h = _combine(dest, h, info, yb, final_norm_w.reshape(1, D), tmc)
    return h.reshape(B, S, D)
```

```python
import functools
import math

import jax
import jax.numpy as jnp
from jax import lax
from jax.experimental import pallas as pl
from jax.experimental.pallas import tpu as pltpu

F32 = jnp.float32
BF16 = jnp.bfloat16

RMS_EPS = 1e-6
SUBLN_EPS = 1e-5
LN_EPS = 1e-5
TOP_K = 2
LANES = 128
SUBLANES = 8
NEG_BIG = -1e30
MIB = 1024 * 1024


def _cparams(sem, vmem_mib):
    return pltpu.CompilerParams(dimension_semantics=sem, vmem_limit_bytes=vmem_mib * MIB)


def _norm_inproj_kernel(x_ref, nw_ref, w_ref, o_ref, u_ref, *, eps, rc):
    @pl.when(pl.program_id(1) == 0)
    def _():
        for r0 in range(0, x_ref.shape[0], rc):
            rows = slice(r0, r0 + rc)
            x = x_ref[rows, :]
            ms = jnp.mean(x * x, axis=-1, keepdims=True)
            u = (x * lax.rsqrt(ms + eps) * nw_ref[...]).astype(u_ref.dtype)
            u_ref[rows, :] = u
            o_ref[rows, :] = jnp.dot(u, w_ref[...], preferred_element_type=F32).astype(o_ref.dtype)

    @pl.when(pl.program_id(1) > 0)
    def _():
        o_ref[...] = jnp.dot(u_ref[...], w_ref[...], preferred_element_type=F32).astype(o_ref.dtype)


def _norm_inproj(x2, nw, w_bf, ncols):
    T, D = x2.shape
    N = ncols
    tm = min(1024, T)
    tn = 1024
    return pl.pallas_call(
        functools.partial(_norm_inproj_kernel, eps=RMS_EPS, rc=256),
        grid=(T // tm, N // tn),
        in_specs=[pl.BlockSpec((tm, D), lambda i, j: (i, 0)),
                  pl.BlockSpec((1, D), lambda i, j: (0, 0)),
                  pl.BlockSpec((D, tn), lambda i, j: (0, j))],
        out_specs=[pl.BlockSpec((tm, tn), lambda i, j: (i, j)),
                   pl.BlockSpec((tm, D), lambda i, j: (i, 0))],
        out_shape=[jax.ShapeDtypeStruct((T, N), BF16), jax.ShapeDtypeStruct((T, D), BF16)],
        compiler_params=_cparams(("arbitrary", "arbitrary"), 48),
        name="norm_inproj",
    )(x2, nw, w_bf)


QKV_CHUNK = 2 * LANES
QKV_LEAD = 4


def _conv_qkv_kernel(a_ref, g_ref, w_ref, b_ref, lw_ref, lb_ref, u_ref, *rest, ts, kw, halo, eps, nwq):
    wq_refs = rest[0:nwq]
    o_ref, qkv_ref, ubuf, cbuf, sh, lhs = rest[nwq:]
    s = pl.program_id(1)
    C = a_ref.shape[1]
    wcols = wq_refs[0].shape[1]
    nmm = nwq * wcols // QKV_CHUNK

    @pl.when(s == 0)
    def _():
        ubuf[0:halo, :] = jnp.zeros((halo, C), F32)

    @pl.when(s > 0)
    def _():
        ubuf[0:halo, :] = ubuf[ts:ts + halo, :]

    lhs[...] = u_ref[...]

    def project(n, after=None):
        if after is not None:
            allrows = jnp.sum(after, axis=0)
            bits = lax.bitcast_convert_type(jnp.concatenate([allrows, allrows], axis=0), jnp.uint32)
            zero = lax.bitcast_convert_type((bits >> 16) >> 16, F32).astype(lhs.dtype)
            lhs[0:2 * SUBLANES, 0:LANES] = lhs[0:2 * SUBLANES, 0:LANES] + zero
        cs = slice(n * QKV_CHUNK, (n + 1) * QKV_CHUNK)
        wq_ref = wq_refs[n * QKV_CHUNK // wcols]
        ws = slice(n * QKV_CHUNK % wcols, n * QKV_CHUNK % wcols + QKV_CHUNK)
        qkv_ref[:, cs] = jnp.dot(lhs[...], wq_ref[:, ws], preferred_element_type=F32).astype(qkv_ref.dtype)

    for n in range(QKV_LEAD):
        project(n)
    nproj = QKV_LEAD

    rg = 64
    for r0 in range(0, ts, rg):
        a = a_ref[r0:r0 + rg, :].astype(F32)
        g = g_ref[r0:r0 + rg, :].astype(F32)
        ubuf[halo + r0:halo + r0 + rg, :] = a * jax.nn.sigmoid(g)

    nsh = sh.shape[1]
    for r in range(1, SUBLANES):
        for i0 in range(0, nsh, rg):
            n = min(rg, nsh - i0)
            sh[r - 1, i0:i0 + n, :] = ubuf[i0 + r:i0 + r + n, :]

    off = halo - (kw - 1)
    rc = 128
    nunit = (ts // rc) * (C // LANES)
    every = max(1, nunit // max(1, nmm - QKV_LEAD))
    unit = 0
    for r0 in range(0, ts, rc):
        for c in range(C // LANES):
            cs = slice(c * LANES, (c + 1) * LANES)
            acc = jnp.broadcast_to(b_ref[0:1, cs], (rc // SUBLANES, SUBLANES, LANES))
            for k in range(kw):
                q, r = divmod(off + k, SUBLANES)
                a0 = r0 + q * SUBLANES
                tap = ubuf[a0:a0 + rc, cs] if r == 0 else sh[r - 1, a0:a0 + rc, cs]
                acc = acc + w_ref[k, :, cs] * tap.reshape(rc // SUBLANES, SUBLANES, LANES)
            cbuf[r0:r0 + rc, cs] = acc.reshape(rc, LANES)
            unit += 1
            if unit % every == 0 and nproj < nmm:
                project(nproj, after=acc)
                nproj += 1
        cv = cbuf[r0:r0 + rc, :]
        mu = jnp.mean(cv, axis=-1, keepdims=True)
        d = cv - mu
        var = jnp.mean(d * d, axis=-1, keepdims=True)
        un = d * lax.rsqrt(var + eps) * lw_ref[...] + lb_ref[...]
        o_ref[r0:r0 + rc, :] = (un * jax.nn.sigmoid(un)).astype(o_ref.dtype)
    while nproj < nmm:
        project(nproj)
        nproj += 1


def _conformer_qkv(proj_ag, u, w_bf, col0, dw_w, dw_b, ln_w, ln_b, B, S):
    T = proj_ag.shape[0]
    kw, C = dw_w.shape
    D = w_bf.shape[0]
    N = w_bf.shape[1] - col0
    wcols = 1024
    nwq = N // wcols
    ts = min(256, S)
    halo = 32
    assert kw - 1 <= halo and S % ts == 0 and ts >= halo and N % wcols == 0 and col0 % wcols == 0
    ns = S // ts
    wp = jnp.broadcast_to(dw_w[:, None, :], (kw, SUBLANES, C))
    return pl.pallas_call(
        functools.partial(_conv_qkv_kernel, ts=ts, kw=kw, halo=halo, eps=LN_EPS, nwq=nwq),
        grid=(B, ns),
        in_specs=[pl.BlockSpec((ts, C), lambda b, s: (b * ns + s, 0)),
                  pl.BlockSpec((ts, C), lambda b, s: (b * ns + s, 1)),
                  pl.BlockSpec((kw, SUBLANES, C), lambda b, s: (0, 0, 0)),
                  pl.BlockSpec((1, C), lambda b, s: (0, 0)),
                  pl.BlockSpec((1, C), lambda b, s: (0, 0)),
                  pl.BlockSpec((1, C), lambda b, s: (0, 0)),
                  pl.BlockSpec((ts, D), lambda b, s: (b * ns + s, 0))]
                 + [pl.BlockSpec((D, wcols), functools.partial(lambda b, s, n: (0, col0 // wcols + n), n=n),
                                 pipeline_mode=pl.Buffered(1)) for n in range(nwq)],
        out_specs=[pl.BlockSpec((ts, C), lambda b, s: (b * ns + s, 0)),
                   pl.BlockSpec((ts, N), lambda b, s: (b * ns + s, 0))],
        out_shape=[jax.ShapeDtypeStruct((T, C), BF16), jax.ShapeDtypeStruct((T, N), BF16)],
        scratch_shapes=[pltpu.VMEM((ts + halo, C), F32), pltpu.VMEM((ts, C), F32),
                        pltpu.VMEM((SUBLANES - 1, ts + halo - SUBLANES, C), F32),
                        pltpu.VMEM((ts, D), BF16)],
        compiler_params=_cparams(("arbitrary", "arbitrary"), 48),
        name="conformer_qkv",
    )(proj_ag, proj_ag, wp, dw_b.reshape(1, C), ln_w.reshape(1, C), ln_b.reshape(1, C), u, *([w_bf] * nwq))


def _attn_kernel(lq1_ref, lk1_ref, lq2_ref, lk2_ref, q_ref, k_ref, v_ref, sw_ref, o_ref,
                 vxt_ref, qq_ref, *scr, tq, hd, hp, ones_rows, lam_init, eps):
    i = pl.program_id(2)
    vd = 2 * hd
    nkb = v_ref.shape[0] // tq
    acc_refs, m_refs, sa_refs, sb_refs = (scr[n * hp:(n + 1) * hp] for n in range(4))

    @pl.when(i == 0)
    def _():
        for u in range(hp):
            for jb in range(nkb):
                vblk = v_ref[jb * tq:(jb + 1) * tq, u * vd:(u + 1) * vd].astype(F32)
                vxt_ref[u, jb, 0:vd, :] = vblk.T.astype(vxt_ref.dtype)
                vxt_ref[u, jb, vd:vd + ones_rows, :] = jnp.ones((ones_rows, tq), vxt_ref.dtype)

    lam = (jnp.exp(jnp.sum(lq1_ref[...] * lk1_ref[...], axis=-1, keepdims=True))
           - jnp.exp(jnp.sum(lq2_ref[...] * lk2_ref[...], axis=-1, keepdims=True)) + lam_init)

    for u in range(hp):
        q = q_ref[:, u * vd:(u + 1) * vd]
        qs = q * jnp.asarray(hd ** -0.5, q.dtype)
        lane = lax.broadcasted_iota(jnp.int32, q.shape, 1)
        zero = jnp.zeros_like(qs)
        qq_ref[u, 0:tq, :] = jnp.where(lane < hd, qs, zero)
        qq_ref[u, tq:2 * tq, :] = jnp.where(lane >= hd, qs, zero)
        acc_refs[u][...] = jnp.zeros(acc_refs[u].shape, F32)
        m_refs[u][...] = jnp.full(m_refs[u].shape, NEG_BIG, F32)

    def scores(j, u, dst):
        rows = pl.ds(pl.multiple_of(j * tq, tq), tq)
        kb = k_ref[rows, u * vd:(u + 1) * vd]
        dst[...] = lax.dot_general(kb, qq_ref[u], (((1,), (1,)), ((), ())), preferred_element_type=F32)

    def softmax_pv(j, u, src, masked):
        st = src[...]
        if masked:
            r = lax.broadcasted_iota(jnp.int32, st.shape, 0)
            c = lax.broadcasted_iota(jnp.int32, st.shape, 1)
            st = jnp.where(r <= jnp.where(c >= tq, c - tq, c), st, NEG_BIG)
        m_old = m_refs[u][...]
        m_new = jnp.maximum(m_old, jnp.max(st, axis=0, keepdims=True))
        alpha = jnp.exp(m_old - m_new)
        pt = jnp.exp(st - m_new).astype(vxt_ref.dtype)
        pv = jnp.dot(vxt_ref[u, j], pt, preferred_element_type=F32)
        acc_refs[u][...] = alpha * acc_refs[u][...] + pv
        m_refs[u][...] = m_new

    def half(j, cur, nxt):
        for u in range(hp):
            scores(j + 1, u, nxt[u])
            softmax_pv(j, u, cur[u], False)

    for u in range(hp):
        scores(0, u, sa_refs[u])

    def pair(t, carry):
        half(2 * t, sa_refs, sb_refs)
        half(2 * t + 1, sb_refs, sa_refs)
        return carry

    lax.fori_loop(0, i // 2, pair, 0)

    @pl.when(i % 2 == 1)
    def _():
        half(i - 1, sa_refs, sb_refs)
        for u in range(hp):
            softmax_pv(i, u, sb_refs[u], True)

    @pl.when(i % 2 == 0)
    def _():
        for u in range(hp):
            softmax_pv(i, u, sa_refs[u], True)

    for u in range(hp):
        acc = acc_refs[u][...]
        o12 = acc[0:vd] * (1.0 / acc[vd:vd + 1])
        ot = o12[:, 0:tq] - lam * o12[:, tq:2 * tq]
        msq = jnp.mean(ot * ot, axis=0, keepdims=True)
        o = (ot * lax.rsqrt(msq + eps)).T
        o_ref[:, u * vd:(u + 1) * vd] = (o * sw_ref[...] * (1.0 - lam_init)).astype(o_ref.dtype)


def _diff_attn(proj, lq1, lk1, lq2, lk2, subln_w, B, S, d_conv, d_attn, lam_init):
    T = proj.shape[0]
    vd = subln_w.shape[0]
    hd = lq1.shape[0]
    assert vd == LANES and 2 * hd == vd
    H = d_attn // vd
    hp = 8
    ones_rows = 16
    tq = min(256, S)
    nq = S // tq
    assert H % hp == 0
    qc = 2 * d_conv // (hp * vd)
    kc = qc + H // hp
    vc = kc + H // hp
    lspec = pl.BlockSpec((1, hd), lambda b, h, i: (0, 0))
    return pl.pallas_call(
        functools.partial(_attn_kernel, tq=tq, hd=hd, hp=hp, ones_rows=ones_rows, lam_init=lam_init, eps=SUBLN_EPS),
        grid=(B, H // hp, nq),
        in_specs=[lspec, lspec, lspec, lspec,
                  pl.BlockSpec((tq, hp * vd), lambda b, h, i: (b * nq + i, qc + h)),
                  pl.BlockSpec((S, hp * vd), lambda b, h, i: (b, kc + h)),
                  pl.BlockSpec((S, hp * vd), lambda b, h, i: (b, vc + h)),
                  pl.BlockSpec((1, vd), lambda b, h, i: (0, 0))],
        out_specs=pl.BlockSpec((tq, hp * vd), lambda b, h, i: (b * nq + i, h)),
        out_shape=jax.ShapeDtypeStruct((T, d_attn), BF16),
        scratch_shapes=[pltpu.VMEM((hp, S // tq, vd + ones_rows, tq), BF16),
                        pltpu.VMEM((hp, 2 * tq, vd), BF16)]
                       + [pltpu.VMEM((vd + ones_rows, 2 * tq), F32) for _ in range(hp)]
                       + [pltpu.VMEM((1, 2 * tq), F32) for _ in range(hp)]
                       + [pltpu.VMEM((tq, 2 * tq), F32) for _ in range(2 * hp)],
        compiler_params=_cparams(("arbitrary", "arbitrary", "arbitrary"), 52),
        name="diff_attn",
    )(lq1.reshape(1, hd), lk1.reshape(1, hd), lq2.reshape(1, hd), lk2.reshape(1, hd),
      proj, proj, proj, subln_w.reshape(1, vd))


def _pack_bf16_pairs(v):
    n = v.shape[1] // 2
    return _pack_bf16_words(v[:, 0:n], v[:, n:2 * n])


def _pack_bf16_words(lo, hi):
    lo_bits = lax.bitcast_convert_type(lo.astype(BF16).astype(F32), jnp.uint32)
    hi_bits = lax.bitcast_convert_type(hi.astype(BF16).astype(F32), jnp.uint32)
    return (lo_bits >> 16) | (hi_bits & jnp.uint32(0xFFFF0000))


def _unpack_bf16_pairs(w):
    lo = lax.bitcast_convert_type(w << 16, F32)
    hi = lax.bitcast_convert_type(w & jnp.uint32(0xFFFF0000), F32)
    return lo, hi


def _store_row_tiles(ref, row0, words):
    m, w = words.shape
    assert w == SUBLANES * LANES
    for s in range(SUBLANES):
        ref[pl.ds(row0 * SUBLANES + s, m, stride=SUBLANES), :] = words[:, s * LANES:(s + 1) * LANES]


def _load_row_tiles(ref, row0, m):
    return jnp.concatenate([ref[pl.ds(row0 * SUBLANES + s, m, stride=SUBLANES), :] for s in range(SUBLANES)], axis=1)


def _outproj_kernel(x_ref, yc_ref, ya_ref, wo_ref, fw_ref, wr_ref, br_ref, h_ref, lg_ref, ug_ref, *, eps, nsub):
    dc = yc_ref.shape[1]
    sub = x_ref.shape[0] // nsub
    mixes = []
    for t in range(nsub):
        rows = slice(t * sub, (t + 1) * sub)
        mixes.append(jnp.dot(yc_ref[rows, :], wo_ref[0:dc, :], preferred_element_type=F32)
                     + jnp.dot(ya_ref[rows, :], wo_ref[dc:, :], preferred_element_type=F32))
    for t in range(nsub):
        rows = slice(t * sub, (t + 1) * sub)
        h = x_ref[rows, :] + mixes[t]
        h_ref[rows, :] = h
        ms = jnp.mean(h * h, axis=-1, keepdims=True)
        un = h * lax.rsqrt(ms + eps) * fw_ref[...]
        hi = un.astype(BF16)
        lo = (un - hi.astype(F32)).astype(BF16)
        hh = jnp.dot(hi, wr_ref[...], preferred_element_type=F32)
        lh = jnp.dot(lo, wr_ref[:, 0:LANES], preferred_element_type=F32)
        lg_ref[rows, :] = hh[:, 0:LANES] + hh[:, LANES:2 * LANES] + lh + br_ref[...]
        _store_row_tiles(ug_ref, t * sub, _pack_bf16_pairs(un))


def _outproj(x2, y_conv, y_attn, wo_bf, ffn_w, wr_hl, br):
    T, D = x2.shape
    dc = y_conv.shape[1]
    da = y_attn.shape[1]
    tm = min(512, T)
    nsub = 2 if tm % 512 == 0 else 1
    return pl.pallas_call(
        functools.partial(_outproj_kernel, eps=RMS_EPS, nsub=nsub),
        grid=(T // tm,),
        in_specs=[pl.BlockSpec((tm, D), lambda i: (i, 0)),
                  pl.BlockSpec((tm, dc), lambda i: (i, 0)),
                  pl.BlockSpec((tm, da), lambda i: (i, 0)),
                  pl.BlockSpec((dc + da, D), lambda i: (0, 0), pipeline_mode=pl.Buffered(1)),
                  pl.BlockSpec((1, D), lambda i: (0, 0)),
                  pl.BlockSpec((D, 2 * LANES), lambda i: (0, 0)),
                  pl.BlockSpec((1, LANES), lambda i: (0, 0))],
        out_specs=[pl.BlockSpec((tm, D), lambda i: (i, 0)),
                   pl.BlockSpec((tm, LANES), lambda i: (i, 0)),
                   pl.BlockSpec((tm * SUBLANES, LANES), lambda i: (i, 0))],
        out_shape=[jax.ShapeDtypeStruct((T, D), F32), jax.ShapeDtypeStruct((T, LANES), F32),
                   jax.ShapeDtypeStruct((T * SUBLANES, LANES), jnp.uint32)],
        compiler_params=_cparams(("arbitrary",), 56),
        name="outproj",
    )(x2, y_conv, y_attn, wo_bf, ffn_w, wr_hl, br)


def _route_kernel(lg_ref, info_ref, cols_ref, cnt_ref, *, ng, epg):
    i = pl.program_id(0)
    lg = lg_ref[...]
    tm = lg.shape[0]
    lane = lax.broadcasted_iota(jnp.int32, lg.shape, 1)
    lanef = lane.astype(F32)
    ne = ng * epg

    def first_argmax(vals):
        mx = jnp.max(vals, axis=-1, keepdims=True)
        idx = jnp.min(jnp.where(vals == mx, lanef, float(LANES)), axis=-1, keepdims=True)
        return mx, idx

    gmask = lane < ng
    gl = jnp.where(gmask, lg, NEG_BIG)
    gmax, gsel = first_argmax(gl)
    gsum = jnp.sum(jnp.where(gmask, jnp.exp(gl - gmax), 0.0), axis=-1, keepdims=True)
    g_w = 1.0 / gsum
    lo = gsel * epg + ng
    emask = (lanef >= lo) & (lanef < lo + epg)
    el = jnp.where(emask, lg, NEG_BIG)
    v1, i1 = first_argmax(el)
    el2 = jnp.where(lanef == i1, NEG_BIG, el)
    v2, i2 = first_argmax(el2)
    e2 = jnp.exp(v2 - v1)
    p1 = 1.0 / (1.0 + e2)
    gate1 = g_w * p1
    gate2 = g_w * (e2 * p1)

    oh1 = lanef == i1
    oh2 = lanef == i2
    cmat = jnp.where(oh1 | oh2, 1.0, 0.0).astype(BF16)
    r = lax.broadcasted_iota(jnp.int32, (tm, tm), 0)
    c = lax.broadcasted_iota(jnp.int32, (tm, tm), 1)
    tri = jnp.where(c < r, 1.0, 0.0).astype(BF16)

    @pl.when(i == 0)
    def _():
        cnt_ref[...] = jnp.zeros(cnt_ref.shape, F32)

    carry = cnt_ref[0:1, :]
    prefix = jnp.dot(tri, cmat, preferred_element_type=F32) + carry
    rank1 = jnp.sum(jnp.where(oh1, prefix, 0.0), axis=-1, keepdims=True)
    rank2 = jnp.sum(jnp.where(oh2, prefix, 0.0), axis=-1, keepdims=True)
    cnt_ref[...] = jnp.broadcast_to(carry + jnp.sum(cmat.astype(F32), axis=0, keepdims=True), cnt_ref.shape)

    info = jnp.where(lane == 0, i1 - ng,
           jnp.where(lane == 1, i2 - ng,
           jnp.where(lane == 2, gate1,
           jnp.where(lane == 3, gate2,
           jnp.where(lane == 4, rank1,
           jnp.where(lane == 5, rank2, 0.0))))))
    info_ref[...] = info
    cols_ref[...] = info.T[0:SUBLANES, :]


def _route(logits, ng, epg):
    T = logits.shape[0]
    tm = min(512, T)
    return pl.pallas_call(
        functools.partial(_route_kernel, ng=ng, epg=epg),
        grid=(T // tm,),
        in_specs=[pl.BlockSpec((tm, LANES), lambda i: (i, 0))],
        out_specs=[pl.BlockSpec((tm, LANES), lambda i: (i, 0)),
                   pl.BlockSpec((SUBLANES, tm), lambda i: (0, i)),
                   pl.BlockSpec((8, LANES), lambda i: (0, 0))],
        out_shape=[jax.ShapeDtypeStruct((T, LANES), F32), jax.ShapeDtypeStruct((SUBLANES, T), F32),
                   jax.ShapeDtypeStruct((8, LANES), F32)],
        compiler_params=_cparams(("arbitrary",), 32),
        name="route",
    )(logits)


def _dest_kernel(pstart_ref, cols_ref, dest_ref, *, ne):
    eid = cols_ref[0:TOP_K, :].astype(jnp.int32)
    rank = cols_ref[4:4 + TOP_K, :].astype(jnp.int32)

    def body(e, acc):
        return acc + jnp.where(eid == e, pstart_ref[e], 0)

    start = lax.fori_loop(0, ne, body, jnp.zeros(eid.shape, jnp.int32))
    dest_ref[...] = jnp.zeros(dest_ref.shape, jnp.int32)
    dest_ref[0:TOP_K, :] = start + rank


def _dest(pad_starts, cols):
    R, T = cols.shape
    grid_spec = pltpu.PrefetchScalarGridSpec(
        num_scalar_prefetch=1,
        grid=(1,),
        in_specs=[pl.BlockSpec((R, T), lambda i, p: (0, 0))],
        out_specs=pl.BlockSpec((R, T), lambda i, p: (0, 0)),
    )
    return pl.pallas_call(
        functools.partial(_dest_kernel, ne=pad_starts.shape[0]),
        grid_spec=grid_spec,
        out_shape=jax.ShapeDtypeStruct((R, T), jnp.int32),
        compiler_params=_cparams(("arbitrary",), 16),
        name="dest",
    )(pad_starts, cols)


ROW_UNROLL = 8


def _dispatch_kernel(dest_ref, pend_ref, ug_ref, xs_hbm, ring, zbuf, sems, zsem, *, tm, blk, ne, ntok):
    i = pl.program_id(0)
    n = pl.num_programs(0)
    slot = i % 2

    def seg_tail(e):
        end = pend_ref[e]
        start = jnp.where(e == 0, 0, pend_ref[jnp.maximum(e - 1, 0)])
        tail = pl.multiple_of(jnp.maximum(end - blk, 0) * SUBLANES, blk * SUBLANES)
        return pltpu.make_async_copy(zbuf, xs_hbm.at[pl.ds(tail, blk * SUBLANES)], zsem), end > start

    @pl.when(i == 0)
    def _():
        zbuf[...] = jnp.zeros(zbuf.shape, zbuf.dtype)

        def zstart(p, carry):
            for q in range(2):
                cp, nonempty = seg_tail(2 * p + q)

                @pl.when(nonempty)
                def _():
                    cp.start(priority=q)
            return carry

        def zwait(e, carry):
            cp, nonempty = seg_tail(e)

            @pl.when(nonempty)
            def _():
                cp.wait()
            return carry

        def spare(b):
            row = pl.multiple_of(b * blk, blk)
            tile = pl.multiple_of(row * SUBLANES, blk * SUBLANES)
            return pltpu.make_async_copy(zbuf, xs_hbm.at[pl.ds(tile, blk * SUBLANES)], zsem), row >= pend_ref[ne - 1]

        def sstart(p, carry):
            for q in range(2):
                cp, unused = spare(2 * p + q)

                @pl.when(unused)
                def _():
                    cp.start(priority=q)
            return carry

        def swait(b, carry):
            cp, unused = spare(b)

            @pl.when(unused)
            def _():
                cp.wait()
            return carry

        nblk = xs_hbm.shape[0] // (blk * SUBLANES)
        assert ne % 2 == 0 and nblk % 2 == 0
        lax.fori_loop(0, ne // 2, zstart, 0)
        lax.fori_loop(0, nblk // 2, sstart, 0)
        lax.fori_loop(0, ne, zwait, 0)
        lax.fori_loop(0, nblk, swait, 0)

    def drain(s):
        for _ in range(TOP_K):
            pltpu.make_async_copy(ring.at[s], xs_hbm.at[pl.ds(0, tm * SUBLANES)], sems.at[s]).wait()

    @pl.when(i >= 2)
    def _():
        drain(slot)

    ring[slot] = ug_ref[...]

    def issue(c, carry):
        r0 = pl.multiple_of(c * ROW_UNROLL, ROW_UNROLL)
        for s in range(ROW_UNROLL):
            for k in range(TOP_K):
                d = dest_ref[k * ntok + i * tm + r0 + s]
                src = ring.at[slot, pl.ds(pl.multiple_of((r0 + s) * SUBLANES, SUBLANES), SUBLANES)]
                dst = xs_hbm.at[pl.ds(pl.multiple_of(d * SUBLANES, SUBLANES), SUBLANES)]
                pltpu.make_async_copy(src, dst, sems.at[slot]).start(priority=k % 2)
        return carry

    lax.fori_loop(0, tm // ROW_UNROLL, issue, 0)

    @pl.when(i == n - 1)
    def _():
        drain(slot)

        @pl.when(n >= 2)
        def _():
            drain(1 - slot)


def _dispatch(dest_flat, pad_ends, ug, nrows, tm, blk):
    T = ug.shape[0] // SUBLANES
    ne = pad_ends.shape[0]
    grid_spec = pltpu.PrefetchScalarGridSpec(
        num_scalar_prefetch=2,
        grid=(T // tm,),
        in_specs=[pl.BlockSpec((tm * SUBLANES, LANES), lambda i, d, p: (i, 0))],
        out_specs=pl.BlockSpec(memory_space=pl.ANY),
        scratch_shapes=[pltpu.VMEM((2, tm * SUBLANES, LANES), ug.dtype),
                        pltpu.VMEM((blk * SUBLANES, LANES), ug.dtype),
                        pltpu.SemaphoreType.DMA((2,)),
                        pltpu.SemaphoreType.DMA(())],
    )
    return pl.pallas_call(
        functools.partial(_dispatch_kernel, tm=tm, blk=blk, ne=ne, ntok=T),
        grid_spec=grid_spec,
        out_shape=jax.ShapeDtypeStruct((nrows * SUBLANES, LANES), ug.dtype),
        compiler_params=_cparams(("arbitrary",), 32),
        name="dispatch",
    )(dest_flat, pad_ends, ug)


WEIGHT_DMA_PRIORITY = (0, 1, 1)
WEIGHT_SLOTS = 3


def _experts_kernel(bexp_ref, first_ref, slot_ref, next_ref, lead_ref, nused_ref, x_ref, w1_hbm, w3_hbm, w2_hbm,
                    y_ref, w1b, w3b, w2b, wsems):
    i = pl.program_id(0)
    nused = nused_ref[0]

    def weight_copies(e, s):
        return (pltpu.make_async_copy(w1_hbm.at[e], w1b.at[s], wsems.at[s, 0]),
                pltpu.make_async_copy(w3_hbm.at[e], w3b.at[s], wsems.at[s, 1]),
                pltpu.make_async_copy(w2_hbm.at[e], w2b.at[s], wsems.at[s, 2]))

    def start_weights(e, s):
        for n, cp in enumerate(weight_copies(e, s)):
            cp.start(priority=WEIGHT_DMA_PRIORITY[n])

    @pl.when(i == 0)
    def _():
        for n in range(WEIGHT_SLOTS - 1):
            @pl.when(lead_ref[n] >= 0)
            def _():
                start_weights(lead_ref[n], n)

    @pl.when(i < nused)
    def _():
        s = slot_ref[i]

        @pl.when(first_ref[i] == 1)
        def _():
            @pl.when(next_ref[i] >= 0)
            def _():
                start_weights(next_ref[i], (s + WEIGHT_SLOTS - 1) % WEIGHT_SLOTS)
            for cp in weight_copies(bexp_ref[i], s):
                cp.wait()

        blk = x_ref.shape[0] // SUBLANES
        x_lo, x_hi = _unpack_bf16_pairs(_load_row_tiles(x_ref, 0, blk))
        half = x_lo.shape[1]
        a = (jnp.dot(x_lo, w1b[s, 0:half, :], preferred_element_type=F32)
             + jnp.dot(x_hi, w1b[s, half:2 * half, :], preferred_element_type=F32))
        b = (jnp.dot(x_lo, w3b[s, 0:half, :], preferred_element_type=F32)
             + jnp.dot(x_hi, w3b[s, half:2 * half, :], preferred_element_type=F32))
        hdn = a * jax.nn.sigmoid(a) * b
        cw = 2 * LANES
        for c0 in range(0, half, cw):
            y_lo = jnp.dot(hdn, w2b[s, :, c0:c0 + cw], preferred_element_type=F32)
            y_hi = jnp.dot(hdn, w2b[s, :, half + c0:half + c0 + cw], preferred_element_type=F32)
            words = _pack_bf16_words(y_lo, y_hi)
            for g in range(cw // LANES):
                y_ref[pl.ds(c0 // LANES + g, blk, stride=SUBLANES), :] = words[:, g * LANES:(g + 1) * LANES]

    @pl.when(i >= nused)
    def _():
        y_ref[...] = jnp.zeros(y_ref.shape, y_ref.dtype)


def _experts(bexp, first, slot, nxt, lead, nused, xs, w1, w3, w2, blk):
    P = xs.shape[0] // SUBLANES
    E, D, Fh = w1.shape
    nblk = P // blk
    smap = lambda i, *_: (i, 0)
    xmap = lambda i, be, fi, sl, nx, ld, nu: (jnp.minimum(i, nu[0] - 1), 0)
    grid_spec = pltpu.PrefetchScalarGridSpec(
        num_scalar_prefetch=6,
        grid=(nblk,),
        in_specs=[pl.BlockSpec((blk * SUBLANES, LANES), xmap),
                  pl.BlockSpec(memory_space=pl.ANY),
                  pl.BlockSpec(memory_space=pl.ANY),
                  pl.BlockSpec(memory_space=pl.ANY)],
        out_specs=pl.BlockSpec((blk * SUBLANES, LANES), smap),
        scratch_shapes=[pltpu.VMEM((WEIGHT_SLOTS, D, Fh), w1.dtype),
                        pltpu.VMEM((WEIGHT_SLOTS, D, Fh), w3.dtype),
                        pltpu.VMEM((WEIGHT_SLOTS, Fh, D), w2.dtype),
                        pltpu.SemaphoreType.DMA((WEIGHT_SLOTS, 3))],
    )
    return pl.pallas_call(
        _experts_kernel,
        grid_spec=grid_spec,
        out_shape=jax.ShapeDtypeStruct(xs.shape, xs.dtype),
        compiler_params=_cparams(("arbitrary",), 56),
        name="experts",
    )(bexp, first, slot, nxt, lead, nused, xs, w1, w3, w2)


def _row_gather(src_hbm, idx_ref, base, dst, sem, n):
    def body(c, carry):
        r0 = pl.multiple_of(c * ROW_UNROLL, ROW_UNROLL)
        for s in range(ROW_UNROLL):
            d = idx_ref[base + r0 + s]
            pltpu.make_async_copy(src_hbm.at[pl.ds(pl.multiple_of(d * SUBLANES, SUBLANES), SUBLANES)],
                                  dst.at[pl.ds(pl.multiple_of((r0 + s) * SUBLANES, SUBLANES), SUBLANES)],
                                  sem).start(priority=s % 2)
        return carry
    lax.fori_loop(0, n // ROW_UNROLL, body, 0)


def _row_gather_wait(src_hbm, dst, sem, n):
    pltpu.make_async_copy(src_hbm.at[pl.ds(0, n * SUBLANES)], dst, sem).wait()


def _combine_kernel(dest_ref, h_ref, info_ref, y_hbm, fw_ref, o_ref, ybuf, sems, *, tm, eps, ntok):
    i = pl.program_id(0)
    n = pl.num_programs(0)
    slot = i % 2
    nrow = TOP_K * tm

    def gather_step(step, s):
        for k in range(TOP_K):
            _row_gather(y_hbm, dest_ref, k * ntok + step * tm,
                        ybuf.at[s, pl.ds(k * tm * SUBLANES, tm * SUBLANES)], sems.at[s], tm)

    @pl.when(i == 0)
    def _():
        gather_step(0, 0)

    @pl.when(i + 1 < n)
    def _():
        gather_step(i + 1, 1 - slot)

    _row_gather_wait(y_hbm, ybuf.at[slot], sems.at[slot], nrow)
    half = o_ref.shape[1] // 2
    rc = tm

    def rows(ci, carry):
        r0 = pl.multiple_of(ci * rc, rc)
        rs = pl.ds(r0, rc)
        info = info_ref[rs, :]
        g1 = info[:, 2:3]
        g2 = info[:, 3:4]
        y1_lo, y1_hi = _unpack_bf16_pairs(_load_row_tiles(ybuf.at[slot], r0, rc))
        y2_lo, y2_hi = _unpack_bf16_pairs(_load_row_tiles(ybuf.at[slot], tm + r0, rc))
        h_lo = h_ref[rs, 0:half] + (g1 * y1_lo + g2 * y2_lo)
        h_hi = h_ref[rs, half:2 * half] + (g1 * y1_hi + g2 * y2_hi)
        ms = (jnp.sum(h_lo * h_lo, axis=-1, keepdims=True)
              + jnp.sum(h_hi * h_hi, axis=-1, keepdims=True)) / (2 * half)
        r = lax.rsqrt(ms + eps)
        o_ref[rs, 0:half] = h_lo * r * fw_ref[:, 0:half]
        o_ref[rs, half:2 * half] = h_hi * r * fw_ref[:, half:2 * half]
        return carry

    lax.fori_loop(0, tm // rc, rows, 0)


def _combine(dest, h, info, yb, final_w, tm):
    T, D = h.shape
    grid_spec = pltpu.PrefetchScalarGridSpec(
        num_scalar_prefetch=1,
        grid=(T // tm,),
        in_specs=[pl.BlockSpec((tm, D), lambda i, d: (i, 0)),
                  pl.BlockSpec((tm, LANES), lambda i, d: (i, 0)),
                  pl.BlockSpec(memory_space=pl.ANY),
                  pl.BlockSpec((1, D), lambda i, d: (0, 0))],
        out_specs=pl.BlockSpec((tm, D), lambda i, d: (i, 0)),
        scratch_shapes=[pltpu.VMEM((2, TOP_K * tm * SUBLANES, LANES), yb.dtype),
                        pltpu.SemaphoreType.DMA((2,))],
    )
    return pl.pallas_call(
        functools.partial(_combine_kernel, tm=tm, eps=RMS_EPS, ntok=T),
        grid_spec=grid_spec,
        out_shape=jax.ShapeDtypeStruct((T, D), F32),
        compiler_params=_cparams(("arbitrary",), 48),
        name="combine",
    )(dest, h, info, yb, final_w)


def _layer(h_in, l, B, S, mix_norm_w, w_in, conv_dw_w, conv_dw_b, conv_ln_w, conv_ln_b,
           lam_q1, lam_k1, lam_q2, lam_k2, attn_subln_w, w_out, ffn_norm_w,
           w_group, b_group, w_expert_gate, b_expert_gate, w1, w3, w2):
    T, D = h_in.shape
    d_conv = conv_dw_w.shape[1]
    d_attn = (w_in.shape[1] - 2 * d_conv) // 3
    ng = w_group.shape[1]
    ne = w_expert_gate.shape[1]
    epg = ne // ng
    assert ng + ne <= LANES
    lam_init = 0.8 - 0.6 * math.exp(-0.3 * l)

    w_bf = w_in.astype(BF16)
    proj_ag, u = _norm_inproj(h_in, mix_norm_w.reshape(1, D), w_bf, 2 * d_conv)
    y_conv, qkv = _conformer_qkv(proj_ag, u, w_bf, 2 * d_conv, conv_dw_w, conv_dw_b, conv_ln_w, conv_ln_b, B, S)
    y_attn = _diff_attn(qkv, lam_q1, lam_k1, lam_q2, lam_k2, attn_subln_w, B, S, 0, d_attn, lam_init)

    wr = jnp.concatenate([w_group, w_expert_gate, jnp.zeros((D, LANES - ng - ne), F32)], axis=1)
    wr_hi = wr.astype(BF16)
    wr_lo = (wr - wr_hi.astype(F32)).astype(BF16)
    br = jnp.concatenate([b_group, b_expert_gate.reshape(-1), jnp.zeros((LANES - ng - ne,), F32)]).reshape(1, LANES)
    wr_hl = jnp.concatenate([wr_hi, wr_lo], axis=1)
    h, logits, ug = _outproj(h_in, y_conv, y_attn, w_out.astype(BF16), ffn_norm_w.reshape(1, D), wr_hl, br)

    info, cols, cnt = _route(logits, ng, epg)

    blk = 256
    tmd = min(256, T)
    A = T * TOP_K
    nblk = (A + ne * (blk - 1) + blk - 1) // blk
    i32 = jnp.int32
    counts = cnt[0, ng:ng + ne].astype(i32)
    padded = (counts + blk - 1) // blk * blk
    pad_ends = jnp.cumsum(padded).astype(i32)
    pad_starts = pad_ends - padded
    dest = _dest(pad_starts, cols)[0:TOP_K].reshape(-1)
    nused = (pad_ends[-1] // blk).astype(i32)
    bpos = jnp.arange(nblk, dtype=i32)
    brow = jnp.minimum(bpos, nused - 1) * blk
    bexp = jnp.minimum(jnp.sum((pad_ends[None, :] <= brow[:, None]).astype(i32), axis=1), ne - 1)
    first = ((bpos < nused) & ((bpos == 0) | (bexp != jnp.roll(bexp, 1)))).astype(i32)
    slot = ((jnp.cumsum(first) - 1) % WEIGHT_SLOTS).astype(i32)
    used_idx = jnp.where(padded > 0, jnp.arange(ne, dtype=i32), ne)
    suffix_min = lax.cummin(used_idx, reverse=True)
    next_used = jnp.concatenate([suffix_min[1:], jnp.full((2,), ne, i32)])
    ahead = jnp.arange(ne, dtype=i32)
    for _ in range(WEIGHT_SLOTS - 1):
        ahead = next_used[ahead]
    ahead = jnp.where(ahead >= ne, -1, ahead)
    nxt = ahead[bexp].astype(i32)
    lead = [suffix_min[0]]
    for _ in range(WEIGHT_SLOTS - 2):
        lead.append(next_used[lead[-1]])
    lead = jnp.stack([jnp.where(e >= ne, -1, e) for e in lead]).astype(i32)

    xs = _dispatch(dest, pad_ends, ug, nblk * blk, tmd, blk)
    yb = _experts(bexp, first, slot, nxt, lead, nused.reshape(1), xs, w1, w3, w2, blk)
    return h, info, dest, yb


def kernel(x, mix_norm_w, w_in, conv_dw_w, conv_dw_b, conv_ln_w, conv_ln_b, lam_q1, lam_k1, lam_q2, lam_k2,
           attn_subln_w, w_out, ffn_norm_w, w_group, b_group, w_expert_gate, b_expert_gate, w1, w3, w2,
           final_norm_w):
    B, S, D = x.shape
    depth = w_in.shape[0]
    assert depth == 1
    T = B * S
    tmc = min(256, T)
    h = x.reshape(T, D)
    for l in range(depth):
        h, info, dest, yb = _layer(
            h, l, B, S, mix_norm_w[l], w_in[l], conv_dw_w[l], conv_dw_b[l], conv_ln_w[l], conv_ln_b[l],
            lam_q1[l], lam_k1[l], lam_q2[l], lam_k2[l], attn_subln_w[l], w_out[l], ffn_norm_w[l],
            w_group[l], b_group[l], w_expert_gate[l], b_expert_gate[l], w1[l], w3[l], w2[l])
        h = _combine(dest, h, info, yb, final_norm_w.reshape(1, D), tmc)
    return h.reshape(B, S, D)
```

```python
import functools
import math

import jax
import jax.numpy as jnp
from jax import lax
from jax.experimental import pallas as pl
from jax.experimental.pallas import tpu as pltpu

F32 = jnp.float32
BF16 = jnp.bfloat16

RMS_EPS = 1e-6
SUBLN_EPS = 1e-5
LN_EPS = 1e-5
TOP_K = 2
LANES = 128
SUBLANES = 8
NEG_BIG = -1e30
MIB = 1024 * 1024


def _cparams(sem, vmem_mib):
    return pltpu.CompilerParams(dimension_semantics=sem, vmem_limit_bytes=vmem_mib * MIB)


def _norm_inproj_kernel(x_ref, nw_ref, w_ref, o_ref, u_ref, *, eps, rc):
    @pl.when(pl.program_id(1) == 0)
    def _():
        for r0 in range(0, x_ref.shape[0], rc):
            rows = slice(r0, r0 + rc)
            x = x_ref[rows, :]
            ms = jnp.mean(x * x, axis=-1, keepdims=True)
            u = (x * lax.rsqrt(ms + eps) * nw_ref[...]).astype(u_ref.dtype)
            u_ref[rows, :] = u
            o_ref[rows, :] = jnp.dot(u, w_ref[...], preferred_element_type=F32).astype(o_ref.dtype)

    @pl.when(pl.program_id(1) > 0)
    def _():
        o_ref[...] = jnp.dot(u_ref[...], w_ref[...], preferred_element_type=F32).astype(o_ref.dtype)


def _norm_inproj(x2, nw, w_bf, ncols):
    T, D = x2.shape
    N = ncols
    tm = min(1024, T)
    tn = 1024
    return pl.pallas_call(
        functools.partial(_norm_inproj_kernel, eps=RMS_EPS, rc=256),
        grid=(T // tm, N // tn),
        in_specs=[pl.BlockSpec((tm, D), lambda i, j: (i, 0)),
                  pl.BlockSpec((1, D), lambda i, j: (0, 0)),
                  pl.BlockSpec((D, tn), lambda i, j: (0, j))],
        out_specs=[pl.BlockSpec((tm, tn), lambda i, j: (i, j)),
                   pl.BlockSpec((tm, D), lambda i, j: (i, 0))],
        out_shape=[jax.ShapeDtypeStruct((T, N), BF16), jax.ShapeDtypeStruct((T, D), BF16)],
        compiler_params=_cparams(("arbitrary", "arbitrary"), 48),
        name="norm_inproj",
    )(x2, nw, w_bf)


QKV_CHUNK = 2 * LANES
QKV_KSTEP = 4 * LANES


def _conv_qkv_kernel(a_ref, g_ref, w_ref, b_ref, lw_ref, lb_ref, u_ref, *rest, ts, kw, halo, eps, nwq):
    wq_refs = rest[0:nwq]
    o_ref, qkv_ref, ubuf, cbuf, sh, lhs = rest[nwq:]
    s = pl.program_id(1)
    C = a_ref.shape[1]
    wcols = wq_refs[0].shape[1]
    nmm = nwq * wcols // QKV_CHUNK

    @pl.when(s == 0)
    def _():
        ubuf[0:halo, :] = jnp.zeros((halo, C), F32)

    @pl.when(s > 0)
    def _():
        ubuf[0:halo, :] = ubuf[ts:ts + halo, :]

    lhs[...] = u_ref[...]

    nks = lhs.shape[1] // QKV_KSTEP
    pending = {"n": 0, "k": 0, "acc": None}

    def piece(after=None):
        n, k = pending["n"], pending["k"]
        if n >= nmm:
            return
        ks = slice(k * QKV_KSTEP, (k + 1) * QKV_KSTEP)
        if after is not None:
            allrows = jnp.sum(after, axis=0)
            bits = lax.bitcast_convert_type(jnp.concatenate([allrows, allrows], axis=0), jnp.uint32)
            zero = lax.bitcast_convert_type((bits >> 16) >> 16, F32).astype(lhs.dtype)
            first = (slice(0, 2 * SUBLANES), slice(k * QKV_KSTEP, k * QKV_KSTEP + LANES))
            lhs[first] = lhs[first] + zero
        wq_ref = wq_refs[n * QKV_CHUNK // wcols]
        ws = slice(n * QKV_CHUNK % wcols, n * QKV_CHUNK % wcols + QKV_CHUNK)
        part = jnp.dot(lhs[:, ks], wq_ref[ks, ws], preferred_element_type=F32)
        pending["acc"] = part if k == 0 else pending["acc"] + part
        if k == nks - 1:
            qkv_ref[:, n * QKV_CHUNK:(n + 1) * QKV_CHUNK] = pending["acc"].astype(qkv_ref.dtype)
            pending["n"], pending["k"] = n + 1, 0
        else:
            pending["k"] = k + 1

    rg = 64
    for r0 in range(0, ts, rg):
        a = a_ref[r0:r0 + rg, :].astype(F32)
        g = g_ref[r0:r0 + rg, :].astype(F32)
        ubuf[halo + r0:halo + r0 + rg, :] = a * jax.nn.sigmoid(g)

    nsh = sh.shape[1]
    for r in range(1, SUBLANES):
        for i0 in range(0, nsh, rg):
            n = min(rg, nsh - i0)
            sh[r - 1, i0:i0 + n, :] = ubuf[i0 + r:i0 + r + n, :]

    off = halo - (kw - 1)
    rc = 128
    ntap = (ts // rc) * (C // LANES) * kw
    tap_every = max(1, ntap // (nmm * nks))
    tapno = 0
    for r0 in range(0, ts, rc):
        for c in range(C // LANES):
            cs = slice(c * LANES, (c + 1) * LANES)
            acc = jnp.broadcast_to(b_ref[0:1, cs], (rc // SUBLANES, SUBLANES, LANES))
            for k in range(kw):
                q, r = divmod(off + k, SUBLANES)
                a0 = r0 + q * SUBLANES
                tap = ubuf[a0:a0 + rc, cs] if r == 0 else sh[r - 1, a0:a0 + rc, cs]
                acc = acc + w_ref[k, :, cs] * tap.reshape(rc // SUBLANES, SUBLANES, LANES)
                tapno += 1
                if tapno % tap_every == 0:
                    piece(after=acc)
            cbuf[r0:r0 + rc, cs] = acc.reshape(rc, LANES)
        cv = cbuf[r0:r0 + rc, :]
        mu = jnp.mean(cv, axis=-1, keepdims=True)
        d = cv - mu
        var = jnp.mean(d * d, axis=-1, keepdims=True)
        un = d * lax.rsqrt(var + eps) * lw_ref[...] + lb_ref[...]
        o_ref[r0:r0 + rc, :] = (un * jax.nn.sigmoid(un)).astype(o_ref.dtype)
    while pending["n"] < nmm:
        piece()


def _conformer_qkv(proj_ag, u, w_bf, col0, dw_w, dw_b, ln_w, ln_b, B, S):
    T = proj_ag.shape[0]
    kw, C = dw_w.shape
    D = w_bf.shape[0]
    N = w_bf.shape[1] - col0
    wcols = 1024
    nwq = N // wcols
    ts = min(256, S)
    halo = 32
    assert kw - 1 <= halo and S % ts == 0 and ts >= halo and N % wcols == 0 and col0 % wcols == 0
    ns = S // ts
    wp = jnp.broadcast_to(dw_w[:, None, :], (kw, SUBLANES, C))
    return pl.pallas_call(
        functools.partial(_conv_qkv_kernel, ts=ts, kw=kw, halo=halo, eps=LN_EPS, nwq=nwq),
        grid=(B, ns),
        in_specs=[pl.BlockSpec((ts, C), lambda b, s: (b * ns + s, 0)),
                  pl.BlockSpec((ts, C), lambda b, s: (b * ns + s, 1)),
                  pl.BlockSpec((kw, SUBLANES, C), lambda b, s: (0, 0, 0)),
                  pl.BlockSpec((1, C), lambda b, s: (0, 0)),
                  pl.BlockSpec((1, C), lambda b, s: (0, 0)),
                  pl.BlockSpec((1, C), lambda b, s: (0, 0)),
                  pl.BlockSpec((ts, D), lambda b, s: (b * ns + s, 0))]
                 + [pl.BlockSpec((D, wcols), functools.partial(lambda b, s, n: (0, col0 // wcols + n), n=n),
                                 pipeline_mode=pl.Buffered(1)) for n in range(nwq)],
        out_specs=[pl.BlockSpec((ts, C), lambda b, s: (b * ns + s, 0)),
                   pl.BlockSpec((ts, N), lambda b, s: (b * ns + s, 0))],
        out_shape=[jax.ShapeDtypeStruct((T, C), BF16), jax.ShapeDtypeStruct((T, N), BF16)],
        scratch_shapes=[pltpu.VMEM((ts + halo, C), F32), pltpu.VMEM((ts, C), F32),
                        pltpu.VMEM((SUBLANES - 1, ts + halo - SUBLANES, C), F32),
                        pltpu.VMEM((ts, D), BF16)],
        compiler_params=_cparams(("arbitrary", "arbitrary"), 48),
        name="conformer_qkv",
    )(proj_ag, proj_ag, wp, dw_b.reshape(1, C), ln_w.reshape(1, C), ln_b.reshape(1, C), u, *([w_bf] * nwq))


def _attn_kernel(lq1_ref, lk1_ref, lq2_ref, lk2_ref, q_ref, k_ref, v_ref, sw_ref, o_ref,
                 vxt_ref, qq_ref, *scr, tq, hd, hp, ones_rows, lam_init, eps):
    i = pl.program_id(2)
    vd = 2 * hd
    nkb = v_ref.shape[0] // tq
    acc_refs, m_refs, sa_refs, sb_refs = (scr[n * hp:(n + 1) * hp] for n in range(4))

    @pl.when(i == 0)
    def _():
        for u in range(hp):
            for jb in range(nkb):
                vblk = v_ref[jb * tq:(jb + 1) * tq, u * vd:(u + 1) * vd].astype(F32)
                vxt_ref[u, jb, 0:vd, :] = vblk.T.astype(vxt_ref.dtype)
                vxt_ref[u, jb, vd:vd + ones_rows, :] = jnp.ones((ones_rows, tq), vxt_ref.dtype)

    lam = (jnp.exp(jnp.sum(lq1_ref[...] * lk1_ref[...], axis=-1, keepdims=True))
           - jnp.exp(jnp.sum(lq2_ref[...] * lk2_ref[...], axis=-1, keepdims=True)) + lam_init)

    for u in range(hp):
        q = q_ref[:, u * vd:(u + 1) * vd]
        qs = q * jnp.asarray(hd ** -0.5, q.dtype)
        lane = lax.broadcasted_iota(jnp.int32, q.shape, 1)
        zero = jnp.zeros_like(qs)
        qq_ref[u, 0:tq, :] = jnp.where(lane < hd, qs, zero)
        qq_ref[u, tq:2 * tq, :] = jnp.where(lane >= hd, qs, zero)
        acc_refs[u][...] = jnp.zeros(acc_refs[u].shape, F32)
        m_refs[u][...] = jnp.full(m_refs[u].shape, NEG_BIG, F32)

    def scores(j, u, dst):
        rows = pl.ds(pl.multiple_of(j * tq, tq), tq)
        kb = k_ref[rows, u * vd:(u + 1) * vd]
        dst[...] = lax.dot_general(kb, qq_ref[u], (((1,), (1,)), ((), ())), preferred_element_type=F32)

    def softmax_pv(j, u, src, masked):
        st = src[...]
        if masked:
            r = lax.broadcasted_iota(jnp.int32, st.shape, 0)
            c = lax.broadcasted_iota(jnp.int32, st.shape, 1)
            st = jnp.where(r <= jnp.where(c >= tq, c - tq, c), st, NEG_BIG)
        m_old = m_refs[u][...]
        m_new = jnp.maximum(m_old, jnp.max(st, axis=0, keepdims=True))
        alpha = jnp.exp(m_old - m_new)
        pt = jnp.exp(st - m_new).astype(vxt_ref.dtype)
        pv = jnp.dot(vxt_ref[u, j], pt, preferred_element_type=F32)
        acc_refs[u][...] = alpha * acc_refs[u][...] + pv
        m_refs[u][...] = m_new

    def half(j, cur, nxt):
        for u in range(hp):
            scores(j + 1, u, nxt[u])
            softmax_pv(j, u, cur[u], False)

    for u in range(hp):
        scores(0, u, sa_refs[u])

    def pair(t, carry):
        half(2 * t, sa_refs, sb_refs)
        half(2 * t + 1, sb_refs, sa_refs)
        return carry

    lax.fori_loop(0, i // 2, pair, 0)

    @pl.when(i % 2 == 1)
    def _():
        half(i - 1, sa_refs, sb_refs)
        for u in range(hp):
            softmax_pv(i, u, sb_refs[u], True)

    @pl.when(i % 2 == 0)
    def _():
        for u in range(hp):
            softmax_pv(i, u, sa_refs[u], True)

    for u in range(hp):
        acc = acc_refs[u][...]
        o12 = acc[0:vd] * (1.0 / acc[vd:vd + 1])
        ot = o12[:, 0:tq] - lam * o12[:, tq:2 * tq]
        msq = jnp.mean(ot * ot, axis=0, keepdims=True)
        o = (ot * lax.rsqrt(msq + eps)).T
        o_ref[:, u * vd:(u + 1) * vd] = (o * sw_ref[...] * (1.0 - lam_init)).astype(o_ref.dtype)


def _diff_attn(proj, lq1, lk1, lq2, lk2, subln_w, B, S, d_conv, d_attn, lam_init):
    T = proj.shape[0]
    vd = subln_w.shape[0]
    hd = lq1.shape[0]
    assert vd == LANES and 2 * hd == vd
    H = d_attn // vd
    hp = 8
    ones_rows = 16
    tq = min(256, S)
    nq = S // tq
    assert H % hp == 0
    qc = 2 * d_conv // (hp * vd)
    kc = qc + H // hp
    vc = kc + H // hp
    lspec = pl.BlockSpec((1, hd), lambda b, h, i: (0, 0))
    return pl.pallas_call(
        functools.partial(_attn_kernel, tq=tq, hd=hd, hp=hp, ones_rows=ones_rows, lam_init=lam_init, eps=SUBLN_EPS),
        grid=(B, H // hp, nq),
        in_specs=[lspec, lspec, lspec, lspec,
                  pl.BlockSpec((tq, hp * vd), lambda b, h, i: (b * nq + i, qc + h)),
                  pl.BlockSpec((S, hp * vd), lambda b, h, i: (b, kc + h)),
                  pl.BlockSpec((S, hp * vd), lambda b, h, i: (b, vc + h)),
                  pl.BlockSpec((1, vd), lambda b, h, i: (0, 0))],
        out_specs=pl.BlockSpec((tq, hp * vd), lambda b, h, i: (b * nq + i, h)),
        out_shape=jax.ShapeDtypeStruct((T, d_attn), BF16),
        scratch_shapes=[pltpu.VMEM((hp, S // tq, vd + ones_rows, tq), BF16),
                        pltpu.VMEM((hp, 2 * tq, vd), BF16)]
                       + [pltpu.VMEM((vd + ones_rows, 2 * tq), F32) for _ in range(hp)]
                       + [pltpu.VMEM((1, 2 * tq), F32) for _ in range(hp)]
                       + [pltpu.VMEM((tq, 2 * tq), F32) for _ in range(2 * hp)],
        compiler_params=_cparams(("arbitrary", "arbitrary", "arbitrary"), 52),
        name="diff_attn",
    )(lq1.reshape(1, hd), lk1.reshape(1, hd), lq2.reshape(1, hd), lk2.reshape(1, hd),
      proj, proj, proj, subln_w.reshape(1, vd))


def _pack_bf16_pairs(v):
    n = v.shape[1] // 2
    return _pack_bf16_words(v[:, 0:n], v[:, n:2 * n])


def _pack_bf16_words(lo, hi):
    lo_bits = lax.bitcast_convert_type(lo.astype(BF16).astype(F32), jnp.uint32)
    hi_bits = lax.bitcast_convert_type(hi.astype(BF16).astype(F32), jnp.uint32)
    return (lo_bits >> 16) | (hi_bits & jnp.uint32(0xFFFF0000))


def _unpack_bf16_pairs(w):
    lo = lax.bitcast_convert_type(w << 16, F32)
    hi = lax.bitcast_convert_type(w & jnp.uint32(0xFFFF0000), F32)
    return lo, hi


def _store_row_tiles(ref, row0, words):
    m, w = words.shape
    assert w == SUBLANES * LANES
    for s in range(SUBLANES):
        ref[pl.ds(row0 * SUBLANES + s, m, stride=SUBLANES), :] = words[:, s * LANES:(s + 1) * LANES]


def _load_row_tiles(ref, row0, m):
    return jnp.concatenate([ref[pl.ds(row0 * SUBLANES + s, m, stride=SUBLANES), :] for s in range(SUBLANES)], axis=1)


def _outproj_kernel(x_ref, yc_ref, ya_ref, wo_ref, fw_ref, wr_ref, br_ref, h_ref, lg_ref, ug_ref, *, eps, nsub):
    dc = yc_ref.shape[1]
    sub = x_ref.shape[0] // nsub
    mixes = []
    for t in range(nsub):
        rows = slice(t * sub, (t + 1) * sub)
        mixes.append(jnp.dot(yc_ref[rows, :], wo_ref[0:dc, :], preferred_element_type=F32)
                     + jnp.dot(ya_ref[rows, :], wo_ref[dc:, :], preferred_element_type=F32))
    for t in range(nsub):
        rows = slice(t * sub, (t + 1) * sub)
        h = x_ref[rows, :] + mixes[t]
        h_ref[rows, :] = h
        ms = jnp.mean(h * h, axis=-1, keepdims=True)
        un = h * lax.rsqrt(ms + eps) * fw_ref[...]
        hi = un.astype(BF16)
        lo = (un - hi.astype(F32)).astype(BF16)
        hh = jnp.dot(hi, wr_ref[...], preferred_element_type=F32)
        lh = jnp.dot(lo, wr_ref[:, 0:LANES], preferred_element_type=F32)
        lg_ref[rows, :] = hh[:, 0:LANES] + hh[:, LANES:2 * LANES] + lh + br_ref[...]
        _store_row_tiles(ug_ref, t * sub, _pack_bf16_pairs(un))


def _outproj(x2, y_conv, y_attn, wo_bf, ffn_w, wr_hl, br):
    T, D = x2.shape
    dc = y_conv.shape[1]
    da = y_attn.shape[1]
    tm = min(512, T)
    nsub = 2 if tm % 512 == 0 else 1
    return pl.pallas_call(
        functools.partial(_outproj_kernel, eps=RMS_EPS, nsub=nsub),
        grid=(T // tm,),
        in_specs=[pl.BlockSpec((tm, D), lambda i: (i, 0)),
                  pl.BlockSpec((tm, dc), lambda i: (i, 0)),
                  pl.BlockSpec((tm, da), lambda i: (i, 0)),
                  pl.BlockSpec((dc + da, D), lambda i: (0, 0), pipeline_mode=pl.Buffered(1)),
                  pl.BlockSpec((1, D), lambda i: (0, 0)),
                  pl.BlockSpec((D, 2 * LANES), lambda i: (0, 0)),
                  pl.BlockSpec((1, LANES), lambda i: (0, 0))],
        out_specs=[pl.BlockSpec((tm, D), lambda i: (i, 0)),
                   pl.BlockSpec((tm, LANES), lambda i: (i, 0)),
                   pl.BlockSpec((tm * SUBLANES, LANES), lambda i: (i, 0))],
        out_shape=[jax.ShapeDtypeStruct((T, D), F32), jax.ShapeDtypeStruct((T, LANES), F32),
                   jax.ShapeDtypeStruct((T * SUBLANES, LANES), jnp.uint32)],
        compiler_params=_cparams(("arbitrary",), 56),
        name="outproj",
    )(x2, y_conv, y_attn, wo_bf, ffn_w, wr_hl, br)


def _route_kernel(lg_ref, info_ref, cols_ref, cnt_ref, *, ng, epg):
    i = pl.program_id(0)
    lg = lg_ref[...]
    tm = lg.shape[0]
    lane = lax.broadcasted_iota(jnp.int32, lg.shape, 1)
    lanef = lane.astype(F32)
    ne = ng * epg

    def first_argmax(vals):
        mx = jnp.max(vals, axis=-1, keepdims=True)
        idx = jnp.min(jnp.where(vals == mx, lanef, float(LANES)), axis=-1, keepdims=True)
        return mx, idx

    gmask = lane < ng
    gl = jnp.where(gmask, lg, NEG_BIG)
    gmax, gsel = first_argmax(gl)
    gsum = jnp.sum(jnp.where(gmask, jnp.exp(gl - gmax), 0.0), axis=-1, keepdims=True)
    g_w = 1.0 / gsum
    lo = gsel * epg + ng
    emask = (lanef >= lo) & (lanef < lo + epg)
    el = jnp.where(emask, lg, NEG_BIG)
    v1, i1 = first_argmax(el)
    el2 = jnp.where(lanef == i1, NEG_BIG, el)
    v2, i2 = first_argmax(el2)
    e2 = jnp.exp(v2 - v1)
    p1 = 1.0 / (1.0 + e2)
    gate1 = g_w * p1
    gate2 = g_w * (e2 * p1)

    oh1 = lanef == i1
    oh2 = lanef == i2
    cmat = jnp.where(oh1 | oh2, 1.0, 0.0).astype(BF16)
    r = lax.broadcasted_iota(jnp.int32, (tm, tm), 0)
    c = lax.broadcasted_iota(jnp.int32, (tm, tm), 1)
    tri = jnp.where(c < r, 1.0, 0.0).astype(BF16)

    @pl.when(i == 0)
    def _():
        cnt_ref[...] = jnp.zeros(cnt_ref.shape, F32)

    carry = cnt_ref[0:1, :]
    prefix = jnp.dot(tri, cmat, preferred_element_type=F32) + carry
    rank1 = jnp.sum(jnp.where(oh1, prefix, 0.0), axis=-1, keepdims=True)
    rank2 = jnp.sum(jnp.where(oh2, prefix, 0.0), axis=-1, keepdims=True)
    cnt_ref[...] = jnp.broadcast_to(carry + jnp.sum(cmat.astype(F32), axis=0, keepdims=True), cnt_ref.shape)

    info = jnp.where(lane == 0, i1 - ng,
           jnp.where(lane == 1, i2 - ng,
           jnp.where(lane == 2, gate1,
           jnp.where(lane == 3, gate2,
           jnp.where(lane == 4, rank1,
           jnp.where(lane == 5, rank2, 0.0))))))
    info_ref[...] = info
    cols_ref[...] = info.T[0:SUBLANES, :]


def _route(logits, ng, epg):
    T = logits.shape[0]
    tm = min(512, T)
    return pl.pallas_call(
        functools.partial(_route_kernel, ng=ng, epg=epg),
        grid=(T // tm,),
        in_specs=[pl.BlockSpec((tm, LANES), lambda i: (i, 0))],
        out_specs=[pl.BlockSpec((tm, LANES), lambda i: (i, 0)),
                   pl.BlockSpec((SUBLANES, tm), lambda i: (0, i)),
                   pl.BlockSpec((8, LANES), lambda i: (0, 0))],
        out_shape=[jax.ShapeDtypeStruct((T, LANES), F32), jax.ShapeDtypeStruct((SUBLANES, T), F32),
                   jax.ShapeDtypeStruct((8, LANES), F32)],
        compiler_params=_cparams(("arbitrary",), 32),
        name="route",
    )(logits)


def _dest_kernel(pstart_ref, cols_ref, dest_ref, *, ne):
    eid = cols_ref[0:TOP_K, :].astype(jnp.int32)
    rank = cols_ref[4:4 + TOP_K, :].astype(jnp.int32)

    def body(e, acc):
        return acc + jnp.where(eid == e, pstart_ref[e], 0)

    start = lax.fori_loop(0, ne, body, jnp.zeros(eid.shape, jnp.int32))
    dest_ref[...] = jnp.zeros(dest_ref.shape, jnp.int32)
    dest_ref[0:TOP_K, :] = start + rank


def _dest(pad_starts, cols):
    R, T = cols.shape
    grid_spec = pltpu.PrefetchScalarGridSpec(
        num_scalar_prefetch=1,
        grid=(1,),
        in_specs=[pl.BlockSpec((R, T), lambda i, p: (0, 0))],
        out_specs=pl.BlockSpec((R, T), lambda i, p: (0, 0)),
    )
    return pl.pallas_call(
        functools.partial(_dest_kernel, ne=pad_starts.shape[0]),
        grid_spec=grid_spec,
        out_shape=jax.ShapeDtypeStruct((R, T), jnp.int32),
        compiler_params=_cparams(("arbitrary",), 16),
        name="dest",
    )(pad_starts, cols)


ROW_UNROLL = 8


def _dispatch_kernel(dest_ref, pend_ref, ug_ref, xs_hbm, ring, zbuf, sems, zsem, *, tm, blk, ne, ntok):
    i = pl.program_id(0)
    n = pl.num_programs(0)
    slot = i % 2

    def seg_tail(e):
        end = pend_ref[e]
        start = jnp.where(e == 0, 0, pend_ref[jnp.maximum(e - 1, 0)])
        tail = pl.multiple_of(jnp.maximum(end - blk, 0) * SUBLANES, blk * SUBLANES)
        return pltpu.make_async_copy(zbuf, xs_hbm.at[pl.ds(tail, blk * SUBLANES)], zsem), end > start

    @pl.when(i == 0)
    def _():
        zbuf[...] = jnp.zeros(zbuf.shape, zbuf.dtype)

        def zstart(p, carry):
            for q in range(2):
                cp, nonempty = seg_tail(2 * p + q)

                @pl.when(nonempty)
                def _():
                    cp.start(priority=q)
            return carry

        def zwait(e, carry):
            cp, nonempty = seg_tail(e)

            @pl.when(nonempty)
            def _():
                cp.wait()
            return carry

        def spare(b):
            row = pl.multiple_of(b * blk, blk)
            tile = pl.multiple_of(row * SUBLANES, blk * SUBLANES)
            return pltpu.make_async_copy(zbuf, xs_hbm.at[pl.ds(tile, blk * SUBLANES)], zsem), row >= pend_ref[ne - 1]

        def sstart(p, carry):
            for q in range(2):
                cp, unused = spare(2 * p + q)

                @pl.when(unused)
                def _():
                    cp.start(priority=q)
            return carry

        def swait(b, carry):
            cp, unused = spare(b)

            @pl.when(unused)
            def _():
                cp.wait()
            return carry

        nblk = xs_hbm.shape[0] // (blk * SUBLANES)
        assert ne % 2 == 0 and nblk % 2 == 0
        lax.fori_loop(0, ne // 2, zstart, 0)
        lax.fori_loop(0, nblk // 2, sstart, 0)
        lax.fori_loop(0, ne, zwait, 0)
        lax.fori_loop(0, nblk, swait, 0)

    def drain(s):
        for _ in range(TOP_K):
            pltpu.make_async_copy(ring.at[s], xs_hbm.at[pl.ds(0, tm * SUBLANES)], sems.at[s]).wait()

    @pl.when(i >= 2)
    def _():
        drain(slot)

    ring[slot] = ug_ref[...]

    def issue(c, carry):
        r0 = pl.multiple_of(c * ROW_UNROLL, ROW_UNROLL)
        for s in range(ROW_UNROLL):
            for k in range(TOP_K):
                d = dest_ref[k * ntok + i * tm + r0 + s]
                src = ring.at[slot, pl.ds(pl.multiple_of((r0 + s) * SUBLANES, SUBLANES), SUBLANES)]
                dst = xs_hbm.at[pl.ds(pl.multiple_of(d * SUBLANES, SUBLANES), SUBLANES)]
                pltpu.make_async_copy(src, dst, sems.at[slot]).start(priority=k % 2)
        return carry

    lax.fori_loop(0, tm // ROW_UNROLL, issue, 0)

    @pl.when(i == n - 1)
    def _():
        drain(slot)

        @pl.when(n >= 2)
        def _():
            drain(1 - slot)


def _dispatch(dest_flat, pad_ends, ug, nrows, tm, blk):
    T = ug.shape[0] // SUBLANES
    ne = pad_ends.shape[0]
    grid_spec = pltpu.PrefetchScalarGridSpec(
        num_scalar_prefetch=2,
        grid=(T // tm,),
        in_specs=[pl.BlockSpec((tm * SUBLANES, LANES), lambda i, d, p: (i, 0))],
        out_specs=pl.BlockSpec(memory_space=pl.ANY),
        scratch_shapes=[pltpu.VMEM((2, tm * SUBLANES, LANES), ug.dtype),
                        pltpu.VMEM((blk * SUBLANES, LANES), ug.dtype),
                        pltpu.SemaphoreType.DMA((2,)),
                        pltpu.SemaphoreType.DMA(())],
    )
    return pl.pallas_call(
        functools.partial(_dispatch_kernel, tm=tm, blk=blk, ne=ne, ntok=T),
        grid_spec=grid_spec,
        out_shape=jax.ShapeDtypeStruct((nrows * SUBLANES, LANES), ug.dtype),
        compiler_params=_cparams(("arbitrary",), 32),
        name="dispatch",
    )(dest_flat, pad_ends, ug)


WEIGHT_DMA_PRIORITY = (0, 1, 1)
WEIGHT_SLOTS = 3


def _experts_kernel(bexp_ref, first_ref, slot_ref, next_ref, lead_ref, nused_ref, x_ref, w1_hbm, w3_hbm, w2_hbm,
                    y_ref, w1b, w3b, w2b, wsems):
    i = pl.program_id(0)
    nused = nused_ref[0]

    def weight_copies(e, s):
        return (pltpu.make_async_copy(w1_hbm.at[e], w1b.at[s], wsems.at[s, 0]),
                pltpu.make_async_copy(w3_hbm.at[e], w3b.at[s], wsems.at[s, 1]),
                pltpu.make_async_copy(w2_hbm.at[e], w2b.at[s], wsems.at[s, 2]))

    def start_weights(e, s):
        for n, cp in enumerate(weight_copies(e, s)):
            cp.start(priority=WEIGHT_DMA_PRIORITY[n])

    @pl.when(i == 0)
    def _():
        for n in range(WEIGHT_SLOTS - 1):
            @pl.when(lead_ref[n] >= 0)
            def _():
                start_weights(lead_ref[n], n)

    @pl.when(i < nused)
    def _():
        s = slot_ref[i]

        @pl.when(first_ref[i] == 1)
        def _():
            @pl.when(next_ref[i] >= 0)
            def _():
                start_weights(next_ref[i], (s + WEIGHT_SLOTS - 1) % WEIGHT_SLOTS)
            for cp in weight_copies(bexp_ref[i], s):
                cp.wait()

        blk = x_ref.shape[0] // SUBLANES
        x_lo, x_hi = _unpack_bf16_pairs(_load_row_tiles(x_ref, 0, blk))
        half = x_lo.shape[1]
        a = (jnp.dot(x_lo, w1b[s, 0:half, :], preferred_element_type=F32)
             + jnp.dot(x_hi, w1b[s, half:2 * half, :], preferred_element_type=F32))
        b = (jnp.dot(x_lo, w3b[s, 0:half, :], preferred_element_type=F32)
             + jnp.dot(x_hi, w3b[s, half:2 * half, :], preferred_element_type=F32))
        hdn = a * jax.nn.sigmoid(a) * b
        cw = 2 * LANES
        for c0 in range(0, half, cw):
            y_lo = jnp.dot(hdn, w2b[s, :, c0:c0 + cw], preferred_element_type=F32)
            y_hi = jnp.dot(hdn, w2b[s, :, half + c0:half + c0 + cw], preferred_element_type=F32)
            words = _pack_bf16_words(y_lo, y_hi)
            for g in range(cw // LANES):
                y_ref[pl.ds(c0 // LANES + g, blk, stride=SUBLANES), :] = words[:, g * LANES:(g + 1) * LANES]

    @pl.when(i >= nused)
    def _():
        y_ref[...] = jnp.zeros(y_ref.shape, y_ref.dtype)


def _experts(bexp, first, slot, nxt, lead, nused, xs, w1, w3, w2, blk):
    P = xs.shape[0] // SUBLANES
    E, D, Fh = w1.shape
    nblk = P // blk
    smap = lambda i, *_: (i, 0)
    xmap = lambda i, be, fi, sl, nx, ld, nu: (jnp.minimum(i, nu[0] - 1), 0)
    grid_spec = pltpu.PrefetchScalarGridSpec(
        num_scalar_prefetch=6,
        grid=(nblk,),
        in_specs=[pl.BlockSpec((blk * SUBLANES, LANES), xmap),
                  pl.BlockSpec(memory_space=pl.ANY),
                  pl.BlockSpec(memory_space=pl.ANY),
                  pl.BlockSpec(memory_space=pl.ANY)],
        out_specs=pl.BlockSpec((blk * SUBLANES, LANES), smap),
        scratch_shapes=[pltpu.VMEM((WEIGHT_SLOTS, D, Fh), w1.dtype),
                        pltpu.VMEM((WEIGHT_SLOTS, D, Fh), w3.dtype),
                        pltpu.VMEM((WEIGHT_SLOTS, Fh, D), w2.dtype),
                        pltpu.SemaphoreType.DMA((WEIGHT_SLOTS, 3))],
    )
    return pl.pallas_call(
        _experts_kernel,
        grid_spec=grid_spec,
        out_shape=jax.ShapeDtypeStruct(xs.shape, xs.dtype),
        compiler_params=_cparams(("arbitrary",), 56),
        name="experts",
    )(bexp, first, slot, nxt, lead, nused, xs, w1, w3, w2)


def _row_gather(src_hbm, idx_ref, base, dst, sem, n):
    def body(c, carry):
        r0 = pl.multiple_of(c * ROW_UNROLL, ROW_UNROLL)
        for s in range(ROW_UNROLL):
            d = idx_ref[base + r0 + s]
            pltpu.make_async_copy(src_hbm.at[pl.ds(pl.multiple_of(d * SUBLANES, SUBLANES), SUBLANES)],
                                  dst.at[pl.ds(pl.multiple_of((r0 + s) * SUBLANES, SUBLANES), SUBLANES)],
                                  sem).start(priority=s % 2)
        return carry
    lax.fori_loop(0, n // ROW_UNROLL, body, 0)


def _row_gather_wait(src_hbm, dst, sem, n):
    pltpu.make_async_copy(src_hbm.at[pl.ds(0, n * SUBLANES)], dst, sem).wait()


def _combine_kernel(dest_ref, h_ref, info_ref, y_hbm, fw_ref, o_ref, ybuf, sems, *, tm, eps, ntok):
    i = pl.program_id(0)
    n = pl.num_programs(0)
    slot = i % 2
    nrow = TOP_K * tm

    def gather_step(step, s):
        for k in range(TOP_K):
            _row_gather(y_hbm, dest_ref, k * ntok + step * tm,
                        ybuf.at[s, pl.ds(k * tm * SUBLANES, tm * SUBLANES)], sems.at[s], tm)

    @pl.when(i == 0)
    def _():
        gather_step(0, 0)

    @pl.when(i + 1 < n)
    def _():
        gather_step(i + 1, 1 - slot)

    _row_gather_wait(y_hbm, ybuf.at[slot], sems.at[slot], nrow)
    half = o_ref.shape[1] // 2
    rc = tm

    def rows(ci, carry):
        r0 = pl.multiple_of(ci * rc, rc)
        rs = pl.ds(r0, rc)
        info = info_ref[rs, :]
        g1 = info[:, 2:3]
        g2 = info[:, 3:4]
        y1_lo, y1_hi = _unpack_bf16_pairs(_load_row_tiles(ybuf.at[slot], r0, rc))
        y2_lo, y2_hi = _unpack_bf16_pairs(_load_row_tiles(ybuf.at[slot], tm + r0, rc))
        h_lo = h_ref[rs, 0:half] + (g1 * y1_lo + g2 * y2_lo)
        h_hi = h_ref[rs, half:2 * half] + (g1 * y1_hi + g2 * y2_hi)
        ms = (jnp.sum(h_lo * h_lo, axis=-1, keepdims=True)
              + jnp.sum(h_hi * h_hi, axis=-1, keepdims=True)) / (2 * half)
        r = lax.rsqrt(ms + eps)
        o_ref[rs, 0:half] = h_lo * r * fw_ref[:, 0:half]
        o_ref[rs, half:2 * half] = h_hi * r * fw_ref[:, half:2 * half]
        return carry

    lax.fori_loop(0, tm // rc, rows, 0)


def _combine(dest, h, info, yb, final_w, tm):
    T, D = h.shape
    grid_spec = pltpu.PrefetchScalarGridSpec(
        num_scalar_prefetch=1,
        grid=(T // tm,),
        in_specs=[pl.BlockSpec((tm, D), lambda i, d: (i, 0)),
                  pl.BlockSpec((tm, LANES), lambda i, d: (i, 0)),
                  pl.BlockSpec(memory_space=pl.ANY),
                  pl.BlockSpec((1, D), lambda i, d: (0, 0))],
        out_specs=pl.BlockSpec((tm, D), lambda i, d: (i, 0)),
        scratch_shapes=[pltpu.VMEM((2, TOP_K * tm * SUBLANES, LANES), yb.dtype),
                        pltpu.SemaphoreType.DMA((2,))],
    )
    return pl.pallas_call(
        functools.partial(_combine_kernel, tm=tm, eps=RMS_EPS, ntok=T),
        grid_spec=grid_spec,
        out_shape=jax.ShapeDtypeStruct((T, D), F32),
        compiler_params=_cparams(("arbitrary",), 48),
        name="combine",
    )(dest, h, info, yb, final_w)


def _layer(h_in, l, B, S, mix_norm_w, w_in, conv_dw_w, conv_dw_b, conv_ln_w, conv_ln_b,
           lam_q1, lam_k1, lam_q2, lam_k2, attn_subln_w, w_out, ffn_norm_w,
           w_group, b_group, w_expert_gate, b_expert_gate, w1, w3, w2):
    T, D = h_in.shape
    d_conv = conv_dw_w.shape[1]
    d_attn = (w_in.shape[1] - 2 * d_conv) // 3
    ng = w_group.shape[1]
    ne = w_expert_gate.shape[1]
    epg = ne // ng
    assert ng + ne <= LANES
    lam_init = 0.8 - 0.6 * math.exp(-0.3 * l)

    w_bf = w_in.astype(BF16)
    proj_ag, u = _norm_inproj(h_in, mix_norm_w.reshape(1, D), w_bf, 2 * d_conv)
    y_conv, qkv = _conformer_qkv(proj_ag, u, w_bf, 2 * d_conv, conv_dw_w, conv_dw_b, conv_ln_w, conv_ln_b, B, S)
    y_attn = _diff_attn(qkv, lam_q1, lam_k1, lam_q2, lam_k2, attn_subln_w, B, S, 0, d_attn, lam_init)

    wr = jnp.concatenate([w_group, w_expert_gate, jnp.zeros((D, LANES - ng - ne), F32)], axis=1)
    wr_hi = wr.astype(BF16)
    wr_lo = (wr - wr_hi.astype(F32)).astype(BF16)
    br = jnp.concatenate([b_group, b_expert_gate.reshape(-1), jnp.zeros((LANES - ng - ne,), F32)]).reshape(1, LANES)
    wr_hl = jnp.concatenate([wr_hi, wr_lo], axis=1)
    h, logits, ug = _outproj(h_in, y_conv, y_attn, w_out.astype(BF16), ffn_norm_w.reshape(1, D), wr_hl, br)

    info, cols, cnt = _route(logits, ng, epg)

    blk = 256
    tmd = min(256, T)
    A = T * TOP_K
    nblk = (A + ne * (blk - 1) + blk - 1) // blk
    i32 = jnp.int32
    counts = cnt[0, ng:ng + ne].astype(i32)
    padded = (counts + blk - 1) // blk * blk
    pad_ends = jnp.cumsum(padded).astype(i32)
    pad_starts = pad_ends - padded
    dest = _dest(pad_starts, cols)[0:TOP_K].reshape(-1)
    nused = (pad_ends[-1] // blk).astype(i32)
    bpos = jnp.arange(nblk, dtype=i32)
    brow = jnp.minimum(bpos, nused - 1) * blk
    bexp = jnp.minimum(jnp.sum((pad_ends[None, :] <= brow[:, None]).astype(i32), axis=1), ne - 1)
    first = ((bpos < nused) & ((bpos == 0) | (bexp != jnp.roll(bexp, 1)))).astype(i32)
    slot = ((jnp.cumsum(first) - 1) % WEIGHT_SLOTS).astype(i32)
    used_idx = jnp.where(padded > 0, jnp.arange(ne, dtype=i32), ne)
    suffix_min = lax.cummin(used_idx, reverse=True)
    next_used = jnp.concatenate([suffix_min[1:], jnp.full((2,), ne, i32)])
    ahead = jnp.arange(ne, dtype=i32)
    for _ in range(WEIGHT_SLOTS - 1):
        ahead = next_used[ahead]
    ahead = jnp.where(ahead >= ne, -1, ahead)
    nxt = ahead[bexp].astype(i32)
    lead = [suffix_min[0]]
    for _ in range(WEIGHT_SLOTS - 2):
        lead.append(next_used[lead[-1]])
    lead = jnp.stack([jnp.where(e >= ne, -1, e) for e in lead]).astype(i32)

    xs = _dispatch(dest, pad_ends, ug, nblk * blk, tmd, blk)
    yb = _experts(bexp, first, slot, nxt, lead, nused.reshape(1), xs, w1, w3, w2, blk)
    return h, info, dest, yb


def kernel(x, mix_norm_w, w_in, conv_dw_w, conv_dw_b, conv_ln_w, conv_ln_b, lam_q1, lam_k1, lam_q2, lam_k2,
           attn_subln_w, w_out, ffn_norm_w, w_group, b_group, w_expert_gate, b_expert_gate, w1, w3, w2,
           final_norm_w):
    B, S, D = x.shape
    depth = w_in.shape[0]
    assert depth == 1
    T = B * S
    tmc = min(256, T)
    h = x.reshape(T, D)
    for l in range(depth):
        h, info, dest, yb = _layer(
            h, l, B, S, mix_norm_w[l], w_in[l], conv_dw_w[l], conv_dw_b[l], conv_ln_w[l], conv_ln_b[l],
            lam_q1[l], lam_k1[l], lam_q2[l], lam_k2[l], attn_subln_w[l], w_out[l], ffn_norm_w[l],
            w_group[l], b_group[l], w_expert_gate[l], b_expert_gate[l], w1[l], w3[l], w2[l])
        h = _combine(dest, h, info, yb, final_norm_w.reshape(1, D), tmc)
    return h.reshape(B, S, D)
```

```python
import functools
import math

import jax
import jax.numpy as jnp
from jax import lax
from jax.experimental import pallas as pl
from jax.experimental.pallas import tpu as pltpu

F32 = jnp.float32
BF16 = jnp.bfloat16

RMS_EPS = 1e-6
SUBLN_EPS = 1e-5
LN_EPS = 1e-5
TOP_K = 2
LANES = 128
SUBLANES = 8
NEG_BIG = -1e30
MIB = 1024 * 1024


def _cparams(sem, vmem_mib):
    return pltpu.CompilerParams(dimension_semantics=sem, vmem_limit_bytes=vmem_mib * MIB)


def _norm_inproj_kernel(x_ref, nw_ref, w_ref, o_ref, u_ref, *, eps, rc):
    @pl.when(pl.program_id(1) == 0)
    def _():
        for r0 in range(0, x_ref.shape[0], rc):
            rows = slice(r0, r0 + rc)
            x = x_ref[rows, :]
            ms = jnp.mean(x * x, axis=-1, keepdims=True)
            u = (x * lax.rsqrt(ms + eps) * nw_ref[...]).astype(u_ref.dtype)
            u_ref[rows, :] = u
            o_ref[rows, :] = jnp.dot(u, w_ref[...], preferred_element_type=F32).astype(o_ref.dtype)

    @pl.when(pl.program_id(1) > 0)
    def _():
        o_ref[...] = jnp.dot(u_ref[...], w_ref[...], preferred_element_type=F32).astype(o_ref.dtype)


def _norm_inproj(x2, nw, w_bf, ncols):
    T, D = x2.shape
    N = ncols
    tm = min(1024, T)
    tn = 1024
    return pl.pallas_call(
        functools.partial(_norm_inproj_kernel, eps=RMS_EPS, rc=256),
        grid=(T // tm, N // tn),
        in_specs=[pl.BlockSpec((tm, D), lambda i, j: (i, 0)),
                  pl.BlockSpec((1, D), lambda i, j: (0, 0)),
                  pl.BlockSpec((D, tn), lambda i, j: (0, j))],
        out_specs=[pl.BlockSpec((tm, tn), lambda i, j: (i, j)),
                   pl.BlockSpec((tm, D), lambda i, j: (i, 0))],
        out_shape=[jax.ShapeDtypeStruct((T, N), BF16), jax.ShapeDtypeStruct((T, D), BF16)],
        compiler_params=_cparams(("arbitrary", "arbitrary"), 48),
        name="norm_inproj",
    )(x2, nw, w_bf)


QKV_CHUNK = 4 * LANES
QKV_KSTEP = 2 * LANES


def _conv_qkv_kernel(a_ref, g_ref, w_ref, b_ref, lw_ref, lb_ref, u_ref, *rest, ts, kw, halo, eps, nwq):
    wq_refs = rest[0:nwq]
    o_ref, qkv_ref, ubuf, cbuf, sh, lhs = rest[nwq:]
    s = pl.program_id(1)
    C = a_ref.shape[1]
    wcols = wq_refs[0].shape[1]
    nmm = nwq * wcols // QKV_CHUNK

    @pl.when(s == 0)
    def _():
        ubuf[0:halo, :] = jnp.zeros((halo, C), F32)

    @pl.when(s > 0)
    def _():
        ubuf[0:halo, :] = ubuf[ts:ts + halo, :]

    lhs[...] = u_ref[...]

    nks = lhs.shape[1] // QKV_KSTEP
    pending = {"n": 0, "k": 0, "acc": None}

    def piece(after=None):
        n, k = pending["n"], pending["k"]
        if n >= nmm:
            return
        ks = slice(k * QKV_KSTEP, (k + 1) * QKV_KSTEP)
        if after is not None:
            allrows = jnp.sum(after, axis=0)
            bits = lax.bitcast_convert_type(jnp.concatenate([allrows, allrows], axis=0), jnp.uint32)
            zero = lax.bitcast_convert_type((bits >> 16) >> 16, F32).astype(lhs.dtype)
            first = (slice(0, 2 * SUBLANES), slice(k * QKV_KSTEP, k * QKV_KSTEP + LANES))
            lhs[first] = lhs[first] + zero
        wq_ref = wq_refs[n * QKV_CHUNK // wcols]
        ws = slice(n * QKV_CHUNK % wcols, n * QKV_CHUNK % wcols + QKV_CHUNK)
        part = jnp.dot(lhs[:, ks], wq_ref[ks, ws], preferred_element_type=F32)
        pending["acc"] = part if k == 0 else pending["acc"] + part
        if k == nks - 1:
            qkv_ref[:, n * QKV_CHUNK:(n + 1) * QKV_CHUNK] = pending["acc"].astype(qkv_ref.dtype)
            pending["n"], pending["k"] = n + 1, 0
        else:
            pending["k"] = k + 1

    rg = 64
    for r0 in range(0, ts, rg):
        a = a_ref[r0:r0 + rg, :].astype(F32)
        g = g_ref[r0:r0 + rg, :].astype(F32)
        ubuf[halo + r0:halo + r0 + rg, :] = a * jax.nn.sigmoid(g)

    nsh = sh.shape[1]
    for r in range(1, SUBLANES):
        for i0 in range(0, nsh, rg):
            n = min(rg, nsh - i0)
            sh[r - 1, i0:i0 + n, :] = ubuf[i0 + r:i0 + r + n, :]

    off = halo - (kw - 1)
    rc = 128
    ntap = (ts // rc) * (C // LANES) * kw
    tap_every = max(1, ntap // (nmm * nks))
    tapno = 0
    for r0 in range(0, ts, rc):
        for c in range(C // LANES):
            cs = slice(c * LANES, (c + 1) * LANES)
            acc = jnp.broadcast_to(b_ref[0:1, cs], (rc // SUBLANES, SUBLANES, LANES))
            for k in range(kw):
                q, r = divmod(off + k, SUBLANES)
                a0 = r0 + q * SUBLANES
                tap = ubuf[a0:a0 + rc, cs] if r == 0 else sh[r - 1, a0:a0 + rc, cs]
                acc = acc + w_ref[k, :, cs] * tap.reshape(rc // SUBLANES, SUBLANES, LANES)
                tapno += 1
                if tapno % tap_every == 0:
                    piece(after=acc)
            cbuf[r0:r0 + rc, cs] = acc.reshape(rc, LANES)
        cv = cbuf[r0:r0 + rc, :]
        mu = jnp.mean(cv, axis=-1, keepdims=True)
        d = cv - mu
        var = jnp.mean(d * d, axis=-1, keepdims=True)
        un = d * lax.rsqrt(var + eps) * lw_ref[...] + lb_ref[...]
        o_ref[r0:r0 + rc, :] = (un * jax.nn.sigmoid(un)).astype(o_ref.dtype)
    while pending["n"] < nmm:
        piece()


def _conformer_qkv(proj_ag, u, w_bf, col0, dw_w, dw_b, ln_w, ln_b, B, S):
    T = proj_ag.shape[0]
    kw, C = dw_w.shape
    D = w_bf.shape[0]
    N = w_bf.shape[1] - col0
    wcols = 1024
    nwq = N // wcols
    ts = min(256, S)
    halo = 32
    assert kw - 1 <= halo and S % ts == 0 and ts >= halo and N % wcols == 0 and col0 % wcols == 0
    ns = S // ts
    wp = jnp.broadcast_to(dw_w[:, None, :], (kw, SUBLANES, C))
    return pl.pallas_call(
        functools.partial(_conv_qkv_kernel, ts=ts, kw=kw, halo=halo, eps=LN_EPS, nwq=nwq),
        grid=(B, ns),
        in_specs=[pl.BlockSpec((ts, C), lambda b, s: (b * ns + s, 0)),
                  pl.BlockSpec((ts, C), lambda b, s: (b * ns + s, 1)),
                  pl.BlockSpec((kw, SUBLANES, C), lambda b, s: (0, 0, 0)),
                  pl.BlockSpec((1, C), lambda b, s: (0, 0)),
                  pl.BlockSpec((1, C), lambda b, s: (0, 0)),
                  pl.BlockSpec((1, C), lambda b, s: (0, 0)),
                  pl.BlockSpec((ts, D), lambda b, s: (b * ns + s, 0))]
                 + [pl.BlockSpec((D, wcols), functools.partial(lambda b, s, n: (0, col0 // wcols + n), n=n),
                                 pipeline_mode=pl.Buffered(1)) for n in range(nwq)],
        out_specs=[pl.BlockSpec((ts, C), lambda b, s: (b * ns + s, 0)),
                   pl.BlockSpec((ts, N), lambda b, s: (b * ns + s, 0))],
        out_shape=[jax.ShapeDtypeStruct((T, C), BF16), jax.ShapeDtypeStruct((T, N), BF16)],
        scratch_shapes=[pltpu.VMEM((ts + halo, C), F32), pltpu.VMEM((ts, C), F32),
                        pltpu.VMEM((SUBLANES - 1, ts + halo - SUBLANES, C), F32),
                        pltpu.VMEM((ts, D), BF16)],
        compiler_params=_cparams(("arbitrary", "arbitrary"), 48),
        name="conformer_qkv",
    )(proj_ag, proj_ag, wp, dw_b.reshape(1, C), ln_w.reshape(1, C), ln_b.reshape(1, C), u, *([w_bf] * nwq))


def _attn_kernel(lq1_ref, lk1_ref, lq2_ref, lk2_ref, q_ref, k_ref, v_ref, sw_ref, o_ref,
                 vxt_ref, qq_ref, *scr, tq, hd, hp, ones_rows, lam_init, eps):
    i = pl.program_id(2)
    vd = 2 * hd
    nkb = v_ref.shape[0] // tq
    acc_refs, m_refs, sa_refs, sb_refs = (scr[n * hp:(n + 1) * hp] for n in range(4))

    @pl.when(i == 0)
    def _():
        for u in range(hp):
            for jb in range(nkb):
                vblk = v_ref[jb * tq:(jb + 1) * tq, u * vd:(u + 1) * vd].astype(F32)
                vxt_ref[u, jb, 0:vd, :] = vblk.T.astype(vxt_ref.dtype)
                vxt_ref[u, jb, vd:vd + ones_rows, :] = jnp.ones((ones_rows, tq), vxt_ref.dtype)

    lam = (jnp.exp(jnp.sum(lq1_ref[...] * lk1_ref[...], axis=-1, keepdims=True))
           - jnp.exp(jnp.sum(lq2_ref[...] * lk2_ref[...], axis=-1, keepdims=True)) + lam_init)

    for u in range(hp):
        q = q_ref[:, u * vd:(u + 1) * vd]
        qs = q * jnp.asarray(hd ** -0.5, q.dtype)
        lane = lax.broadcasted_iota(jnp.int32, q.shape, 1)
        zero = jnp.zeros_like(qs)
        qq_ref[u, 0:tq, :] = jnp.where(lane < hd, qs, zero)
        qq_ref[u, tq:2 * tq, :] = jnp.where(lane >= hd, qs, zero)
        acc_refs[u][...] = jnp.zeros(acc_refs[u].shape, F32)
        m_refs[u][...] = jnp.full(m_refs[u].shape, NEG_BIG, F32)

    def scores(j, u, dst):
        rows = pl.ds(pl.multiple_of(j * tq, tq), tq)
        kb = k_ref[rows, u * vd:(u + 1) * vd]
        dst[...] = lax.dot_general(kb, qq_ref[u], (((1,), (1,)), ((), ())), preferred_element_type=F32)

    def softmax_pv(j, u, src, masked):
        st = src[...]
        if masked:
            r = lax.broadcasted_iota(jnp.int32, st.shape, 0)
            c = lax.broadcasted_iota(jnp.int32, st.shape, 1)
            st = jnp.where(r <= jnp.where(c >= tq, c - tq, c), st, NEG_BIG)
        m_old = m_refs[u][...]
        m_new = jnp.maximum(m_old, jnp.max(st, axis=0, keepdims=True))
        alpha = jnp.exp(m_old - m_new)
        pt = jnp.exp(st - m_new).astype(vxt_ref.dtype)
        pv = jnp.dot(vxt_ref[u, j], pt, preferred_element_type=F32)
        acc_refs[u][...] = alpha * acc_refs[u][...] + pv
        m_refs[u][...] = m_new

    def half(j, cur, nxt):
        for u in range(hp):
            scores(j + 1, u, nxt[u])
            softmax_pv(j, u, cur[u], False)

    for u in range(hp):
        scores(0, u, sa_refs[u])

    def pair(t, carry):
        half(2 * t, sa_refs, sb_refs)
        half(2 * t + 1, sb_refs, sa_refs)
        return carry

    lax.fori_loop(0, i // 2, pair, 0)

    @pl.when(i % 2 == 1)
    def _():
        half(i - 1, sa_refs, sb_refs)
        for u in range(hp):
            softmax_pv(i, u, sb_refs[u], True)

    @pl.when(i % 2 == 0)
    def _():
        for u in range(hp):
            softmax_pv(i, u, sa_refs[u], True)

    for u in range(hp):
        acc = acc_refs[u][...]
        o12 = acc[0:vd] * (1.0 / acc[vd:vd + 1])
        ot = o12[:, 0:tq] - lam * o12[:, tq:2 * tq]
        msq = jnp.mean(ot * ot, axis=0, keepdims=True)
        o = (ot * lax.rsqrt(msq + eps)).T
        o_ref[:, u * vd:(u + 1) * vd] = (o * sw_ref[...] * (1.0 - lam_init)).astype(o_ref.dtype)


def _diff_attn(proj, lq1, lk1, lq2, lk2, subln_w, B, S, d_conv, d_attn, lam_init):
    T = proj.shape[0]
    vd = subln_w.shape[0]
    hd = lq1.shape[0]
    assert vd == LANES and 2 * hd == vd
    H = d_attn // vd
    hp = 8
    ones_rows = 16
    tq = min(256, S)
    nq = S // tq
    assert H % hp == 0
    qc = 2 * d_conv // (hp * vd)
    kc = qc + H // hp
    vc = kc + H // hp
    lspec = pl.BlockSpec((1, hd), lambda b, h, i: (0, 0))
    return pl.pallas_call(
        functools.partial(_attn_kernel, tq=tq, hd=hd, hp=hp, ones_rows=ones_rows, lam_init=lam_init, eps=SUBLN_EPS),
        grid=(B, H // hp, nq),
        in_specs=[lspec, lspec, lspec, lspec,
                  pl.BlockSpec((tq, hp * vd), lambda b, h, i: (b * nq + i, qc + h)),
                  pl.BlockSpec((S, hp * vd), lambda b, h, i: (b, kc + h)),
                  pl.BlockSpec((S, hp * vd), lambda b, h, i: (b, vc + h)),
                  pl.BlockSpec((1, vd), lambda b, h, i: (0, 0))],
        out_specs=pl.BlockSpec((tq, hp * vd), lambda b, h, i: (b * nq + i, h)),
        out_shape=jax.ShapeDtypeStruct((T, d_attn), BF16),
        scratch_shapes=[pltpu.VMEM((hp, S // tq, vd + ones_rows, tq), BF16),
                        pltpu.VMEM((hp, 2 * tq, vd), BF16)]
                       + [pltpu.VMEM((vd + ones_rows, 2 * tq), F32) for _ in range(hp)]
                       + [pltpu.VMEM((1, 2 * tq), F32) for _ in range(hp)]
                       + [pltpu.VMEM((tq, 2 * tq), F32) for _ in range(2 * hp)],
        compiler_params=_cparams(("arbitrary", "arbitrary", "arbitrary"), 52),
        name="diff_attn",
    )(lq1.reshape(1, hd), lk1.reshape(1, hd), lq2.reshape(1, hd), lk2.reshape(1, hd),
      proj, proj, proj, subln_w.reshape(1, vd))


def _pack_bf16_pairs(v):
    n = v.shape[1] // 2
    return _pack_bf16_words(v[:, 0:n], v[:, n:2 * n])


def _pack_bf16_words(lo, hi):
    lo_bits = lax.bitcast_convert_type(lo.astype(BF16).astype(F32), jnp.uint32)
    hi_bits = lax.bitcast_convert_type(hi.astype(BF16).astype(F32), jnp.uint32)
    return (lo_bits >> 16) | (hi_bits & jnp.uint32(0xFFFF0000))


def _unpack_bf16_pairs(w):
    lo = lax.bitcast_convert_type(w << 16, F32)
    hi = lax.bitcast_convert_type(w & jnp.uint32(0xFFFF0000), F32)
    return lo, hi


def _store_row_tiles(ref, row0, words):
    m, w = words.shape
    assert w == SUBLANES * LANES
    for s in range(SUBLANES):
        ref[pl.ds(row0 * SUBLANES + s, m, stride=SUBLANES), :] = words[:, s * LANES:(s + 1) * LANES]


def _load_row_tiles(ref, row0, m):
    return jnp.concatenate([ref[pl.ds(row0 * SUBLANES + s, m, stride=SUBLANES), :] for s in range(SUBLANES)], axis=1)


def _outproj_kernel(x_ref, yc_ref, ya_ref, wo_ref, fw_ref, wr_ref, br_ref, h_ref, lg_ref, ug_ref, *, eps, nsub):
    dc = yc_ref.shape[1]
    sub = x_ref.shape[0] // nsub
    mixes = []
    for t in range(nsub):
        rows = slice(t * sub, (t + 1) * sub)
        mixes.append(jnp.dot(yc_ref[rows, :], wo_ref[0:dc, :], preferred_element_type=F32)
                     + jnp.dot(ya_ref[rows, :], wo_ref[dc:, :], preferred_element_type=F32))
    for t in range(nsub):
        rows = slice(t * sub, (t + 1) * sub)
        h = x_ref[rows, :] + mixes[t]
        h_ref[rows, :] = h
        ms = jnp.mean(h * h, axis=-1, keepdims=True)
        un = h * lax.rsqrt(ms + eps) * fw_ref[...]
        hi = un.astype(BF16)
        lo = (un - hi.astype(F32)).astype(BF16)
        hh = jnp.dot(hi, wr_ref[...], preferred_element_type=F32)
        lh = jnp.dot(lo, wr_ref[:, 0:LANES], preferred_element_type=F32)
        lg_ref[rows, :] = hh[:, 0:LANES] + hh[:, LANES:2 * LANES] + lh + br_ref[...]
        _store_row_tiles(ug_ref, t * sub, _pack_bf16_pairs(un))


def _outproj(x2, y_conv, y_attn, wo_bf, ffn_w, wr_hl, br):
    T, D = x2.shape
    dc = y_conv.shape[1]
    da = y_attn.shape[1]
    tm = min(512, T)
    nsub = 2 if tm % 512 == 0 else 1
    return pl.pallas_call(
        functools.partial(_outproj_kernel, eps=RMS_EPS, nsub=nsub),
        grid=(T // tm,),
        in_specs=[pl.BlockSpec((tm, D), lambda i: (i, 0)),
                  pl.BlockSpec((tm, dc), lambda i: (i, 0)),
                  pl.BlockSpec((tm, da), lambda i: (i, 0)),
                  pl.BlockSpec((dc + da, D), lambda i: (0, 0), pipeline_mode=pl.Buffered(1)),
                  pl.BlockSpec((1, D), lambda i: (0, 0)),
                  pl.BlockSpec((D, 2 * LANES), lambda i: (0, 0)),
                  pl.BlockSpec((1, LANES), lambda i: (0, 0))],
        out_specs=[pl.BlockSpec((tm, D), lambda i: (i, 0)),
                   pl.BlockSpec((tm, LANES), lambda i: (i, 0)),
                   pl.BlockSpec((tm * SUBLANES, LANES), lambda i: (i, 0))],
        out_shape=[jax.ShapeDtypeStruct((T, D), F32), jax.ShapeDtypeStruct((T, LANES), F32),
                   jax.ShapeDtypeStruct((T * SUBLANES, LANES), jnp.uint32)],
        compiler_params=_cparams(("arbitrary",), 56),
        name="outproj",
    )(x2, y_conv, y_attn, wo_bf, ffn_w, wr_hl, br)


def _route_kernel(lg_ref, info_ref, cols_ref, cnt_ref, *, ng, epg):
    i = pl.program_id(0)
    lg = lg_ref[...]
    tm = lg.shape[0]
    lane = lax.broadcasted_iota(jnp.int32, lg.shape, 1)
    lanef = lane.astype(F32)
    ne = ng * epg

    def first_argmax(vals):
        mx = jnp.max(vals, axis=-1, keepdims=True)
        idx = jnp.min(jnp.where(vals == mx, lanef, float(LANES)), axis=-1, keepdims=True)
        return mx, idx

    gmask = lane < ng
    gl = jnp.where(gmask, lg, NEG_BIG)
    gmax, gsel = first_argmax(gl)
    gsum = jnp.sum(jnp.where(gmask, jnp.exp(gl - gmax), 0.0), axis=-1, keepdims=True)
    g_w = 1.0 / gsum
    lo = gsel * epg + ng
    emask = (lanef >= lo) & (lanef < lo + epg)
    el = jnp.where(emask, lg, NEG_BIG)
    v1, i1 = first_argmax(el)
    el2 = jnp.where(lanef == i1, NEG_BIG, el)
    v2, i2 = first_argmax(el2)
    e2 = jnp.exp(v2 - v1)
    p1 = 1.0 / (1.0 + e2)
    gate1 = g_w * p1
    gate2 = g_w * (e2 * p1)

    oh1 = lanef == i1
    oh2 = lanef == i2
    cmat = jnp.where(oh1 | oh2, 1.0, 0.0).astype(BF16)
    r = lax.broadcasted_iota(jnp.int32, (tm, tm), 0)
    c = lax.broadcasted_iota(jnp.int32, (tm, tm), 1)
    tri = jnp.where(c < r, 1.0, 0.0).astype(BF16)

    @pl.when(i == 0)
    def _():
        cnt_ref[...] = jnp.zeros(cnt_ref.shape, F32)

    carry = cnt_ref[0:1, :]
    prefix = jnp.dot(tri, cmat, preferred_element_type=F32) + carry
    rank1 = jnp.sum(jnp.where(oh1, prefix, 0.0), axis=-1, keepdims=True)
    rank2 = jnp.sum(jnp.where(oh2, prefix, 0.0), axis=-1, keepdims=True)
    cnt_ref[...] = jnp.broadcast_to(carry + jnp.sum(cmat.astype(F32), axis=0, keepdims=True), cnt_ref.shape)

    info = jnp.where(lane == 0, i1 - ng,
           jnp.where(lane == 1, i2 - ng,
           jnp.where(lane == 2, gate1,
           jnp.where(lane == 3, gate2,
           jnp.where(lane == 4, rank1,
           jnp.where(lane == 5, rank2, 0.0))))))
    info_ref[...] = info
    cols_ref[...] = info.T[0:SUBLANES, :]


def _route(logits, ng, epg):
    T = logits.shape[0]
    tm = min(512, T)
    return pl.pallas_call(
        functools.partial(_route_kernel, ng=ng, epg=epg),
        grid=(T // tm,),
        in_specs=[pl.BlockSpec((tm, LANES), lambda i: (i, 0))],
        out_specs=[pl.BlockSpec((tm, LANES), lambda i: (i, 0)),
                   pl.BlockSpec((SUBLANES, tm), lambda i: (0, i)),
                   pl.BlockSpec((8, LANES), lambda i: (0, 0))],
        out_shape=[jax.ShapeDtypeStruct((T, LANES), F32), jax.ShapeDtypeStruct((SUBLANES, T), F32),
                   jax.ShapeDtypeStruct((8, LANES), F32)],
        compiler_params=_cparams(("arbitrary",), 32),
        name="route",
    )(logits)


def _dest_kernel(pstart_ref, cols_ref, dest_ref, *, ne):
    eid = cols_ref[0:TOP_K, :].astype(jnp.int32)
    rank = cols_ref[4:4 + TOP_K, :].astype(jnp.int32)

    def body(e, acc):
        return acc + jnp.where(eid == e, pstart_ref[e], 0)

    start = lax.fori_loop(0, ne, body, jnp.zeros(eid.shape, jnp.int32))
    dest_ref[...] = jnp.zeros(dest_ref.shape, jnp.int32)
    dest_ref[0:TOP_K, :] = start + rank


def _dest(pad_starts, cols):
    R, T = cols.shape
    grid_spec = pltpu.PrefetchScalarGridSpec(
        num_scalar_prefetch=1,
        grid=(1,),
        in_specs=[pl.BlockSpec((R, T), lambda i, p: (0, 0))],
        out_specs=pl.BlockSpec((R, T), lambda i, p: (0, 0)),
    )
    return pl.pallas_call(
        functools.partial(_dest_kernel, ne=pad_starts.shape[0]),
        grid_spec=grid_spec,
        out_shape=jax.ShapeDtypeStruct((R, T), jnp.int32),
        compiler_params=_cparams(("arbitrary",), 16),
        name="dest",
    )(pad_starts, cols)


ROW_UNROLL = 8


def _dispatch_kernel(dest_ref, pend_ref, ug_ref, xs_hbm, ring, zbuf, sems, zsem, *, tm, blk, ne, ntok):
    i = pl.program_id(0)
    n = pl.num_programs(0)
    slot = i % 2

    def seg_tail(e):
        end = pend_ref[e]
        start = jnp.where(e == 0, 0, pend_ref[jnp.maximum(e - 1, 0)])
        tail = pl.multiple_of(jnp.maximum(end - blk, 0) * SUBLANES, blk * SUBLANES)
        return pltpu.make_async_copy(zbuf, xs_hbm.at[pl.ds(tail, blk * SUBLANES)], zsem), end > start

    @pl.when(i == 0)
    def _():
        zbuf[...] = jnp.zeros(zbuf.shape, zbuf.dtype)

        def zstart(p, carry):
            for q in range(2):
                cp, nonempty = seg_tail(2 * p + q)

                @pl.when(nonempty)
                def _():
                    cp.start(priority=q)
            return carry

        def zwait(e, carry):
            cp, nonempty = seg_tail(e)

            @pl.when(nonempty)
            def _():
                cp.wait()
            return carry

        def spare(b):
            row = pl.multiple_of(b * blk, blk)
            tile = pl.multiple_of(row * SUBLANES, blk * SUBLANES)
            return pltpu.make_async_copy(zbuf, xs_hbm.at[pl.ds(tile, blk * SUBLANES)], zsem), row >= pend_ref[ne - 1]

        def sstart(p, carry):
            for q in range(2):
                cp, unused = spare(2 * p + q)

                @pl.when(unused)
                def _():
                    cp.start(priority=q)
            return carry

        def swait(b, carry):
            cp, unused = spare(b)

            @pl.when(unused)
            def _():
                cp.wait()
            return carry

        nblk = xs_hbm.shape[0] // (blk * SUBLANES)
        assert ne % 2 == 0 and nblk % 2 == 0
        lax.fori_loop(0, ne // 2, zstart, 0)
        lax.fori_loop(0, nblk // 2, sstart, 0)
        lax.fori_loop(0, ne, zwait, 0)
        lax.fori_loop(0, nblk, swait, 0)

    def drain(s):
        for _ in range(TOP_K):
            pltpu.make_async_copy(ring.at[s], xs_hbm.at[pl.ds(0, tm * SUBLANES)], sems.at[s]).wait()

    @pl.when(i >= 2)
    def _():
        drain(slot)

    ring[slot] = ug_ref[...]

    def issue(c, carry):
        r0 = pl.multiple_of(c * ROW_UNROLL, ROW_UNROLL)
        for s in range(ROW_UNROLL):
            for k in range(TOP_K):
                d = dest_ref[k * ntok + i * tm + r0 + s]
                src = ring.at[slot, pl.ds(pl.multiple_of((r0 + s) * SUBLANES, SUBLANES), SUBLANES)]
                dst = xs_hbm.at[pl.ds(pl.multiple_of(d * SUBLANES, SUBLANES), SUBLANES)]
                pltpu.make_async_copy(src, dst, sems.at[slot]).start(priority=k % 2)
        return carry

    lax.fori_loop(0, tm // ROW_UNROLL, issue, 0)

    @pl.when(i == n - 1)
    def _():
        drain(slot)

        @pl.when(n >= 2)
        def _():
            drain(1 - slot)


def _dispatch(dest_flat, pad_ends, ug, nrows, tm, blk):
    T = ug.shape[0] // SUBLANES
    ne = pad_ends.shape[0]
    grid_spec = pltpu.PrefetchScalarGridSpec(
        num_scalar_prefetch=2,
        grid=(T // tm,),
        in_specs=[pl.BlockSpec((tm * SUBLANES, LANES), lambda i, d, p: (i, 0))],
        out_specs=pl.BlockSpec(memory_space=pl.ANY),
        scratch_shapes=[pltpu.VMEM((2, tm * SUBLANES, LANES), ug.dtype),
                        pltpu.VMEM((blk * SUBLANES, LANES), ug.dtype),
                        pltpu.SemaphoreType.DMA((2,)),
                        pltpu.SemaphoreType.DMA(())],
    )
    return pl.pallas_call(
        functools.partial(_dispatch_kernel, tm=tm, blk=blk, ne=ne, ntok=T),
        grid_spec=grid_spec,
        out_shape=jax.ShapeDtypeStruct((nrows * SUBLANES, LANES), ug.dtype),
        compiler_params=_cparams(("arbitrary",), 32),
        name="dispatch",
    )(dest_flat, pad_ends, ug)


WEIGHT_DMA_PRIORITY = (0, 1, 1)
WEIGHT_SLOTS = 3


def _experts_kernel(bexp_ref, first_ref, slot_ref, next_ref, lead_ref, nused_ref, x_ref, w1_hbm, w3_hbm, w2_hbm,
                    y_ref, w1b, w3b, w2b, wsems):
    i = pl.program_id(0)
    nused = nused_ref[0]

    def weight_copies(e, s):
        return (pltpu.make_async_copy(w1_hbm.at[e], w1b.at[s], wsems.at[s, 0]),
                pltpu.make_async_copy(w3_hbm.at[e], w3b.at[s], wsems.at[s, 1]),
                pltpu.make_async_copy(w2_hbm.at[e], w2b.at[s], wsems.at[s, 2]))

    def start_weights(e, s):
        for n, cp in enumerate(weight_copies(e, s)):
            cp.start(priority=WEIGHT_DMA_PRIORITY[n])

    @pl.when(i == 0)
    def _():
        for n in range(WEIGHT_SLOTS - 1):
            @pl.when(lead_ref[n] >= 0)
            def _():
                start_weights(lead_ref[n], n)

    @pl.when(i < nused)
    def _():
        s = slot_ref[i]

        @pl.when(first_ref[i] == 1)
        def _():
            @pl.when(next_ref[i] >= 0)
            def _():
                start_weights(next_ref[i], (s + WEIGHT_SLOTS - 1) % WEIGHT_SLOTS)
            for cp in weight_copies(bexp_ref[i], s):
                cp.wait()

        blk = x_ref.shape[0] // SUBLANES
        x_lo, x_hi = _unpack_bf16_pairs(_load_row_tiles(x_ref, 0, blk))
        half = x_lo.shape[1]
        a = (jnp.dot(x_lo, w1b[s, 0:half, :], preferred_element_type=F32)
             + jnp.dot(x_hi, w1b[s, half:2 * half, :], preferred_element_type=F32))
        b = (jnp.dot(x_lo, w3b[s, 0:half, :], preferred_element_type=F32)
             + jnp.dot(x_hi, w3b[s, half:2 * half, :], preferred_element_type=F32))
        hdn = a * jax.nn.sigmoid(a) * b
        cw = 2 * LANES
        for c0 in range(0, half, cw):
            y_lo = jnp.dot(hdn, w2b[s, :, c0:c0 + cw], preferred_element_type=F32)
            y_hi = jnp.dot(hdn, w2b[s, :, half + c0:half + c0 + cw], preferred_element_type=F32)
            words = _pack_bf16_words(y_lo, y_hi)
            for g in range(cw // LANES):
                y_ref[pl.ds(c0 // LANES + g, blk, stride=SUBLANES), :] = words[:, g * LANES:(g + 1) * LANES]

    @pl.when(i >= nused)
    def _():
        y_ref[...] = jnp.zeros(y_ref.shape, y_ref.dtype)


def _experts(bexp, first, slot, nxt, lead, nused, xs, w1, w3, w2, blk):
    P = xs.shape[0] // SUBLANES
    E, D, Fh = w1.shape
    nblk = P // blk
    smap = lambda i, *_: (i, 0)
    xmap = lambda i, be, fi, sl, nx, ld, nu: (jnp.minimum(i, nu[0] - 1), 0)
    grid_spec = pltpu.PrefetchScalarGridSpec(
        num_scalar_prefetch=6,
        grid=(nblk,),
        in_specs=[pl.BlockSpec((blk * SUBLANES, LANES), xmap),
                  pl.BlockSpec(memory_space=pl.ANY),
                  pl.BlockSpec(memory_space=pl.ANY),
                  pl.BlockSpec(memory_space=pl.ANY)],
        out_specs=pl.BlockSpec((blk * SUBLANES, LANES), smap),
        scratch_shapes=[pltpu.VMEM((WEIGHT_SLOTS, D, Fh), w1.dtype),
                        pltpu.VMEM((WEIGHT_SLOTS, D, Fh), w3.dtype),
                        pltpu.VMEM((WEIGHT_SLOTS, Fh, D), w2.dtype),
                        pltpu.SemaphoreType.DMA((WEIGHT_SLOTS, 3))],
    )
    return pl.pallas_call(
        _experts_kernel,
        grid_spec=grid_spec,
        out_shape=jax.ShapeDtypeStruct(xs.shape, xs.dtype),
        compiler_params=_cparams(("arbitrary",), 56),
        name="experts",
    )(bexp, first, slot, nxt, lead, nused, xs, w1, w3, w2)


def _row_gather(src_hbm, idx_ref, base, dst, sem, n):
    def body(c, carry):
        r0 = pl.multiple_of(c * ROW_UNROLL, ROW_UNROLL)
        for s in range(ROW_UNROLL):
            d = idx_ref[base + r0 + s]
            pltpu.make_async_copy(src_hbm.at[pl.ds(pl.multiple_of(d * SUBLANES, SUBLANES), SUBLANES)],
                                  dst.at[pl.ds(pl.multiple_of((r0 + s) * SUBLANES, SUBLANES), SUBLANES)],
                                  sem).start(priority=s % 2)
        return carry
    lax.fori_loop(0, n // ROW_UNROLL, body, 0)


def _row_gather_wait(src_hbm, dst, sem, n):
    pltpu.make_async_copy(src_hbm.at[pl.ds(0, n * SUBLANES)], dst, sem).wait()


def _combine_kernel(dest_ref, h_ref, info_ref, y_hbm, fw_ref, o_ref, ybuf, sems, *, tm, eps, ntok):
    i = pl.program_id(0)
    n = pl.num_programs(0)
    slot = i % 2
    nrow = TOP_K * tm

    def gather_step(step, s):
        for k in range(TOP_K):
            _row_gather(y_hbm, dest_ref, k * ntok + step * tm,
                        ybuf.at[s, pl.ds(k * tm * SUBLANES, tm * SUBLANES)], sems.at[s], tm)

    @pl.when(i == 0)
    def _():
        gather_step(0, 0)

    @pl.when(i + 1 < n)
    def _():
        gather_step(i + 1, 1 - slot)

    _row_gather_wait(y_hbm, ybuf.at[slot], sems.at[slot], nrow)
    half = o_ref.shape[1] // 2
    rc = tm

    def rows(ci, carry):
        r0 = pl.multiple_of(ci * rc, rc)
        rs = pl.ds(r0, rc)
        info = info_ref[rs, :]
        g1 = info[:, 2:3]
        g2 = info[:, 3:4]
        y1_lo, y1_hi = _unpack_bf16_pairs(_load_row_tiles(ybuf.at[slot], r0, rc))
        y2_lo, y2_hi = _unpack_bf16_pairs(_load_row_tiles(ybuf.at[slot], tm + r0, rc))
        h_lo = h_ref[rs, 0:half] + (g1 * y1_lo + g2 * y2_lo)
        h_hi = h_ref[rs, half:2 * half] + (g1 * y1_hi + g2 * y2_hi)
        ms = (jnp.sum(h_lo * h_lo, axis=-1, keepdims=True)
              + jnp.sum(h_hi * h_hi, axis=-1, keepdims=True)) / (2 * half)
        r = lax.rsqrt(ms + eps)
        o_ref[rs, 0:half] = h_lo * r * fw_ref[:, 0:half]
        o_ref[rs, half:2 * half] = h_hi * r * fw_ref[:, half:2 * half]
        return carry

    lax.fori_loop(0, tm // rc, rows, 0)


def _combine(dest, h, info, yb, final_w, tm):
    T, D = h.shape
    grid_spec = pltpu.PrefetchScalarGridSpec(
        num_scalar_prefetch=1,
        grid=(T // tm,),
        in_specs=[pl.BlockSpec((tm, D), lambda i, d: (i, 0)),
                  pl.BlockSpec((tm, LANES), lambda i, d: (i, 0)),
                  pl.BlockSpec(memory_space=pl.ANY),
                  pl.BlockSpec((1, D), lambda i, d: (0, 0))],
        out_specs=pl.BlockSpec((tm, D), lambda i, d: (i, 0)),
        scratch_shapes=[pltpu.VMEM((2, TOP_K * tm * SUBLANES, LANES), yb.dtype),
                        pltpu.SemaphoreType.DMA((2,))],
    )
    return pl.pallas_call(
        functools.partial(_combine_kernel, tm=tm, eps=RMS_EPS, ntok=T),
        grid_spec=grid_spec,
        out_shape=jax.ShapeDtypeStruct((T, D), F32),
        compiler_params=_cparams(("arbitrary",), 48),
        name="combine",
    )(dest, h, info, yb, final_w)


def _layer(h_in, l, B, S, mix_norm_w, w_in, conv_dw_w, conv_dw_b, conv_ln_w, conv_ln_b,
           lam_q1, lam_k1, lam_q2, lam_k2, attn_subln_w, w_out, ffn_norm_w,
           w_group, b_group, w_expert_gate, b_expert_gate, w1, w3, w2):
    T, D = h_in.shape
    d_conv = conv_dw_w.shape[1]
    d_attn = (w_in.shape[1] - 2 * d_conv) // 3
    ng = w_group.shape[1]
    ne = w_expert_gate.shape[1]
    epg = ne // ng
    assert ng + ne <= LANES
    lam_init = 0.8 - 0.6 * math.exp(-0.3 * l)

    w_bf = w_in.astype(BF16)
    proj_ag, u = _norm_inproj(h_in, mix_norm_w.reshape(1, D), w_bf, 2 * d_conv)
    y_conv, qkv = _conformer_qkv(proj_ag, u, w_bf, 2 * d_conv, conv_dw_w, conv_dw_b, conv_ln_w, conv_ln_b, B, S)
    y_attn = _diff_attn(qkv, lam_q1, lam_k1, lam_q2, lam_k2, attn_subln_w, B, S, 0, d_attn, lam_init)

    wr = jnp.concatenate([w_group, w_expert_gate, jnp.zeros((D, LANES - ng - ne), F32)], axis=1)
    wr_hi = wr.astype(BF16)
    wr_lo = (wr - wr_hi.astype(F32)).astype(BF16)
    br = jnp.concatenate([b_group, b_expert_gate.reshape(-1), jnp.zeros((LANES - ng - ne,), F32)]).reshape(1, LANES)
    wr_hl = jnp.concatenate([wr_hi, wr_lo], axis=1)
    h, logits, ug = _outproj(h_in, y_conv, y_attn, w_out.astype(BF16), ffn_norm_w.reshape(1, D), wr_hl, br)

    info, cols, cnt = _route(logits, ng, epg)

    blk = 256
    tmd = min(256, T)
    A = T * TOP_K
    nblk = (A + ne * (blk - 1) + blk - 1) // blk
    i32 = jnp.int32
    counts = cnt[0, ng:ng + ne].astype(i32)
    padded = (counts + blk - 1) // blk * blk
    pad_ends = jnp.cumsum(padded).astype(i32)
    pad_starts = pad_ends - padded
    dest = _dest(pad_starts, cols)[0:TOP_K].reshape(-1)
    nused = (pad_ends[-1] // blk).astype(i32)
    bpos = jnp.arange(nblk, dtype=i32)
    brow = jnp.minimum(bpos, nused - 1) * blk
    bexp = jnp.minimum(jnp.sum((pad_ends[None, :] <= brow[:, None]).astype(i32), axis=1), ne - 1)
    first = ((bpos < nused) & ((bpos == 0) | (bexp != jnp.roll(bexp, 1)))).astype(i32)
    slot = ((jnp.cumsum(first) - 1) % WEIGHT_SLOTS).astype(i32)
    used_idx = jnp.where(padded > 0, jnp.arange(ne, dtype=i32), ne)
    suffix_min = lax.cummin(used_idx, reverse=True)
    next_used = jnp.concatenate([suffix_min[1:], jnp.full((2,), ne, i32)])
    ahead = jnp.arange(ne, dtype=i32)
    for _ in range(WEIGHT_SLOTS - 1):
        ahead = next_used[ahead]
    ahead = jnp.where(ahead >= ne, -1, ahead)
    nxt = ahead[bexp].astype(i32)
    lead = [suffix_min[0]]
    for _ in range(WEIGHT_SLOTS - 2):
        lead.append(next_used[lead[-1]])
    lead = jnp.stack([jnp.where(e >= ne, -1, e) for e in lead]).astype(i32)

    xs = _dispatch(dest, pad_ends, ug, nblk * blk, tmd, blk)
    yb = _experts(bexp, first, slot, nxt, lead, nused.reshape(1), xs, w1, w3, w2, blk)
    return h, info, dest, yb


def kernel(x, mix_norm_w, w_in, conv_dw_w, conv_dw_b, conv_ln_w, conv_ln_b, lam_q1, lam_k1, lam_q2, lam_k2,
           attn_subln_w, w_out, ffn_norm_w, w_group, b_group, w_expert_gate, b_expert_gate, w1, w3, w2,
           final_norm_w):
    B, S, D = x.shape
    depth = w_in.shape[0]
    assert depth == 1
    T = B * S
    tmc = min(256, T)
    h = x.reshape(T, D)
    for l in range(depth):
        h, info, dest, yb = _layer(
            h, l, B, S, mix_norm_w[l], w_in[l], conv_dw_w[l], conv_dw_b[l], conv_ln_w[l], conv_ln_b[l],
            lam_q1[l], lam_k1[l], lam_q2[l], lam_k2[l], attn_subln_w[l], w_out[l], ffn_norm_w[l],
            w_group[l], b_group[l], w_expert_gate[l], b_expert_gate[l], w1[l], w3[l], w2[l])
        h = _combine(dest, h, info, yb, final_norm_w.reshape(1, D), tmc)
    return h.reshape(B, S, D)
```

```python
import functools
import math

import jax
import jax.numpy as jnp
from jax import lax
from jax.experimental import pallas as pl
from jax.experimental.pallas import tpu as pltpu

F32 = jnp.float32
BF16 = jnp.bfloat16

RMS_EPS = 1e-6
SUBLN_EPS = 1e-5
LN_EPS = 1e-5
TOP_K = 2
LANES = 128
SUBLANES = 8
NEG_BIG = -1e30
MIB = 1024 * 1024


def _cparams(sem, vmem_mib):
    return pltpu.CompilerParams(dimension_semantics=sem, vmem_limit_bytes=vmem_mib * MIB)


def _norm_inproj_kernel(x_ref, nw_ref, w_ref, o_ref, u_ref, *, eps, rc):
    @pl.when(pl.program_id(1) == 0)
    def _():
        for r0 in range(0, x_ref.shape[0], rc):
            rows = slice(r0, r0 + rc)
            x = x_ref[rows, :]
            ms = jnp.mean(x * x, axis=-1, keepdims=True)
            u = (x * lax.rsqrt(ms + eps) * nw_ref[...]).astype(u_ref.dtype)
            u_ref[rows, :] = u
            o_ref[rows, :] = jnp.dot(u, w_ref[...], preferred_element_type=F32).astype(o_ref.dtype)

    @pl.when(pl.program_id(1) > 0)
    def _():
        o_ref[...] = jnp.dot(u_ref[...], w_ref[...], preferred_element_type=F32).astype(o_ref.dtype)


def _norm_inproj(x2, nw, w_bf, ncols):
    T, D = x2.shape
    N = ncols
    tm = min(1024, T)
    tn = 1024
    return pl.pallas_call(
        functools.partial(_norm_inproj_kernel, eps=RMS_EPS, rc=256),
        grid=(T // tm, N // tn),
        in_specs=[pl.BlockSpec((tm, D), lambda i, j: (i, 0)),
                  pl.BlockSpec((1, D), lambda i, j: (0, 0)),
                  pl.BlockSpec((D, tn), lambda i, j: (0, j))],
        out_specs=[pl.BlockSpec((tm, tn), lambda i, j: (i, j)),
                   pl.BlockSpec((tm, D), lambda i, j: (i, 0))],
        out_shape=[jax.ShapeDtypeStruct((T, N), BF16), jax.ShapeDtypeStruct((T, D), BF16)],
        compiler_params=_cparams(("arbitrary", "arbitrary"), 48),
        name="norm_inproj",
    )(x2, nw, w_bf)


QKV_CHUNK = 2 * LANES
QKV_KSTEP = 8 * LANES


def _conv_qkv_kernel(a_ref, g_ref, w_ref, b_ref, lw_ref, lb_ref, u_ref, *rest, ts, kw, halo, eps, nwq):
    wq_refs = rest[0:nwq]
    o_ref, qkv_ref, ubuf, cbuf, sh, lhs = rest[nwq:]
    s = pl.program_id(1)
    C = a_ref.shape[1]
    wcols = wq_refs[0].shape[1]
    nmm = nwq * wcols // QKV_CHUNK

    @pl.when(s == 0)
    def _():
        ubuf[0:halo, :] = jnp.zeros((halo, C), F32)

    @pl.when(s > 0)
    def _():
        ubuf[0:halo, :] = ubuf[ts:ts + halo, :]

    lhs[...] = u_ref[...]

    nks = lhs.shape[1] // QKV_KSTEP
    pending = {"n": 0, "k": 0, "acc": None}

    def piece(after=None):
        n, k = pending["n"], pending["k"]
        if n >= nmm:
            return
        ks = slice(k * QKV_KSTEP, (k + 1) * QKV_KSTEP)
        if after is not None:
            allrows = jnp.sum(after, axis=0)
            bits = lax.bitcast_convert_type(jnp.concatenate([allrows, allrows], axis=0), jnp.uint32)
            zero = lax.bitcast_convert_type((bits >> 16) >> 16, F32).astype(lhs.dtype)
            first = (slice(0, 2 * SUBLANES), slice(k * QKV_KSTEP, k * QKV_KSTEP + LANES))
            lhs[first] = lhs[first] + zero
        wq_ref = wq_refs[n * QKV_CHUNK // wcols]
        ws = slice(n * QKV_CHUNK % wcols, n * QKV_CHUNK % wcols + QKV_CHUNK)
        part = jnp.dot(lhs[:, ks], wq_ref[ks, ws], preferred_element_type=F32)
        pending["acc"] = part if k == 0 else pending["acc"] + part
        if k == nks - 1:
            qkv_ref[:, n * QKV_CHUNK:(n + 1) * QKV_CHUNK] = pending["acc"].astype(qkv_ref.dtype)
            pending["n"], pending["k"] = n + 1, 0
        else:
            pending["k"] = k + 1

    rg = 64
    for r0 in range(0, ts, rg):
        a = a_ref[r0:r0 + rg, :].astype(F32)
        g = g_ref[r0:r0 + rg, :].astype(F32)
        ubuf[halo + r0:halo + r0 + rg, :] = a * jax.nn.sigmoid(g)

    nsh = sh.shape[1]
    for r in range(1, SUBLANES):
        for i0 in range(0, nsh, rg):
            n = min(rg, nsh - i0)
            sh[r - 1, i0:i0 + n, :] = ubuf[i0 + r:i0 + r + n, :]

    off = halo - (kw - 1)
    rc = 128
    ntap = (ts // rc) * (C // LANES) * kw
    tap_every = max(1, ntap // (nmm * nks))
    tapno = 0
    for r0 in range(0, ts, rc):
        for c in range(C // LANES):
            cs = slice(c * LANES, (c + 1) * LANES)
            acc = jnp.broadcast_to(b_ref[0:1, cs], (rc // SUBLANES, SUBLANES, LANES))
            for k in range(kw):
                q, r = divmod(off + k, SUBLANES)
                a0 = r0 + q * SUBLANES
                tap = ubuf[a0:a0 + rc, cs] if r == 0 else sh[r - 1, a0:a0 + rc, cs]
                acc = acc + w_ref[k, :, cs] * tap.reshape(rc // SUBLANES, SUBLANES, LANES)
                tapno += 1
                if tapno % tap_every == 0:
                    piece(after=acc)
            cbuf[r0:r0 + rc, cs] = acc.reshape(rc, LANES)
        cv = cbuf[r0:r0 + rc, :]
        mu = jnp.mean(cv, axis=-1, keepdims=True)
        d = cv - mu
        var = jnp.mean(d * d, axis=-1, keepdims=True)
        un = d * lax.rsqrt(var + eps) * lw_ref[...] + lb_ref[...]
        o_ref[r0:r0 + rc, :] = (un * jax.nn.sigmoid(un)).astype(o_ref.dtype)
    while pending["n"] < nmm:
        piece()


def _conformer_qkv(proj_ag, u, w_bf, col0, dw_w, dw_b, ln_w, ln_b, B, S):
    T = proj_ag.shape[0]
    kw, C = dw_w.shape
    D = w_bf.shape[0]
    N = w_bf.shape[1] - col0
    wcols = 1024
    nwq = N // wcols
    ts = min(256, S)
    halo = 32
    assert kw - 1 <= halo and S % ts == 0 and ts >= halo and N % wcols == 0 and col0 % wcols == 0
    ns = S // ts
    wp = jnp.broadcast_to(dw_w[:, None, :], (kw, SUBLANES, C))
    return pl.pallas_call(
        functools.partial(_conv_qkv_kernel, ts=ts, kw=kw, halo=halo, eps=LN_EPS, nwq=nwq),
        grid=(B, ns),
        in_specs=[pl.BlockSpec((ts, C), lambda b, s: (b * ns + s, 0)),
                  pl.BlockSpec((ts, C), lambda b, s: (b * ns + s, 1)),
                  pl.BlockSpec((kw, SUBLANES, C), lambda b, s: (0, 0, 0)),
                  pl.BlockSpec((1, C), lambda b, s: (0, 0)),
                  pl.BlockSpec((1, C), lambda b, s: (0, 0)),
                  pl.BlockSpec((1, C), lambda b, s: (0, 0)),
                  pl.BlockSpec((ts, D), lambda b, s: (b * ns + s, 0))]
                 + [pl.BlockSpec((D, wcols), functools.partial(lambda b, s, n: (0, col0 // wcols + n), n=n),
                                 pipeline_mode=pl.Buffered(1)) for n in range(nwq)],
        out_specs=[pl.BlockSpec((ts, C), lambda b, s: (b * ns + s, 0)),
                   pl.BlockSpec((ts, N), lambda b, s: (b * ns + s, 0))],
        out_shape=[jax.ShapeDtypeStruct((T, C), BF16), jax.ShapeDtypeStruct((T, N), BF16)],
        scratch_shapes=[pltpu.VMEM((ts + halo, C), F32), pltpu.VMEM((ts, C), F32),
                        pltpu.VMEM((SUBLANES - 1, ts + halo - SUBLANES, C), F32),
                        pltpu.VMEM((ts, D), BF16)],
        compiler_params=_cparams(("arbitrary", "arbitrary"), 48),
        name="conformer_qkv",
    )(proj_ag, proj_ag, wp, dw_b.reshape(1, C), ln_w.reshape(1, C), ln_b.reshape(1, C), u, *([w_bf] * nwq))


def _attn_kernel(lq1_ref, lk1_ref, lq2_ref, lk2_ref, q_ref, k_ref, v_ref, sw_ref, o_ref,
                 vxt_ref, qq_ref, *scr, tq, hd, hp, ones_rows, lam_init, eps):
    i = pl.program_id(2)
    vd = 2 * hd
    nkb = v_ref.shape[0] // tq
    acc_refs, m_refs, sa_refs, sb_refs = (scr[n * hp:(n + 1) * hp] for n in range(4))

    @pl.when(i == 0)
    def _():
        for u in range(hp):
            for jb in range(nkb):
                vblk = v_ref[jb * tq:(jb + 1) * tq, u * vd:(u + 1) * vd].astype(F32)
                vxt_ref[u, jb, 0:vd, :] = vblk.T.astype(vxt_ref.dtype)
                vxt_ref[u, jb, vd:vd + ones_rows, :] = jnp.ones((ones_rows, tq), vxt_ref.dtype)

    lam = (jnp.exp(jnp.sum(lq1_ref[...] * lk1_ref[...], axis=-1, keepdims=True))
           - jnp.exp(jnp.sum(lq2_ref[...] * lk2_ref[...], axis=-1, keepdims=True)) + lam_init)

    for u in range(hp):
        q = q_ref[:, u * vd:(u + 1) * vd]
        qs = q * jnp.asarray(hd ** -0.5, q.dtype)
        lane = lax.broadcasted_iota(jnp.int32, q.shape, 1)
        zero = jnp.zeros_like(qs)
        qq_ref[u, 0:tq, :] = jnp.where(lane < hd, qs, zero)
        qq_ref[u, tq:2 * tq, :] = jnp.where(lane >= hd, qs, zero)
        acc_refs[u][...] = jnp.zeros(acc_refs[u].shape, F32)
        m_refs[u][...] = jnp.full(m_refs[u].shape, NEG_BIG, F32)

    def scores(j, u, dst):
        rows = pl.ds(pl.multiple_of(j * tq, tq), tq)
        kb = k_ref[rows, u * vd:(u + 1) * vd]
        dst[...] = lax.dot_general(kb, qq_ref[u], (((1,), (1,)), ((), ())), preferred_element_type=F32)

    def softmax_pv(j, u, src, masked):
        st = src[...]
        if masked:
            r = lax.broadcasted_iota(jnp.int32, st.shape, 0)
            c = lax.broadcasted_iota(jnp.int32, st.shape, 1)
            st = jnp.where(r <= jnp.where(c >= tq, c - tq, c), st, NEG_BIG)
        m_old = m_refs[u][...]
        m_new = jnp.maximum(m_old, jnp.max(st, axis=0, keepdims=True))
        alpha = jnp.exp(m_old - m_new)
        pt = jnp.exp(st - m_new).astype(vxt_ref.dtype)
        pv = jnp.dot(vxt_ref[u, j], pt, preferred_element_type=F32)
        acc_refs[u][...] = alpha * acc_refs[u][...] + pv
        m_refs[u][...] = m_new

    def half(j, cur, nxt):
        for u in range(hp):
            scores(j + 1, u, nxt[u])
            softmax_pv(j, u, cur[u], False)

    for u in range(hp):
        scores(0, u, sa_refs[u])

    def pair(t, carry):
        half(2 * t, sa_refs, sb_refs)
        half(2 * t + 1, sb_refs, sa_refs)
        return carry

    lax.fori_loop(0, i // 2, pair, 0)

    @pl.when(i % 2 == 1)
    def _():
        half(i - 1, sa_refs, sb_refs)
        for u in range(hp):
            softmax_pv(i, u, sb_refs[u], True)

    @pl.when(i % 2 == 0)
    def _():
        for u in range(hp):
            softmax_pv(i, u, sa_refs[u], True)

    for u in range(hp):
        acc = acc_refs[u][...]
        o12 = acc[0:vd] * (1.0 / acc[vd:vd + 1])
        ot = o12[:, 0:tq] - lam * o12[:, tq:2 * tq]
        msq = jnp.mean(ot * ot, axis=0, keepdims=True)
        o = (ot * lax.rsqrt(msq + eps)).T
        o_ref[:, u * vd:(u + 1) * vd] = (o * sw_ref[...] * (1.0 - lam_init)).astype(o_ref.dtype)


def _diff_attn(proj, lq1, lk1, lq2, lk2, subln_w, B, S, d_conv, d_attn, lam_init):
    T = proj.shape[0]
    vd = subln_w.shape[0]
    hd = lq1.shape[0]
    assert vd == LANES and 2 * hd == vd
    H = d_attn // vd
    hp = 8
    ones_rows = 16
    tq = min(256, S)
    nq = S // tq
    assert H % hp == 0
    qc = 2 * d_conv // (hp * vd)
    kc = qc + H // hp
    vc = kc + H // hp
    lspec = pl.BlockSpec((1, hd), lambda b, h, i: (0, 0))
    return pl.pallas_call(
        functools.partial(_attn_kernel, tq=tq, hd=hd, hp=hp, ones_rows=ones_rows, lam_init=lam_init, eps=SUBLN_EPS),
        grid=(B, H // hp, nq),
        in_specs=[lspec, lspec, lspec, lspec,
                  pl.BlockSpec((tq, hp * vd), lambda b, h, i: (b * nq + i, qc + h)),
                  pl.BlockSpec((S, hp * vd), lambda b, h, i: (b, kc + h)),
                  pl.BlockSpec((S, hp * vd), lambda b, h, i: (b, vc + h)),
                  pl.BlockSpec((1, vd), lambda b, h, i: (0, 0))],
        out_specs=pl.BlockSpec((tq, hp * vd), lambda b, h, i: (b * nq + i, h)),
        out_shape=jax.ShapeDtypeStruct((T, d_attn), BF16),
        scratch_shapes=[pltpu.VMEM((hp, S // tq, vd + ones_rows, tq), BF16),
                        pltpu.VMEM((hp, 2 * tq, vd), BF16)]
                       + [pltpu.VMEM((vd + ones_rows, 2 * tq), F32) for _ in range(hp)]
                       + [pltpu.VMEM((1, 2 * tq), F32) for _ in range(hp)]
                       + [pltpu.VMEM((tq, 2 * tq), F32) for _ in range(2 * hp)],
        compiler_params=_cparams(("arbitrary", "arbitrary", "arbitrary"), 52),
        name="diff_attn",
    )(lq1.reshape(1, hd), lk1.reshape(1, hd), lq2.reshape(1, hd), lk2.reshape(1, hd),
      proj, proj, proj, subln_w.reshape(1, vd))


def _pack_bf16_pairs(v):
    n = v.shape[1] // 2
    return _pack_bf16_words(v[:, 0:n], v[:, n:2 * n])


def _pack_bf16_words(lo, hi):
    lo_bits = lax.bitcast_convert_type(lo.astype(BF16).astype(F32), jnp.uint32)
    hi_bits = lax.bitcast_convert_type(hi.astype(BF16).astype(F32), jnp.uint32)
    return (lo_bits >> 16) | (hi_bits & jnp.uint32(0xFFFF0000))


def _unpack_bf16_pairs(w):
    lo = lax.bitcast_convert_type(w << 16, F32)
    hi = lax.bitcast_convert_type(w & jnp.uint32(0xFFFF0000), F32)
    return lo, hi


def _store_row_tiles(ref, row0, words):
    m, w = words.shape
    assert w == SUBLANES * LANES
    for s in range(SUBLANES):
        ref[pl.ds(row0 * SUBLANES + s, m, stride=SUBLANES), :] = words[:, s * LANES:(s + 1) * LANES]


def _load_row_tiles(ref, row0, m):
    return jnp.concatenate([ref[pl.ds(row0 * SUBLANES + s, m, stride=SUBLANES), :] for s in range(SUBLANES)], axis=1)


def _outproj_kernel(x_ref, yc_ref, ya_ref, wo_ref, fw_ref, wr_ref, br_ref, h_ref, lg_ref, ug_ref, *, eps, nsub):
    dc = yc_ref.shape[1]
    sub = x_ref.shape[0] // nsub
    mixes = []
    for t in range(nsub):
        rows = slice(t * sub, (t + 1) * sub)
        mixes.append(jnp.dot(yc_ref[rows, :], wo_ref[0:dc, :], preferred_element_type=F32)
                     + jnp.dot(ya_ref[rows, :], wo_ref[dc:, :], preferred_element_type=F32))
    for t in range(nsub):
        rows = slice(t * sub, (t + 1) * sub)
        h = x_ref[rows, :] + mixes[t]
        h_ref[rows, :] = h
        ms = jnp.mean(h * h, axis=-1, keepdims=True)
        un = h * lax.rsqrt(ms + eps) * fw_ref[...]
        hi = un.astype(BF16)
        lo = (un - hi.astype(F32)).astype(BF16)
        hh = jnp.dot(hi, wr_ref[...], preferred_element_type=F32)
        lh = jnp.dot(lo, wr_ref[:, 0:LANES], preferred_element_type=F32)
        lg_ref[rows, :] = hh[:, 0:LANES] + hh[:, LANES:2 * LANES] + lh + br_ref[...]
        _store_row_tiles(ug_ref, t * sub, _pack_bf16_pairs(un))


def _outproj(x2, y_conv, y_attn, wo_bf, ffn_w, wr_hl, br):
    T, D = x2.shape
    dc = y_conv.shape[1]
    da = y_attn.shape[1]
    tm = min(512, T)
    nsub = 2 if tm % 512 == 0 else 1
    return pl.pallas_call(
        functools.partial(_outproj_kernel, eps=RMS_EPS, nsub=nsub),
        grid=(T // tm,),
        in_specs=[pl.BlockSpec((tm, D), lambda i: (i, 0)),
                  pl.BlockSpec((tm, dc), lambda i: (i, 0)),
                  pl.BlockSpec((tm, da), lambda i: (i, 0)),
                  pl.BlockSpec((dc + da, D), lambda i: (0, 0), pipeline_mode=pl.Buffered(1)),
                  pl.BlockSpec((1, D), lambda i: (0, 0)),
                  pl.BlockSpec((D, 2 * LANES), lambda i: (0, 0)),
                  pl.BlockSpec((1, LANES), lambda i: (0, 0))],
        out_specs=[pl.BlockSpec((tm, D), lambda i: (i, 0)),
                   pl.BlockSpec((tm, LANES), lambda i: (i, 0)),
                   pl.BlockSpec((tm * SUBLANES, LANES), lambda i: (i, 0))],
        out_shape=[jax.ShapeDtypeStruct((T, D), F32), jax.ShapeDtypeStruct((T, LANES), F32),
                   jax.ShapeDtypeStruct((T * SUBLANES, LANES), jnp.uint32)],
        compiler_params=_cparams(("arbitrary",), 56),
        name="outproj",
    )(x2, y_conv, y_attn, wo_bf, ffn_w, wr_hl, br)


def _route_kernel(lg_ref, info_ref, cols_ref, cnt_ref, *, ng, epg):
    i = pl.program_id(0)
    lg = lg_ref[...]
    tm = lg.shape[0]
    lane = lax.broadcasted_iota(jnp.int32, lg.shape, 1)
    lanef = lane.astype(F32)
    ne = ng * epg

    def first_argmax(vals):
        mx = jnp.max(vals, axis=-1, keepdims=True)
        idx = jnp.min(jnp.where(vals == mx, lanef, float(LANES)), axis=-1, keepdims=True)
        return mx, idx

    gmask = lane < ng
    gl = jnp.where(gmask, lg, NEG_BIG)
    gmax, gsel = first_argmax(gl)
    gsum = jnp.sum(jnp.where(gmask, jnp.exp(gl - gmax), 0.0), axis=-1, keepdims=True)
    g_w = 1.0 / gsum
    lo = gsel * epg + ng
    emask = (lanef >= lo) & (lanef < lo + epg)
    el = jnp.where(emask, lg, NEG_BIG)
    v1, i1 = first_argmax(el)
    el2 = jnp.where(lanef == i1, NEG_BIG, el)
    v2, i2 = first_argmax(el2)
    e2 = jnp.exp(v2 - v1)
    p1 = 1.0 / (1.0 + e2)
    gate1 = g_w * p1
    gate2 = g_w * (e2 * p1)

    oh1 = lanef == i1
    oh2 = lanef == i2
    cmat = jnp.where(oh1 | oh2, 1.0, 0.0).astype(BF16)
    r = lax.broadcasted_iota(jnp.int32, (tm, tm), 0)
    c = lax.broadcasted_iota(jnp.int32, (tm, tm), 1)
    tri = jnp.where(c < r, 1.0, 0.0).astype(BF16)

    @pl.when(i == 0)
    def _():
        cnt_ref[...] = jnp.zeros(cnt_ref.shape, F32)

    carry = cnt_ref[0:1, :]
    prefix = jnp.dot(tri, cmat, preferred_element_type=F32) + carry
    rank1 = jnp.sum(jnp.where(oh1, prefix, 0.0), axis=-1, keepdims=True)
    rank2 = jnp.sum(jnp.where(oh2, prefix, 0.0), axis=-1, keepdims=True)
    cnt_ref[...] = jnp.broadcast_to(carry + jnp.sum(cmat.astype(F32), axis=0, keepdims=True), cnt_ref.shape)

    info = jnp.where(lane == 0, i1 - ng,
           jnp.where(lane == 1, i2 - ng,
           jnp.where(lane == 2, gate1,
           jnp.where(lane == 3, gate2,
           jnp.where(lane == 4, rank1,
           jnp.where(lane == 5, rank2, 0.0))))))
    info_ref[...] = info
    cols_ref[...] = info.T[0:SUBLANES, :]


def _route(logits, ng, epg):
    T = logits.shape[0]
    tm = min(512, T)
    return pl.pallas_call(
        functools.partial(_route_kernel, ng=ng, epg=epg),
        grid=(T // tm,),
        in_specs=[pl.BlockSpec((tm, LANES), lambda i: (i, 0))],
        out_specs=[pl.BlockSpec((tm, LANES), lambda i: (i, 0)),
                   pl.BlockSpec((SUBLANES, tm), lambda i: (0, i)),
                   pl.BlockSpec((8, LANES), lambda i: (0, 0))],
        out_shape=[jax.ShapeDtypeStruct((T, LANES), F32), jax.ShapeDtypeStruct((SUBLANES, T), F32),
                   jax.ShapeDtypeStruct((8, LANES), F32)],
        compiler_params=_cparams(("arbitrary",), 32),
        name="route",
    )(logits)


def _dest_kernel(pstart_ref, cols_ref, dest_ref, *, ne):
    eid = cols_ref[0:TOP_K, :].astype(jnp.int32)
    rank = cols_ref[4:4 + TOP_K, :].astype(jnp.int32)

    def body(e, acc):
        return acc + jnp.where(eid == e, pstart_ref[e], 0)

    start = lax.fori_loop(0, ne, body, jnp.zeros(eid.shape, jnp.int32))
    dest_ref[...] = jnp.zeros(dest_ref.shape, jnp.int32)
    dest_ref[0:TOP_K, :] = start + rank


def _dest(pad_starts, cols):
    R, T = cols.shape
    grid_spec = pltpu.PrefetchScalarGridSpec(
        num_scalar_prefetch=1,
        grid=(1,),
        in_specs=[pl.BlockSpec((R, T), lambda i, p: (0, 0))],
        out_specs=pl.BlockSpec((R, T), lambda i, p: (0, 0)),
    )
    return pl.pallas_call(
        functools.partial(_dest_kernel, ne=pad_starts.shape[0]),
        grid_spec=grid_spec,
        out_shape=jax.ShapeDtypeStruct((R, T), jnp.int32),
        compiler_params=_cparams(("arbitrary",), 16),
        name="dest",
    )(pad_starts, cols)


ROW_UNROLL = 8


def _dispatch_kernel(dest_ref, pend_ref, ug_ref, xs_hbm, ring, zbuf, sems, zsem, *, tm, blk, ne, ntok):
    i = pl.program_id(0)
    n = pl.num_programs(0)
    slot = i % 2

    def seg_tail(e):
        end = pend_ref[e]
        start = jnp.where(e == 0, 0, pend_ref[jnp.maximum(e - 1, 0)])
        tail = pl.multiple_of(jnp.maximum(end - blk, 0) * SUBLANES, blk * SUBLANES)
        return pltpu.make_async_copy(zbuf, xs_hbm.at[pl.ds(tail, blk * SUBLANES)], zsem), end > start

    @pl.when(i == 0)
    def _():
        zbuf[...] = jnp.zeros(zbuf.shape, zbuf.dtype)

        def zstart(p, carry):
            for q in range(2):
                cp, nonempty = seg_tail(2 * p + q)

                @pl.when(nonempty)
                def _():
                    cp.start(priority=q)
            return carry

        def zwait(e, carry):
            cp, nonempty = seg_tail(e)

            @pl.when(nonempty)
            def _():
                cp.wait()
            return carry

        def spare(b):
            row = pl.multiple_of(b * blk, blk)
            tile = pl.multiple_of(row * SUBLANES, blk * SUBLANES)
            return pltpu.make_async_copy(zbuf, xs_hbm.at[pl.ds(tile, blk * SUBLANES)], zsem), row >= pend_ref[ne - 1]

        def sstart(p, carry):
            for q in range(2):
                cp, unused = spare(2 * p + q)

                @pl.when(unused)
                def _():
                    cp.start(priority=q)
            return carry

        def swait(b, carry):
            cp, unused = spare(b)

            @pl.when(unused)
            def _():
                cp.wait()
            return carry

        nblk = xs_hbm.shape[0] // (blk * SUBLANES)
        assert ne % 2 == 0 and nblk % 2 == 0
        lax.fori_loop(0, ne // 2, zstart, 0)
        lax.fori_loop(0, nblk // 2, sstart, 0)
        lax.fori_loop(0, ne, zwait, 0)
        lax.fori_loop(0, nblk, swait, 0)

    def drain(s):
        for _ in range(TOP_K):
            pltpu.make_async_copy(ring.at[s], xs_hbm.at[pl.ds(0, tm * SUBLANES)], sems.at[s]).wait()

    @pl.when(i >= 2)
    def _():
        drain(slot)

    ring[slot] = ug_ref[...]

    def issue(c, carry):
        r0 = pl.multiple_of(c * ROW_UNROLL, ROW_UNROLL)
        for s in range(ROW_UNROLL):
            for k in range(TOP_K):
                d = dest_ref[k * ntok + i * tm + r0 + s]
                src = ring.at[slot, pl.ds(pl.multiple_of((r0 + s) * SUBLANES, SUBLANES), SUBLANES)]
                dst = xs_hbm.at[pl.ds(pl.multiple_of(d * SUBLANES, SUBLANES), SUBLANES)]
                pltpu.make_async_copy(src, dst, sems.at[slot]).start(priority=k % 2)
        return carry

    lax.fori_loop(0, tm // ROW_UNROLL, issue, 0)

    @pl.when(i == n - 1)
    def _():
        drain(slot)

        @pl.when(n >= 2)
        def _():
            drain(1 - slot)


def _dispatch(dest_flat, pad_ends, ug, nrows, tm, blk):
    T = ug.shape[0] // SUBLANES
    ne = pad_ends.shape[0]
    grid_spec = pltpu.PrefetchScalarGridSpec(
        num_scalar_prefetch=2,
        grid=(T // tm,),
        in_specs=[pl.BlockSpec((tm * SUBLANES, LANES), lambda i, d, p: (i, 0))],
        out_specs=pl.BlockSpec(memory_space=pl.ANY),
        scratch_shapes=[pltpu.VMEM((2, tm * SUBLANES, LANES), ug.dtype),
                        pltpu.VMEM((blk * SUBLANES, LANES), ug.dtype),
                        pltpu.SemaphoreType.DMA((2,)),
                        pltpu.SemaphoreType.DMA(())],
    )
    return pl.pallas_call(
        functools.partial(_dispatch_kernel, tm=tm, blk=blk, ne=ne, ntok=T),
        grid_spec=grid_spec,
        out_shape=jax.ShapeDtypeStruct((nrows * SUBLANES, LANES), ug.dtype),
        compiler_params=_cparams(("arbitrary",), 32),
        name="dispatch",
    )(dest_flat, pad_ends, ug)


WEIGHT_DMA_PRIORITY = (0, 1, 1)
WEIGHT_SLOTS = 3


def _experts_kernel(bexp_ref, first_ref, slot_ref, next_ref, lead_ref, nused_ref, x_ref, w1_hbm, w3_hbm, w2_hbm,
                    y_ref, w1b, w3b, w2b, wsems):
    i = pl.program_id(0)
    nused = nused_ref[0]

    def weight_copies(e, s):
        return (pltpu.make_async_copy(w1_hbm.at[e], w1b.at[s], wsems.at[s, 0]),
                pltpu.make_async_copy(w3_hbm.at[e], w3b.at[s], wsems.at[s, 1]),
                pltpu.make_async_copy(w2_hbm.at[e], w2b.at[s], wsems.at[s, 2]))

    def start_weights(e, s):
        for n, cp in enumerate(weight_copies(e, s)):
            cp.start(priority=WEIGHT_DMA_PRIORITY[n])

    @pl.when(i == 0)
    def _():
        for n in range(WEIGHT_SLOTS - 1):
            @pl.when(lead_ref[n] >= 0)
            def _():
                start_weights(lead_ref[n], n)

    @pl.when(i < nused)
    def _():
        s = slot_ref[i]

        @pl.when(first_ref[i] == 1)
        def _():
            @pl.when(next_ref[i] >= 0)
            def _():
                start_weights(next_ref[i], (s + WEIGHT_SLOTS - 1) % WEIGHT_SLOTS)
            for cp in weight_copies(bexp_ref[i], s):
                cp.wait()

        blk = x_ref.shape[0] // SUBLANES
        x_lo, x_hi = _unpack_bf16_pairs(_load_row_tiles(x_ref, 0, blk))
        half = x_lo.shape[1]
        a = (jnp.dot(x_lo, w1b[s, 0:half, :], preferred_element_type=F32)
             + jnp.dot(x_hi, w1b[s, half:2 * half, :], preferred_element_type=F32))
        b = (jnp.dot(x_lo, w3b[s, 0:half, :], preferred_element_type=F32)
             + jnp.dot(x_hi, w3b[s, half:2 * half, :], preferred_element_type=F32))
        hdn = a * jax.nn.sigmoid(a) * b
        cw = 2 * LANES
        for c0 in range(0, half, cw):
            y_lo = jnp.dot(hdn, w2b[s, :, c0:c0 + cw], preferred_element_type=F32)
            y_hi = jnp.dot(hdn, w2b[s, :, half + c0:half + c0 + cw], preferred_element_type=F32)
            words = _pack_bf16_words(y_lo, y_hi)
            for g in range(cw // LANES):
                y_ref[pl.ds(c0 // LANES + g, blk, stride=SUBLANES), :] = words[:, g * LANES:(g + 1) * LANES]

    @pl.when(i >= nused)
    def _():
        y_ref[...] = jnp.zeros(y_ref.shape, y_ref.dtype)


def _experts(bexp, first, slot, nxt, lead, nused, xs, w1, w3, w2, blk):
    P = xs.shape[0] // SUBLANES
    E, D, Fh = w1.shape
    nblk = P // blk
    smap = lambda i, *_: (i, 0)
    xmap = lambda i, be, fi, sl, nx, ld, nu: (jnp.minimum(i, nu[0] - 1), 0)
    grid_spec = pltpu.PrefetchScalarGridSpec(
        num_scalar_prefetch=6,
        grid=(nblk,),
        in_specs=[pl.BlockSpec((blk * SUBLANES, LANES), xmap),
                  pl.BlockSpec(memory_space=pl.ANY),
                  pl.BlockSpec(memory_space=pl.ANY),
                  pl.BlockSpec(memory_space=pl.ANY)],
        out_specs=pl.BlockSpec((blk * SUBLANES, LANES), smap),
        scratch_shapes=[pltpu.VMEM((WEIGHT_SLOTS, D, Fh), w1.dtype),
                        pltpu.VMEM((WEIGHT_SLOTS, D, Fh), w3.dtype),
                        pltpu.VMEM((WEIGHT_SLOTS, Fh, D), w2.dtype),
                        pltpu.SemaphoreType.DMA((WEIGHT_SLOTS, 3))],
    )
    return pl.pallas_call(
        _experts_kernel,
        grid_spec=grid_spec,
        out_shape=jax.ShapeDtypeStruct(xs.shape, xs.dtype),
        compiler_params=_cparams(("arbitrary",), 56),
        name="experts",
    )(bexp, first, slot, nxt, lead, nused, xs, w1, w3, w2)


def _row_gather(src_hbm, idx_ref, base, dst, sem, n):
    def body(c, carry):
        r0 = pl.multiple_of(c * ROW_UNROLL, ROW_UNROLL)
        for s in range(ROW_UNROLL):
            d = idx_ref[base + r0 + s]
            pltpu.make_async_copy(src_hbm.at[pl.ds(pl.multiple_of(d * SUBLANES, SUBLANES), SUBLANES)],
                                  dst.at[pl.ds(pl.multiple_of((r0 + s) * SUBLANES, SUBLANES), SUBLANES)],
                                  sem).start(priority=s % 2)
        return carry
    lax.fori_loop(0, n // ROW_UNROLL, body, 0)


def _row_gather_wait(src_hbm, dst, sem, n):
    pltpu.make_async_copy(src_hbm.at[pl.ds(0, n * SUBLANES)], dst, sem).wait()


def _combine_kernel(dest_ref, h_ref, info_ref, y_hbm, fw_ref, o_ref, ybuf, sems, *, tm, eps, ntok):
    i = pl.program_id(0)
    n = pl.num_programs(0)
    slot = i % 2
    nrow = TOP_K * tm

    def gather_step(step, s):
        for k in range(TOP_K):
            _row_gather(y_hbm, dest_ref, k * ntok + step * tm,
                        ybuf.at[s, pl.ds(k * tm * SUBLANES, tm * SUBLANES)], sems.at[s], tm)

    @pl.when(i == 0)
    def _():
        gather_step(0, 0)

    @pl.when(i + 1 < n)
    def _():
        gather_step(i + 1, 1 - slot)

    _row_gather_wait(y_hbm, ybuf.at[slot], sems.at[slot], nrow)
    half = o_ref.shape[1] // 2
    rc = tm

    def rows(ci, carry):
        r0 = pl.multiple_of(ci * rc, rc)
        rs = pl.ds(r0, rc)
        info = info_ref[rs, :]
        g1 = info[:, 2:3]
        g2 = info[:, 3:4]
        y1_lo, y1_hi = _unpack_bf16_pairs(_load_row_tiles(ybuf.at[slot], r0, rc))
        y2_lo, y2_hi = _unpack_bf16_pairs(_load_row_tiles(ybuf.at[slot], tm + r0, rc))
        h_lo = h_ref[rs, 0:half] + (g1 * y1_lo + g2 * y2_lo)
        h_hi = h_ref[rs, half:2 * half] + (g1 * y1_hi + g2 * y2_hi)
        ms = (jnp.sum(h_lo * h_lo, axis=-1, keepdims=True)
              + jnp.sum(h_hi * h_hi, axis=-1, keepdims=True)) / (2 * half)
        r = lax.rsqrt(ms + eps)
        o_ref[rs, 0:half] = h_lo * r * fw_ref[:, 0:half]
        o_ref[rs, half:2 * half] = h_hi * r * fw_ref[:, half:2 * half]
        return carry

    lax.fori_loop(0, tm // rc, rows, 0)


def _combine(dest, h, info, yb, final_w, tm):
    T, D = h.shape
    grid_spec = pltpu.PrefetchScalarGridSpec(
        num_scalar_prefetch=1,
        grid=(T // tm,),
        in_specs=[pl.BlockSpec((tm, D), lambda i, d: (i, 0)),
                  pl.BlockSpec((tm, LANES), lambda i, d: (i, 0)),
                  pl.BlockSpec(memory_space=pl.ANY),
                  pl.BlockSpec((1, D), lambda i, d: (0, 0))],
        out_specs=pl.BlockSpec((tm, D), lambda i, d: (i, 0)),
        scratch_shapes=[pltpu.VMEM((2, TOP_K * tm * SUBLANES, LANES), yb.dtype),
                        pltpu.SemaphoreType.DMA((2,))],
    )
    return pl.pallas_call(
        functools.partial(_combine_kernel, tm=tm, eps=RMS_EPS, ntok=T),
        grid_spec=grid_spec,
        out_shape=jax.ShapeDtypeStruct((T, D), F32),
        compiler_params=_cparams(("arbitrary",), 48),
        name="combine",
    )(dest, h, info, yb, final_w)


def _layer(h_in, l, B, S, mix_norm_w, w_in, conv_dw_w, conv_dw_b, conv_ln_w, conv_ln_b,
           lam_q1, lam_k1, lam_q2, lam_k2, attn_subln_w, w_out, ffn_norm_w,
           w_group, b_group, w_expert_gate, b_expert_gate, w1, w3, w2):
    T, D = h_in.shape
    d_conv = conv_dw_w.shape[1]
    d_attn = (w_in.shape[1] - 2 * d_conv) // 3
    ng = w_group.shape[1]
    ne = w_expert_gate.shape[1]
    epg = ne // ng
    assert ng + ne <= LANES
    lam_init = 0.8 - 0.6 * math.exp(-0.3 * l)

    w_bf = w_in.astype(BF16)
    proj_ag, u = _norm_inproj(h_in, mix_norm_w.reshape(1, D), w_bf, 2 * d_conv)
    y_conv, qkv = _conformer_qkv(proj_ag, u, w_bf, 2 * d_conv, conv_dw_w, conv_dw_b, conv_ln_w, conv_ln_b, B, S)
    y_attn = _diff_attn(qkv, lam_q1, lam_k1, lam_q2, lam_k2, attn_subln_w, B, S, 0, d_attn, lam_init)

    wr = jnp.concatenate([w_group, w_expert_gate, jnp.zeros((D, LANES - ng - ne), F32)], axis=1)
    wr_hi = wr.astype(BF16)
    wr_lo = (wr - wr_hi.astype(F32)).astype(BF16)
    br = jnp.concatenate([b_group, b_expert_gate.reshape(-1), jnp.zeros((LANES - ng - ne,), F32)]).reshape(1, LANES)
    wr_hl = jnp.concatenate([wr_hi, wr_lo], axis=1)
    h, logits, ug = _outproj(h_in, y_conv, y_attn, w_out.astype(BF16), ffn_norm_w.reshape(1, D), wr_hl, br)

    info, cols, cnt = _route(logits, ng, epg)

    blk = 256
    tmd = min(256, T)
    A = T * TOP_K
    nblk = (A + ne * (blk - 1) + blk - 1) // blk
    i32 = jnp.int32
    counts = cnt[0, ng:ng + ne].astype(i32)
    padded = (counts + blk - 1) // blk * blk
    pad_ends = jnp.cumsum(padded).astype(i32)
    pad_starts = pad_ends - padded
    dest = _dest(pad_starts, cols)[0:TOP_K].reshape(-1)
    nused = (pad_ends[-1] // blk).astype(i32)
    bpos = jnp.arange(nblk, dtype=i32)
    brow = jnp.minimum(bpos, nused - 1) * blk
    bexp = jnp.minimum(jnp.sum((pad_ends[None, :] <= brow[:, None]).astype(i32), axis=1), ne - 1)
    first = ((bpos < nused) & ((bpos == 0) | (bexp != jnp.roll(bexp, 1)))).astype(i32)
    slot = ((jnp.cumsum(first) - 1) % WEIGHT_SLOTS).astype(i32)
    used_idx = jnp.where(padded > 0, jnp.arange(ne, dtype=i32), ne)
    suffix_min = lax.cummin(used_idx, reverse=True)
    next_used = jnp.concatenate([suffix_min[1:], jnp.full((2,), ne, i32)])
    ahead = jnp.arange(ne, dtype=i32)
    for _ in range(WEIGHT_SLOTS - 1):
        ahead = next_used[ahead]
    ahead = jnp.where(ahead >= ne, -1, ahead)
    nxt = ahead[bexp].astype(i32)
    lead = [suffix_min[0]]
    for _ in range(WEIGHT_SLOTS - 2):
        lead.append(next_used[lead[-1]])
    lead = jnp.stack([jnp.where(e >= ne, -1, e) for e in lead]).astype(i32)

    xs = _dispatch(dest, pad_ends, ug, nblk * blk, tmd, blk)
    yb = _experts(bexp, first, slot, nxt, lead, nused.reshape(1), xs, w1, w3, w2, blk)
    return h, info, dest, yb


def kernel(x, mix_norm_w, w_in, conv_dw_w, conv_dw_b, conv_ln_w, conv_ln_b, lam_q1, lam_k1, lam_q2, lam_k2,
           attn_subln_w, w_out, ffn_norm_w, w_group, b_group, w_expert_gate, b_expert_gate, w1, w3, w2,
           final_norm_w):
    B, S, D = x.shape
    depth = w_in.shape[0]
    assert depth == 1
    T = B * S
    tmc = min(256, T)
    h = x.reshape(T, D)
    for l in range(depth):
        h, info, dest, yb = _layer(
            h, l, B, S, mix_norm_w[l], w_in[l], conv_dw_w[l], conv_dw_b[l], conv_ln_w[l], conv_ln_b[l],
            lam_q1[l], lam_k1[l], lam_q2[l], lam_k2[l], attn_subln_w[l], w_out[l], ffn_norm_w[l],
            w_group[l], b_group[l], w_expert_gate[l], b_expert_gate[l], w1[l], w3[l], w2[l])
        h = _combine(dest, h, info, yb, final_norm_w.reshape(1, D), tmc)
    return h.reshape(B, S, D)
```

```python
import functools
import math

import jax
import jax.numpy as jnp
from jax import lax
from jax.experimental import pallas as pl
from jax.experimental.pallas import tpu as pltpu

F32 = jnp.float32
BF16 = jnp.bfloat16

RMS_EPS = 1e-6
SUBLN_EPS = 1e-5
LN_EPS = 1e-5
TOP_K = 2
LANES = 128
SUBLANES = 8
NEG_BIG = -1e30
MIB = 1024 * 1024


def _cparams(sem, vmem_mib):
    return pltpu.CompilerParams(dimension_semantics=sem, vmem_limit_bytes=vmem_mib * MIB)


def _norm_inproj_kernel(x_ref, nw_ref, w_ref, o_ref, u_ref, *, eps, rc):
    @pl.when(pl.program_id(1) == 0)
    def _():
        for r0 in range(0, x_ref.shape[0], rc):
            rows = slice(r0, r0 + rc)
            x = x_ref[rows, :]
            ms = jnp.mean(x * x, axis=-1, keepdims=True)
            u = (x * lax.rsqrt(ms + eps) * nw_ref[...]).astype(u_ref.dtype)
            u_ref[rows, :] = u
            o_ref[rows, :] = jnp.dot(u, w_ref[...], preferred_element_type=F32).astype(o_ref.dtype)

    @pl.when(pl.program_id(1) > 0)
    def _():
        o_ref[...] = jnp.dot(u_ref[...], w_ref[...], preferred_element_type=F32).astype(o_ref.dtype)


def _norm_inproj(x2, nw, w_bf, ncols):
    T, D = x2.shape
    N = ncols
    tm = min(1024, T)
    tn = 1024
    return pl.pallas_call(
        functools.partial(_norm_inproj_kernel, eps=RMS_EPS, rc=256),
        grid=(T // tm, N // tn),
        in_specs=[pl.BlockSpec((tm, D), lambda i, j: (i, 0)),
                  pl.BlockSpec((1, D), lambda i, j: (0, 0)),
                  pl.BlockSpec((D, tn), lambda i, j: (0, j))],
        out_specs=[pl.BlockSpec((tm, tn), lambda i, j: (i, j)),
                   pl.BlockSpec((tm, D), lambda i, j: (i, 0))],
        out_shape=[jax.ShapeDtypeStruct((T, N), BF16), jax.ShapeDtypeStruct((T, D), BF16)],
        compiler_params=_cparams(("arbitrary", "arbitrary"), 48),
        name="norm_inproj",
    )(x2, nw, w_bf)


QKV_CHUNK = 2 * LANES
QKV_KSTEP = 4 * LANES


def _conv_qkv_kernel(a_ref, g_ref, w_ref, b_ref, lw_ref, lb_ref, u_ref, *rest, ts, kw, halo, eps, nwq):
    wq_refs = rest[0:nwq]
    o_ref, qkv_ref, ubuf, cbuf, sh, lhs = rest[nwq:]
    s = pl.program_id(1)
    C = a_ref.shape[1]
    wcols = wq_refs[0].shape[1]
    nmm = nwq * wcols // QKV_CHUNK

    @pl.when(s == 0)
    def _():
        ubuf[0:halo, :] = jnp.zeros((halo, C), F32)

    @pl.when(s > 0)
    def _():
        ubuf[0:halo, :] = ubuf[ts:ts + halo, :]

    lhs[...] = u_ref[...]

    nks = lhs.shape[1] // QKV_KSTEP
    pending = {"n": 0, "k": 0, "acc": None}

    def piece(after=None):
        n, k = pending["n"], pending["k"]
        if n >= nmm:
            return
        ks = slice(k * QKV_KSTEP, (k + 1) * QKV_KSTEP)
        if after is not None:
            allrows = jnp.sum(after[0:4], axis=0)
            bits = lax.bitcast_convert_type(jnp.concatenate([allrows, allrows], axis=0), jnp.uint32)
            zero = lax.bitcast_convert_type((bits >> 16) >> 16, F32).astype(lhs.dtype)
            first = (slice(0, 2 * SUBLANES), slice(k * QKV_KSTEP, k * QKV_KSTEP + LANES))
            lhs[first] = lhs[first] + zero
        wq_ref = wq_refs[n * QKV_CHUNK // wcols]
        ws = slice(n * QKV_CHUNK % wcols, n * QKV_CHUNK % wcols + QKV_CHUNK)
        part = jnp.dot(lhs[:, ks], wq_ref[ks, ws], preferred_element_type=F32)
        pending["acc"] = part if k == 0 else pending["acc"] + part
        if k == nks - 1:
            qkv_ref[:, n * QKV_CHUNK:(n + 1) * QKV_CHUNK] = pending["acc"].astype(qkv_ref.dtype)
            pending["n"], pending["k"] = n + 1, 0
        else:
            pending["k"] = k + 1

    rg = 64
    for r0 in range(0, ts, rg):
        a = a_ref[r0:r0 + rg, :].astype(F32)
        g = g_ref[r0:r0 + rg, :].astype(F32)
        ubuf[halo + r0:halo + r0 + rg, :] = a * jax.nn.sigmoid(g)

    nsh = sh.shape[1]
    for r in range(1, SUBLANES):
        for i0 in range(0, nsh, rg):
            n = min(rg, nsh - i0)
            sh[r - 1, i0:i0 + n, :] = ubuf[i0 + r:i0 + r + n, :]

    off = halo - (kw - 1)
    rc = 128
    ntap = (ts // rc) * (C // LANES) * kw
    tap_every = max(1, ntap // (nmm * nks))
    tapno = 0
    for r0 in range(0, ts, rc):
        for c in range(C // LANES):
            cs = slice(c * LANES, (c + 1) * LANES)
            acc = jnp.broadcast_to(b_ref[0:1, cs], (rc // SUBLANES, SUBLANES, LANES))
            for k in range(kw):
                q, r = divmod(off + k, SUBLANES)
                a0 = r0 + q * SUBLANES
                tap = ubuf[a0:a0 + rc, cs] if r == 0 else sh[r - 1, a0:a0 + rc, cs]
                acc = acc + w_ref[k, :, cs] * tap.reshape(rc // SUBLANES, SUBLANES, LANES)
                tapno += 1
                if tapno % tap_every == 0:
                    piece(after=acc)
            cbuf[r0:r0 + rc, cs] = acc.reshape(rc, LANES)
        cv = cbuf[r0:r0 + rc, :]
        mu = jnp.mean(cv, axis=-1, keepdims=True)
        d = cv - mu
        var = jnp.mean(d * d, axis=-1, keepdims=True)
        un = d * lax.rsqrt(var + eps) * lw_ref[...] + lb_ref[...]
        o_ref[r0:r0 + rc, :] = (un * jax.nn.sigmoid(un)).astype(o_ref.dtype)
    while pending["n"] < nmm:
        piece()


def _conformer_qkv(proj_ag, u, w_bf, col0, dw_w, dw_b, ln_w, ln_b, B, S):
    T = proj_ag.shape[0]
    kw, C = dw_w.shape
    D = w_bf.shape[0]
    N = w_bf.shape[1] - col0
    wcols = 1024
    nwq = N // wcols
    ts = min(256, S)
    halo = 32
    assert kw - 1 <= halo and S % ts == 0 and ts >= halo and N % wcols == 0 and col0 % wcols == 0
    ns = S // ts
    wp = jnp.broadcast_to(dw_w[:, None, :], (kw, SUBLANES, C))
    return pl.pallas_call(
        functools.partial(_conv_qkv_kernel, ts=ts, kw=kw, halo=halo, eps=LN_EPS, nwq=nwq),
        grid=(B, ns),
        in_specs=[pl.BlockSpec((ts, C), lambda b, s: (b * ns + s, 0)),
                  pl.BlockSpec((ts, C), lambda b, s: (b * ns + s, 1)),
                  pl.BlockSpec((kw, SUBLANES, C), lambda b, s: (0, 0, 0)),
                  pl.BlockSpec((1, C), lambda b, s: (0, 0)),
                  pl.BlockSpec((1, C), lambda b, s: (0, 0)),
                  pl.BlockSpec((1, C), lambda b, s: (0, 0)),
                  pl.BlockSpec((ts, D), lambda b, s: (b * ns + s, 0))]
                 + [pl.BlockSpec((D, wcols), functools.partial(lambda b, s, n: (0, col0 // wcols + n), n=n),
                                 pipeline_mode=pl.Buffered(1)) for n in range(nwq)],
        out_specs=[pl.BlockSpec((ts, C), lambda b, s: (b * ns + s, 0)),
                   pl.BlockSpec((ts, N), lambda b, s: (b * ns + s, 0))],
        out_shape=[jax.ShapeDtypeStruct((T, C), BF16), jax.ShapeDtypeStruct((T, N), BF16)],
        scratch_shapes=[pltpu.VMEM((ts + halo, C), F32), pltpu.VMEM((ts, C), F32),
                        pltpu.VMEM((SUBLANES - 1, ts + halo - SUBLANES, C), F32),
                        pltpu.VMEM((ts, D), BF16)],
        compiler_params=_cparams(("arbitrary", "arbitrary"), 48),
        name="conformer_qkv",
    )(proj_ag, proj_ag, wp, dw_b.reshape(1, C), ln_w.reshape(1, C), ln_b.reshape(1, C), u, *([w_bf] * nwq))


def _attn_kernel(lq1_ref, lk1_ref, lq2_ref, lk2_ref, q_ref, k_ref, v_ref, sw_ref, o_ref,
                 vxt_ref, qq_ref, *scr, tq, hd, hp, ones_rows, lam_init, eps):
    i = pl.program_id(2)
    vd = 2 * hd
    nkb = v_ref.shape[0] // tq
    acc_refs, m_refs, sa_refs, sb_refs = (scr[n * hp:(n + 1) * hp] for n in range(4))

    @pl.when(i == 0)
    def _():
        for u in range(hp):
            for jb in range(nkb):
                vblk = v_ref[jb * tq:(jb + 1) * tq, u * vd:(u + 1) * vd].astype(F32)
                vxt_ref[u, jb, 0:vd, :] = vblk.T.astype(vxt_ref.dtype)
                vxt_ref[u, jb, vd:vd + ones_rows, :] = jnp.ones((ones_rows, tq), vxt_ref.dtype)

    lam = (jnp.exp(jnp.sum(lq1_ref[...] * lk1_ref[...], axis=-1, keepdims=True))
           - jnp.exp(jnp.sum(lq2_ref[...] * lk2_ref[...], axis=-1, keepdims=True)) + lam_init)

    for u in range(hp):
        q = q_ref[:, u * vd:(u + 1) * vd]
        qs = q * jnp.asarray(hd ** -0.5, q.dtype)
        lane = lax.broadcasted_iota(jnp.int32, q.shape, 1)
        zero = jnp.zeros_like(qs)
        qq_ref[u, 0:tq, :] = jnp.where(lane < hd, qs, zero)
        qq_ref[u, tq:2 * tq, :] = jnp.where(lane >= hd, qs, zero)
        acc_refs[u][...] = jnp.zeros(acc_refs[u].shape, F32)
        m_refs[u][...] = jnp.full(m_refs[u].shape, NEG_BIG, F32)

    def scores(j, u, dst):
        rows = pl.ds(pl.multiple_of(j * tq, tq), tq)
        kb = k_ref[rows, u * vd:(u + 1) * vd]
        dst[...] = lax.dot_general(kb, qq_ref[u], (((1,), (1,)), ((), ())), preferred_element_type=F32)

    def softmax_pv(j, u, src, masked):
        st = src[...]
        if masked:
            r = lax.broadcasted_iota(jnp.int32, st.shape, 0)
            c = lax.broadcasted_iota(jnp.int32, st.shape, 1)
            st = jnp.where(r <= jnp.where(c >= tq, c - tq, c), st, NEG_BIG)
        m_old = m_refs[u][...]
        m_new = jnp.maximum(m_old, jnp.max(st, axis=0, keepdims=True))
        alpha = jnp.exp(m_old - m_new)
        pt = jnp.exp(st - m_new).astype(vxt_ref.dtype)
        pv = jnp.dot(vxt_ref[u, j], pt, preferred_element_type=F32)
        acc_refs[u][...] = alpha * acc_refs[u][...] + pv
        m_refs[u][...] = m_new

    def half(j, cur, nxt):
        for u in range(hp):
            scores(j + 1, u, nxt[u])
            softmax_pv(j, u, cur[u], False)

    for u in range(hp):
        scores(0, u, sa_refs[u])

    def pair(t, carry):
        half(2 * t, sa_refs, sb_refs)
        half(2 * t + 1, sb_refs, sa_refs)
        return carry

    lax.fori_loop(0, i // 2, pair, 0)

    @pl.when(i % 2 == 1)
    def _():
        half(i - 1, sa_refs, sb_refs)
        for u in range(hp):
            softmax_pv(i, u, sb_refs[u], True)

    @pl.when(i % 2 == 0)
    def _():
        for u in range(hp):
            softmax_pv(i, u, sa_refs[u], True)

    for u in range(hp):
        acc = acc_refs[u][...]
        o12 = acc[0:vd] * (1.0 / acc[vd:vd + 1])
        ot = o12[:, 0:tq] - lam * o12[:, tq:2 * tq]
        msq = jnp.mean(ot * ot, axis=0, keepdims=True)
        o = (ot * lax.rsqrt(msq + eps)).T
        o_ref[:, u * vd:(u + 1) * vd] = (o * sw_ref[...] * (1.0 - lam_init)).astype(o_ref.dtype)


def _diff_attn(proj, lq1, lk1, lq2, lk2, subln_w, B, S, d_conv, d_attn, lam_init):
    T = proj.shape[0]
    vd = subln_w.shape[0]
    hd = lq1.shape[0]
    assert vd == LANES and 2 * hd == vd
    H = d_attn // vd
    hp = 8
    ones_rows = 16
    tq = min(256, S)
    nq = S // tq
    assert H % hp == 0
    qc = 2 * d_conv // (hp * vd)
    kc = qc + H // hp
    vc = kc + H // hp
    lspec = pl.BlockSpec((1, hd), lambda b, h, i: (0, 0))
    return pl.pallas_call(
        functools.partial(_attn_kernel, tq=tq, hd=hd, hp=hp, ones_rows=ones_rows, lam_init=lam_init, eps=SUBLN_EPS),
        grid=(B, H // hp, nq),
        in_specs=[lspec, lspec, lspec, lspec,
                  pl.BlockSpec((tq, hp * vd), lambda b, h, i: (b * nq + i, qc + h)),
                  pl.BlockSpec((S, hp * vd), lambda b, h, i: (b, kc + h)),
                  pl.BlockSpec((S, hp * vd), lambda b, h, i: (b, vc + h)),
                  pl.BlockSpec((1, vd), lambda b, h, i: (0, 0))],
        out_specs=pl.BlockSpec((tq, hp * vd), lambda b, h, i: (b * nq + i, h)),
        out_shape=jax.ShapeDtypeStruct((T, d_attn), BF16),
        scratch_shapes=[pltpu.VMEM((hp, S // tq, vd + ones_rows, tq), BF16),
                        pltpu.VMEM((hp, 2 * tq, vd), BF16)]
                       + [pltpu.VMEM((vd + ones_rows, 2 * tq), F32) for _ in range(hp)]
                       + [pltpu.VMEM((1, 2 * tq), F32) for _ in range(hp)]
                       + [pltpu.VMEM((tq, 2 * tq), F32) for _ in range(2 * hp)],
        compiler_params=_cparams(("arbitrary", "arbitrary", "arbitrary"), 52),
        name="diff_attn",
    )(lq1.reshape(1, hd), lk1.reshape(1, hd), lq2.reshape(1, hd), lk2.reshape(1, hd),
      proj, proj, proj, subln_w.reshape(1, vd))


def _pack_bf16_pairs(v):
    n = v.shape[1] // 2
    return _pack_bf16_words(v[:, 0:n], v[:, n:2 * n])


def _pack_bf16_words(lo, hi):
    lo_bits = lax.bitcast_convert_type(lo.astype(BF16).astype(F32), jnp.uint32)
    hi_bits = lax.bitcast_convert_type(hi.astype(BF16).astype(F32), jnp.uint32)
    return (lo_bits >> 16) | (hi_bits & jnp.uint32(0xFFFF0000))


def _unpack_bf16_pairs(w):
    lo = lax.bitcast_convert_type(w << 16, F32)
    hi = lax.bitcast_convert_type(w & jnp.uint32(0xFFFF0000), F32)
    return lo, hi


def _store_row_tiles(ref, row0, words):
    m, w = words.shape
    assert w == SUBLANES * LANES
    for s in range(SUBLANES):
        ref[pl.ds(row0 * SUBLANES + s, m, stride=SUBLANES), :] = words[:, s * LANES:(s + 1) * LANES]


def _load_row_tiles(ref, row0, m):
    return jnp.concatenate([ref[pl.ds(row0 * SUBLANES + s, m, stride=SUBLANES), :] for s in range(SUBLANES)], axis=1)


def _outproj_kernel(x_ref, yc_ref, ya_ref, wo_ref, fw_ref, wr_ref, br_ref, h_ref, lg_ref, ug_ref, *, eps, nsub):
    dc = yc_ref.shape[1]
    sub = x_ref.shape[0] // nsub
    mixes = []
    for t in range(nsub):
        rows = slice(t * sub, (t + 1) * sub)
        mixes.append(jnp.dot(yc_ref[rows, :], wo_ref[0:dc, :], preferred_element_type=F32)
                     + jnp.dot(ya_ref[rows, :], wo_ref[dc:, :], preferred_element_type=F32))
    for t in range(nsub):
        rows = slice(t * sub, (t + 1) * sub)
        h = x_ref[rows, :] + mixes[t]
        h_ref[rows, :] = h
        ms = jnp.mean(h * h, axis=-1, keepdims=True)
        un = h * lax.rsqrt(ms + eps) * fw_ref[...]
        hi = un.astype(BF16)
        lo = (un - hi.astype(F32)).astype(BF16)
        hh = jnp.dot(hi, wr_ref[...], preferred_element_type=F32)
        lh = jnp.dot(lo, wr_ref[:, 0:LANES], preferred_element_type=F32)
        lg_ref[rows, :] = hh[:, 0:LANES] + hh[:, LANES:2 * LANES] + lh + br_ref[...]
        _store_row_tiles(ug_ref, t * sub, _pack_bf16_pairs(un))


def _outproj(x2, y_conv, y_attn, wo_bf, ffn_w, wr_hl, br):
    T, D = x2.shape
    dc = y_conv.shape[1]
    da = y_attn.shape[1]
    tm = min(512, T)
    nsub = 2 if tm % 512 == 0 else 1
    return pl.pallas_call(
        functools.partial(_outproj_kernel, eps=RMS_EPS, nsub=nsub),
        grid=(T // tm,),
        in_specs=[pl.BlockSpec((tm, D), lambda i: (i, 0)),
                  pl.BlockSpec((tm, dc), lambda i: (i, 0)),
                  pl.BlockSpec((tm, da), lambda i: (i, 0)),
                  pl.BlockSpec((dc + da, D), lambda i: (0, 0), pipeline_mode=pl.Buffered(1)),
                  pl.BlockSpec((1, D), lambda i: (0, 0)),
                  pl.BlockSpec((D, 2 * LANES), lambda i: (0, 0)),
                  pl.BlockSpec((1, LANES), lambda i: (0, 0))],
        out_specs=[pl.BlockSpec((tm, D), lambda i: (i, 0)),
                   pl.BlockSpec((tm, LANES), lambda i: (i, 0)),
                   pl.BlockSpec((tm * SUBLANES, LANES), lambda i: (i, 0))],
        out_shape=[jax.ShapeDtypeStruct((T, D), F32), jax.ShapeDtypeStruct((T, LANES), F32),
                   jax.ShapeDtypeStruct((T * SUBLANES, LANES), jnp.uint32)],
        compiler_params=_cparams(("arbitrary",), 56),
        name="outproj",
    )(x2, y_conv, y_attn, wo_bf, ffn_w, wr_hl, br)


def _route_kernel(lg_ref, info_ref, cols_ref, cnt_ref, *, ng, epg):
    i = pl.program_id(0)
    lg = lg_ref[...]
    tm = lg.shape[0]
    lane = lax.broadcasted_iota(jnp.int32, lg.shape, 1)
    lanef = lane.astype(F32)
    ne = ng * epg

    def first_argmax(vals):
        mx = jnp.max(vals, axis=-1, keepdims=True)
        idx = jnp.min(jnp.where(vals == mx, lanef, float(LANES)), axis=-1, keepdims=True)
        return mx, idx

    gmask = lane < ng
    gl = jnp.where(gmask, lg, NEG_BIG)
    gmax, gsel = first_argmax(gl)
    gsum = jnp.sum(jnp.where(gmask, jnp.exp(gl - gmax), 0.0), axis=-1, keepdims=True)
    g_w = 1.0 / gsum
    lo = gsel * epg + ng
    emask = (lanef >= lo) & (lanef < lo + epg)
    el = jnp.where(emask, lg, NEG_BIG)
    v1, i1 = first_argmax(el)
    el2 = jnp.where(lanef == i1, NEG_BIG, el)
    v2, i2 = first_argmax(el2)
    e2 = jnp.exp(v2 - v1)
    p1 = 1.0 / (1.0 + e2)
    gate1 = g_w * p1
    gate2 = g_w * (e2 * p1)

    oh1 = lanef == i1
    oh2 = lanef == i2
    cmat = jnp.where(oh1 | oh2, 1.0, 0.0).astype(BF16)
    r = lax.broadcasted_iota(jnp.int32, (tm, tm), 0)
    c = lax.broadcasted_iota(jnp.int32, (tm, tm), 1)
    tri = jnp.where(c < r, 1.0, 0.0).astype(BF16)

    @pl.when(i == 0)
    def _():
        cnt_ref[...] = jnp.zeros(cnt_ref.shape, F32)

    carry = cnt_ref[0:1, :]
    prefix = jnp.dot(tri, cmat, preferred_element_type=F32) + carry
    rank1 = jnp.sum(jnp.where(oh1, prefix, 0.0), axis=-1, keepdims=True)
    rank2 = jnp.sum(jnp.where(oh2, prefix, 0.0), axis=-1, keepdims=True)
    cnt_ref[...] = jnp.broadcast_to(carry + jnp.sum(cmat.astype(F32), axis=0, keepdims=True), cnt_ref.shape)

    info = jnp.where(lane == 0, i1 - ng,
           jnp.where(lane == 1, i2 - ng,
           jnp.where(lane == 2, gate1,
           jnp.where(lane == 3, gate2,
           jnp.where(lane == 4, rank1,
           jnp.where(lane == 5, rank2, 0.0))))))
    info_ref[...] = info
    cols_ref[...] = info.T[0:SUBLANES, :]


def _route(logits, ng, epg):
    T = logits.shape[0]
    tm = min(512, T)
    return pl.pallas_call(
        functools.partial(_route_kernel, ng=ng, epg=epg),
        grid=(T // tm,),
        in_specs=[pl.BlockSpec((tm, LANES), lambda i: (i, 0))],
        out_specs=[pl.BlockSpec((tm, LANES), lambda i: (i, 0)),
                   pl.BlockSpec((SUBLANES, tm), lambda i: (0, i)),
                   pl.BlockSpec((8, LANES), lambda i: (0, 0))],
        out_shape=[jax.ShapeDtypeStruct((T, LANES), F32), jax.ShapeDtypeStruct((SUBLANES, T), F32),
                   jax.ShapeDtypeStruct((8, LANES), F32)],
        compiler_params=_cparams(("arbitrary",), 32),
        name="route",
    )(logits)


def _dest_kernel(pstart_ref, cols_ref, dest_ref, *, ne):
    eid = cols_ref[0:TOP_K, :].astype(jnp.int32)
    rank = cols_ref[4:4 + TOP_K, :].astype(jnp.int32)

    def body(e, acc):
        return acc + jnp.where(eid == e, pstart_ref[e], 0)

    start = lax.fori_loop(0, ne, body, jnp.zeros(eid.shape, jnp.int32))
    dest_ref[...] = jnp.zeros(dest_ref.shape, jnp.int32)
    dest_ref[0:TOP_K, :] = start + rank


def _dest(pad_starts, cols):
    R, T = cols.shape
    grid_spec = pltpu.PrefetchScalarGridSpec(
        num_scalar_prefetch=1,
        grid=(1,),
        in_specs=[pl.BlockSpec((R, T), lambda i, p: (0, 0))],
        out_specs=pl.BlockSpec((R, T), lambda i, p: (0, 0)),
    )
    return pl.pallas_call(
        functools.partial(_dest_kernel, ne=pad_starts.shape[0]),
        grid_spec=grid_spec,
        out_shape=jax.ShapeDtypeStruct((R, T), jnp.int32),
        compiler_params=_cparams(("arbitrary",), 16),
        name="dest",
    )(pad_starts, cols)


ROW_UNROLL = 8


def _dispatch_kernel(dest_ref, pend_ref, ug_ref, xs_hbm, ring, zbuf, sems, zsem, *, tm, blk, ne, ntok):
    i = pl.program_id(0)
    n = pl.num_programs(0)
    slot = i % 2

    def seg_tail(e):
        end = pend_ref[e]
        start = jnp.where(e == 0, 0, pend_ref[jnp.maximum(e - 1, 0)])
        tail = pl.multiple_of(jnp.maximum(end - blk, 0) * SUBLANES, blk * SUBLANES)
        return pltpu.make_async_copy(zbuf, xs_hbm.at[pl.ds(tail, blk * SUBLANES)], zsem), end > start

    @pl.when(i == 0)
    def _():
        zbuf[...] = jnp.zeros(zbuf.shape, zbuf.dtype)

        def zstart(p, carry):
            for q in range(2):
                cp, nonempty = seg_tail(2 * p + q)

                @pl.when(nonempty)
                def _():
                    cp.start(priority=q)
            return carry

        def zwait(e, carry):
            cp, nonempty = seg_tail(e)

            @pl.when(nonempty)
            def _():
                cp.wait()
            return carry

        def spare(b):
            row = pl.multiple_of(b * blk, blk)
            tile = pl.multiple_of(row * SUBLANES, blk * SUBLANES)
            return pltpu.make_async_copy(zbuf, xs_hbm.at[pl.ds(tile, blk * SUBLANES)], zsem), row >= pend_ref[ne - 1]

        def sstart(p, carry):
            for q in range(2):
                cp, unused = spare(2 * p + q)

                @pl.when(unused)
                def _():
                    cp.start(priority=q)
            return carry

        def swait(b, carry):
            cp, unused = spare(b)

            @pl.when(unused)
            def _():
                cp.wait()
            return carry

        nblk = xs_hbm.shape[0] // (blk * SUBLANES)
        assert ne % 2 == 0 and nblk % 2 == 0
        lax.fori_loop(0, ne // 2, zstart, 0)
        lax.fori_loop(0, nblk // 2, sstart, 0)
        lax.fori_loop(0, ne, zwait, 0)
        lax.fori_loop(0, nblk, swait, 0)

    def drain(s):
        for _ in range(TOP_K):
            pltpu.make_async_copy(ring.at[s], xs_hbm.at[pl.ds(0, tm * SUBLANES)], sems.at[s]).wait()

    @pl.when(i >= 2)
    def _():
        drain(slot)

    ring[slot] = ug_ref[...]

    def issue(c, carry):
        r0 = pl.multiple_of(c * ROW_UNROLL, ROW_UNROLL)
        for s in range(ROW_UNROLL):
            for k in range(TOP_K):
                d = dest_ref[k * ntok + i * tm + r0 + s]
                src = ring.at[slot, pl.ds(pl.multiple_of((r0 + s) * SUBLANES, SUBLANES), SUBLANES)]
                dst = xs_hbm.at[pl.ds(pl.multiple_of(d * SUBLANES, SUBLANES), SUBLANES)]
                pltpu.make_async_copy(src, dst, sems.at[slot]).start(priority=k % 2)
        return carry

    lax.fori_loop(0, tm // ROW_UNROLL, issue, 0)

    @pl.when(i == n - 1)
    def _():
        drain(slot)

        @pl.when(n >= 2)
        def _():
            drain(1 - slot)


def _dispatch(dest_flat, pad_ends, ug, nrows, tm, blk):
    T = ug.shape[0] // SUBLANES
    ne = pad_ends.shape[0]
    grid_spec = pltpu.PrefetchScalarGridSpec(
        num_scalar_prefetch=2,
        grid=(T // tm,),
        in_specs=[pl.BlockSpec((tm * SUBLANES, LANES), lambda i, d, p: (i, 0))],
        out_specs=pl.BlockSpec(memory_space=pl.ANY),
        scratch_shapes=[pltpu.VMEM((2, tm * SUBLANES, LANES), ug.dtype),
                        pltpu.VMEM((blk * SUBLANES, LANES), ug.dtype),
                        pltpu.SemaphoreType.DMA((2,)),
                        pltpu.SemaphoreType.DMA(())],
    )
    return pl.pallas_call(
        functools.partial(_dispatch_kernel, tm=tm, blk=blk, ne=ne, ntok=T),
        grid_spec=grid_spec,
        out_shape=jax.ShapeDtypeStruct((nrows * SUBLANES, LANES), ug.dtype),
        compiler_params=_cparams(("arbitrary",), 32),
        name="dispatch",
    )(dest_flat, pad_ends, ug)


WEIGHT_DMA_PRIORITY = (0, 1, 1)
WEIGHT_SLOTS = 3


def _experts_kernel(bexp_ref, first_ref, slot_ref, next_ref, lead_ref, nused_ref, x_ref, w1_hbm, w3_hbm, w2_hbm,
                    y_ref, w1b, w3b, w2b, wsems):
    i = pl.program_id(0)
    nused = nused_ref[0]

    def weight_copies(e, s):
        return (pltpu.make_async_copy(w1_hbm.at[e], w1b.at[s], wsems.at[s, 0]),
                pltpu.make_async_copy(w3_hbm.at[e], w3b.at[s], wsems.at[s, 1]),
                pltpu.make_async_copy(w2_hbm.at[e], w2b.at[s], wsems.at[s, 2]))

    def start_weights(e, s):
        for n, cp in enumerate(weight_copies(e, s)):
            cp.start(priority=WEIGHT_DMA_PRIORITY[n])

    @pl.when(i == 0)
    def _():
        for n in range(WEIGHT_SLOTS - 1):
            @pl.when(lead_ref[n] >= 0)
            def _():
                start_weights(lead_ref[n], n)

    @pl.when(i < nused)
    def _():
        s = slot_ref[i]

        @pl.when(first_ref[i] == 1)
        def _():
            @pl.when(next_ref[i] >= 0)
            def _():
                start_weights(next_ref[i], (s + WEIGHT_SLOTS - 1) % WEIGHT_SLOTS)
            for cp in weight_copies(bexp_ref[i], s):
                cp.wait()

        blk = x_ref.shape[0] // SUBLANES
        x_lo, x_hi = _unpack_bf16_pairs(_load_row_tiles(x_ref, 0, blk))
        half = x_lo.shape[1]
        a = (jnp.dot(x_lo, w1b[s, 0:half, :], preferred_element_type=F32)
             + jnp.dot(x_hi, w1b[s, half:2 * half, :], preferred_element_type=F32))
        b = (jnp.dot(x_lo, w3b[s, 0:half, :], preferred_element_type=F32)
             + jnp.dot(x_hi, w3b[s, half:2 * half, :], preferred_element_type=F32))
        hdn = a * jax.nn.sigmoid(a) * b
        cw = 2 * LANES
        for c0 in range(0, half, cw):
            y_lo = jnp.dot(hdn, w2b[s, :, c0:c0 + cw], preferred_element_type=F32)
            y_hi = jnp.dot(hdn, w2b[s, :, half + c0:half + c0 + cw], preferred_element_type=F32)
            words = _pack_bf16_words(y_lo, y_hi)
            for g in range(cw // LANES):
                y_ref[pl.ds(c0 // LANES + g, blk, stride=SUBLANES), :] = words[:, g * LANES:(g + 1) * LANES]

    @pl.when(i >= nused)
    def _():
        y_ref[...] = jnp.zeros(y_ref.shape, y_ref.dtype)


def _experts(bexp, first, slot, nxt, lead, nused, xs, w1, w3, w2, blk):
    P = xs.shape[0] // SUBLANES
    E, D, Fh = w1.shape
    nblk = P // blk
    smap = lambda i, *_: (i, 0)
    xmap = lambda i, be, fi, sl, nx, ld, nu: (jnp.minimum(i, nu[0] - 1), 0)
    grid_spec = pltpu.PrefetchScalarGridSpec(
        num_scalar_prefetch=6,
        grid=(nblk,),
        in_specs=[pl.BlockSpec((blk * SUBLANES, LANES), xmap),
                  pl.BlockSpec(memory_space=pl.ANY),
                  pl.BlockSpec(memory_space=pl.ANY),
                  pl.BlockSpec(memory_space=pl.ANY)],
        out_specs=pl.BlockSpec((blk * SUBLANES, LANES), smap),
        scratch_shapes=[pltpu.VMEM((WEIGHT_SLOTS, D, Fh), w1.dtype),
                        pltpu.VMEM((WEIGHT_SLOTS, D, Fh), w3.dtype),
                        pltpu.VMEM((WEIGHT_SLOTS, Fh, D), w2.dtype),
                        pltpu.SemaphoreType.DMA((WEIGHT_SLOTS, 3))],
    )
    return pl.pallas_call(
        _experts_kernel,
        grid_spec=grid_spec,
        out_shape=jax.ShapeDtypeStruct(xs.shape, xs.dtype),
        compiler_params=_cparams(("arbitrary",), 56),
        name="experts",
    )(bexp, first, slot, nxt, lead, nused, xs, w1, w3, w2)


def _row_gather(src_hbm, idx_ref, base, dst, sem, n):
    def body(c, carry):
        r0 = pl.multiple_of(c * ROW_UNROLL, ROW_UNROLL)
        for s in range(ROW_UNROLL):
            d = idx_ref[base + r0 + s]
            pltpu.make_async_copy(src_hbm.at[pl.ds(pl.multiple_of(d * SUBLANES, SUBLANES), SUBLANES)],
                                  dst.at[pl.ds(pl.multiple_of((r0 + s) * SUBLANES, SUBLANES), SUBLANES)],
                                  sem).start(priority=s % 2)
        return carry
    lax.fori_loop(0, n // ROW_UNROLL, body, 0)


def _row_gather_wait(src_hbm, dst, sem, n):
    pltpu.make_async_copy(src_hbm.at[pl.ds(0, n * SUBLANES)], dst, sem).wait()


def _combine_kernel(dest_ref, h_ref, info_ref, y_hbm, fw_ref, o_ref, ybuf, sems, *, tm, eps, ntok):
    i = pl.program_id(0)
    n = pl.num_programs(0)
    slot = i % 2
    nrow = TOP_K * tm

    def gather_step(step, s):
        for k in range(TOP_K):
            _row_gather(y_hbm, dest_ref, k * ntok + step * tm,
                        ybuf.at[s, pl.ds(k * tm * SUBLANES, tm * SUBLANES)], sems.at[s], tm)

    @pl.when(i == 0)
    def _():
        gather_step(0, 0)

    @pl.when(i + 1 < n)
    def _():
        gather_step(i + 1, 1 - slot)

    _row_gather_wait(y_hbm, ybuf.at[slot], sems.at[slot], nrow)
    half = o_ref.shape[1] // 2
    rc = tm

    def rows(ci, carry):
        r0 = pl.multiple_of(ci * rc, rc)
        rs = pl.ds(r0, rc)
        info = info_ref[rs, :]
        g1 = info[:, 2:3]
        g2 = info[:, 3:4]
        y1_lo, y1_hi = _unpack_bf16_pairs(_load_row_tiles(ybuf.at[slot], r0, rc))
        y2_lo, y2_hi = _unpack_bf16_pairs(_load_row_tiles(ybuf.at[slot], tm + r0, rc))
        h_lo = h_ref[rs, 0:half] + (g1 * y1_lo + g2 * y2_lo)
        h_hi = h_ref[rs, half:2 * half] + (g1 * y1_hi + g2 * y2_hi)
        ms = (jnp.sum(h_lo * h_lo, axis=-1, keepdims=True)
              + jnp.sum(h_hi * h_hi, axis=-1, keepdims=True)) / (2 * half)
        r = lax.rsqrt(ms + eps)
        o_ref[rs, 0:half] = h_lo * r * fw_ref[:, 0:half]
        o_ref[rs, half:2 * half] = h_hi * r * fw_ref[:, half:2 * half]
        return carry

    lax.fori_loop(0, tm // rc, rows, 0)


def _combine(dest, h, info, yb, final_w, tm):
    T, D = h.shape
    grid_spec = pltpu.PrefetchScalarGridSpec(
        num_scalar_prefetch=1,
        grid=(T // tm,),
        in_specs=[pl.BlockSpec((tm, D), lambda i, d: (i, 0)),
                  pl.BlockSpec((tm, LANES), lambda i, d: (i, 0)),
                  pl.BlockSpec(memory_space=pl.ANY),
                  pl.BlockSpec((1, D), lambda i, d: (0, 0))],
        out_specs=pl.BlockSpec((tm, D), lambda i, d: (i, 0)),
        scratch_shapes=[pltpu.VMEM((2, TOP_K * tm * SUBLANES, LANES), yb.dtype),
                        pltpu.SemaphoreType.DMA((2,))],
    )
    return pl.pallas_call(
        functools.partial(_combine_kernel, tm=tm, eps=RMS_EPS, ntok=T),
        grid_spec=grid_spec,
        out_shape=jax.ShapeDtypeStruct((T, D), F32),
        compiler_params=_cparams(("arbitrary",), 48),
        name="combine",
    )(dest, h, info, yb, final_w)


def _layer(h_in, l, B, S, mix_norm_w, w_in, conv_dw_w, conv_dw_b, conv_ln_w, conv_ln_b,
           lam_q1, lam_k1, lam_q2, lam_k2, attn_subln_w, w_out, ffn_norm_w,
           w_group, b_group, w_expert_gate, b_expert_gate, w1, w3, w2):
    T, D = h_in.shape
    d_conv = conv_dw_w.shape[1]
    d_attn = (w_in.shape[1] - 2 * d_conv) // 3
    ng = w_group.shape[1]
    ne = w_expert_gate.shape[1]
    epg = ne // ng
    assert ng + ne <= LANES
    lam_init = 0.8 - 0.6 * math.exp(-0.3 * l)

    w_bf = w_in.astype(BF16)
    proj_ag, u = _norm_inproj(h_in, mix_norm_w.reshape(1, D), w_bf, 2 * d_conv)
    y_conv, qkv = _conformer_qkv(proj_ag, u, w_bf, 2 * d_conv, conv_dw_w, conv_dw_b, conv_ln_w, conv_ln_b, B, S)
    y_attn = _diff_attn(qkv, lam_q1, lam_k1, lam_q2, lam_k2, attn_subln_w, B, S, 0, d_attn, lam_init)

    wr = jnp.concatenate([w_group, w_expert_gate, jnp.zeros((D, LANES - ng - ne), F32)], axis=1)
    wr_hi = wr.astype(BF16)
    wr_lo = (wr - wr_hi.astype(F32)).astype(BF16)
    br = jnp.concatenate([b_group, b_expert_gate.reshape(-1), jnp.zeros((LANES - ng - ne,), F32)]).reshape(1, LANES)
    wr_hl = jnp.concatenate([wr_hi, wr_lo], axis=1)
    h, logits, ug = _outproj(h_in, y_conv, y_attn, w_out.astype(BF16), ffn_norm_w.reshape(1, D), wr_hl, br)

    info, cols, cnt = _route(logits, ng, epg)

    blk = 256
    tmd = min(256, T)
    A = T * TOP_K
    nblk = (A + ne * (blk - 1) + blk - 1) // blk
    i32 = jnp.int32
    counts = cnt[0, ng:ng + ne].astype(i32)
    padded = (counts + blk - 1) // blk * blk
    pad_ends = jnp.cumsum(padded).astype(i32)
    pad_starts = pad_ends - padded
    dest = _dest(pad_starts, cols)[0:TOP_K].reshape(-1)
    nused = (pad_ends[-1] // blk).astype(i32)
    bpos = jnp.arange(nblk, dtype=i32)
    brow = jnp.minimum(bpos, nused - 1) * blk
    bexp = jnp.minimum(jnp.sum((pad_ends[None, :] <= brow[:, None]).astype(i32), axis=1), ne - 1)
    first = ((bpos < nused) & ((bpos == 0) | (bexp != jnp.roll(bexp, 1)))).astype(i32)
    slot = ((jnp.cumsum(first) - 1) % WEIGHT_SLOTS).astype(i32)
    used_idx = jnp.where(padded > 0, jnp.arange(ne, dtype=i32), ne)
    suffix_min = lax.cummin(used_idx, reverse=True)
    next_used = jnp.concatenate([suffix_min[1:], jnp.full((2,), ne, i32)])
    ahead = jnp.arange(ne, dtype=i32)
    for _ in range(WEIGHT_SLOTS - 1):
        ahead = next_used[ahead]
    ahead = jnp.where(ahead >= ne, -1, ahead)
    nxt = ahead[bexp].astype(i32)
    lead = [suffix_min[0]]
    for _ in range(WEIGHT_SLOTS - 2):
        lead.append(next_used[lead[-1]])
    lead = jnp.stack([jnp.where(e >= ne, -1, e) for e in lead]).astype(i32)

    xs = _dispatch(dest, pad_ends, ug, nblk * blk, tmd, blk)
    yb = _experts(bexp, first, slot, nxt, lead, nused.reshape(1), xs, w1, w3, w2, blk)
    return h, info, dest, yb


def kernel(x, mix_norm_w, w_in, conv_dw_w, conv_dw_b, conv_ln_w, conv_ln_b, lam_q1, lam_k1, lam_q2, lam_k2,
           attn_subln_w, w_out, ffn_norm_w, w_group, b_group, w_expert_gate, b_expert_gate, w1, w3, w2,
           final_norm_w):
    B, S, D = x.shape
    depth = w_in.shape[0]
    assert depth == 1
    T = B * S
    tmc = min(256, T)
    h = x.reshape(T, D)
    for l in range(depth):
        h, info, dest, yb = _layer(
            h, l, B, S, mix_norm_w[l], w_in[l], conv_dw_w[l], conv_dw_b[l], conv_ln_w[l], conv_ln_b[l],
            lam_q1[l], lam_k1[l], lam_q2[l], lam_k2[l], attn_subln_w[l], w_out[l], ffn_norm_w[l],
            w_group[l], b_group[l], w_expert_gate[l], b_expert_gate[l], w1[l], w3[l], w2[l])
        h = _combine(dest, h, info, yb, final_norm_w.reshape(1, D), tmc)
    return h.reshape(B, S, D)
```

```python
import functools
import math

import jax
import jax.numpy as jnp
from jax import lax
from jax.experimental import pallas as pl
from jax.experimental.pallas import tpu as pltpu

F32 = jnp.float32
BF16 = jnp.bfloat16

RMS_EPS = 1e-6
SUBLN_EPS = 1e-5
LN_EPS = 1e-5
TOP_K = 2
LANES = 128
SUBLANES = 8
NEG_BIG = -1e30
MIB = 1024 * 1024


def _cparams(sem, vmem_mib):
    return pltpu.CompilerParams(dimension_semantics=sem, vmem_limit_bytes=vmem_mib * MIB)


def _norm_inproj_kernel(x_ref, nw_ref, w_ref, o_ref, u_ref, *, eps, rc):
    @pl.when(pl.program_id(1) == 0)
    def _():
        for r0 in range(0, x_ref.shape[0], rc):
            rows = slice(r0, r0 + rc)
            x = x_ref[rows, :]
            ms = jnp.mean(x * x, axis=-1, keepdims=True)
            u = (x * lax.rsqrt(ms + eps) * nw_ref[...]).astype(u_ref.dtype)
            u_ref[rows, :] = u
            o_ref[rows, :] = jnp.dot(u, w_ref[...], preferred_element_type=F32).astype(o_ref.dtype)

    @pl.when(pl.program_id(1) > 0)
    def _():
        o_ref[...] = jnp.dot(u_ref[...], w_ref[...], preferred_element_type=F32).astype(o_ref.dtype)


def _norm_inproj(x2, nw, w_bf, ncols):
    T, D = x2.shape
    N = ncols
    tm = min(1024, T)
    tn = 1024
    return pl.pallas_call(
        functools.partial(_norm_inproj_kernel, eps=RMS_EPS, rc=256),
        grid=(T // tm, N // tn),
        in_specs=[pl.BlockSpec((tm, D), lambda i, j: (i, 0)),
                  pl.BlockSpec((1, D), lambda i, j: (0, 0)),
                  pl.BlockSpec((D, tn), lambda i, j: (0, j))],
        out_specs=[pl.BlockSpec((tm, tn), lambda i, j: (i, j)),
                   pl.BlockSpec((tm, D), lambda i, j: (i, 0))],
        out_shape=[jax.ShapeDtypeStruct((T, N), BF16), jax.ShapeDtypeStruct((T, D), BF16)],
        compiler_params=_cparams(("arbitrary", "arbitrary"), 48),
        name="norm_inproj",
    )(x2, nw, w_bf)


QKV_CHUNK = 2 * LANES
QKV_KSTEP = 4 * LANES


def _conv_qkv_kernel(a_ref, g_ref, w_ref, b_ref, lw_ref, lb_ref, u_ref, *rest, ts, kw, halo, eps, nwq):
    wq_refs = rest[0:nwq]
    o_ref, qkv_ref, ubuf, cbuf, sh, lhs = rest[nwq:]
    s = pl.program_id(1)
    C = a_ref.shape[1]
    wcols = wq_refs[0].shape[1]
    nmm = nwq * wcols // QKV_CHUNK

    @pl.when(s == 0)
    def _():
        ubuf[0:halo, :] = jnp.zeros((halo, C), F32)

    @pl.when(s > 0)
    def _():
        ubuf[0:halo, :] = ubuf[ts:ts + halo, :]

    lhs[...] = u_ref[...]

    nks = lhs.shape[1] // QKV_KSTEP
    pending = {"n": 0, "k": 0, "acc": None}

    def piece(after=None):
        n, k = pending["n"], pending["k"]
        if n >= nmm:
            return
        ks = slice(k * QKV_KSTEP, (k + 1) * QKV_KSTEP)
        if after is not None:
            allrows = jnp.sum(after, axis=0)
            bits = lax.bitcast_convert_type(jnp.concatenate([allrows, allrows], axis=0), jnp.uint32)
            zero = lax.bitcast_convert_type((bits >> 16) >> 16, F32).astype(lhs.dtype)
            first = (slice(0, 2 * SUBLANES), slice(k * QKV_KSTEP, k * QKV_KSTEP + LANES))
            lhs[first] = lhs[first] + zero
        wq_ref = wq_refs[n * QKV_CHUNK // wcols]
        ws = slice(n * QKV_CHUNK % wcols, n * QKV_CHUNK % wcols + QKV_CHUNK)
        part = jnp.dot(lhs[:, ks], wq_ref[ks, ws], preferred_element_type=F32)
        pending["acc"] = part if k == 0 else pending["acc"] + part
        if k == nks - 1:
            qkv_ref[:, n * QKV_CHUNK:(n + 1) * QKV_CHUNK] = pending["acc"].astype(qkv_ref.dtype)
            pending["n"], pending["k"] = n + 1, 0
        else:
            pending["k"] = k + 1

    rg = 64
    for r0 in range(0, ts, rg):
        a = a_ref[r0:r0 + rg, :].astype(F32)
        g = g_ref[r0:r0 + rg, :].astype(F32)
        ubuf[halo + r0:halo + r0 + rg, :] = a * jax.nn.sigmoid(g)

    nsh = sh.shape[1]
    for r in range(1, SUBLANES):
        for i0 in range(0, nsh, rg):
            n = min(rg, nsh - i0)
            sh[r - 1, i0:i0 + n, :] = ubuf[i0 + r:i0 + r + n, :]

    off = halo - (kw - 1)
    rc = 128
    ntap = (ts // rc) * (C // LANES) * kw
    tap_every = max(1, ntap // (nmm * nks))
    tapno = 0
    for r0 in range(0, ts, rc):
        for c in range(C // LANES):
            cs = slice(c * LANES, (c + 1) * LANES)
            acc = jnp.broadcast_to(b_ref[0:1, cs], (rc // SUBLANES, SUBLANES, LANES))
            for k in range(kw):
                q, r = divmod(off + k, SUBLANES)
                a0 = r0 + q * SUBLANES
                tap = ubuf[a0:a0 + rc, cs] if r == 0 else sh[r - 1, a0:a0 + rc, cs]
                acc = acc + w_ref[k, :, cs] * tap.reshape(rc // SUBLANES, SUBLANES, LANES)
                tapno += 1
                if tapno % tap_every == 0:
                    piece(after=acc)
            cbuf[r0:r0 + rc, cs] = acc.reshape(rc, LANES)
        cv = cbuf[r0:r0 + rc, :]
        mu = jnp.mean(cv, axis=-1, keepdims=True)
        d = cv - mu
        var = jnp.mean(d * d, axis=-1, keepdims=True)
        un = d * lax.rsqrt(var + eps) * lw_ref[...] + lb_ref[...]
        o_ref[r0:r0 + rc, :] = (un * jax.nn.sigmoid(un)).astype(o_ref.dtype)
    while pending["n"] < nmm:
        piece()


def _conformer_qkv(proj_ag, u, w_bf, col0, dw_w, dw_b, ln_w, ln_b, B, S):
    T = proj_ag.shape[0]
    kw, C = dw_w.shape
    D = w_bf.shape[0]
    N = w_bf.shape[1] - col0
    wcols = 1024
    nwq = N // wcols
    ts = min(256, S)
    halo = 32
    assert kw - 1 <= halo and S % ts == 0 and ts >= halo and N % wcols == 0 and col0 % wcols == 0
    ns = S // ts
    wp = jnp.broadcast_to(dw_w[:, None, :], (kw, SUBLANES, C))
    return pl.pallas_call(
        functools.partial(_conv_qkv_kernel, ts=ts, kw=kw, halo=halo, eps=LN_EPS, nwq=nwq),
        grid=(B, ns),
        in_specs=[pl.BlockSpec((ts, C), lambda b, s: (b * ns + s, 0)),
                  pl.BlockSpec((ts, C), lambda b, s: (b * ns + s, 1)),
                  pl.BlockSpec((kw, SUBLANES, C), lambda b, s: (0, 0, 0)),
                  pl.BlockSpec((1, C), lambda b, s: (0, 0)),
                  pl.BlockSpec((1, C), lambda b, s: (0, 0)),
                  pl.BlockSpec((1, C), lambda b, s: (0, 0)),
                  pl.BlockSpec((ts, D), lambda b, s: (b * ns + s, 0))]
                 + [pl.BlockSpec((D, wcols), functools.partial(lambda b, s, n: (0, col0 // wcols + n), n=n),
                                 pipeline_mode=pl.Buffered(1)) for n in range(nwq)],
        out_specs=[pl.BlockSpec((ts, C), lambda b, s: (b * ns + s, 0)),
                   pl.BlockSpec((ts, N), lambda b, s: (b * ns + s, 0))],
        out_shape=[jax.ShapeDtypeStruct((T, C), BF16), jax.ShapeDtypeStruct((T, N), BF16)],
        scratch_shapes=[pltpu.VMEM((ts + halo, C), F32), pltpu.VMEM((ts, C), F32),
                        pltpu.VMEM((SUBLANES - 1, ts + halo - SUBLANES, C), F32),
                        pltpu.VMEM((ts, D), BF16)],
        compiler_params=_cparams(("arbitrary", "arbitrary"), 48),
        name="conformer_qkv",
    )(proj_ag, proj_ag, wp, dw_b.reshape(1, C), ln_w.reshape(1, C), ln_b.reshape(1, C), u, *([w_bf] * nwq))


def _attn_kernel(lq1_ref, lk1_ref, lq2_ref, lk2_ref, q_ref, k_ref, v_ref, sw_ref, o_ref,
                 vxt_ref, qq_ref, *scr, tq, hd, hp, ones_rows, lam_init, eps):
    i = pl.program_id(2)
    vd = 2 * hd
    nkb = v_ref.shape[0] // tq
    acc_refs, m_refs, sa_refs, sb_refs = (scr[n * hp:(n + 1) * hp] for n in range(4))

    @pl.when(i == 0)
    def _():
        for u in range(hp):
            for jb in range(nkb):
                vblk = v_ref[jb * tq:(jb + 1) * tq, u * vd:(u + 1) * vd].astype(F32)
                vxt_ref[u, jb, 0:vd, :] = vblk.T.astype(vxt_ref.dtype)
                vxt_ref[u, jb, vd:vd + ones_rows, :] = jnp.ones((ones_rows, tq), vxt_ref.dtype)

    lam = (jnp.exp(jnp.sum(lq1_ref[...] * lk1_ref[...], axis=-1, keepdims=True))
           - jnp.exp(jnp.sum(lq2_ref[...] * lk2_ref[...], axis=-1, keepdims=True)) + lam_init)

    for u in range(hp):
        q = q_ref[:, u * vd:(u + 1) * vd]
        qs = q * jnp.asarray(hd ** -0.5, q.dtype)
        lane = lax.broadcasted_iota(jnp.int32, q.shape, 1)
        zero = jnp.zeros_like(qs)
        qq_ref[u, 0:tq, :] = jnp.where(lane < hd, qs, zero)
        qq_ref[u, tq:2 * tq, :] = jnp.where(lane >= hd, qs, zero)
        acc_refs[u][...] = jnp.zeros(acc_refs[u].shape, F32)
        m_refs[u][...] = jnp.full(m_refs[u].shape, NEG_BIG, F32)

    def scores(j, u, dst):
        rows = pl.ds(pl.multiple_of(j * tq, tq), tq)
        kb = k_ref[rows, u * vd:(u + 1) * vd]
        dst[...] = lax.dot_general(kb, qq_ref[u], (((1,), (1,)), ((), ())), preferred_element_type=F32)

    def softmax_pv(j, u, src, masked):
        st = src[...]
        if masked:
            r = lax.broadcasted_iota(jnp.int32, st.shape, 0)
            c = lax.broadcasted_iota(jnp.int32, st.shape, 1)
            st = jnp.where(r <= jnp.where(c >= tq, c - tq, c), st, NEG_BIG)
        m_old = m_refs[u][...]
        m_new = jnp.maximum(m_old, jnp.max(st, axis=0, keepdims=True))
        alpha = jnp.exp(m_old - m_new)
        pt = jnp.exp(st - m_new).astype(vxt_ref.dtype)
        pv = jnp.dot(vxt_ref[u, j], pt, preferred_element_type=F32)
        acc_refs[u][...] = alpha * acc_refs[u][...] + pv
        m_refs[u][...] = m_new

    def half(j, cur, nxt):
        for u in range(hp):
            scores(j + 1, u, nxt[u])
            softmax_pv(j, u, cur[u], False)

    for u in range(hp):
        scores(0, u, sa_refs[u])

    def pair(t, carry):
        half(2 * t, sa_refs, sb_refs)
        half(2 * t + 1, sb_refs, sa_refs)
        return carry

    lax.fori_loop(0, i // 2, pair, 0)

    @pl.when(i % 2 == 1)
    def _():
        half(i - 1, sa_refs, sb_refs)
        for u in range(hp):
            softmax_pv(i, u, sb_refs[u], True)

    @pl.when(i % 2 == 0)
    def _():
        for u in range(hp):
            softmax_pv(i, u, sa_refs[u], True)

    for u in range(hp):
        acc = acc_refs[u][...]
        o12 = acc[0:vd] * (1.0 / acc[vd:vd + 1])
        ot = o12[:, 0:tq] - lam * o12[:, tq:2 * tq]
        msq = jnp.mean(ot * ot, axis=0, keepdims=True)
        o = (ot * lax.rsqrt(msq + eps)).T
        o_ref[:, u * vd:(u + 1) * vd] = (o * sw_ref[...] * (1.0 - lam_init)).astype(o_ref.dtype)


def _diff_attn(proj, lq1, lk1, lq2, lk2, subln_w, B, S, d_conv, d_attn, lam_init):
    T = proj.shape[0]
    vd = subln_w.shape[0]
    hd = lq1.shape[0]
    assert vd == LANES and 2 * hd == vd
    H = d_attn // vd
    hp = 8
    ones_rows = 16
    tq = min(256, S)
    nq = S // tq
    assert H % hp == 0
    qc = 2 * d_conv // (hp * vd)
    kc = qc + H // hp
    vc = kc + H // hp
    lspec = pl.BlockSpec((1, hd), lambda b, h, i: (0, 0))
    return pl.pallas_call(
        functools.partial(_attn_kernel, tq=tq, hd=hd, hp=hp, ones_rows=ones_rows, lam_init=lam_init, eps=SUBLN_EPS),
        grid=(B, H // hp, nq),
        in_specs=[lspec, lspec, lspec, lspec,
                  pl.BlockSpec((tq, hp * vd), lambda b, h, i: (b * nq + i, qc + h)),
                  pl.BlockSpec((S, hp * vd), lambda b, h, i: (b, kc + h)),
                  pl.BlockSpec((S, hp * vd), lambda b, h, i: (b, vc + h)),
                  pl.BlockSpec((1, vd), lambda b, h, i: (0, 0))],
        out_specs=pl.BlockSpec((tq, hp * vd), lambda b, h, i: (b * nq + i, h)),
        out_shape=jax.ShapeDtypeStruct((T, d_attn), BF16),
        scratch_shapes=[pltpu.VMEM((hp, S // tq, vd + ones_rows, tq), BF16),
                        pltpu.VMEM((hp, 2 * tq, vd), BF16)]
                       + [pltpu.VMEM((vd + ones_rows, 2 * tq), F32) for _ in range(hp)]
                       + [pltpu.VMEM((1, 2 * tq), F32) for _ in range(hp)]
                       + [pltpu.VMEM((tq, 2 * tq), F32) for _ in range(2 * hp)],
        compiler_params=_cparams(("arbitrary", "arbitrary", "arbitrary"), 52),
        name="diff_attn",
    )(lq1.reshape(1, hd), lk1.reshape(1, hd), lq2.reshape(1, hd), lk2.reshape(1, hd),
      proj, proj, proj, subln_w.reshape(1, vd))


def _pack_bf16_pairs(v):
    n = v.shape[1] // 2
    return _pack_bf16_words(v[:, 0:n], v[:, n:2 * n])


def _pack_bf16_words(lo, hi):
    lo_bits = lax.bitcast_convert_type(lo.astype(BF16).astype(F32), jnp.uint32)
    hi_bits = lax.bitcast_convert_type(hi.astype(BF16).astype(F32), jnp.uint32)
    return (lo_bits >> 16) | (hi_bits & jnp.uint32(0xFFFF0000))


def _unpack_bf16_pairs(w):
    lo = lax.bitcast_convert_type(w << 16, F32)
    hi = lax.bitcast_convert_type(w & jnp.uint32(0xFFFF0000), F32)
    return lo, hi


def _store_row_tiles(ref, row0, words):
    m, w = words.shape
    assert w == SUBLANES * LANES
    for s in range(SUBLANES):
        ref[pl.ds(row0 * SUBLANES + s, m, stride=SUBLANES), :] = words[:, s * LANES:(s + 1) * LANES]


def _load_row_tiles(ref, row0, m):
    return jnp.concatenate([ref[pl.ds(row0 * SUBLANES + s, m, stride=SUBLANES), :] for s in range(SUBLANES)], axis=1)


def _outproj_kernel(x_ref, yc_ref, ya_ref, wo_ref, fw_ref, wr_ref, br_ref, h_ref, lg_ref, ug_ref, *, eps, nsub):
    dc = yc_ref.shape[1]
    sub = x_ref.shape[0] // nsub
    D = x_ref.shape[1]
    hw = D // 2
    ssq = []
    for t in range(nsub):
        rows = slice(t * sub, (t + 1) * sub)
        acc = None
        for c0 in (0, hw):
            cols = slice(c0, c0 + hw)
            hc = x_ref[rows, cols] + (jnp.dot(yc_ref[rows, :], wo_ref[0:dc, cols], preferred_element_type=F32)
                                      + jnp.dot(ya_ref[rows, :], wo_ref[dc:, cols], preferred_element_type=F32))
            h_ref[rows, cols] = hc
            part = jnp.sum(hc * hc, axis=-1, keepdims=True)
            acc = part if acc is None else acc + part
        ssq.append(acc)
    for t in range(nsub):
        rows = slice(t * sub, (t + 1) * sub)
        h = h_ref[rows, :]
        ms = ssq[t] / D
        un = h * lax.rsqrt(ms + eps) * fw_ref[...]
        hi = un.astype(BF16)
        lo = (un - hi.astype(F32)).astype(BF16)
        hh = jnp.dot(hi, wr_ref[...], preferred_element_type=F32)
        lh = jnp.dot(lo, wr_ref[:, 0:LANES], preferred_element_type=F32)
        lg_ref[rows, :] = hh[:, 0:LANES] + hh[:, LANES:2 * LANES] + lh + br_ref[...]
        _store_row_tiles(ug_ref, t * sub, _pack_bf16_pairs(un))


def _outproj(x2, y_conv, y_attn, wo_bf, ffn_w, wr_hl, br):
    T, D = x2.shape
    dc = y_conv.shape[1]
    da = y_attn.shape[1]
    tm = min(512, T)
    nsub = 2 if tm % 512 == 0 else 1
    return pl.pallas_call(
        functools.partial(_outproj_kernel, eps=RMS_EPS, nsub=nsub),
        grid=(T // tm,),
        in_specs=[pl.BlockSpec((tm, D), lambda i: (i, 0)),
                  pl.BlockSpec((tm, dc), lambda i: (i, 0)),
                  pl.BlockSpec((tm, da), lambda i: (i, 0)),
                  pl.BlockSpec((dc + da, D), lambda i: (0, 0), pipeline_mode=pl.Buffered(1)),
                  pl.BlockSpec((1, D), lambda i: (0, 0)),
                  pl.BlockSpec((D, 2 * LANES), lambda i: (0, 0)),
                  pl.BlockSpec((1, LANES), lambda i: (0, 0))],
        out_specs=[pl.BlockSpec((tm, D), lambda i: (i, 0)),
                   pl.BlockSpec((tm, LANES), lambda i: (i, 0)),
                   pl.BlockSpec((tm * SUBLANES, LANES), lambda i: (i, 0))],
        out_shape=[jax.ShapeDtypeStruct((T, D), F32), jax.ShapeDtypeStruct((T, LANES), F32),
                   jax.ShapeDtypeStruct((T * SUBLANES, LANES), jnp.uint32)],
        compiler_params=_cparams(("arbitrary",), 56),
        name="outproj",
    )(x2, y_conv, y_attn, wo_bf, ffn_w, wr_hl, br)


def _route_kernel(lg_ref, info_ref, cols_ref, cnt_ref, *, ng, epg):
    i = pl.program_id(0)
    lg = lg_ref[...]
    tm = lg.shape[0]
    lane = lax.broadcasted_iota(jnp.int32, lg.shape, 1)
    lanef = lane.astype(F32)
    ne = ng * epg

    def first_argmax(vals):
        mx = jnp.max(vals, axis=-1, keepdims=True)
        idx = jnp.min(jnp.where(vals == mx, lanef, float(LANES)), axis=-1, keepdims=True)
        return mx, idx

    gmask = lane < ng
    gl = jnp.where(gmask, lg, NEG_BIG)
    gmax, gsel = first_argmax(gl)
    gsum = jnp.sum(jnp.where(gmask, jnp.exp(gl - gmax), 0.0), axis=-1, keepdims=True)
    g_w = 1.0 / gsum
    lo = gsel * epg + ng
    emask = (lanef >= lo) & (lanef < lo + epg)
    el = jnp.where(emask, lg, NEG_BIG)
    v1, i1 = first_argmax(el)
    el2 = jnp.where(lanef == i1, NEG_BIG, el)
    v2, i2 = first_argmax(el2)
    e2 = jnp.exp(v2 - v1)
    p1 = 1.0 / (1.0 + e2)
    gate1 = g_w * p1
    gate2 = g_w * (e2 * p1)

    oh1 = lanef == i1
    oh2 = lanef == i2
    cmat = jnp.where(oh1 | oh2, 1.0, 0.0).astype(BF16)
    r = lax.broadcasted_iota(jnp.int32, (tm, tm), 0)
    c = lax.broadcasted_iota(jnp.int32, (tm, tm), 1)
    tri = jnp.where(c < r, 1.0, 0.0).astype(BF16)

    @pl.when(i == 0)
    def _():
        cnt_ref[...] = jnp.zeros(cnt_ref.shape, F32)

    carry = cnt_ref[0:1, :]
    prefix = jnp.dot(tri, cmat, preferred_element_type=F32) + carry
    rank1 = jnp.sum(jnp.where(oh1, prefix, 0.0), axis=-1, keepdims=True)
    rank2 = jnp.sum(jnp.where(oh2, prefix, 0.0), axis=-1, keepdims=True)
    cnt_ref[...] = jnp.broadcast_to(carry + jnp.sum(cmat.astype(F32), axis=0, keepdims=True), cnt_ref.shape)

    info = jnp.where(lane == 0, i1 - ng,
           jnp.where(lane == 1, i2 - ng,
           jnp.where(lane == 2, gate1,
           jnp.where(lane == 3, gate2,
           jnp.where(lane == 4, rank1,
           jnp.where(lane == 5, rank2, 0.0))))))
    info_ref[...] = info
    cols_ref[...] = info.T[0:SUBLANES, :]


def _route(logits, ng, epg):
    T = logits.shape[0]
    tm = min(512, T)
    return pl.pallas_call(
        functools.partial(_route_kernel, ng=ng, epg=epg),
        grid=(T // tm,),
        in_specs=[pl.BlockSpec((tm, LANES), lambda i: (i, 0))],
        out_specs=[pl.BlockSpec((tm, LANES), lambda i: (i, 0)),
                   pl.BlockSpec((SUBLANES, tm), lambda i: (0, i)),
                   pl.BlockSpec((8, LANES), lambda i: (0, 0))],
        out_shape=[jax.ShapeDtypeStruct((T, LANES), F32), jax.ShapeDtypeStruct((SUBLANES, T), F32),
                   jax.ShapeDtypeStruct((8, LANES), F32)],
        compiler_params=_cparams(("arbitrary",), 32),
        name="route",
    )(logits)


def _dest_kernel(pstart_ref, cols_ref, dest_ref, *, ne):
    eid = cols_ref[0:TOP_K, :].astype(jnp.int32)
    rank = cols_ref[4:4 + TOP_K, :].astype(jnp.int32)

    def body(e, acc):
        return acc + jnp.where(eid == e, pstart_ref[e], 0)

    start = lax.fori_loop(0, ne, body, jnp.zeros(eid.shape, jnp.int32))
    dest_ref[...] = jnp.zeros(dest_ref.shape, jnp.int32)
    dest_ref[0:TOP_K, :] = start + rank


def _dest(pad_starts, cols):
    R, T = cols.shape
    grid_spec = pltpu.PrefetchScalarGridSpec(
        num_scalar_prefetch=1,
        grid=(1,),
        in_specs=[pl.BlockSpec((R, T), lambda i, p: (0, 0))],
        out_specs=pl.BlockSpec((R, T), lambda i, p: (0, 0)),
    )
    return pl.pallas_call(
        functools.partial(_dest_kernel, ne=pad_starts.shape[0]),
        grid_spec=grid_spec,
        out_shape=jax.ShapeDtypeStruct((R, T), jnp.int32),
        compiler_params=_cparams(("arbitrary",), 16),
        name="dest",
    )(pad_starts, cols)


ROW_UNROLL = 8


def _dispatch_kernel(dest_ref, pend_ref, ug_ref, xs_hbm, ring, zbuf, sems, zsem, *, tm, blk, ne, ntok):
    i = pl.program_id(0)
    n = pl.num_programs(0)
    slot = i % 2

    def seg_tail(e):
        end = pend_ref[e]
        start = jnp.where(e == 0, 0, pend_ref[jnp.maximum(e - 1, 0)])
        tail = pl.multiple_of(jnp.maximum(end - blk, 0) * SUBLANES, blk * SUBLANES)
        return pltpu.make_async_copy(zbuf, xs_hbm.at[pl.ds(tail, blk * SUBLANES)], zsem), end > start

    @pl.when(i == 0)
    def _():
        zbuf[...] = jnp.zeros(zbuf.shape, zbuf.dtype)

        def zstart(p, carry):
            for q in range(2):
                cp, nonempty = seg_tail(2 * p + q)

                @pl.when(nonempty)
                def _():
                    cp.start(priority=q)
            return carry

        def zwait(e, carry):
            cp, nonempty = seg_tail(e)

            @pl.when(nonempty)
            def _():
                cp.wait()
            return carry

        def spare(b):
            row = pl.multiple_of(b * blk, blk)
            tile = pl.multiple_of(row * SUBLANES, blk * SUBLANES)
            return pltpu.make_async_copy(zbuf, xs_hbm.at[pl.ds(tile, blk * SUBLANES)], zsem), row >= pend_ref[ne - 1]

        def sstart(p, carry):
            for q in range(2):
                cp, unused = spare(2 * p + q)

                @pl.when(unused)
                def _():
                    cp.start(priority=q)
            return carry

        def swait(b, carry):
            cp, unused = spare(b)

            @pl.when(unused)
            def _():
                cp.wait()
            return carry

        nblk = xs_hbm.shape[0] // (blk * SUBLANES)
        assert ne % 2 == 0 and nblk % 2 == 0
        lax.fori_loop(0, ne // 2, zstart, 0)
        lax.fori_loop(0, nblk // 2, sstart, 0)
        lax.fori_loop(0, ne, zwait, 0)
        lax.fori_loop(0, nblk, swait, 0)

    def drain(s):
        for _ in range(TOP_K):
            pltpu.make_async_copy(ring.at[s], xs_hbm.at[pl.ds(0, tm * SUBLANES)], sems.at[s]).wait()

    @pl.when(i >= 2)
    def _():
        drain(slot)

    ring[slot] = ug_ref[...]

    def issue(c, carry):
        r0 = pl.multiple_of(c * ROW_UNROLL, ROW_UNROLL)
        for s in range(ROW_UNROLL):
            for k in range(TOP_K):
                d = dest_ref[k * ntok + i * tm + r0 + s]
                src = ring.at[slot, pl.ds(pl.multiple_of((r0 + s) * SUBLANES, SUBLANES), SUBLANES)]
                dst = xs_hbm.at[pl.ds(pl.multiple_of(d * SUBLANES, SUBLANES), SUBLANES)]
                pltpu.make_async_copy(src, dst, sems.at[slot]).start(priority=k % 2)
        return carry

    lax.fori_loop(0, tm // ROW_UNROLL, issue, 0)

    @pl.when(i == n - 1)
    def _():
        drain(slot)

        @pl.when(n >= 2)
        def _():
            drain(1 - slot)


def _dispatch(dest_flat, pad_ends, ug, nrows, tm, blk):
    T = ug.shape[0] // SUBLANES
    ne = pad_ends.shape[0]
    grid_spec = pltpu.PrefetchScalarGridSpec(
        num_scalar_prefetch=2,
        grid=(T // tm,),
        in_specs=[pl.BlockSpec((tm * SUBLANES, LANES), lambda i, d, p: (i, 0))],
        out_specs=pl.BlockSpec(memory_space=pl.ANY),
        scratch_shapes=[pltpu.VMEM((2, tm * SUBLANES, LANES), ug.dtype),
                        pltpu.VMEM((blk * SUBLANES, LANES), ug.dtype),
                        pltpu.SemaphoreType.DMA((2,)),
                        pltpu.SemaphoreType.DMA(())],
    )
    return pl.pallas_call(
        functools.partial(_dispatch_kernel, tm=tm, blk=blk, ne=ne, ntok=T),
        grid_spec=grid_spec,
        out_shape=jax.ShapeDtypeStruct((nrows * SUBLANES, LANES), ug.dtype),
        compiler_params=_cparams(("arbitrary",), 32),
        name="dispatch",
    )(dest_flat, pad_ends, ug)


WEIGHT_DMA_PRIORITY = (0, 1, 1)
WEIGHT_SLOTS = 3


def _experts_kernel(bexp_ref, first_ref, slot_ref, next_ref, lead_ref, nused_ref, x_ref, w1_hbm, w3_hbm, w2_hbm,
                    y_ref, w1b, w3b, w2b, wsems):
    i = pl.program_id(0)
    nused = nused_ref[0]

    def weight_copies(e, s):
        return (pltpu.make_async_copy(w1_hbm.at[e], w1b.at[s], wsems.at[s, 0]),
                pltpu.make_async_copy(w3_hbm.at[e], w3b.at[s], wsems.at[s, 1]),
                pltpu.make_async_copy(w2_hbm.at[e], w2b.at[s], wsems.at[s, 2]))

    def start_weights(e, s):
        for n, cp in enumerate(weight_copies(e, s)):
            cp.start(priority=WEIGHT_DMA_PRIORITY[n])

    @pl.when(i == 0)
    def _():
        for n in range(WEIGHT_SLOTS - 1):
            @pl.when(lead_ref[n] >= 0)
            def _():
                start_weights(lead_ref[n], n)

    @pl.when(i < nused)
    def _():
        s = slot_ref[i]

        @pl.when(first_ref[i] == 1)
        def _():
            @pl.when(next_ref[i] >= 0)
            def _():
                start_weights(next_ref[i], (s + WEIGHT_SLOTS - 1) % WEIGHT_SLOTS)
            for cp in weight_copies(bexp_ref[i], s):
                cp.wait()

        blk = x_ref.shape[0] // SUBLANES
        x_lo, x_hi = _unpack_bf16_pairs(_load_row_tiles(x_ref, 0, blk))
        half = x_lo.shape[1]
        a = (jnp.dot(x_lo, w1b[s, 0:half, :], preferred_element_type=F32)
             + jnp.dot(x_hi, w1b[s, half:2 * half, :], preferred_element_type=F32))
        b = (jnp.dot(x_lo, w3b[s, 0:half, :], preferred_element_type=F32)
             + jnp.dot(x_hi, w3b[s, half:2 * half, :], preferred_element_type=F32))
        hdn = a * jax.nn.sigmoid(a) * b
        cw = 2 * LANES
        for c0 in range(0, half, cw):
            y_lo = jnp.dot(hdn, w2b[s, :, c0:c0 + cw], preferred_element_type=F32)
            y_hi = jnp.dot(hdn, w2b[s, :, half + c0:half + c0 + cw], preferred_element_type=F32)
            words = _pack_bf16_words(y_lo, y_hi)
            for g in range(cw // LANES):
                y_ref[pl.ds(c0 // LANES + g, blk, stride=SUBLANES), :] = words[:, g * LANES:(g + 1) * LANES]

    @pl.when(i >= nused)
    def _():
        y_ref[...] = jnp.zeros(y_ref.shape, y_ref.dtype)


def _experts(bexp, first, slot, nxt, lead, nused, xs, w1, w3, w2, blk):
    P = xs.shape[0] // SUBLANES
    E, D, Fh = w1.shape
    nblk = P // blk
    smap = lambda i, *_: (i, 0)
    xmap = lambda i, be, fi, sl, nx, ld, nu: (jnp.minimum(i, nu[0] - 1), 0)
    grid_spec = pltpu.PrefetchScalarGridSpec(
        num_scalar_prefetch=6,
        grid=(nblk,),
        in_specs=[pl.BlockSpec((blk * SUBLANES, LANES), xmap),
                  pl.BlockSpec(memory_space=pl.ANY),
                  pl.BlockSpec(memory_space=pl.ANY),
                  pl.BlockSpec(memory_space=pl.ANY)],
        out_specs=pl.BlockSpec((blk * SUBLANES, LANES), smap),
        scratch_shapes=[pltpu.VMEM((WEIGHT_SLOTS, D, Fh), w1.dtype),
                        pltpu.VMEM((WEIGHT_SLOTS, D, Fh), w3.dtype),
                        pltpu.VMEM((WEIGHT_SLOTS, Fh, D), w2.dtype),
                        pltpu.SemaphoreType.DMA((WEIGHT_SLOTS, 3))],
    )
    return pl.pallas_call(
        _experts_kernel,
        grid_spec=grid_spec,
        out_shape=jax.ShapeDtypeStruct(xs.shape, xs.dtype),
        compiler_params=_cparams(("arbitrary",), 56),
        name="experts",
    )(bexp, first, slot, nxt, lead, nused, xs, w1, w3, w2)


def _row_gather(src_hbm, idx_ref, base, dst, sem, n):
    def body(c, carry):
        r0 = pl.multiple_of(c * ROW_UNROLL, ROW_UNROLL)
        for s in range(ROW_UNROLL):
            d = idx_ref[base + r0 + s]
            pltpu.make_async_copy(src_hbm.at[pl.ds(pl.multiple_of(d * SUBLANES, SUBLANES), SUBLANES)],
                                  dst.at[pl.ds(pl.multiple_of((r0 + s) * SUBLANES, SUBLANES), SUBLANES)],
                                  sem).start(priority=s % 2)
        return carry
    lax.fori_loop(0, n // ROW_UNROLL, body, 0)


def _row_gather_wait(src_hbm, dst, sem, n):
    pltpu.make_async_copy(src_hbm.at[pl.ds(0, n * SUBLANES)], dst, sem).wait()


def _combine_kernel(dest_ref, h_ref, info_ref, y_hbm, fw_ref, o_ref, ybuf, sems, *, tm, eps, ntok):
    i = pl.program_id(0)
    n = pl.num_programs(0)
    slot = i % 2
    nrow = TOP_K * tm

    def gather_step(step, s):
        for k in range(TOP_K):
            _row_gather(y_hbm, dest_ref, k * ntok + step * tm,
                        ybuf.at[s, pl.ds(k * tm * SUBLANES, tm * SUBLANES)], sems.at[s], tm)

    @pl.when(i == 0)
    def _():
        gather_step(0, 0)

    @pl.when(i + 1 < n)
    def _():
        gather_step(i + 1, 1 - slot)

    _row_gather_wait(y_hbm, ybuf.at[slot], sems.at[slot], nrow)
    half = o_ref.shape[1] // 2
    rc = tm

    def rows(ci, carry):
        r0 = pl.multiple_of(ci * rc, rc)
        rs = pl.ds(r0, rc)
        info = info_ref[rs, :]
        g1 = info[:, 2:3]
        g2 = info[:, 3:4]
        y1_lo, y1_hi = _unpack_bf16_pairs(_load_row_tiles(ybuf.at[slot], r0, rc))
        y2_lo, y2_hi = _unpack_bf16_pairs(_load_row_tiles(ybuf.at[slot], tm + r0, rc))
        h_lo = h_ref[rs, 0:half] + (g1 * y1_lo + g2 * y2_lo)
        h_hi = h_ref[rs, half:2 * half] + (g1 * y1_hi + g2 * y2_hi)
        ms = (jnp.sum(h_lo * h_lo, axis=-1, keepdims=True)
              + jnp.sum(h_hi * h_hi, axis=-1, keepdims=True)) / (2 * half)
        r = lax.rsqrt(ms + eps)
        o_ref[rs, 0:half] = h_lo * r * fw_ref[:, 0:half]
        o_ref[rs, half:2 * half] = h_hi * r * fw_ref[:, half:2 * half]
        return carry

    lax.fori_loop(0, tm // rc, rows, 0)


def _combine(dest, h, info, yb, final_w, tm):
    T, D = h.shape
    grid_spec = pltpu.PrefetchScalarGridSpec(
        num_scalar_prefetch=1,
        grid=(T // tm,),
        in_specs=[pl.BlockSpec((tm, D), lambda i, d: (i, 0)),
                  pl.BlockSpec((tm, LANES), lambda i, d: (i, 0)),
                  pl.BlockSpec(memory_space=pl.ANY),
                  pl.BlockSpec((1, D), lambda i, d: (0, 0))],
        out_specs=pl.BlockSpec((tm, D), lambda i, d: (i, 0)),
        scratch_shapes=[pltpu.VMEM((2, TOP_K * tm * SUBLANES, LANES), yb.dtype),
                        pltpu.SemaphoreType.DMA((2,))],
    )
    return pl.pallas_call(
        functools.partial(_combine_kernel, tm=tm, eps=RMS_EPS, ntok=T),
        grid_spec=grid_spec,
        out_shape=jax.ShapeDtypeStruct((T, D), F32),
        compiler_params=_cparams(("arbitrary",), 48),
        name="combine",
    )(dest, h, info, yb, final_w)


def _layer(h_in, l, B, S, mix_norm_w, w_in, conv_dw_w, conv_dw_b, conv_ln_w, conv_ln_b,
           lam_q1, lam_k1, lam_q2, lam_k2, attn_subln_w, w_out, ffn_norm_w,
           w_group, b_group, w_expert_gate, b_expert_gate, w1, w3, w2):
    T, D = h_in.shape
    d_conv = conv_dw_w.shape[1]
    d_attn = (w_in.shape[1] - 2 * d_conv) // 3
    ng = w_group.shape[1]
    ne = w_expert_gate.shape[1]
    epg = ne // ng
    assert ng + ne <= LANES
    lam_init = 0.8 - 0.6 * math.exp(-0.3 * l)

    w_bf = w_in.astype(BF16)
    proj_ag, u = _norm_inproj(h_in, mix_norm_w.reshape(1, D), w_bf, 2 * d_conv)
    y_conv, qkv = _conformer_qkv(proj_ag, u, w_bf, 2 * d_conv, conv_dw_w, conv_dw_b, conv_ln_w, conv_ln_b, B, S)
    y_attn = _diff_attn(qkv, lam_q1, lam_k1, lam_q2, lam_k2, attn_subln_w, B, S, 0, d_attn, lam_init)

    wr = jnp.concatenate([w_group, w_expert_gate, jnp.zeros((D, LANES - ng - ne), F32)], axis=1)
    wr_hi = wr.astype(BF16)
    wr_lo = (wr - wr_hi.astype(F32)).astype(BF16)
    br = jnp.concatenate([b_group, b_expert_gate.reshape(-1), jnp.zeros((LANES - ng - ne,), F32)]).reshape(1, LANES)
    wr_hl = jnp.concatenate([wr_hi, wr_lo], axis=1)
    h, logits, ug = _outproj(h_in, y_conv, y_attn, w_out.astype(BF16), ffn_norm_w.reshape(1, D), wr_hl, br)

    info, cols, cnt = _route(logits, ng, epg)

    blk = 256
    tmd = min(256, T)
    A = T * TOP_K
    nblk = (A + ne * (blk - 1) + blk - 1) // blk
    i32 = jnp.int32
    counts = cnt[0, ng:ng + ne].astype(i32)
    padded = (counts + blk - 1) // blk * blk
    pad_ends = jnp.cumsum(padded).astype(i32)
    pad_starts = pad_ends - padded
    dest = _dest(pad_starts, cols)[0:TOP_K].reshape(-1)
    nused = (pad_ends[-1] // blk).astype(i32)
    bpos = jnp.arange(nblk, dtype=i32)
    brow = jnp.minimum(bpos, nused - 1) * blk
    bexp = jnp.minimum(jnp.sum((pad_ends[None, :] <= brow[:, None]).astype(i32), axis=1), ne - 1)
    first = ((bpos < nused) & ((bpos == 0) | (bexp != jnp.roll(bexp, 1)))).astype(i32)
    slot = ((jnp.cumsum(first) - 1) % WEIGHT_SLOTS).astype(i32)
    used_idx = jnp.where(padded > 0, jnp.arange(ne, dtype=i32), ne)
    suffix_min = lax.cummin(used_idx, reverse=True)
    next_used = jnp.concatenate([suffix_min[1:], jnp.full((2,), ne, i32)])
    ahead = jnp.arange(ne, dtype=i32)
    for _ in range(WEIGHT_SLOTS - 1):
        ahead = next_used[ahead]
    ahead = jnp.where(ahead >= ne, -1, ahead)
    nxt = ahead[bexp].astype(i32)
    lead = [suffix_min[0]]
    for _ in range(WEIGHT_SLOTS - 2):
        lead.append(next_used[lead[-1]])
    lead = jnp.stack([jnp.where(e >= ne, -1, e) for e in lead]).astype(i32)

    xs = _dispatch(dest, pad_ends, ug, nblk * blk, tmd, blk)
    yb = _experts(bexp, first, slot, nxt, lead, nused.reshape(1), xs, w1, w3, w2, blk)
    return h, info, dest, yb


def kernel(x, mix_norm_w, w_in, conv_dw_w, conv_dw_b, conv_ln_w, conv_ln_b, lam_q1, lam_k1, lam_q2, lam_k2,
           attn_subln_w, w_out, ffn_norm_w, w_group, b_group, w_expert_gate, b_expert_gate, w1, w3, w2,
           final_norm_w):
    B, S, D = x.shape
    depth = w_in.shape[0]
    assert depth == 1
    T = B * S
    tmc = min(256, T)
    h = x.reshape(T, D)
    for l in range(depth):
        h, info, dest, yb = _layer(
            h, l, B, S, mix_norm_w[l], w_in[l], conv_dw_w[l], conv_dw_b[l], conv_ln_w[l], conv_ln_b[l],
            lam_q1[l], lam_k1[l], lam_q2[l], lam_k2[l], attn_subln_w[l], w_out[l], ffn_norm_w[l],
            w_group[l], b_group[l], w_expert_gate[l], b_expert_gate[l], w1[l], w3[l], w2[l])
        h = _combine(dest, h, info, yb, final_norm_w.reshape(1, D), tmc)
    return h.reshape(B, S, D)
```
